```python
import jax, jax.numpy as jnp
from jax import lax
import numpy as np

D_MODEL = 1024
BATCH = 8
SEQ = 8192
DEPTH = 1

MIX_WIDTH = D_MODEL
CONV_CH = MIX_WIDTH // 2
POOL_WIDTH = MIX_WIDTH - CONV_CH
CONV_K = 3
POOL_WINDOWS = (2, 4, 8, 16)
N_POOL_GROUPS = len(POOL_WINDOWS)
POOL_GROUP_DIM = POOL_WIDTH // N_POOL_GROUPS
IN_PROJ_WIDTH = 3 * CONV_CH + POOL_WIDTH
D_FF = 4 * D_MODEL
N_MOD = 6
LN_EPS = 1e-5
DEEPNORM_ALPHA = (2.0 * DEPTH) ** 0.25
DEEPNORM_BETA = (8.0 * DEPTH) ** -0.25
ADA_INIT = 0.25

kernel_name = "hybrid_conv_pool_sqrelu_deepnorm_adaln"


def _layer_norm(x, g, b):
    xf = x.astype(jnp.float32)
    mu = jnp.mean(xf, axis=-1, keepdims=True)
    var = jnp.mean(jnp.square(xf - mu), axis=-1, keepdims=True)
    y = (xf - mu) * lax.rsqrt(var + LN_EPS) * g.astype(jnp.float32) + b.astype(jnp.float32)
    return y.astype(x.dtype)


def _short_conv(u, w):
    s = u.shape[1]
    up = jnp.pad(u, ((0, 0), (CONV_K - 1, 0), (0, 0)))
    y = up[:, 0:s] * w[0]
    for k in range(1, CONV_K):
        y = y + up[:, k:k + s] * w[k]
    return y


def _multiscale_pool(u, w_pool, pool_scale):
    b, s, _ = u.shape
    uf = u.astype(jnp.float32)
    cs = jnp.cumsum(uf, axis=1)
    pos = jnp.arange(1, s + 1, dtype=jnp.float32)[None, :, None]
    outs = []
    for gi, win in enumerate(POOL_WINDOWS):
        sl = slice(gi * POOL_GROUP_DIM, (gi + 1) * POOL_GROUP_DIM)
        cs_g = cs[..., sl]
        prev = jnp.pad(cs_g, ((0, 0), (win, 0), (0, 0)))[:, :s]
        mean = (cs_g - prev) / jnp.minimum(pos, float(win))
        outs.append(mean - uf[..., sl])
    p = jnp.stack(outs, axis=2)
    p = jnp.einsum("bsgc,gcd->bsgd", p, w_pool.astype(jnp.float32))
    p = p.reshape(b, s, POOL_WIDTH) * pool_scale.astype(jnp.float32)
    return p.astype(u.dtype)


def _fwd_setup_inputs(seed: int = 0) -> dict:
    key = jax.random.key(seed)
    ks = jax.random.split(key, 16)
    f32 = jnp.float32
    x = jax.random.normal(ks[0], (BATCH, SEQ, D_MODEL), f32)
    c = jax.random.normal(ks[1], (BATCH, D_MODEL), f32)
    w_ada = jax.random.normal(ks[2], (DEPTH, D_MODEL, N_MOD * D_MODEL), f32) * (ADA_INIT * D_MODEL ** -0.5)
    b_ada = 0.02 * jax.random.normal(ks[3], (DEPTH, N_MOD * D_MODEL), f32)
    w_in = jax.random.normal(ks[4], (DEPTH, D_MODEL, IN_PROJ_WIDTH), f32) * D_MODEL ** -0.5
    conv_w = jax.random.normal(ks[5], (DEPTH, CONV_K, CONV_CH), f32) * CONV_K ** -0.5
    w_pool = jax.random.normal(ks[6], (DEPTH, N_POOL_GROUPS, POOL_GROUP_DIM, POOL_GROUP_DIM), f32) * POOL_GROUP_DIM ** -0.5
    pool_scale = 1.0 + 0.1 * jax.random.normal(ks[7], (DEPTH, POOL_WIDTH), f32)
    w_out = jax.random.normal(ks[8], (DEPTH, MIX_WIDTH, D_MODEL), f32) * (DEEPNORM_BETA * MIX_WIDTH ** -0.5)
    ln1_g = 1.0 + 0.02 * jax.random.normal(ks[9], (DEPTH, D_MODEL), f32)
    ln1_b = 0.02 * jax.random.normal(ks[10], (DEPTH, D_MODEL), f32)
    w_mlp_in = jax.random.normal(ks[11], (DEPTH, D_MODEL, D_FF), f32) * D_MODEL ** -0.5
    w_mlp_out = jax.random.normal(ks[12], (DEPTH, D_FF, D_MODEL), f32) * (DEEPNORM_BETA * D_FF ** -0.5)
    ln2_g = 1.0 + 0.02 * jax.random.normal(ks[13], (DEPTH, D_MODEL), f32)
    ln2_b = 0.02 * jax.random.normal(ks[14], (DEPTH, D_MODEL), f32)
    return {"x": x, "c": c, "w_ada": w_ada, "b_ada": b_ada, "w_in": w_in, "conv_w": conv_w,
            "w_pool": w_pool, "pool_scale": pool_scale, "w_out": w_out, "ln1_g": ln1_g,
            "ln1_b": ln1_b, "w_mlp_in": w_mlp_in, "w_mlp_out": w_mlp_out, "ln2_g": ln2_g,
            "ln2_b": ln2_b}


def _fwd_reference(x, c, w_ada, b_ada, w_in, conv_w, w_pool, pool_scale, w_out, ln1_g, ln1_b,
              w_mlp_in, w_mlp_out, ln2_g, ln2_b):
    cond = jax.nn.silu(c)
    for l in range(DEPTH):
        mod = (cond @ w_ada[l] + b_ada[l])[:, None, :]
        sh1, sc1, g1, sh2, sc2, g2 = jnp.split(mod, N_MOD, axis=-1)

        h = x * (1.0 + sc1) + sh1
        z = h @ w_in[l]
        gate_b, gate_c, v_conv, v_pool = jnp.split(
            z, [CONV_CH, 2 * CONV_CH, 3 * CONV_CH], axis=-1)
        y_conv = gate_b * _short_conv(gate_c * v_conv, conv_w[l])
        y_pool = _multiscale_pool(v_pool, w_pool[l], pool_scale[l])
        mix = jnp.concatenate([y_conv, y_pool], axis=-1) @ w_out[l]
        x = _layer_norm(DEEPNORM_ALPHA * x + (1.0 + g1) * mix, ln1_g[l], ln1_b[l])

        h = x * (1.0 + sc2) + sh2
        f = jnp.square(jax.nn.relu(h @ w_mlp_in[l])) @ w_mlp_out[l]
        x = _layer_norm(DEEPNORM_ALPHA * x + (1.0 + g2) * f, ln2_g[l], ln2_b[l])
    return x


import jax as _jax
import jax.numpy as _jnp

TWIN_FORMAT = 'train_step'
FWD_PARAMS = ['x', 'c', 'w_ada', 'b_ada', 'w_in', 'conv_w', 'w_pool', 'pool_scale', 'w_out', 'ln1_g', 'ln1_b', 'w_mlp_in', 'w_mlp_out', 'ln2_g', 'ln2_b']
TWIN_WEIGHTS = ['w_ada', 'b_ada', 'w_in', 'conv_w', 'w_pool', 'pool_scale', 'w_out', 'ln1_g', 'ln1_b', 'w_mlp_in', 'w_mlp_out', 'ln2_g', 'ln2_b']
TWIN_DIFF_INPUT = 'x'
TWIN_INPUTS = ['x', 'c', 'w_ada', 'b_ada', 'w_in', 'conv_w', 'w_pool', 'pool_scale', 'w_out', 'ln1_g', 'ln1_b', 'w_mlp_in', 'w_mlp_out', 'ln2_g', 'ln2_b', 'loss_target', 'm_w_ada', 'm_b_ada', 'm_w_in', 'm_conv_w', 'm_w_pool', 'm_pool_scale', 'm_w_out', 'm_ln1_g', 'm_ln1_b', 'm_w_mlp_in', 'm_w_mlp_out', 'm_ln2_g', 'm_ln2_b', 'v_w_ada', 'v_b_ada', 'v_w_in', 'v_conv_w', 'v_w_pool', 'v_pool_scale', 'v_w_out', 'v_ln1_g', 'v_ln1_b', 'v_w_mlp_in', 'v_w_mlp_out', 'v_ln2_g', 'v_ln2_b']
TWIN_OUTPUTS = ['loss', 'grad_x', 'grad_w_ada', 'grad_b_ada', 'grad_w_in', 'grad_conv_w', 'grad_w_pool', 'grad_pool_scale', 'grad_w_out', 'grad_ln1_g', 'grad_ln1_b', 'grad_w_mlp_in', 'grad_w_mlp_out', 'grad_ln2_g', 'grad_ln2_b', 'delta_w_ada', 'delta_b_ada', 'delta_w_in', 'delta_conv_w', 'delta_w_pool', 'delta_pool_scale', 'delta_w_out', 'delta_ln1_g', 'delta_ln1_b', 'delta_w_mlp_in', 'delta_w_mlp_out', 'delta_ln2_g', 'delta_ln2_b', 'new_m_w_ada', 'new_m_b_ada', 'new_m_w_in', 'new_m_conv_w', 'new_m_w_pool', 'new_m_pool_scale', 'new_m_w_out', 'new_m_ln1_g', 'new_m_ln1_b', 'new_m_w_mlp_in', 'new_m_w_mlp_out', 'new_m_ln2_g', 'new_m_ln2_b', 'new_v_w_ada', 'new_v_b_ada', 'new_v_w_in', 'new_v_conv_w', 'new_v_w_pool', 'new_v_pool_scale', 'new_v_w_out', 'new_v_ln1_g', 'new_v_ln1_b', 'new_v_w_mlp_in', 'new_v_w_mlp_out', 'new_v_ln2_g', 'new_v_ln2_b']
TWIN_LEAF_KINDS = {'loss': 'loss', 'grad_x': 'grad_x', 'grad_w_ada': 'grad_w', 'grad_b_ada': 'grad_w', 'grad_w_in': 'grad_w', 'grad_conv_w': 'grad_w', 'grad_w_pool': 'grad_w', 'grad_pool_scale': 'grad_w', 'grad_w_out': 'grad_w', 'grad_ln1_g': 'grad_w', 'grad_ln1_b': 'grad_w', 'grad_w_mlp_in': 'grad_w', 'grad_w_mlp_out': 'grad_w', 'grad_ln2_g': 'grad_w', 'grad_ln2_b': 'grad_w', 'delta_w_ada': 'delta_w', 'delta_b_ada': 'delta_w', 'delta_w_in': 'delta_w', 'delta_conv_w': 'delta_w', 'delta_w_pool': 'delta_w', 'delta_pool_scale': 'delta_w', 'delta_w_out': 'delta_w', 'delta_ln1_g': 'delta_w', 'delta_ln1_b': 'delta_w', 'delta_w_mlp_in': 'delta_w', 'delta_w_mlp_out': 'delta_w', 'delta_ln2_g': 'delta_w', 'delta_ln2_b': 'delta_w', 'new_m_w_ada': 'new_m', 'new_m_b_ada': 'new_m', 'new_m_w_in': 'new_m', 'new_m_conv_w': 'new_m', 'new_m_w_pool': 'new_m', 'new_m_pool_scale': 'new_m', 'new_m_w_out': 'new_m', 'new_m_ln1_g': 'new_m', 'new_m_ln1_b': 'new_m', 'new_m_w_mlp_in': 'new_m', 'new_m_w_mlp_out': 'new_m', 'new_m_ln2_g': 'new_m', 'new_m_ln2_b': 'new_m', 'new_v_w_ada': 'new_v', 'new_v_b_ada': 'new_v', 'new_v_w_in': 'new_v', 'new_v_conv_w': 'new_v', 'new_v_w_pool': 'new_v', 'new_v_pool_scale': 'new_v', 'new_v_w_out': 'new_v', 'new_v_ln1_g': 'new_v', 'new_v_ln1_b': 'new_v', 'new_v_w_mlp_in': 'new_v', 'new_v_w_mlp_out': 'new_v', 'new_v_ln2_g': 'new_v', 'new_v_ln2_b': 'new_v'}


def _forward(args):
    return _fwd_reference(*[args[k] for k in FWD_PARAMS])


def _output_shape():
    def fwd():
        inp = _fwd_setup_inputs(0)
        return _fwd_reference(*[inp[k] for k in FWD_PARAMS])
    out = _jax.eval_shape(fwd)
    return out.shape, out.dtype

N_MICROBATCH = 1
ADAM_LR = 0.001
ADAM_B1 = 0.9
ADAM_B2 = 0.999
ADAM_EPS = 1e-08
ADAM_WD = 0.01
ADAM_STEP = 10
PER_EXAMPLE_BATCH_AXIS = {'x': 0, 'c': 0, 'loss_target': 0}
SHARED_INPUTS = []
_WEIGHT_DTYPES = {'w_ada': _jnp.float32, 'b_ada': _jnp.float32, 'w_in': _jnp.float32, 'conv_w': _jnp.float32, 'w_pool': _jnp.float32, 'pool_scale': _jnp.float32, 'w_out': _jnp.float32, 'ln1_g': _jnp.float32, 'ln1_b': _jnp.float32, 'w_mlp_in': _jnp.float32, 'w_mlp_out': _jnp.float32, 'ln2_g': _jnp.float32, 'ln2_b': _jnp.float32}
MOMENT_SCALE = {'w_ada': 1.127921e-01, 'b_ada': 3.000034e-01, 'w_in': 1.233678e-01, 'conv_w': 1.305627e-01, 'w_pool': 1.067680e-01, 'pool_scale': 1.172906e-01, 'w_out': 1.994422e-01, 'ln1_g': 1.618598e+00, 'ln1_b': 9.439408e-01, 'w_mlp_in': 7.979777e-02, 'w_mlp_out': 3.925598e-01, 'ln2_g': 6.420851e+01, 'ln2_b': 1.429843e+01}


def _to_microbatches(a, axis):
    t = _jnp.moveaxis(a, axis, 0)
    t = t.reshape((N_MICROBATCH, t.shape[0] // N_MICROBATCH) + t.shape[1:])
    return _jnp.moveaxis(t, 1, axis + 1)


def setup_inputs(seed: int = 0) -> dict:
    inp = _fwd_setup_inputs(seed)
    key = _jax.random.fold_in(_jax.random.key(seed), 7919)
    shape, _ = _output_shape()
    out = dict(inp)
    out["loss_target"] = _jax.random.normal(_jax.random.fold_in(key, 0), shape, _jnp.float32)
    for i, name in enumerate(TWIN_WEIGHTS):
        w = inp[name].astype(_jnp.float32)
        if MOMENT_SCALE is None:
            s = _jnp.sqrt(_jnp.mean(_jnp.square(w)) + 1e-30)
        else:
            s = MOMENT_SCALE[name]
        km, kv = _jax.random.split(_jax.random.fold_in(key, i + 1))
        out[name] = w
        out["m_" + name] = s * _jax.random.normal(km, w.shape, _jnp.float32)
        out["v_" + name] = (s * s) * _jax.random.uniform(kv, w.shape, _jnp.float32, 0.5, 1.5)
    if N_MICROBATCH > 1:
        for name, axis in PER_EXAMPLE_BATCH_AXIS.items():
            out[name] = _to_microbatches(out[name], axis)
    return {'x': out['x'], 'c': out['c'], 'w_ada': out['w_ada'], 'b_ada': out['b_ada'], 'w_in': out['w_in'], 'conv_w': out['conv_w'], 'w_pool': out['w_pool'], 'pool_scale': out['pool_scale'], 'w_out': out['w_out'], 'ln1_g': out['ln1_g'], 'ln1_b': out['ln1_b'], 'w_mlp_in': out['w_mlp_in'], 'w_mlp_out': out['w_mlp_out'], 'ln2_g': out['ln2_g'], 'ln2_b': out['ln2_b'], 'loss_target': out['loss_target'], 'm_w_ada': out['m_w_ada'], 'm_b_ada': out['m_b_ada'], 'm_w_in': out['m_w_in'], 'm_conv_w': out['m_conv_w'], 'm_w_pool': out['m_w_pool'], 'm_pool_scale': out['m_pool_scale'], 'm_w_out': out['m_w_out'], 'm_ln1_g': out['m_ln1_g'], 'm_ln1_b': out['m_ln1_b'], 'm_w_mlp_in': out['m_w_mlp_in'], 'm_w_mlp_out': out['m_w_mlp_out'], 'm_ln2_g': out['m_ln2_g'], 'm_ln2_b': out['m_ln2_b'], 'v_w_ada': out['v_w_ada'], 'v_b_ada': out['v_b_ada'], 'v_w_in': out['v_w_in'], 'v_conv_w': out['v_conv_w'], 'v_w_pool': out['v_w_pool'], 'v_pool_scale': out['v_pool_scale'], 'v_w_out': out['v_w_out'], 'v_ln1_g': out['v_ln1_g'], 'v_ln1_b': out['v_ln1_b'], 'v_w_mlp_in': out['v_w_mlp_in'], 'v_w_mlp_out': out['v_w_mlp_out'], 'v_ln2_g': out['v_ln2_g'], 'v_ln2_b': out['v_ln2_b']}


def _loss(weights, diff, rest, loss_target):
    with _jax.named_scope("forward"):
        args = {**rest, TWIN_DIFF_INPUT: diff, **{k: w.astype(_WEIGHT_DTYPES[k]) for k, w in weights.items()}}
        y = _forward(args)
    with _jax.named_scope("loss_head"):
        err = _jnp.square(y.astype(_jnp.float32) - loss_target)
        return 0.5 * _jnp.sum(_jnp.mean(err, axis=-1)) if err.ndim else 0.5 * err


def _adamw(w, g, m, v):
    m = ADAM_B1 * m + (1.0 - ADAM_B1) * g
    v = ADAM_B2 * v + (1.0 - ADAM_B2) * _jnp.square(g)
    m_hat = m / (1.0 - ADAM_B1 ** ADAM_STEP)
    v_hat = v / (1.0 - ADAM_B2 ** ADAM_STEP)
    delta = -ADAM_LR * (m_hat / (_jnp.sqrt(v_hat) + ADAM_EPS) + ADAM_WD * w)
    return delta, m, v


def reference(x, c, w_ada, b_ada, w_in, conv_w, w_pool, pool_scale, w_out, ln1_g, ln1_b, w_mlp_in, w_mlp_out, ln2_g, ln2_b, loss_target, m_w_ada, m_b_ada, m_w_in, m_conv_w, m_w_pool, m_pool_scale, m_w_out, m_ln1_g, m_ln1_b, m_w_mlp_in, m_w_mlp_out, m_ln2_g, m_ln2_b, v_w_ada, v_b_ada, v_w_in, v_conv_w, v_w_pool, v_pool_scale, v_w_out, v_ln1_g, v_ln1_b, v_w_mlp_in, v_w_mlp_out, v_ln2_g, v_ln2_b):
    given = dict(x=x, c=c, w_ada=w_ada, b_ada=b_ada, w_in=w_in, conv_w=conv_w, w_pool=w_pool, pool_scale=pool_scale, w_out=w_out, ln1_g=ln1_g, ln1_b=ln1_b, w_mlp_in=w_mlp_in, w_mlp_out=w_mlp_out, ln2_g=ln2_g, ln2_b=ln2_b, loss_target=loss_target, m_w_ada=m_w_ada, m_b_ada=m_b_ada, m_w_in=m_w_in, m_conv_w=m_conv_w, m_w_pool=m_w_pool, m_pool_scale=m_pool_scale, m_w_out=m_w_out, m_ln1_g=m_ln1_g, m_ln1_b=m_ln1_b, m_w_mlp_in=m_w_mlp_in, m_w_mlp_out=m_w_mlp_out, m_ln2_g=m_ln2_g, m_ln2_b=m_ln2_b, v_w_ada=v_w_ada, v_b_ada=v_b_ada, v_w_in=v_w_in, v_conv_w=v_conv_w, v_w_pool=v_w_pool, v_pool_scale=v_pool_scale, v_w_out=v_w_out, v_ln1_g=v_ln1_g, v_ln1_b=v_ln1_b, v_w_mlp_in=v_w_mlp_in, v_w_mlp_out=v_w_mlp_out, v_ln2_g=v_ln2_g, v_ln2_b=v_ln2_b)
    weights = {n: given[n] for n in TWIN_WEIGHTS}
    shared = {n: given[n] for n in SHARED_INPUTS}
    per_example = {n: given[n] for n in ['x', 'c']}
    grad_fn = _jax.value_and_grad(_loss, argnums=(0, 1))

    def one_microbatch(ex, loss_target):
        ex = dict(ex)
        diff = ex.pop(TWIN_DIFF_INPUT)
        return grad_fn(weights, diff, {**shared, **ex}, loss_target)

    if N_MICROBATCH == 1:
        loss, (grad_w, grad_x) = one_microbatch(per_example, given["loss_target"])
    else:
        def body(carry, xs):
            loss_sum, grad_sum = carry
            l_k, (gw_k, gx_k) = one_microbatch(xs[0], xs[1])
            with _jax.named_scope("update"):
                return (loss_sum + l_k, _jax.tree.map(_jnp.add, grad_sum, gw_k)), gx_k

        init = (_jnp.zeros((), _jnp.float32), _jax.tree.map(_jnp.zeros_like, weights))
        (loss, grad_w), grad_x = _jax.lax.scan(body, init, (per_example, given["loss_target"]))
    with _jax.named_scope("update"):
        delta_w, new_m, new_v = {}, {}, {}
        for n in TWIN_WEIGHTS:
            delta_w[n], new_m[n], new_v[n] = _adamw(weights[n], grad_w[n], given["m_" + n], given["v_" + n])
    return (loss, grad_x, *[grad_w[n] for n in TWIN_WEIGHTS], *[delta_w[n] for n in TWIN_WEIGHTS],
            *[new_m[n] for n in TWIN_WEIGHTS], *[new_v[n] for n in TWIN_WEIGHTS])
```

```python
import jax
import jax.numpy as jnp
from jax import lax
from jax.experimental import pallas as pl
from jax.experimental.pallas import tpu as pltpu

F32 = jnp.float32
BF16 = jnp.bfloat16
MESH = pl.DeviceIdType.MESH

LN_EPS = 1e-5
DEEPNORM_ALPHA = 2.0 ** 0.25
POOL_WINDOWS = (2, 4, 8, 16)
N_GROUPS = len(POOL_WINDOWS)
HALO = 16
N_CHIPS = 4
N_DEV = 8
LANES = 128
SUBLANES = 8
VMEM_LIMIT = 56 * 1024 * 1024

ADAM_LR = 0.001
ADAM_B1 = 0.9
ADAM_B2 = 0.999
ADAM_EPS = 1e-08
ADAM_WD = 0.01
ADAM_STEP = 10

VMEM_SPEC = pl.BlockSpec(memory_space=pltpu.VMEM)
HBM_SPEC = pl.BlockSpec(memory_space=pltpu.HBM)


def _dot(a, b):
    return jnp.dot(a, b, preferred_element_type=F32)


def _dot_nt(a, b):
    return lax.dot_general(a, b, (((1,), (1,)), ((), ())), preferred_element_type=F32)


def _dot_tn(a, b):
    return lax.dot_general(a, b, (((0,), (0,)), ((), ())), preferred_element_type=F32)


def _fold8(v):
    r, n = v.shape
    return jnp.sum(v.reshape(r // SUBLANES, SUBLANES, n), axis=0)


def _mesh_pos():
    return lax.axis_index("x"), lax.axis_index("y"), lax.axis_index("c")


def _flip(v, bit):
    return 1 - v if bit else v


def _params(**kw):
    return pltpu.CompilerParams(vmem_limit_bytes=VMEM_LIMIT, **kw)


def _mod_exchange(c_row, w_ada_s, b_ada, conv_w_s):
    d = c_row.shape[1]
    na = w_ada_s.shape[1]
    cq = conv_w_s.shape[1]

    def body(c_ref, w_ref, b_ref, cw_ref, mod_ref, cond_ref, conv_ref,
             csend, cbuf, mbuf, cvsend, cvbuf, s1, r1, s2, r2, s3, r3):
        x, y, c = _mesh_pos()
        me = 4 * x + 2 * y + c
        chip = 2 * x + y
        csend[...] = jnp.broadcast_to(c_ref[...], (SUBLANES, d))
        cbuf[me] = csend[...]
        first = []
        for m in range(1, N_DEV):
            peer = (_flip(x, m & 4), _flip(y, m & 2), _flip(c, m & 1))
            cp = pltpu.make_async_remote_copy(csend, cbuf.at[me], s1.at[m - 1], r1.at[m - 1],
                                              device_id=peer, device_id_type=MESH)
            cp.start()
            first.append(cp)
        cvsend[...] = jnp.zeros((SUBLANES, cq), F32)
        cvsend[0:3, :] = cw_ref[...]
        cvbuf[chip] = cvsend[...]
        for m in range(1, N_DEV):
            src = 4 * _flip(x, m & 4) + 2 * _flip(y, m & 2) + _flip(c, m & 1)
            pltpu.make_async_remote_copy(csend, cbuf.at[src], s1.at[m - 1], r1.at[m - 1],
                                         device_id=(x, y, c), device_id_type=MESH).wait_recv()
        rows = lax.broadcasted_iota(jnp.int32, (SUBLANES, d), 0)
        call = jnp.zeros((SUBLANES, d), F32)
        for b in range(N_DEV):
            call = jnp.where(rows == b, cbuf[b], call)
        cond = call * jax.nn.sigmoid(call)
        cond_ref[...] = cond
        part = jnp.dot(cond, w_ref[...], preferred_element_type=F32, precision=lax.Precision.HIGHEST)
        mbuf[chip] = part
        second = []
        for m in range(1, N_CHIPS):
            peer = (_flip(x, m & 2), _flip(y, m & 1), c)
            cp = pltpu.make_async_remote_copy(mbuf.at[chip], mbuf.at[chip], s2.at[m - 1], r2.at[m - 1],
                                              device_id=peer, device_id_type=MESH)
            cp.start()
            second.append(cp)
            cp = pltpu.make_async_remote_copy(cvsend, cvbuf.at[chip], s3.at[m - 1], r3.at[m - 1],
                                              device_id=peer, device_id_type=MESH)
            cp.start()
            second.append(cp)
        for m in range(1, N_CHIPS):
            src = 2 * _flip(x, m & 2) + _flip(y, m & 1)
            pltpu.make_async_remote_copy(mbuf.at[src], mbuf.at[src], s2.at[m - 1], r2.at[m - 1],
                                         device_id=(x, y, c), device_id_type=MESH).wait_recv()
            pltpu.make_async_remote_copy(cvsend, cvbuf.at[src], s3.at[m - 1], r3.at[m - 1],
                                         device_id=(x, y, c), device_id_type=MESH).wait_recv()
        for cp in first + second:
            cp.wait_send()
        rows_n = lax.broadcasted_iota(jnp.int32, (SUBLANES, na), 0)
        for k in range(N_CHIPS):
            mine = jnp.sum(jnp.where(rows_n == me, mbuf[k], 0.0), axis=0, keepdims=True)
            mod_ref[:, k * na:(k + 1) * na] = jnp.broadcast_to(mine + b_ref[:, k * na:(k + 1) * na], (SUBLANES, na))
            conv_ref[:, k * cq:(k + 1) * cq] = cvbuf[k]

    return pl.pallas_call(
        body, name="mod_exchange",
        out_shape=(jax.ShapeDtypeStruct((SUBLANES, N_CHIPS * na), F32),
                   jax.ShapeDtypeStruct((SUBLANES, d), F32),
                   jax.ShapeDtypeStruct((SUBLANES, N_CHIPS * cq), F32)),
        in_specs=[VMEM_SPEC] * 4, out_specs=(VMEM_SPEC,) * 3,
        scratch_shapes=[
            pltpu.VMEM((SUBLANES, d), F32), pltpu.VMEM((N_DEV, SUBLANES, d), F32),
            pltpu.VMEM((N_CHIPS, SUBLANES, na), F32),
            pltpu.VMEM((SUBLANES, cq), F32), pltpu.VMEM((N_CHIPS, SUBLANES, cq), F32),
            pltpu.SemaphoreType.DMA((N_DEV - 1,)), pltpu.SemaphoreType.DMA((N_DEV - 1,)),
            pltpu.SemaphoreType.DMA((N_CHIPS - 1,)), pltpu.SemaphoreType.DMA((N_CHIPS - 1,)),
            pltpu.SemaphoreType.DMA((N_CHIPS - 1,)), pltpu.SemaphoreType.DMA((N_CHIPS - 1,)),
        ],
        compiler_params=_params(),
    )(c_row, w_ada_s, b_ada, conv_w_s)


def _gather_weights(shards):
    n = len(shards)

    def body(*refs):
        ins, outs = refs[:n], refs[n:2 * n]
        s_ici, r_ici, s_d2d, r_d2d, s_loc = refs[2 * n:]
        x, y, c = _mesh_pos()
        chip = 2 * x + y
        local = []
        for i in range(n):
            cp = pltpu.make_async_copy(ins[i], outs[i].at[chip], s_loc.at[i])
            cp.start()
            local.append(cp)

        def half(i, ref, which):
            rh = shards[i].shape[0] // 2
            return ref.at[pl.ds(which * rh, rh), :]

        sends = []
        for i in range(n):
            for m in range(1, N_CHIPS):
                peer = (_flip(x, m & 2), _flip(y, m & 1), c)
                cp = pltpu.make_async_remote_copy(half(i, ins[i], c), half(i, outs[i].at[chip], c),
                                                  s_ici.at[i * 3 + m - 1], r_ici.at[i * 3 + m - 1],
                                                  device_id=peer, device_id_type=MESH)
                cp.start()
                sends.append(cp)
        for i in range(n):
            for m in range(1, N_CHIPS):
                src = 2 * _flip(x, m & 2) + _flip(y, m & 1)
                blk = half(i, outs[i].at[src], c)
                pltpu.make_async_remote_copy(blk, blk, s_ici.at[i * 3 + m - 1], r_ici.at[i * 3 + m - 1],
                                             device_id=(x, y, c), device_id_type=MESH).wait_recv()
                cp = pltpu.make_async_remote_copy(blk, blk, s_d2d.at[i * 3 + m - 1], r_d2d.at[i * 3 + m - 1],
                                                  device_id=(x, y, 1 - c), device_id_type=MESH)
                cp.start()
                sends.append(cp)
        for i in range(n):
            for m in range(1, N_CHIPS):
                src = 2 * _flip(x, m & 2) + _flip(y, m & 1)
                blk = half(i, outs[i].at[src], 1 - c)
                pltpu.make_async_remote_copy(blk, blk, s_d2d.at[i * 3 + m - 1], r_d2d.at[i * 3 + m - 1],
                                             device_id=(x, y, c), device_id_type=MESH).wait_recv()
        for cp in sends:
            cp.wait_send()
        for cp in local:
            cp.wait()

    return pl.pallas_call(
        body, name="gather_weights",
        out_shape=tuple(jax.ShapeDtypeStruct((N_CHIPS,) + s.shape, s.dtype) for s in shards),
        in_specs=[HBM_SPEC] * n, out_specs=(HBM_SPEC,) * n,
        scratch_shapes=[pltpu.SemaphoreType.DMA((3 * n,)), pltpu.SemaphoreType.DMA((3 * n,)),
                        pltpu.SemaphoreType.DMA((3 * n,)), pltpu.SemaphoreType.DMA((3 * n,)),
                        pltpu.SemaphoreType.DMA((n,))],
        compiler_params=_params(),
    )(*shards)


def _sibling_halves(stacks):
    n = len(stacks)

    def body(*refs):
        ins, outs = refs[:n], refs[n:2 * n]
        s_sem, r_sem = refs[2 * n:]
        x, y, c = _mesh_pos()
        cps = []
        for i in range(n):
            rh = stacks[i].shape[1] // 2
            cp = pltpu.make_async_remote_copy(ins[i].at[:, pl.ds((1 - c) * rh, rh), :], outs[i],
                                              s_sem.at[i], r_sem.at[i],
                                              device_id=(x, y, 1 - c), device_id_type=MESH)
            cp.start()
            cps.append(cp)
        for cp in cps:
            cp.wait()

    return pl.pallas_call(
        body, name="rs_sibling_halves",
        out_shape=tuple(jax.ShapeDtypeStruct((s.shape[0], s.shape[1] // 2, s.shape[2]), s.dtype) for s in stacks),
        in_specs=[HBM_SPEC] * n, out_specs=(HBM_SPEC,) * n,
        scratch_shapes=[pltpu.SemaphoreType.DMA((n,)), pltpu.SemaphoreType.DMA((n,))],
        compiler_params=_params(),
    )(*stacks)


def _chip_exchange(qs):
    n = len(qs)

    def body(*refs):
        ins, outs = refs[:n], refs[n:2 * n]
        s_sem, r_sem, s_loc = refs[2 * n:]
        x, y, c = _mesh_pos()
        chip = 2 * x + y
        local, sends = [], []
        for i in range(n):
            cp = pltpu.make_async_copy(ins[i].at[chip], outs[i].at[chip], s_loc.at[i])
            cp.start()
            local.append(cp)
            for m in range(1, N_CHIPS):
                dst = 2 * _flip(x, m & 2) + _flip(y, m & 1)
                cp = pltpu.make_async_remote_copy(ins[i].at[dst], outs[i].at[chip],
                                                  s_sem.at[i * 3 + m - 1], r_sem.at[i * 3 + m - 1],
                                                  device_id=(_flip(x, m & 2), _flip(y, m & 1), c),
                                                  device_id_type=MESH)
                cp.start()
                sends.append(cp)
        for i in range(n):
            for m in range(1, N_CHIPS):
                src = 2 * _flip(x, m & 2) + _flip(y, m & 1)
                pltpu.make_async_remote_copy(ins[i].at[src], outs[i].at[src],
                                             s_sem.at[i * 3 + m - 1], r_sem.at[i * 3 + m - 1],
                                             device_id=(x, y, c), device_id_type=MESH).wait_recv()
        for cp in sends:
            cp.wait_send()
        for cp in local:
            cp.wait()

    return pl.pallas_call(
        body, name="rs_chip_exchange",
        out_shape=tuple(jax.ShapeDtypeStruct(q.shape, q.dtype) for q in qs),
        in_specs=[HBM_SPEC] * n, out_specs=(HBM_SPEC,) * n,
        scratch_shapes=[pltpu.SemaphoreType.DMA((3 * n,)), pltpu.SemaphoreType.DMA((3 * n,)),
                        pltpu.SemaphoreType.DMA((n,))],
        compiler_params=_params(),
    )(*qs)


def _sibling_join(tots):
    n = len(tots)

    def body(*refs):
        ins, outs = refs[:n], refs[n:2 * n]
        s_sem, r_sem, s_loc = refs[2 * n:]
        x, y, c = _mesh_pos()
        cps, local = [], []
        for i in range(n):
            rh = tots[i].shape[0]
            mine = outs[i].at[pl.ds(c * rh, rh), :]
            lc = pltpu.make_async_copy(ins[i], mine, s_loc.at[i])
            lc.start()
            local.append(lc)
            cp = pltpu.make_async_remote_copy(ins[i], mine, s_sem.at[i], r_sem.at[i],
                                              device_id=(x, y, 1 - c), device_id_type=MESH)
            cp.start()
            cps.append(cp)
        for i in range(n):
            rh = tots[i].shape[0]
            theirs = outs[i].at[pl.ds((1 - c) * rh, rh), :]
            pltpu.make_async_remote_copy(ins[i], theirs, s_sem.at[i], r_sem.at[i],
                                         device_id=(x, y, c), device_id_type=MESH).wait_recv()
        for cp in cps:
            cp.wait_send()
        for lc in local:
            lc.wait()

    return pl.pallas_call(
        body, name="rs_sibling_join",
        out_shape=tuple(jax.ShapeDtypeStruct((2 * t.shape[0], t.shape[1]), t.dtype) for t in tots),
        in_specs=[HBM_SPEC] * n, out_specs=(HBM_SPEC,) * n,
        scratch_shapes=[pltpu.SemaphoreType.DMA((n,)), pltpu.SemaphoreType.DMA((n,)),
                        pltpu.SemaphoreType.DMA((n,))],
        compiler_params=_params(),
    )(*tots)


def _small_allreduce(packed):
    r = packed.shape[0]

    def body(p_ref, sum_ref, all_ref, s_sem, r_sem):
        x, y, c = _mesh_pos()
        me = 4 * x + 2 * y + c
        all_ref[me] = p_ref[...]
        cps = []
        for m in range(1, N_DEV):
            peer = (_flip(x, m & 4), _flip(y, m & 2), _flip(c, m & 1))
            cp = pltpu.make_async_remote_copy(p_ref, all_ref.at[me], s_sem.at[m - 1], r_sem.at[m - 1],
                                              device_id=peer, device_id_type=MESH)
            cp.start()
            cps.append(cp)
        for m in range(1, N_DEV):
            src = 4 * _flip(x, m & 4) + 2 * _flip(y, m & 2) + _flip(c, m & 1)
            pltpu.make_async_remote_copy(p_ref, all_ref.at[src], s_sem.at[m - 1], r_sem.at[m - 1],
                                         device_id=(x, y, c), device_id_type=MESH).wait_recv()
        for cp in cps:
            cp.wait_send()
        sum_ref[...] = (((all_ref[0] + all_ref[1]) + (all_ref[2] + all_ref[3]))
                        + ((all_ref[4] + all_ref[5]) + (all_ref[6] + all_ref[7])))

    return pl.pallas_call(
        body, name="small_allreduce",
        out_shape=(jax.ShapeDtypeStruct((r, LANES), F32), jax.ShapeDtypeStruct((N_DEV, r, LANES), F32)),
        in_specs=[VMEM_SPEC], out_specs=(VMEM_SPEC, VMEM_SPEC),
        scratch_shapes=[pltpu.SemaphoreType.DMA((N_DEV - 1,)), pltpu.SemaphoreType.DMA((N_DEV - 1,))],
        compiler_params=_params(),
    )(packed)


def _row_block(rows, cap=256):
    rb = min(rows, cap)
    assert rows % rb == 0
    return rb


def _cast_bf16(arrs):
    def body(*refs):
        n = len(refs) // 2
        for i in range(n):
            refs[n + i][...] = refs[i][...].astype(BF16)

    return pl.pallas_call(
        body, name="cast_weights_bf16",
        out_shape=tuple(jax.ShapeDtypeStruct(a.shape, BF16) for a in arrs),
        in_specs=[VMEM_SPEC] * len(arrs), out_specs=(VMEM_SPEC,) * len(arrs),
        compiler_params=_params(),
    )(*arrs)


def _chip_sum_bf16(stack, recv, c_arr, name):
    _, r, cc = stack.shape
    rh = r // 2
    rb = _row_block(rh)
    nb = rh // rb

    def body(c_ref, a_ref, b_ref, o_ref):
        o_ref[...] = (a_ref[...] + b_ref[...]).astype(BF16)

    return pl.pallas_call(
        body, name=name,
        out_shape=jax.ShapeDtypeStruct((N_CHIPS, rh, cc), BF16),
        grid_spec=pltpu.PrefetchScalarGridSpec(
            num_scalar_prefetch=1, grid=(N_CHIPS, nb),
            in_specs=[pl.BlockSpec((None, rb, cc), lambda j, i, c_ref: (j, c_ref[0] * nb + i, 0)),
                      pl.BlockSpec((None, rb, cc), lambda j, i, c_ref: (j, i, 0))],
            out_specs=pl.BlockSpec((None, rb, cc), lambda j, i, c_ref: (j, i, 0))),
        compiler_params=_params(dimension_semantics=("arbitrary", "arbitrary")),
    )(c_arr, stack, recv)


def _sum_chips(rb4, name):
    _, rh, cc = rb4.shape
    rb = _row_block(rh)

    def body(a_ref, o_ref):
        a = a_ref[...].astype(F32)
        o_ref[...] = (a[0] + a[1]) + (a[2] + a[3])

    return pl.pallas_call(
        body, name=name,
        out_shape=jax.ShapeDtypeStruct((rh, cc), F32),
        grid=(rh // rb,),
        in_specs=[pl.BlockSpec((N_CHIPS, rb, cc), lambda i: (0, i, 0))],
        out_specs=pl.BlockSpec((rb, cc), lambda i: (i, 0)),
        compiler_params=_params(dimension_semantics=("arbitrary",)),
    )(rb4)


def _adamw_math(w, g, m, v):
    m = ADAM_B1 * m + (1.0 - ADAM_B1) * g
    v = ADAM_B2 * v + (1.0 - ADAM_B2) * jnp.square(g)
    m_hat = m / (1.0 - ADAM_B1 ** ADAM_STEP)
    v_hat = v / (1.0 - ADAM_B2 ** ADAM_STEP)
    delta = -ADAM_LR * (m_hat / (jnp.sqrt(v_hat) + ADAM_EPS) + ADAM_WD * w)
    return delta, m, v


def _adamw_big(g, w, m, v, name):
    r, cc = w.shape
    rb = _row_block(r)

    def body(g_ref, w_ref, m_ref, v_ref, d_ref, mo_ref, vo_ref):
        d, mn, vn = _adamw_math(w_ref[...], g_ref[...], m_ref[...], v_ref[...])
        d_ref[...] = d
        mo_ref[...] = mn
        vo_ref[...] = vn

    spec = pl.BlockSpec((rb, cc), lambda i: (i, 0))
    return pl.pallas_call(
        body, name=name,
        out_shape=(jax.ShapeDtypeStruct((r, cc), F32),) * 3,
        grid=(r // rb,), in_specs=[spec] * 4, out_specs=(spec,) * 3,
        compiler_params=_params(dimension_semantics=("arbitrary",)),
    )(g, w, m, v)


def _adamw_ada(cond_t, dmod, w, m, v):
    d, na = w.shape
    cb = min(na, 256)
    assert na % cb == 0

    def body(ct_ref, dm_ref, w_ref, m_ref, v_ref, g_ref, d_ref, mo_ref, vo_ref):
        g = jnp.dot(ct_ref[...], dm_ref[...], preferred_element_type=F32, precision=lax.Precision.HIGHEST)
        dl, mn, vn = _adamw_math(w_ref[...], g, m_ref[...], v_ref[...])
        g_ref[...] = g
        d_ref[...] = dl
        mo_ref[...] = mn
        vo_ref[...] = vn

    spec = pl.BlockSpec((d, cb), lambda i: (0, i))
    return pl.pallas_call(
        body, name="adamw_w_ada",
        out_shape=(jax.ShapeDtypeStruct((d, na), F32),) * 4,
        grid=(na // cb,),
        in_specs=[pl.BlockSpec((d, N_DEV), lambda i: (0, 0)), pl.BlockSpec((N_DEV, cb), lambda i: (0, i)),
                  spec, spec, spec],
        out_specs=(spec,) * 4,
        compiler_params=_params(dimension_semantics=("arbitrary",)),
    )(cond_t, dmod, w, m, v)


def _adamw_small(gs, ws, ms, vs):
    n = len(gs)

    def body(*refs):
        ins, outs = refs[:4 * n], refs[4 * n:]
        for i in range(n):
            d, mn, vn = _adamw_math(ins[n + i][...], ins[i][...], ins[2 * n + i][...], ins[3 * n + i][...])
            outs[i][...] = d
            outs[n + i][...] = mn
            outs[2 * n + i][...] = vn

    shapes = tuple(jax.ShapeDtypeStruct(w.shape, F32) for w in ws)
    return pl.pallas_call(
        body, name="adamw_small",
        out_shape=shapes * 3,
        in_specs=[VMEM_SPEC] * (4 * n), out_specs=(VMEM_SPEC,) * (3 * n),
        compiler_params=_params(),
    )(*gs, *ws, *ms, *vs)


def _layer_norm_fwd(r):
    mu = jnp.mean(r, axis=-1, keepdims=True)
    xc = r - mu
    var = jnp.mean(jnp.square(xc), axis=-1, keepdims=True)
    rstd = lax.rsqrt(var + LN_EPS)
    return xc * rstd, rstd


def _layer_norm_bwd(dy, xhat, rstd, gain):
    dxh = dy * gain
    m1 = jnp.mean(dxh, axis=-1, keepdims=True)
    m2 = jnp.mean(dxh * xhat, axis=-1, keepdims=True)
    return rstd * (dxh - m1 - xhat * m2)


def _inv_count(tile, tm, win):
    t = (tile * tm + lax.broadcasted_iota(jnp.int32, (tm, 1), 0) + 1).astype(F32)
    return 1.0 / jnp.minimum(t, float(win))


def _fwd_mix(x, mod, conv_w, w_in4, w_pool, pool_scale, w_out, ln_g, ln_b, tm):
    t_len, d = x.shape
    cw = w_in4.shape[2]
    dg = cw // N_GROUPS
    nt = t_len // tm

    def body(x_ref, mod_ref, cw_ref, win_ref, wp_ref, ps_ref, wout_ref, g_ref, b_ref,
             x1_ref, zs_ref, mix_ref, h1_ref, cat_ref, ubuf, vbuf):
        i = pl.program_id(0)

        @pl.when(i == 0)
        def _():
            ubuf[0:HALO, :] = jnp.zeros((HALO, cw), F32)
            vbuf[0:HALO, :] = jnp.zeros((HALO, cw), F32)

        xv = x_ref[...]
        sh1, sc1, g1 = mod_ref[0:1, 0:d], mod_ref[0:1, d:2 * d], mod_ref[0:1, 2 * d:3 * d]
        h1 = (xv * (1.0 + sc1) + sh1).astype(BF16)
        h1_ref[...] = h1
        zb = _dot(h1, win_ref[0])
        zc = _dot(h1, win_ref[1])
        zv = _dot(h1, win_ref[2])
        zp = _dot(h1, win_ref[3])
        u = zc * zv
        ubuf[HALO:HALO + tm, :] = u
        u1 = ubuf[pl.ds(HALO - 1, tm), :]
        u2 = ubuf[pl.ds(HALO - 2, tm), :]
        conv = cw_ref[0:1, :] * u2 + cw_ref[1:2, :] * u1 + cw_ref[2:3, :] * u
        ubuf[0:HALO, :] = ubuf[tm:tm + HALO, :]
        yc = zb * conv
        vbuf[HALO:HALO + tm, :] = zp
        ps, pms = [], []
        for gi, win in enumerate(POOL_WINDOWS):
            sl = slice(gi * dg, (gi + 1) * dg)
            acc = zp[:, sl]
            for s in range(1, win):
                acc = acc + vbuf[pl.ds(HALO - s, tm), sl]
            p_g = acc * _inv_count(i, tm, win) - zp[:, sl]
            ps.append(p_g)
            pms.append(_dot(p_g.astype(BF16), wp_ref[gi]))
        vbuf[0:HALO, :] = vbuf[tm:tm + HALO, :]
        pooled = jnp.concatenate(ps, axis=1)
        yp = jnp.concatenate(pms, axis=1) * ps_ref[...]
        cat = jnp.concatenate([yc, yp], axis=1).astype(BF16)
        cat_ref[...] = cat
        mix = _dot(cat, wout_ref[...])
        mix_ref[...] = mix
        xhat, _ = _layer_norm_fwd(DEEPNORM_ALPHA * xv + (1.0 + g1) * mix)
        x1_ref[...] = xhat * g_ref[...] + b_ref[...]
        zs_ref[:, 0 * cw:1 * cw] = zb
        zs_ref[:, 1 * cw:2 * cw] = zc
        zs_ref[:, 2 * cw:3 * cw] = zv
        zs_ref[:, 3 * cw:4 * cw] = conv
        zs_ref[:, 4 * cw:5 * cw] = pooled

    tile = lambda w: pl.BlockSpec((tm, w), lambda i: (i, 0))
    return pl.pallas_call(
        body, name="fwd_mix",
        out_shape=(jax.ShapeDtypeStruct((t_len, d), F32), jax.ShapeDtypeStruct((t_len, 5 * cw), F32),
                   jax.ShapeDtypeStruct((t_len, d), F32), jax.ShapeDtypeStruct((t_len, d), BF16),
                   jax.ShapeDtypeStruct((t_len, 2 * cw), BF16)),
        grid=(nt,),
        in_specs=[tile(d)] + [VMEM_SPEC] * 8,
        out_specs=(tile(d), tile(5 * cw), tile(d), tile(d), tile(2 * cw)),
        scratch_shapes=[pltpu.VMEM((tm + HALO, cw), F32), pltpu.VMEM((tm + HALO, cw), F32)],
        compiler_params=_params(dimension_semantics=("arbitrary",)),
    )(x, mod, conv_w, w_in4, w_pool, pool_scale, w_out, ln_g, ln_b)


def _mlp_fwd_bwd(x1, tgt, mod, w_mi4, w_mo4, ln_g, ln_b, tm):
    t_len, d = x1.shape
    fq = w_mi4.shape[2]
    nt = t_len // tm

    def body(x1_ref, tgt_ref, mod_ref, wi_ref, wo_ref, g_ref, b_ref,
             dx1_ref, da_ref, s_ref, df_ref, h2_ref, st_ref, relu_buf, acc):
        i = pl.program_id(0)

        @pl.when(i == 0)
        def _():
            acc[...] = jnp.zeros(acc.shape, F32)

        x1v = x1_ref[...]
        sh2, sc2, g2 = mod_ref[0:1, 3 * d:4 * d], mod_ref[0:1, 4 * d:5 * d], mod_ref[0:1, 5 * d:6 * d]
        h2 = (x1v * (1.0 + sc2) + sh2).astype(BF16)
        h2_ref[...] = h2
        f = jnp.zeros((tm, d), F32)
        for j in range(N_CHIPS):
            a = jnp.maximum(_dot(h2, wi_ref[j]), 0.0)
            relu_buf[j] = a
            s = (a * a).astype(BF16)
            s_ref[j] = s
            f = f + _dot(s, wo_ref[j])
        xhat, rstd = _layer_norm_fwd(DEEPNORM_ALPHA * x1v + (1.0 + g2) * f)
        gain = g_ref[...]
        err = xhat * gain + b_ref[...] - tgt_ref[...]
        dy = err * (1.0 / d)
        dr2 = _layer_norm_bwd(dy, xhat, rstd, gain)
        df = ((1.0 + g2) * dr2).astype(BF16)
        df_ref[...] = df
        dh2 = jnp.zeros((tm, d), F32)
        for j in range(N_CHIPS):
            ds = _dot_nt(df, wo_ref[j])
            da = (ds * (2.0 * relu_buf[j])).astype(BF16)
            da_ref[j] = da
            dh2 = dh2 + _dot_nt(da, wi_ref[j])
        dx1_ref[...] = DEEPNORM_ALPHA * dr2 + dh2 * (1.0 + sc2)
        acc[0] += _fold8(dy * xhat)
        acc[1] += _fold8(dy)
        acc[2] += _fold8(dr2 * f)
        acc[3] += _fold8(dh2 * x1v)
        acc[4] += _fold8(dh2)
        acc[5] += _fold8(err * err)

        @pl.when(i == nt - 1)
        def _():
            for k in range(5):
                st_ref[k:k + 1, :] = jnp.sum(acc[k], axis=0, keepdims=True)
            loss = jnp.sum(acc[5]) * (0.5 / d)
            st_ref[5:6, :] = jnp.broadcast_to(loss, (1, d))
            st_ref[6:8, :] = jnp.zeros((2, d), F32)

    tile = lambda w: pl.BlockSpec((tm, w), lambda i: (i, 0))
    tile4 = pl.BlockSpec((N_CHIPS, tm, fq), lambda i: (0, i, 0))
    return pl.pallas_call(
        body, name="mlp_fwd_bwd",
        out_shape=(jax.ShapeDtypeStruct((t_len, d), F32),
                   jax.ShapeDtypeStruct((N_CHIPS, t_len, fq), BF16), jax.ShapeDtypeStruct((N_CHIPS, t_len, fq), BF16),
                   jax.ShapeDtypeStruct((t_len, d), BF16), jax.ShapeDtypeStruct((t_len, d), BF16),
                   jax.ShapeDtypeStruct((SUBLANES, d), F32)),
        grid=(nt,),
        in_specs=[tile(d), tile(d)] + [VMEM_SPEC] * 5,
        out_specs=(tile(d), tile4, tile4, tile(d), tile(d), pl.BlockSpec((SUBLANES, d), lambda i: (0, 0))),
        scratch_shapes=[pltpu.VMEM((N_CHIPS, tm, fq), F32), pltpu.VMEM((6, SUBLANES, d), F32)],
        compiler_params=_params(dimension_semantics=("arbitrary",)),
    )(x1, tgt, mod, w_mi4, w_mo4, ln_g, ln_b)


def _mlp_wgrad(h2, da4, s4, df, tt):
    t_len, d = h2.shape
    fq = da4.shape[2]
    nt = t_len // tt

    def body(h2_ref, da_ref, s_ref, df_ref, dwi_ref, dwo_ref):
        @pl.when(pl.program_id(1) == 0)
        def _():
            dwi_ref[...] = jnp.zeros((d, fq), F32)
            dwo_ref[...] = jnp.zeros((fq, d), F32)

        dwi_ref[...] += _dot_tn(h2_ref[...], da_ref[...])
        dwo_ref[...] += _dot_tn(s_ref[...], df_ref[...])

    return pl.pallas_call(
        body, name="mlp_wgrad",
        out_shape=(jax.ShapeDtypeStruct((N_CHIPS, d, fq), F32), jax.ShapeDtypeStruct((N_CHIPS, fq, d), F32)),
        grid=(N_CHIPS, nt),
        in_specs=[pl.BlockSpec((tt, d), lambda j, t: (t, 0)), pl.BlockSpec((None, tt, fq), lambda j, t: (j, t, 0)),
                  pl.BlockSpec((None, tt, fq), lambda j, t: (j, t, 0)), pl.BlockSpec((tt, d), lambda j, t: (t, 0))],
        out_specs=(pl.BlockSpec((None, d, fq), lambda j, t: (j, 0, 0)),
                   pl.BlockSpec((None, fq, d), lambda j, t: (j, 0, 0))),
        compiler_params=_params(dimension_semantics=("arbitrary", "arbitrary")),
    )(h2, da4, s4, df)


def _bwd_mix(x, mix, dx1, zs, mod, conv_w, w_in4, w_pool, pool_scale, w_out, ln_g, tm):
    t_len, d = x.shape
    cw = w_in4.shape[2]
    dg = cw // N_GROUPS
    nt = t_len // tm

    def body(x_ref, mix_ref, dx1_ref, zs_ref, mod_ref, cw_ref, win_ref, wp_ref, ps_ref, wout_ref, g_ref,
             gx_ref, dz_ref, dmix_ref, sd_ref, sc_ref, dwp_ref, dbuf, qbuf, acc_d, acc_c, acc_p):
        i = pl.program_id(0)
        tile_idx = nt - 1 - i

        @pl.when(i == 0)
        def _():
            dbuf[tm:tm + HALO, :] = jnp.zeros((HALO, cw), F32)
            qbuf[tm:tm + HALO, :] = jnp.zeros((HALO, cw), F32)
            acc_d[...] = jnp.zeros(acc_d.shape, F32)
            acc_c[...] = jnp.zeros(acc_c.shape, F32)
            acc_p[...] = jnp.zeros(acc_p.shape, F32)

        xv, mixv, dx1v = x_ref[...], mix_ref[...], dx1_ref[...]
        sc1, g1 = mod_ref[0:1, d:2 * d], mod_ref[0:1, 2 * d:3 * d]
        xhat, rstd = _layer_norm_fwd(DEEPNORM_ALPHA * xv + (1.0 + g1) * mixv)
        dr1 = _layer_norm_bwd(dx1v, xhat, rstd, g_ref[...])
        dmix = ((1.0 + g1) * dr1).astype(BF16)
        dmix_ref[...] = dmix
        dcat = _dot_nt(dmix, wout_ref[...])
        dyc, dyp = dcat[:, 0:cw], dcat[:, cw:2 * cw]
        zb, zc, zv = zs_ref[:, 0:cw], zs_ref[:, cw:2 * cw], zs_ref[:, 2 * cw:3 * cw]
        conv, pooled = zs_ref[:, 3 * cw:4 * cw], zs_ref[:, 4 * cw:5 * cw]
        dzb = dyc * conv
        dcv = dyc * zb
        dbuf[0:tm, :] = dcv
        d1 = dbuf[pl.ds(1, tm), :]
        d2 = dbuf[pl.ds(2, tm), :]
        du = cw_ref[2:3, :] * dcv + cw_ref[1:2, :] * d1 + cw_ref[0:1, :] * d2
        dbuf[tm:tm + HALO, :] = dbuf[0:HALO, :]
        u = zc * zv
        acc_c[0] += _fold8(u * d2)
        acc_c[1] += _fold8(u * d1)
        acc_c[2] += _fold8(u * dcv)
        dzc = du * zv
        dzv = du * zc
        dpw = dyp * ps_ref[...]
        dps, pms = [], []
        for gi, win in enumerate(POOL_WINDOWS):
            sl = slice(gi * dg, (gi + 1) * dg)
            p_g = pooled[:, sl].astype(BF16)
            dpw_g = dpw[:, sl].astype(BF16)
            pms.append(_dot(p_g, wp_ref[gi]))
            acc_p[gi] += _dot_tn(p_g, dpw_g)
            dp_g = _dot_nt(dpw_g, wp_ref[gi])
            dps.append(dp_g)
            qbuf[0:tm, sl] = dp_g * _inv_count(tile_idx, tm, win)
        acc_c[3] += _fold8(dyp * jnp.concatenate(pms, axis=1))
        dzps = []
        for gi, win in enumerate(POOL_WINDOWS):
            sl = slice(gi * dg, (gi + 1) * dg)
            acc = qbuf[0:tm, sl]
            for s in range(1, win):
                acc = acc + qbuf[pl.ds(s, tm), sl]
            dzps.append(acc - dps[gi])
        qbuf[tm:tm + HALO, :] = qbuf[0:HALO, :]
        dz = [dzb.astype(BF16), dzc.astype(BF16), dzv.astype(BF16), jnp.concatenate(dzps, axis=1).astype(BF16)]
        dh1 = jnp.zeros((tm, d), F32)
        for j in range(N_CHIPS):
            dz_ref[j] = dz[j]
            dh1 = dh1 + _dot_nt(dz[j], win_ref[j])
        gx_ref[...] = DEEPNORM_ALPHA * dr1 + dh1 * (1.0 + sc1)
        acc_d[0] += _fold8(dx1v * xhat)
        acc_d[1] += _fold8(dx1v)
        acc_d[2] += _fold8(dr1 * mixv)
        acc_d[3] += _fold8(dh1 * xv)
        acc_d[4] += _fold8(dh1)

        @pl.when(i == nt - 1)
        def _():
            for k in range(5):
                sd_ref[k:k + 1, :] = jnp.sum(acc_d[k], axis=0, keepdims=True)
            sd_ref[5:8, :] = jnp.zeros((3, d), F32)
            for k in range(4):
                sc_ref[k:k + 1, :] = jnp.sum(acc_c[k], axis=0, keepdims=True)
            sc_ref[4:8, :] = jnp.zeros((4, cw), F32)
            dwp_ref[...] = acc_p[...]

    rtile = lambda w: pl.BlockSpec((tm, w), lambda i: (nt - 1 - i, 0))
    whole = lambda shape: pl.BlockSpec(shape, lambda i: tuple(0 for _ in shape))
    return pl.pallas_call(
        body, name="bwd_mix",
        out_shape=(jax.ShapeDtypeStruct((t_len, d), F32), jax.ShapeDtypeStruct((N_CHIPS, t_len, cw), BF16),
                   jax.ShapeDtypeStruct((t_len, d), BF16), jax.ShapeDtypeStruct((SUBLANES, d), F32),
                   jax.ShapeDtypeStruct((SUBLANES, cw), F32), jax.ShapeDtypeStruct((N_GROUPS, dg, dg), F32)),
        grid=(nt,),
        in_specs=[rtile(d), rtile(d), rtile(d), rtile(5 * cw)] + [VMEM_SPEC] * 7,
        out_specs=(rtile(d), pl.BlockSpec((N_CHIPS, tm, cw), lambda i: (0, nt - 1 - i, 0)), rtile(d),
                   whole((SUBLANES, d)), whole((SUBLANES, cw)), whole((N_GROUPS, dg, dg))),
        scratch_shapes=[pltpu.VMEM((tm + HALO, cw), F32), pltpu.VMEM((tm + HALO, cw), F32),
                        pltpu.VMEM((5, SUBLANES, d), F32), pltpu.VMEM((4, SUBLANES, cw), F32),
                        pltpu.VMEM((N_GROUPS, dg, dg), F32)],
        compiler_params=_params(dimension_semantics=("arbitrary",)),
    )(x, mix, dx1, zs, mod, conv_w, w_in4, w_pool, pool_scale, w_out, ln_g)


def _mix_wgrad(h1, dz4, cat, dmix, tt):
    t_len, d = h1.shape
    cw = dz4.shape[2]
    nt = t_len // tt

    def body(h1_ref, dz_ref, cat_ref, dmix_ref, dwin_ref, dwout_ref):
        @pl.when(pl.program_id(0) == 0)
        def _():
            dwin_ref[...] = jnp.zeros(dwin_ref.shape, F32)
            dwout_ref[...] = jnp.zeros(dwout_ref.shape, F32)

        h1v = h1_ref[...]
        for j in range(N_CHIPS):
            dwin_ref[j] += _dot_tn(h1v, dz_ref[j])
        dwout_ref[...] += _dot_tn(cat_ref[...], dmix_ref[...])

    return pl.pallas_call(
        body, name="mix_wgrad",
        out_shape=(jax.ShapeDtypeStruct((N_CHIPS, d, cw), F32), jax.ShapeDtypeStruct((2 * cw, d), F32)),
        grid=(nt,),
        in_specs=[pl.BlockSpec((tt, d), lambda t: (t, 0)), pl.BlockSpec((N_CHIPS, tt, cw), lambda t: (0, t, 0)),
                  pl.BlockSpec((tt, 2 * cw), lambda t: (t, 0)), pl.BlockSpec((tt, d), lambda t: (t, 0))],
        out_specs=(pl.BlockSpec((N_CHIPS, d, cw), lambda t: (0, 0, 0)), pl.BlockSpec((2 * cw, d), lambda t: (0, 0))),
        compiler_params=_params(dimension_semantics=("arbitrary",)),
    )(h1, dz4, cat, dmix)


def _rows128(v):
    v = v.reshape(-1, LANES)
    pad = (-v.shape[0]) % SUBLANES
    if pad:
        v = jnp.concatenate([v, jnp.zeros((pad, LANES), v.dtype)], axis=0)
    return v


def kernel(x, c, w_ada, b_ada, w_in, conv_w, w_pool, pool_scale, w_out, ln1_g, ln1_b, w_mlp_in, w_mlp_out, ln2_g, ln2_b, loss_target, m_w_ada, m_b_ada, m_w_in, m_conv_w, m_w_pool, m_pool_scale, m_w_out, m_ln1_g, m_ln1_b, m_w_mlp_in, m_w_mlp_out, m_ln2_g, m_ln2_b, v_w_ada, v_b_ada, v_w_in, v_conv_w, v_w_pool, v_pool_scale, v_w_out, v_ln1_g, v_ln1_b, v_w_mlp_in, v_w_mlp_out, v_ln2_g, v_ln2_b):
    t_len, d = x.shape[1], x.shape[2]
    cw = w_in.shape[2]
    cq = conv_w.shape[2]
    dg = w_pool.shape[2]
    assert cw == N_GROUPS * dg and cq * N_CHIPS == cw and LANES % cq == 0
    tm = min(256, t_len)
    tt = min(512, t_len)
    chip = 2 * lax.axis_index("x") + lax.axis_index("y")
    c_arr = jnp.reshape(lax.axis_index("c"), (1,)).astype(jnp.int32)

    x2, tgt = x[0], loss_target[0]
    big_w = [w_in[0], w_out[0], w_mlp_in[0], w_mlp_out[0]]
    big_m = [m_w_in[0], m_w_out[0], m_w_mlp_in[0], m_w_mlp_out[0]]
    big_v = [v_w_in[0], v_w_out[0], v_w_mlp_in[0], v_w_mlp_out[0]]

    mod, cond_all, conv_full = _mod_exchange(c, w_ada[0], b_ada, conv_w[0])
    w_in4, w_out4, w_mi4, w_mo4 = _gather_weights(_cast_bf16(big_w))
    w_out_full = w_out4.reshape(2 * cw, d)
    w_pool_bf = w_pool[0].astype(BF16)

    x1, zs, mix, h1, cat = _fwd_mix(x2, mod, conv_full, w_in4, w_pool_bf, pool_scale, w_out_full, ln1_g, ln1_b, tm)
    dx1, da4, s4, df, h2, st2 = _mlp_fwd_bwd(x1, tgt, mod, w_mi4, w_mo4, ln2_g, ln2_b, tm)
    dw_mi4, dw_mo4 = _mlp_wgrad(h2, da4, s4, df, tt)
    grad_x, dz4, dmix, st1, stc, dw_pool = _bwd_mix(x2, mix, dx1, zs, mod, conv_full, w_in4, w_pool_bf,
                                                    pool_scale, w_out_full, ln1_g, tm)
    dw_in4, dw_out = _mix_wgrad(h1, dz4, cat, dmix, tt)

    stacks = [dw_in4, dw_out.reshape(N_CHIPS, (2 * cw) // N_CHIPS, d), dw_mi4, dw_mo4]
    names = ["w_in", "w_out", "w_mlp_in", "w_mlp_out"]
    recv = _sibling_halves(stacks)
    qs = [_chip_sum_bf16(s, r, c_arr, "rs_chip_sum_" + n) for s, r, n in zip(stacks, recv, names)]
    got = _chip_exchange(qs)
    tots = [_sum_chips(g, "rs_total_" + n) for g, n in zip(got, names)]
    big_g = _sibling_join(tots)
    big_out = [_adamw_big(g, w, m, v, "adamw_" + n) for g, w, m, v, n in zip(big_g, big_w, big_m, big_v, names)]

    conv_part = jnp.transpose(stc[0:3].reshape(3, N_CHIPS, cq), (1, 0, 2))
    conv_part = jnp.pad(conv_part, ((0, 0), (0, SUBLANES - 3), (0, LANES - cq)))
    dmod = jnp.concatenate([st1[4], st1[3], st1[2], st2[4], st2[3], st2[2]])
    pieces = [st2[5, 0:LANES], st1[0], st1[1], st2[0], st2[1], stc[3], conv_part, dw_pool, dmod]
    rows = [_rows128(p) for p in pieces]
    offs = [0]
    for r in rows:
        offs.append(offs[-1] + r.shape[0])
    total, gathered = _small_allreduce(jnp.concatenate(rows, axis=0))

    def sec(k, shape):
        n = 1
        for s in shape:
            n *= s
        return total[offs[k]:offs[k] + n // LANES].reshape(shape)

    loss = total[0, 0]
    g_ln1_g, g_ln1_b, g_ln2_g, g_ln2_b = sec(1, (1, d)), sec(2, (1, d)), sec(3, (1, d)), sec(4, (1, d))
    g_pool_scale = sec(5, (1, cw))
    conv_blocks = total[offs[6]:offs[7]].reshape(N_CHIPS, SUBLANES, LANES)
    g_conv = lax.dynamic_index_in_dim(conv_blocks, chip, axis=0, keepdims=False)[0:3, 0:cq]
    g_w_pool = sec(7, (N_GROUPS, dg, dg))
    g_b_ada = sec(8, (1, 6 * d))
    dmod_all = gathered[:, offs[8]:offs[9], :].reshape(N_DEV, 6 * d)
    na = w_ada.shape[2]
    dmod_mine = lax.dynamic_slice_in_dim(dmod_all, chip * na, na, axis=1)

    g_ada, d_ada, nm_ada, nv_ada = _adamw_ada(cond_all.T, dmod_mine, w_ada[0], m_w_ada[0], v_w_ada[0])

    small_g = [g_b_ada, g_conv, g_w_pool.reshape(-1, dg), g_pool_scale, g_ln1_g, g_ln1_b, g_ln2_g, g_ln2_b]
    small_w = [b_ada, conv_w[0], w_pool.reshape(-1, dg), pool_scale, ln1_g, ln1_b, ln2_g, ln2_b]
    small_m = [m_b_ada, m_conv_w[0], m_w_pool.reshape(-1, dg), m_pool_scale, m_ln1_g, m_ln1_b, m_ln2_g, m_ln2_b]
    small_v = [v_b_ada, v_conv_w[0], v_w_pool.reshape(-1, dg), v_pool_scale, v_ln1_g, v_ln1_b, v_ln2_g, v_ln2_b]
    sm = _adamw_small(small_g, small_w, small_m, small_v)
    ns = len(small_g)
    s_delta, s_m, s_v = sm[:ns], sm[ns:2 * ns], sm[2 * ns:]

    def assemble(ada, small, big):
        return [ada[None], small[0], big[0][None], small[1][None], small[2].reshape(w_pool.shape), small[3],
                big[1][None], small[4], small[5], big[2][None], big[3][None], small[6], small[7]]

    grads = assemble(g_ada, small_g, big_g)
    deltas = assemble(d_ada, s_delta, [o[0] for o in big_out])
    new_m = assemble(nm_ada, s_m, [o[1] for o in big_out])
    new_v = assemble(nv_ada, s_v, [o[2] for o in big_out])
    return (loss, grad_x[None], *grads, *deltas, *new_m, *new_v)
```

```python
import jax
import jax.numpy as jnp
from jax import lax
from jax.experimental import pallas as pl
from jax.experimental.pallas import tpu as pltpu

F32 = jnp.float32
BF16 = jnp.bfloat16
MESH = pl.DeviceIdType.MESH

LN_EPS = 1e-5
DEEPNORM_ALPHA = 2.0 ** 0.25
POOL_WINDOWS = (2, 4, 8, 16)
N_GROUPS = len(POOL_WINDOWS)
HALO = 16
N_CHIPS = 4
N_DEV = 8
LANES = 128
SUBLANES = 8
VMEM_LIMIT = 56 * 1024 * 1024
TOKEN_TILE = 256
WGRAD_TILE = 512

ADAM_LR = 0.001
ADAM_B1 = 0.9
ADAM_B2 = 0.999
ADAM_EPS = 1e-08
ADAM_WD = 0.01
ADAM_STEP = 10

VMEM_SPEC = pl.BlockSpec(memory_space=pltpu.VMEM)
HBM_SPEC = pl.BlockSpec(memory_space=pltpu.HBM)


def _dot(a, b):
    return jnp.dot(a, b, preferred_element_type=F32)


def _dot_nt(a, b):
    return lax.dot_general(a, b, (((1,), (1,)), ((), ())), preferred_element_type=F32)


def _dot_tn(a, b):
    return lax.dot_general(a, b, (((0,), (0,)), ((), ())), preferred_element_type=F32)


def _fold8(v):
    r, n = v.shape
    return jnp.sum(v.reshape(r // SUBLANES, SUBLANES, n), axis=0)


def _mesh_pos():
    return lax.axis_index("x"), lax.axis_index("y"), lax.axis_index("c")


def _flip(v, bit):
    return 1 - v if bit else v


def _params(**kw):
    return pltpu.CompilerParams(vmem_limit_bytes=VMEM_LIMIT, **kw)


def _dma_sems(n):
    return pltpu.SemaphoreType.DMA((n,))


def _stack_gather(stacks, rows, s_ici, r_ici, s_d2d, r_d2d):
    x, y, c = _mesh_pos()
    chip = 2 * x + y
    pairs = [(i, m) for i in range(len(stacks)) for m in range(1, N_CHIPS)]

    def blk(i, slot, which):
        rh = rows[i] // 2
        return stacks[i].at[slot].at[pl.ds(which * rh, rh), :]

    def other(m):
        return 2 * _flip(x, m & 2) + _flip(y, m & 1)

    def ici(i, m, slot, to):
        k = i * (N_CHIPS - 1) + m - 1
        return pltpu.make_async_remote_copy(blk(i, slot, c), blk(i, slot, c), s_ici.at[k], r_ici.at[k],
                                            device_id=to, device_id_type=MESH)

    def d2d(i, m, which, to):
        k = i * (N_CHIPS - 1) + m - 1
        return pltpu.make_async_remote_copy(blk(i, other(m), which), blk(i, other(m), which),
                                            s_d2d.at[k], r_d2d.at[k], device_id=to, device_id_type=MESH)

    def start():
        for i, m in pairs:
            ici(i, m, chip, (_flip(x, m & 2), _flip(y, m & 1), c)).start()

    def forward():
        for i, m in pairs:
            ici(i, m, other(m), (x, y, c)).wait_recv()
            d2d(i, m, c, (x, y, 1 - c)).start()

    def finish():
        for i, m in pairs:
            d2d(i, m, 1 - c, (x, y, c)).wait_recv()
        for i, m in pairs:
            ici(i, m, chip, (x, y, c)).wait_send()
            d2d(i, m, c, (x, y, c)).wait_send()

    return start, forward, finish


def _chunk_exchange(qs, gots, s_sem, r_sem):
    x, y, c = _mesh_pos()
    chip = 2 * x + y
    pairs = [(i, m) for i in range(len(qs)) for m in range(1, N_CHIPS)]

    def other(m):
        return 2 * _flip(x, m & 2) + _flip(y, m & 1)

    def send(i, m, to):
        k = i * (N_CHIPS - 1) + m - 1
        return pltpu.make_async_remote_copy(qs[i].at[other(m)], gots[i].at[chip], s_sem.at[k], r_sem.at[k],
                                            device_id=to, device_id_type=MESH)

    def arrival(i, m):
        k = i * (N_CHIPS - 1) + m - 1
        return pltpu.make_async_remote_copy(qs[i].at[other(m)], gots[i].at[other(m)], s_sem.at[k], r_sem.at[k],
                                            device_id=(x, y, c), device_id_type=MESH)

    def start():
        for i, m in pairs:
            send(i, m, (_flip(x, m & 2), _flip(y, m & 1), c)).start()

    def finish():
        for i, m in pairs:
            arrival(i, m).wait_recv()
        for i, m in pairs:
            send(i, m, (x, y, c)).wait_send()

    return start, finish


def _mod_exchange(c_row, w_ada_s, b_ada, conv_w_s):
    d = c_row.shape[1]
    na = w_ada_s.shape[1]
    cq = conv_w_s.shape[1]

    def body(c_ref, w_ref, b_ref, cw_ref, mod_ref, cond_ref, conv_ref,
             csend, cbuf, mbuf, cvsend, cvbuf, s1, r1, s2, r2, s3, r3):
        x, y, c = _mesh_pos()
        me = 4 * x + 2 * y + c
        chip = 2 * x + y
        csend[...] = jnp.broadcast_to(c_ref[...], (SUBLANES, d))
        cbuf[me] = csend[...]
        first = []
        for m in range(1, N_DEV):
            peer = (_flip(x, m & 4), _flip(y, m & 2), _flip(c, m & 1))
            cp = pltpu.make_async_remote_copy(csend, cbuf.at[me], s1.at[m - 1], r1.at[m - 1],
                                              device_id=peer, device_id_type=MESH)
            cp.start()
            first.append(cp)
        cvsend[...] = jnp.zeros((SUBLANES, cq), F32)
        cvsend[0:3, :] = cw_ref[...]
        cvbuf[chip] = cvsend[...]
        for m in range(1, N_DEV):
            src = 4 * _flip(x, m & 4) + 2 * _flip(y, m & 2) + _flip(c, m & 1)
            pltpu.make_async_remote_copy(csend, cbuf.at[src], s1.at[m - 1], r1.at[m - 1],
                                         device_id=(x, y, c), device_id_type=MESH).wait_recv()
        rows = lax.broadcasted_iota(jnp.int32, (SUBLANES, d), 0)
        call = jnp.zeros((SUBLANES, d), F32)
        for b in range(N_DEV):
            call = jnp.where(rows == b, cbuf[b], call)
        cond = call * jax.nn.sigmoid(call)
        cond_ref[...] = cond
        part = jnp.dot(cond, w_ref[...], preferred_element_type=F32, precision=lax.Precision.HIGHEST)
        mbuf[chip] = part
        second = []
        for m in range(1, N_CHIPS):
            peer = (_flip(x, m & 2), _flip(y, m & 1), c)
            cp = pltpu.make_async_remote_copy(mbuf.at[chip], mbuf.at[chip], s2.at[m - 1], r2.at[m - 1],
                                              device_id=peer, device_id_type=MESH)
            cp.start()
            second.append(cp)
            cp = pltpu.make_async_remote_copy(cvsend, cvbuf.at[chip], s3.at[m - 1], r3.at[m - 1],
                                              device_id=peer, device_id_type=MESH)
            cp.start()
            second.append(cp)
        for m in range(1, N_CHIPS):
            src = 2 * _flip(x, m & 2) + _flip(y, m & 1)
            pltpu.make_async_remote_copy(mbuf.at[src], mbuf.at[src], s2.at[m - 1], r2.at[m - 1],
                                         device_id=(x, y, c), device_id_type=MESH).wait_recv()
            pltpu.make_async_remote_copy(cvsend, cvbuf.at[src], s3.at[m - 1], r3.at[m - 1],
                                         device_id=(x, y, c), device_id_type=MESH).wait_recv()
        for cp in first + second:
            cp.wait_send()
        rows_n = lax.broadcasted_iota(jnp.int32, (SUBLANES, na), 0)
        for k in range(N_CHIPS):
            mine = jnp.sum(jnp.where(rows_n == me, mbuf[k], 0.0), axis=0, keepdims=True)
            mod_ref[:, k * na:(k + 1) * na] = jnp.broadcast_to(mine + b_ref[:, k * na:(k + 1) * na], (SUBLANES, na))
            conv_ref[:, k * cq:(k + 1) * cq] = cvbuf[k]

    return pl.pallas_call(
        body, name="mod_exchange",
        out_shape=(jax.ShapeDtypeStruct((SUBLANES, N_CHIPS * na), F32),
                   jax.ShapeDtypeStruct((SUBLANES, d), F32),
                   jax.ShapeDtypeStruct((SUBLANES, N_CHIPS * cq), F32)),
        in_specs=[VMEM_SPEC] * 4, out_specs=(VMEM_SPEC,) * 3,
        scratch_shapes=[
            pltpu.VMEM((SUBLANES, d), F32), pltpu.VMEM((N_DEV, SUBLANES, d), F32),
            pltpu.VMEM((N_CHIPS, SUBLANES, na), F32),
            pltpu.VMEM((SUBLANES, cq), F32), pltpu.VMEM((N_CHIPS, SUBLANES, cq), F32),
            _dma_sems(N_DEV - 1), _dma_sems(N_DEV - 1),
            _dma_sems(N_CHIPS - 1), _dma_sems(N_CHIPS - 1), _dma_sems(N_CHIPS - 1), _dma_sems(N_CHIPS - 1),
        ],
        compiler_params=_params(),
    )(c_row, w_ada_s, b_ada, conv_w_s)


def _gather_first(shards, n_now):
    n = len(shards)
    n_cp = n_now * (N_CHIPS - 1)

    def body(*refs):
        ins, outs, bufs = refs[:n], refs[n:2 * n], refs[2 * n:3 * n]
        s_ici, r_ici, s_d2d, r_d2d, s_loc = refs[3 * n:]
        x, y, _ = _mesh_pos()
        chip = 2 * x + y
        local = []
        for i in range(n):
            bufs[i][...] = ins[i][...].astype(BF16)
            cp = pltpu.make_async_copy(bufs[i], outs[i].at[chip], s_loc.at[i])
            cp.start()
            local.append(cp)
        for cp in local:
            cp.wait()
        start, forward, finish = _stack_gather(outs[:n_now], [s.shape[0] for s in shards[:n_now]],
                                               s_ici, r_ici, s_d2d, r_d2d)
        start()
        forward()
        finish()

    return pl.pallas_call(
        body, name="gather_first",
        out_shape=tuple(jax.ShapeDtypeStruct((N_CHIPS,) + s.shape, BF16) for s in shards),
        in_specs=[VMEM_SPEC] * n, out_specs=(HBM_SPEC,) * n,
        scratch_shapes=[pltpu.VMEM(s.shape, BF16) for s in shards]
        + [_dma_sems(n_cp), _dma_sems(n_cp), _dma_sems(n_cp), _dma_sems(n_cp), _dma_sems(n)],
        compiler_params=_params(),
    )(*shards)


def _sibling_halves(stacks, tag):
    n = len(stacks)

    def body(*refs):
        ins, outs = refs[:n], refs[n:2 * n]
        s_sem, r_sem = refs[2 * n:]
        x, y, c = _mesh_pos()
        cps = []
        for i in range(n):
            rh = stacks[i].shape[1] // 2
            cp = pltpu.make_async_remote_copy(ins[i].at[:, pl.ds((1 - c) * rh, rh), :], outs[i],
                                              s_sem.at[i], r_sem.at[i],
                                              device_id=(x, y, 1 - c), device_id_type=MESH)
            cp.start()
            cps.append(cp)
        for cp in cps:
            cp.wait()

    return pl.pallas_call(
        body, name="rs_sibling_halves_" + tag,
        out_shape=tuple(jax.ShapeDtypeStruct((s.shape[0], s.shape[1] // 2, s.shape[2]), s.dtype) for s in stacks),
        in_specs=[HBM_SPEC] * n, out_specs=(HBM_SPEC,) * n,
        scratch_shapes=[_dma_sems(n), _dma_sems(n)],
        compiler_params=_params(),
    )(*stacks)


def _chip_exchange(qs, gots):
    n = len(qs)
    n_cp = n * (N_CHIPS - 1)

    def body(*refs):
        q_refs, outs = refs[:n], refs[2 * n:3 * n]
        s_sem, r_sem = refs[3 * n:]
        start, finish = _chunk_exchange(q_refs, outs, s_sem, r_sem)
        start()
        finish()

    return pl.pallas_call(
        body, name="rs_chip_exchange",
        out_shape=tuple(jax.ShapeDtypeStruct(g.shape, g.dtype) for g in gots),
        in_specs=[HBM_SPEC] * (2 * n), out_specs=(HBM_SPEC,) * n,
        input_output_aliases={n + i: i for i in range(n)},
        scratch_shapes=[_dma_sems(n_cp), _dma_sems(n_cp)],
        compiler_params=_params(),
    )(*qs, *gots)


def _sibling_join(gs):
    n = len(gs)

    def body(*refs):
        outs = refs[n:2 * n]
        s_sem, r_sem = refs[2 * n:]
        x, y, c = _mesh_pos()
        cps = []
        for i in range(n):
            rh = gs[i].shape[0] // 2
            mine = outs[i].at[pl.ds(c * rh, rh), :]
            cp = pltpu.make_async_remote_copy(mine, mine, s_sem.at[i], r_sem.at[i],
                                              device_id=(x, y, 1 - c), device_id_type=MESH)
            cp.start()
            cps.append(cp)
        for i in range(n):
            rh = gs[i].shape[0] // 2
            theirs = outs[i].at[pl.ds((1 - c) * rh, rh), :]
            pltpu.make_async_remote_copy(theirs, theirs, s_sem.at[i], r_sem.at[i],
                                         device_id=(x, y, c), device_id_type=MESH).wait_recv()
        for cp in cps:
            cp.wait_send()

    return pl.pallas_call(
        body, name="rs_sibling_join",
        out_shape=tuple(jax.ShapeDtypeStruct(g.shape, g.dtype) for g in gs),
        in_specs=[HBM_SPEC] * n, out_specs=(HBM_SPEC,) * n,
        input_output_aliases={i: i for i in range(n)},
        scratch_shapes=[_dma_sems(n), _dma_sems(n)],
        compiler_params=_params(),
    )(*gs)


def _small_allreduce(packed):
    r = packed.shape[0]

    def body(p_ref, sum_ref, all_ref, s_sem, r_sem):
        x, y, c = _mesh_pos()
        me = 4 * x + 2 * y + c
        all_ref[me] = p_ref[...]
        cps = []
        for m in range(1, N_DEV):
            peer = (_flip(x, m & 4), _flip(y, m & 2), _flip(c, m & 1))
            cp = pltpu.make_async_remote_copy(p_ref, all_ref.at[me], s_sem.at[m - 1], r_sem.at[m - 1],
                                              device_id=peer, device_id_type=MESH)
            cp.start()
            cps.append(cp)
        for m in range(1, N_DEV):
            src = 4 * _flip(x, m & 4) + 2 * _flip(y, m & 2) + _flip(c, m & 1)
            pltpu.make_async_remote_copy(p_ref, all_ref.at[src], s_sem.at[m - 1], r_sem.at[m - 1],
                                         device_id=(x, y, c), device_id_type=MESH).wait_recv()
        for cp in cps:
            cp.wait_send()
        sum_ref[...] = (((all_ref[0] + all_ref[1]) + (all_ref[2] + all_ref[3]))
                        + ((all_ref[4] + all_ref[5]) + (all_ref[6] + all_ref[7])))

    return pl.pallas_call(
        body, name="small_allreduce",
        out_shape=(jax.ShapeDtypeStruct((r, LANES), F32), jax.ShapeDtypeStruct((N_DEV, r, LANES), F32)),
        in_specs=[VMEM_SPEC], out_specs=(VMEM_SPEC, VMEM_SPEC),
        scratch_shapes=[_dma_sems(N_DEV - 1), _dma_sems(N_DEV - 1)],
        compiler_params=_params(),
    )(packed)


def _row_block(rows, cap=256):
    rb = min(rows, cap)
    assert rows % rb == 0
    return rb


def _chip_sum_bf16(stack, recv, pos, name):
    _, r, cc = stack.shape
    rh = r // 2
    rb = _row_block(rh)
    nb = rh // rb

    def body(pos_ref, a_ref, b_ref, q_ref, got_ref):
        q = (a_ref[...] + b_ref[...]).astype(BF16)
        q_ref[...] = q

        @pl.when(pl.program_id(1) == pos_ref[1])
        def _():
            got_ref[...] = q

    blk = (None, rb, cc)
    return pl.pallas_call(
        body, name=name,
        out_shape=(jax.ShapeDtypeStruct((N_CHIPS, rh, cc), BF16),) * 2,
        grid_spec=pltpu.PrefetchScalarGridSpec(
            num_scalar_prefetch=1, grid=(nb, N_CHIPS),
            in_specs=[pl.BlockSpec(blk, lambda i, j, p: (j, p[0] * nb + i, 0)),
                      pl.BlockSpec(blk, lambda i, j, p: (j, i, 0))],
            out_specs=(pl.BlockSpec(blk, lambda i, j, p: (j, i, 0)),
                       pl.BlockSpec(blk, lambda i, j, p: (p[1], i, 0)))),
        compiler_params=_params(dimension_semantics=("arbitrary", "arbitrary")),
    )(pos, stack, recv)


def _sum_chips(got, pos, name):
    _, rh, cc = got.shape
    rb = _row_block(rh)
    nb = rh // rb

    def body(pos_ref, a_ref, o_ref):
        a = a_ref[...].astype(F32)
        o_ref[...] = (a[0] + a[1]) + (a[2] + a[3])

    return pl.pallas_call(
        body, name=name,
        out_shape=jax.ShapeDtypeStruct((2 * rh, cc), F32),
        grid_spec=pltpu.PrefetchScalarGridSpec(
            num_scalar_prefetch=1, grid=(nb,),
            in_specs=[pl.BlockSpec((N_CHIPS, rb, cc), lambda i, p: (0, i, 0))],
            out_specs=pl.BlockSpec((rb, cc), lambda i, p: (p[0] * nb + i, 0))),
        compiler_params=_params(dimension_semantics=("arbitrary",)),
    )(pos, got)


def _adamw_math(w, g, m, v):
    m = ADAM_B1 * m + (1.0 - ADAM_B1) * g
    v = ADAM_B2 * v + (1.0 - ADAM_B2) * jnp.square(g)
    m_hat = m / (1.0 - ADAM_B1 ** ADAM_STEP)
    v_hat = v / (1.0 - ADAM_B2 ** ADAM_STEP)
    delta = -ADAM_LR * (m_hat / (jnp.sqrt(v_hat) + ADAM_EPS) + ADAM_WD * w)
    return delta, m, v


def _adamw_big(g, w, m, v, name):
    r, cc = w.shape
    rb = _row_block(r)

    def body(g_ref, w_ref, m_ref, v_ref, d_ref, mo_ref, vo_ref):
        d, mn, vn = _adamw_math(w_ref[...], g_ref[...], m_ref[...], v_ref[...])
        d_ref[...] = d
        mo_ref[...] = mn
        vo_ref[...] = vn

    spec = pl.BlockSpec((rb, cc), lambda i: (i, 0))
    return pl.pallas_call(
        body, name=name,
        out_shape=(jax.ShapeDtypeStruct((r, cc), F32),) * 3,
        grid=(r // rb,), in_specs=[spec] * 4, out_specs=(spec,) * 3,
        compiler_params=_params(dimension_semantics=("arbitrary",)),
    )(g, w, m, v)


def _adamw_ada(cond_t, dmod, w, m, v):
    d, na = w.shape
    cb = min(na, 256)
    assert na % cb == 0

    def body(ct_ref, dm_ref, w_ref, m_ref, v_ref, g_ref, d_ref, mo_ref, vo_ref):
        g = jnp.dot(ct_ref[...], dm_ref[...], preferred_element_type=F32, precision=lax.Precision.HIGHEST)
        dl, mn, vn = _adamw_math(w_ref[...], g, m_ref[...], v_ref[...])
        g_ref[...] = g
        d_ref[...] = dl
        mo_ref[...] = mn
        vo_ref[...] = vn

    spec = pl.BlockSpec((d, cb), lambda i: (0, i))
    return pl.pallas_call(
        body, name="adamw_w_ada",
        out_shape=(jax.ShapeDtypeStruct((d, na), F32),) * 4,
        grid=(na // cb,),
        in_specs=[pl.BlockSpec((d, N_DEV), lambda i: (0, 0)), pl.BlockSpec((N_DEV, cb), lambda i: (0, i)),
                  spec, spec, spec],
        out_specs=(spec,) * 4,
        compiler_params=_params(dimension_semantics=("arbitrary",)),
    )(cond_t, dmod, w, m, v)


def _adamw_small(gs, ws, ms, vs):
    n = len(gs)

    def body(*refs):
        ins, outs = refs[:4 * n], refs[4 * n:]
        for i in range(n):
            d, mn, vn = _adamw_math(ins[n + i][...], ins[i][...], ins[2 * n + i][...], ins[3 * n + i][...])
            outs[i][...] = d
            outs[n + i][...] = mn
            outs[2 * n + i][...] = vn

    shapes = tuple(jax.ShapeDtypeStruct(w.shape, F32) for w in ws)
    return pl.pallas_call(
        body, name="adamw_small",
        out_shape=shapes * 3,
        in_specs=[VMEM_SPEC] * (4 * n), out_specs=(VMEM_SPEC,) * (3 * n),
        compiler_params=_params(),
    )(*gs, *ws, *ms, *vs)


def _layer_norm_fwd(r):
    mu = jnp.mean(r, axis=-1, keepdims=True)
    xc = r - mu
    var = jnp.mean(jnp.square(xc), axis=-1, keepdims=True)
    rstd = lax.rsqrt(var + LN_EPS)
    return xc * rstd, rstd


def _layer_norm_bwd(dy, xhat, rstd, gain):
    dxh = dy * gain
    m1 = jnp.mean(dxh, axis=-1, keepdims=True)
    m2 = jnp.mean(dxh * xhat, axis=-1, keepdims=True)
    return rstd * (dxh - m1 - xhat * m2)


def _inv_count(tile, tm, win):
    t = (tile * tm + lax.broadcasted_iota(jnp.int32, (tm, 1), 0) + 1).astype(F32)
    return 1.0 / jnp.minimum(t, float(win))


def _fwd_mix(x, mod, conv_w, w_in4, w_pool, pool_scale, w_out, ln_g, ln_b, late_stacks, tm):
    t_len, d = x.shape
    cw = w_in4.shape[2]
    dg = cw // N_GROUPS
    nt = t_len // tm
    n_late = len(late_stacks)
    n_cp = n_late * (N_CHIPS - 1)
    pass_step = max(nt - 3, 0)

    def body(x_ref, mod_ref, cw_ref, win_ref, wp_ref, ps_ref, wout_ref, g_ref, b_ref, *rest):
        late = rest[n_late:2 * n_late]
        x1_ref, zs_ref, mix_ref, h1_ref, cat_ref = rest[2 * n_late:2 * n_late + 5]
        ubuf, vbuf, s_ici, r_ici, s_d2d, r_d2d = rest[2 * n_late + 5:]
        i = pl.program_id(0)
        start, forward, finish = _stack_gather(late, [s.shape[1] for s in late_stacks], s_ici, r_ici, s_d2d, r_d2d)

        @pl.when(i == 0)
        def _():
            start()
            ubuf[0:HALO, :] = jnp.zeros((HALO, cw), F32)
            vbuf[0:HALO, :] = jnp.zeros((HALO, cw), F32)

        xv = x_ref[...]
        sh1, sc1, g1 = mod_ref[0:1, 0:d], mod_ref[0:1, d:2 * d], mod_ref[0:1, 2 * d:3 * d]
        h1 = (xv * (1.0 + sc1) + sh1).astype(BF16)
        h1_ref[...] = h1
        zb = _dot(h1, win_ref[0])
        zc = _dot(h1, win_ref[1])
        zv = _dot(h1, win_ref[2])
        zp = _dot(h1, win_ref[3])
        u = zc * zv
        ubuf[HALO:HALO + tm, :] = u
        u1 = ubuf[pl.ds(HALO - 1, tm), :]
        u2 = ubuf[pl.ds(HALO - 2, tm), :]
        conv = cw_ref[0:1, :] * u2 + cw_ref[1:2, :] * u1 + cw_ref[2:3, :] * u
        ubuf[0:HALO, :] = ubuf[tm:tm + HALO, :]
        yc = zb * conv
        vbuf[HALO:HALO + tm, :] = zp
        ps, pms = [], []
        for gi, win in enumerate(POOL_WINDOWS):
            sl = slice(gi * dg, (gi + 1) * dg)
            acc = zp[:, sl]
            for s in range(1, win):
                acc = acc + vbuf[pl.ds(HALO - s, tm), sl]
            p_g = acc * _inv_count(i, tm, win) - zp[:, sl]
            ps.append(p_g)
            pms.append(_dot(p_g.astype(BF16), wp_ref[gi]))
        vbuf[0:HALO, :] = vbuf[tm:tm + HALO, :]
        pooled = jnp.concatenate(ps, axis=1)
        yp = jnp.concatenate(pms, axis=1) * ps_ref[...]
        cat = jnp.concatenate([yc, yp], axis=1).astype(BF16)
        cat_ref[...] = cat
        mix = _dot(cat, wout_ref[...])
        mix_ref[...] = mix
        xhat, _ = _layer_norm_fwd(DEEPNORM_ALPHA * xv + (1.0 + g1) * mix)
        x1_ref[...] = xhat * g_ref[...] + b_ref[...]
        zs_ref[:, 0 * cw:1 * cw] = zb
        zs_ref[:, 1 * cw:2 * cw] = zc
        zs_ref[:, 2 * cw:3 * cw] = zv
        zs_ref[:, 3 * cw:4 * cw] = conv
        zs_ref[:, 4 * cw:5 * cw] = pooled

        @pl.when(i == pass_step)
        def _():
            forward()

        @pl.when(i == nt - 1)
        def _():
            finish()

    tile = lambda w: pl.BlockSpec((tm, w), lambda i: (i, 0))
    n_in = 9
    return pl.pallas_call(
        body, name="fwd_mix",
        out_shape=tuple(jax.ShapeDtypeStruct(s.shape, s.dtype) for s in late_stacks)
        + (jax.ShapeDtypeStruct((t_len, d), F32), jax.ShapeDtypeStruct((t_len, 5 * cw), F32),
           jax.ShapeDtypeStruct((t_len, d), F32), jax.ShapeDtypeStruct((t_len, d), BF16),
           jax.ShapeDtypeStruct((t_len, 2 * cw), BF16)),
        grid=(nt,),
        in_specs=[tile(d)] + [VMEM_SPEC] * 8 + [HBM_SPEC] * n_late,
        out_specs=(HBM_SPEC,) * n_late + (tile(d), tile(5 * cw), tile(d), tile(d), tile(2 * cw)),
        input_output_aliases={n_in + k: k for k in range(n_late)},
        scratch_shapes=[pltpu.VMEM((tm + HALO, cw), F32), pltpu.VMEM((tm + HALO, cw), F32),
                        _dma_sems(n_cp), _dma_sems(n_cp), _dma_sems(n_cp), _dma_sems(n_cp)],
        compiler_params=_params(dimension_semantics=("arbitrary",)),
    )(x, mod, conv_w, w_in4, w_pool, pool_scale, w_out, ln_g, ln_b, *late_stacks)


def _mlp_fwd_bwd(x1, tgt, mod, w_mi4, w_mo4, ln_g, ln_b, tm):
    t_len, d = x1.shape
    fq = w_mi4.shape[2]
    nt = t_len // tm

    def body(x1_ref, tgt_ref, mod_ref, wi_ref, wo_ref, g_ref, b_ref,
             dx1_ref, da_ref, s_ref, df_ref, h2_ref, st_ref, relu_buf, acc):
        i = pl.program_id(0)

        @pl.when(i == 0)
        def _():
            acc[...] = jnp.zeros(acc.shape, F32)

        x1v = x1_ref[...]
        sh2, sc2, g2 = mod_ref[0:1, 3 * d:4 * d], mod_ref[0:1, 4 * d:5 * d], mod_ref[0:1, 5 * d:6 * d]
        h2 = (x1v * (1.0 + sc2) + sh2).astype(BF16)
        h2_ref[...] = h2
        f = jnp.zeros((tm, d), F32)
        for j in range(N_CHIPS):
            a = jnp.maximum(_dot(h2, wi_ref[j]), 0.0)
            relu_buf[j] = a
            s = (a * a).astype(BF16)
            s_ref[j] = s
            f = f + _dot(s, wo_ref[j])
        xhat, rstd = _layer_norm_fwd(DEEPNORM_ALPHA * x1v + (1.0 + g2) * f)
        gain = g_ref[...]
        err = xhat * gain + b_ref[...] - tgt_ref[...]
        dy = err * (1.0 / d)
        dr2 = _layer_norm_bwd(dy, xhat, rstd, gain)
        df = ((1.0 + g2) * dr2).astype(BF16)
        df_ref[...] = df
        dh2 = jnp.zeros((tm, d), F32)
        for j in range(N_CHIPS):
            ds = _dot_nt(df, wo_ref[j])
            da = (ds * (2.0 * relu_buf[j])).astype(BF16)
            da_ref[j] = da
            dh2 = dh2 + _dot_nt(da, wi_ref[j])
        dx1_ref[...] = DEEPNORM_ALPHA * dr2 + dh2 * (1.0 + sc2)
        acc[0] += _fold8(dy * xhat)
        acc[1] += _fold8(dy)
        acc[2] += _fold8(dr2 * f)
        acc[3] += _fold8(dh2 * x1v)
        acc[4] += _fold8(dh2)
        acc[5] += _fold8(err * err)

        @pl.when(i == nt - 1)
        def _():
            for k in range(5):
                st_ref[k:k + 1, :] = jnp.sum(acc[k], axis=0, keepdims=True)
            loss = jnp.sum(acc[5]) * (0.5 / d)
            st_ref[5:6, :] = jnp.broadcast_to(loss, (1, d))
            st_ref[6:8, :] = jnp.zeros((2, d), F32)

    tile = lambda w: pl.BlockSpec((tm, w), lambda i: (i, 0))
    tile4 = pl.BlockSpec((N_CHIPS, tm, fq), lambda i: (0, i, 0))
    return pl.pallas_call(
        body, name="mlp_fwd_bwd",
        out_shape=(jax.ShapeDtypeStruct((t_len, d), F32),
                   jax.ShapeDtypeStruct((N_CHIPS, t_len, fq), BF16), jax.ShapeDtypeStruct((N_CHIPS, t_len, fq), BF16),
                   jax.ShapeDtypeStruct((t_len, d), BF16), jax.ShapeDtypeStruct((t_len, d), BF16),
                   jax.ShapeDtypeStruct((SUBLANES, d), F32)),
        grid=(nt,),
        in_specs=[tile(d), tile(d)] + [VMEM_SPEC] * 5,
        out_specs=(tile(d), tile4, tile4, tile(d), tile(d), pl.BlockSpec((SUBLANES, d), lambda i: (0, 0))),
        scratch_shapes=[pltpu.VMEM((N_CHIPS, tm, fq), F32), pltpu.VMEM((6, SUBLANES, d), F32)],
        compiler_params=_params(dimension_semantics=("arbitrary",)),
    )(x1, tgt, mod, w_mi4, w_mo4, ln_g, ln_b)


def _mlp_wgrad(h2, da4, s4, df, tt):
    t_len, d = h2.shape
    fq = da4.shape[2]
    nt = t_len // tt

    def body(h2_ref, da_ref, s_ref, df_ref, dwi_ref, dwo_ref):
        @pl.when(pl.program_id(1) == 0)
        def _():
            dwi_ref[...] = jnp.zeros((d, fq), F32)
            dwo_ref[...] = jnp.zeros((fq, d), F32)

        dwi_ref[...] += _dot_tn(h2_ref[...], da_ref[...])
        dwo_ref[...] += _dot_tn(s_ref[...], df_ref[...])

    return pl.pallas_call(
        body, name="mlp_wgrad",
        out_shape=(jax.ShapeDtypeStruct((N_CHIPS, d, fq), F32), jax.ShapeDtypeStruct((N_CHIPS, fq, d), F32)),
        grid=(N_CHIPS, nt),
        in_specs=[pl.BlockSpec((tt, d), lambda j, t: (t, 0)), pl.BlockSpec((None, tt, fq), lambda j, t: (j, t, 0)),
                  pl.BlockSpec((None, tt, fq), lambda j, t: (j, t, 0)), pl.BlockSpec((tt, d), lambda j, t: (t, 0))],
        out_specs=(pl.BlockSpec((None, d, fq), lambda j, t: (j, 0, 0)),
                   pl.BlockSpec((None, fq, d), lambda j, t: (j, 0, 0))),
        compiler_params=_params(dimension_semantics=("arbitrary", "arbitrary")),
    )(h2, da4, s4, df)


def _bwd_mix(x, mix, dx1, zs, mod, conv_w, w_in4, w_pool, pool_scale, w_out, ln_g, qs, gots, tm):
    t_len, d = x.shape
    cw = w_in4.shape[2]
    dg = cw // N_GROUPS
    nt = t_len // tm
    n_q = len(qs)
    n_cp = n_q * (N_CHIPS - 1)

    def body(x_ref, mix_ref, dx1_ref, zs_ref, mod_ref, cw_ref, win_ref, wp_ref, ps_ref, wout_ref, g_ref, *rest):
        q_refs = rest[:n_q]
        got_refs = rest[2 * n_q:3 * n_q]
        gx_ref, dz_ref, dmix_ref, sd_ref, sc_ref, dwp_ref = rest[3 * n_q:3 * n_q + 6]
        dbuf, qbuf, acc_d, acc_c, acc_p, s_sem, r_sem = rest[3 * n_q + 6:]
        i = pl.program_id(0)
        tile_idx = nt - 1 - i
        start, finish = _chunk_exchange(q_refs, got_refs, s_sem, r_sem)

        @pl.when(i == 0)
        def _():
            start()
            dbuf[tm:tm + HALO, :] = jnp.zeros((HALO, cw), F32)
            qbuf[tm:tm + HALO, :] = jnp.zeros((HALO, cw), F32)
            acc_d[...] = jnp.zeros(acc_d.shape, F32)
            acc_c[...] = jnp.zeros(acc_c.shape, F32)
            acc_p[...] = jnp.zeros(acc_p.shape, F32)

        xv, mixv, dx1v = x_ref[...], mix_ref[...], dx1_ref[...]
        sc1, g1 = mod_ref[0:1, d:2 * d], mod_ref[0:1, 2 * d:3 * d]
        xhat, rstd = _layer_norm_fwd(DEEPNORM_ALPHA * xv + (1.0 + g1) * mixv)
        dr1 = _layer_norm_bwd(dx1v, xhat, rstd, g_ref[...])
        dmix = ((1.0 + g1) * dr1).astype(BF16)
        dmix_ref[...] = dmix
        dcat = _dot_nt(dmix, wout_ref[...])
        dyc, dyp = dcat[:, 0:cw], dcat[:, cw:2 * cw]
        zb, zc, zv = zs_ref[:, 0:cw], zs_ref[:, cw:2 * cw], zs_ref[:, 2 * cw:3 * cw]
        conv, pooled = zs_ref[:, 3 * cw:4 * cw], zs_ref[:, 4 * cw:5 * cw]
        dzb = dyc * conv
        dcv = dyc * zb
        dbuf[0:tm, :] = dcv
        d1 = dbuf[pl.ds(1, tm), :]
        d2 = dbuf[pl.ds(2, tm), :]
        du = cw_ref[2:3, :] * dcv + cw_ref[1:2, :] * d1 + cw_ref[0:1, :] * d2
        dbuf[tm:tm + HALO, :] = dbuf[0:HALO, :]
        u = zc * zv
        acc_c[0] += _fold8(u * d2)
        acc_c[1] += _fold8(u * d1)
        acc_c[2] += _fold8(u * dcv)
        dzc = du * zv
        dzv = du * zc
        dpw = dyp * ps_ref[...]
        dps, pms = [], []
        for gi, win in enumerate(POOL_WINDOWS):
            sl = slice(gi * dg, (gi + 1) * dg)
            p_g = pooled[:, sl].astype(BF16)
            dpw_g = dpw[:, sl].astype(BF16)
            pms.append(_dot(p_g, wp_ref[gi]))
            acc_p[gi] += _dot_tn(p_g, dpw_g)
            dp_g = _dot_nt(dpw_g, wp_ref[gi])
            dps.append(dp_g)
            qbuf[0:tm, sl] = dp_g * _inv_count(tile_idx, tm, win)
        acc_c[3] += _fold8(dyp * jnp.concatenate(pms, axis=1))
        dzps = []
        for gi, win in enumerate(POOL_WINDOWS):
            sl = slice(gi * dg, (gi + 1) * dg)
            acc = qbuf[0:tm, sl]
            for s in range(1, win):
                acc = acc + qbuf[pl.ds(s, tm), sl]
            dzps.append(acc - dps[gi])
        qbuf[tm:tm + HALO, :] = qbuf[0:HALO, :]
        dz = [dzb.astype(BF16), dzc.astype(BF16), dzv.astype(BF16), jnp.concatenate(dzps, axis=1).astype(BF16)]
        dh1 = jnp.zeros((tm, d), F32)
        for j in range(N_CHIPS):
            dz_ref[j] = dz[j]
            dh1 = dh1 + _dot_nt(dz[j], win_ref[j])
        gx_ref[...] = DEEPNORM_ALPHA * dr1 + dh1 * (1.0 + sc1)
        acc_d[0] += _fold8(dx1v * xhat)
        acc_d[1] += _fold8(dx1v)
        acc_d[2] += _fold8(dr1 * mixv)
        acc_d[3] += _fold8(dh1 * xv)
        acc_d[4] += _fold8(dh1)

        @pl.when(i == nt - 1)
        def _():
            for k in range(5):
                sd_ref[k:k + 1, :] = jnp.sum(acc_d[k], axis=0, keepdims=True)
            sd_ref[5:8, :] = jnp.zeros((3, d), F32)
            for k in range(4):
                sc_ref[k:k + 1, :] = jnp.sum(acc_c[k], axis=0, keepdims=True)
            sc_ref[4:8, :] = jnp.zeros((4, cw), F32)
            dwp_ref[...] = acc_p[...]
            finish()

    rtile = lambda w: pl.BlockSpec((tm, w), lambda i: (nt - 1 - i, 0))
    whole = lambda shape: pl.BlockSpec(shape, lambda i: tuple(0 for _ in shape))
    n_in = 11
    return pl.pallas_call(
        body, name="bwd_mix",
        out_shape=tuple(jax.ShapeDtypeStruct(g.shape, g.dtype) for g in gots)
        + (jax.ShapeDtypeStruct((t_len, d), F32), jax.ShapeDtypeStruct((N_CHIPS, t_len, cw), BF16),
           jax.ShapeDtypeStruct((t_len, d), BF16), jax.ShapeDtypeStruct((SUBLANES, d), F32),
           jax.ShapeDtypeStruct((SUBLANES, cw), F32), jax.ShapeDtypeStruct((N_GROUPS, dg, dg), F32)),
        grid=(nt,),
        in_specs=[rtile(d), rtile(d), rtile(d), rtile(5 * cw)] + [VMEM_SPEC] * 7 + [HBM_SPEC] * (2 * n_q),
        out_specs=(HBM_SPEC,) * n_q
        + (rtile(d), pl.BlockSpec((N_CHIPS, tm, cw), lambda i: (0, nt - 1 - i, 0)), rtile(d),
           whole((SUBLANES, d)), whole((SUBLANES, cw)), whole((N_GROUPS, dg, dg))),
        input_output_aliases={n_in + n_q + k: k for k in range(n_q)},
        scratch_shapes=[pltpu.VMEM((tm + HALO, cw), F32), pltpu.VMEM((tm + HALO, cw), F32),
                        pltpu.VMEM((5, SUBLANES, d), F32), pltpu.VMEM((4, SUBLANES, cw), F32),
                        pltpu.VMEM((N_GROUPS, dg, dg), F32), _dma_sems(n_cp), _dma_sems(n_cp)],
        compiler_params=_params(dimension_semantics=("arbitrary",)),
    )(x, mix, dx1, zs, mod, conv_w, w_in4, w_pool, pool_scale, w_out, ln_g, *qs, *gots)


def _mix_wgrad(h1, dz4, cat, dmix, tt):
    t_len, d = h1.shape
    cw = dz4.shape[2]
    nt = t_len // tt

    def body(h1_ref, dz_ref, cat_ref, dmix_ref, dwin_ref, dwout_ref):
        @pl.when(pl.program_id(0) == 0)
        def _():
            dwin_ref[...] = jnp.zeros(dwin_ref.shape, F32)
            dwout_ref[...] = jnp.zeros(dwout_ref.shape, F32)

        h1v = h1_ref[...]
        for j in range(N_CHIPS):
            dwin_ref[j] += _dot_tn(h1v, dz_ref[j])
        dwout_ref[...] += _dot_tn(cat_ref[...], dmix_ref[...])

    return pl.pallas_call(
        body, name="mix_wgrad",
        out_shape=(jax.ShapeDtypeStruct((N_CHIPS, d, cw), F32), jax.ShapeDtypeStruct((2 * cw, d), F32)),
        grid=(nt,),
        in_specs=[pl.BlockSpec((tt, d), lambda t: (t, 0)), pl.BlockSpec((N_CHIPS, tt, cw), lambda t: (0, t, 0)),
                  pl.BlockSpec((tt, 2 * cw), lambda t: (t, 0)), pl.BlockSpec((tt, d), lambda t: (t, 0))],
        out_specs=(pl.BlockSpec((N_CHIPS, d, cw), lambda t: (0, 0, 0)), pl.BlockSpec((2 * cw, d), lambda t: (0, 0))),
        compiler_params=_params(dimension_semantics=("arbitrary",)),
    )(h1, dz4, cat, dmix)


def _rows128(v):
    v = v.reshape(-1, LANES)
    pad = (-v.shape[0]) % SUBLANES
    if pad:
        v = jnp.concatenate([v, jnp.zeros((pad, LANES), v.dtype)], axis=0)
    return v


def kernel(x, c, w_ada, b_ada, w_in, conv_w, w_pool, pool_scale, w_out, ln1_g, ln1_b, w_mlp_in, w_mlp_out, ln2_g, ln2_b, loss_target, m_w_ada, m_b_ada, m_w_in, m_conv_w, m_w_pool, m_pool_scale, m_w_out, m_ln1_g, m_ln1_b, m_w_mlp_in, m_w_mlp_out, m_ln2_g, m_ln2_b, v_w_ada, v_b_ada, v_w_in, v_conv_w, v_w_pool, v_pool_scale, v_w_out, v_ln1_g, v_ln1_b, v_w_mlp_in, v_w_mlp_out, v_ln2_g, v_ln2_b):
    t_len, d = x.shape[1], x.shape[2]
    cw = w_in.shape[2]
    cq = conv_w.shape[2]
    dg = w_pool.shape[2]
    assert cw == N_GROUPS * dg and cq * N_CHIPS == cw and LANES % cq == 0
    tm = min(TOKEN_TILE, t_len)
    tt = min(WGRAD_TILE, t_len)
    chip = 2 * lax.axis_index("x") + lax.axis_index("y")
    pos = jnp.stack([lax.axis_index("c"), chip]).astype(jnp.int32)

    x2, tgt = x[0], loss_target[0]
    big_w = [w_in[0], w_out[0], w_mlp_in[0], w_mlp_out[0]]
    big_m = [m_w_in[0], m_w_out[0], m_w_mlp_in[0], m_w_mlp_out[0]]
    big_v = [v_w_in[0], v_w_out[0], v_w_mlp_in[0], v_w_mlp_out[0]]
    names = ["w_in", "w_out", "w_mlp_in", "w_mlp_out"]

    mod, cond_all, conv_full = _mod_exchange(c, w_ada[0], b_ada, conv_w[0])
    w_in4, w_out4, w_mi_own, w_mo_own = _gather_first(big_w, 2)
    w_out_full = w_out4.reshape(2 * cw, d)
    w_pool_bf = w_pool[0].astype(BF16)

    w_mi4, w_mo4, x1, zs, mix, h1, cat = _fwd_mix(x2, mod, conv_full, w_in4, w_pool_bf, pool_scale, w_out_full,
                                                  ln1_g, ln1_b, [w_mi_own, w_mo_own], tm)
    dx1, da4, s4, df, h2, st2 = _mlp_fwd_bwd(x1, tgt, mod, w_mi4, w_mo4, ln2_g, ln2_b, tm)
    dw_mi4, dw_mo4 = _mlp_wgrad(h2, da4, s4, df, tt)

    mlp_stacks = [dw_mi4, dw_mo4]
    mlp_recv = _sibling_halves(mlp_stacks, "mlp")
    mlp_q, mlp_got = zip(*[_chip_sum_bf16(s, r, pos, "rs_chip_sum_" + n)
                           for s, r, n in zip(mlp_stacks, mlp_recv, names[2:])])

    out = _bwd_mix(x2, mix, dx1, zs, mod, conv_full, w_in4, w_pool_bf, pool_scale, w_out_full, ln1_g,
                   list(mlp_q), list(mlp_got), tm)
    mlp_got, (grad_x, dz4, dmix, st1, stc, dw_pool) = out[:2], out[2:]
    dw_in4, dw_out = _mix_wgrad(h1, dz4, cat, dmix, tt)

    mix_stacks = [dw_in4, dw_out.reshape(N_CHIPS, (2 * cw) // N_CHIPS, d)]
    mix_recv = _sibling_halves(mix_stacks, "mix")
    mix_q, mix_got = zip(*[_chip_sum_bf16(s, r, pos, "rs_chip_sum_" + n)
                           for s, r, n in zip(mix_stacks, mix_recv, names[:2])])
    mix_got = _chip_exchange(list(mix_q), list(mix_got))
    halves = [_sum_chips(g, pos, "rs_total_" + n) for g, n in zip(list(mix_got) + list(mlp_got), names)]
    big_g = _sibling_join(halves)
    big_out = [_adamw_big(g, w, m, v, "adamw_" + n) for g, w, m, v, n in zip(big_g, big_w, big_m, big_v, names)]

    conv_part = jnp.transpose(stc[0:3].reshape(3, N_CHIPS, cq), (1, 0, 2))
    conv_part = jnp.pad(conv_part, ((0, 0), (0, SUBLANES - 3), (0, LANES - cq)))
    dmod = jnp.concatenate([st1[4], st1[3], st1[2], st2[4], st2[3], st2[2]])
    pieces = [st2[5, 0:LANES], st1[0], st1[1], st2[0], st2[1], stc[3], conv_part, dw_pool, dmod]
    rows = [_rows128(p) for p in pieces]
    offs = [0]
    for r in rows:
        offs.append(offs[-1] + r.shape[0])
    total, gathered = _small_allreduce(jnp.concatenate(rows, axis=0))

    def sec(k, shape):
        n = 1
        for s in shape:
            n *= s
        return total[offs[k]:offs[k] + n // LANES].reshape(shape)

    loss = total[0, 0]
    g_ln1_g, g_ln1_b, g_ln2_g, g_ln2_b = sec(1, (1, d)), sec(2, (1, d)), sec(3, (1, d)), sec(4, (1, d))
    g_pool_scale = sec(5, (1, cw))
    conv_blocks = total[offs[6]:offs[7]].reshape(N_CHIPS, SUBLANES, LANES)
    g_conv = lax.dynamic_index_in_dim(conv_blocks, chip, axis=0, keepdims=False)[0:3, 0:cq]
    g_w_pool = sec(7, (N_GROUPS, dg, dg))
    g_b_ada = sec(8, (1, 6 * d))
    dmod_all = gathered[:, offs[8]:offs[9], :].reshape(N_DEV, 6 * d)
    na = w_ada.shape[2]
    dmod_mine = lax.dynamic_slice_in_dim(dmod_all, chip * na, na, axis=1)

    g_ada, d_ada, nm_ada, nv_ada = _adamw_ada(cond_all.T, dmod_mine, w_ada[0], m_w_ada[0], v_w_ada[0])

    small_g = [g_b_ada, g_conv, g_w_pool.reshape(-1, dg), g_pool_scale, g_ln1_g, g_ln1_b, g_ln2_g, g_ln2_b]
    small_w = [b_ada, conv_w[0], w_pool.reshape(-1, dg), pool_scale, ln1_g, ln1_b, ln2_g, ln2_b]
    small_m = [m_b_ada, m_conv_w[0], m_w_pool.reshape(-1, dg), m_pool_scale, m_ln1_g, m_ln1_b, m_ln2_g, m_ln2_b]
    small_v = [v_b_ada, v_conv_w[0], v_w_pool.reshape(-1, dg), v_pool_scale, v_ln1_g, v_ln1_b, v_ln2_g, v_ln2_b]
    sm = _adamw_small(small_g, small_w, small_m, small_v)
    ns = len(small_g)
    s_delta, s_m, s_v = sm[:ns], sm[ns:2 * ns], sm[2 * ns:]

    def assemble(ada, small, big):
        return [ada[None], small[0], big[0][None], small[1][None], small[2].reshape(w_pool.shape), small[3],
                big[1][None], small[4], small[5], big[2][None], big[3][None], small[6], small[7]]

    grads = assemble(g_ada, small_g, big_g)
    deltas = assemble(d_ada, s_delta, [o[0] for o in big_out])
    new_m = assemble(nm_ada, s_m, [o[1] for o in big_out])
    new_v = assemble(nv_ada, s_v, [o[2] for o in big_out])
    return (loss, grad_x[None], *grads, *deltas, *new_m, *new_v)
```

```python
import jax
import jax.numpy as jnp
from jax import lax
from jax.experimental import pallas as pl
from jax.experimental.pallas import tpu as pltpu

F32 = jnp.float32
BF16 = jnp.bfloat16
MESH = pl.DeviceIdType.MESH

LN_EPS = 1e-5
DEEPNORM_ALPHA = 2.0 ** 0.25
POOL_WINDOWS = (2, 4, 8, 16)
N_GROUPS = len(POOL_WINDOWS)
HALO = 16
N_CHIPS = 4
N_DEV = 8
LANES = 128
SUBLANES = 8
VMEM_LIMIT = 56 * 1024 * 1024
TOKEN_TILE = 256
WGRAD_TILE = 512

ADAM_LR = 0.001
ADAM_B1 = 0.9
ADAM_B2 = 0.999
ADAM_EPS = 1e-08
ADAM_WD = 0.01
ADAM_STEP = 10

VMEM_SPEC = pl.BlockSpec(memory_space=pltpu.VMEM)
HBM_SPEC = pl.BlockSpec(memory_space=pltpu.HBM)


def _dot(a, b):
    return jnp.dot(a, b, preferred_element_type=F32)


def _dot_nt(a, b):
    return lax.dot_general(a, b, (((1,), (1,)), ((), ())), preferred_element_type=F32)


def _dot_tn(a, b):
    return lax.dot_general(a, b, (((0,), (0,)), ((), ())), preferred_element_type=F32)


def _fold8(v):
    r, n = v.shape
    return jnp.sum(v.reshape(r // SUBLANES, SUBLANES, n), axis=0)


def _mesh_pos():
    return lax.axis_index("x"), lax.axis_index("y"), lax.axis_index("c")


def _flip(v, bit):
    return 1 - v if bit else v


def _params(**kw):
    return pltpu.CompilerParams(vmem_limit_bytes=VMEM_LIMIT, **kw)


def _dma_sems(n):
    return pltpu.SemaphoreType.DMA((n,))


def _stack_gather(stacks, rows, s_ici, r_ici, s_d2d, r_d2d):
    x, y, c = _mesh_pos()
    chip = 2 * x + y
    pairs = [(i, m) for i in range(len(stacks)) for m in range(1, N_CHIPS)]

    def blk(i, slot, which):
        rh = rows[i] // 2
        return stacks[i].at[slot].at[pl.ds(which * rh, rh), :]

    def other(m):
        return 2 * _flip(x, m & 2) + _flip(y, m & 1)

    def ici(i, m, slot, to):
        k = i * (N_CHIPS - 1) + m - 1
        return pltpu.make_async_remote_copy(blk(i, slot, c), blk(i, slot, c), s_ici.at[k], r_ici.at[k],
                                            device_id=to, device_id_type=MESH)

    def d2d(i, m, which, to):
        k = i * (N_CHIPS - 1) + m - 1
        return pltpu.make_async_remote_copy(blk(i, other(m), which), blk(i, other(m), which),
                                            s_d2d.at[k], r_d2d.at[k], device_id=to, device_id_type=MESH)

    def start():
        for i, m in pairs:
            ici(i, m, chip, (_flip(x, m & 2), _flip(y, m & 1), c)).start()

    def forward():
        for i, m in pairs:
            ici(i, m, other(m), (x, y, c)).wait_recv()
            d2d(i, m, c, (x, y, 1 - c)).start()

    def finish():
        for i, m in pairs:
            d2d(i, m, 1 - c, (x, y, c)).wait_recv()
        for i, m in pairs:
            ici(i, m, chip, (x, y, c)).wait_send()
            d2d(i, m, c, (x, y, c)).wait_send()

    return start, forward, finish


def _chunk_exchange(qs, gots, s_sem, r_sem):
    x, y, c = _mesh_pos()
    chip = 2 * x + y
    pairs = [(i, m) for i in range(len(qs)) for m in range(1, N_CHIPS)]

    def other(m):
        return 2 * _flip(x, m & 2) + _flip(y, m & 1)

    def send(i, m, to):
        k = i * (N_CHIPS - 1) + m - 1
        return pltpu.make_async_remote_copy(qs[i].at[other(m)], gots[i].at[chip], s_sem.at[k], r_sem.at[k],
                                            device_id=to, device_id_type=MESH)

    def arrival(i, m):
        k = i * (N_CHIPS - 1) + m - 1
        return pltpu.make_async_remote_copy(qs[i].at[other(m)], gots[i].at[other(m)], s_sem.at[k], r_sem.at[k],
                                            device_id=(x, y, c), device_id_type=MESH)

    def start():
        for i, m in pairs:
            send(i, m, (_flip(x, m & 2), _flip(y, m & 1), c)).start()

    def finish():
        for i, m in pairs:
            arrival(i, m).wait_recv()
        for i, m in pairs:
            send(i, m, (x, y, c)).wait_send()

    return start, finish


def _mod_scratch(d, na, cq):
    return [pltpu.VMEM((SUBLANES, d), F32), pltpu.VMEM((N_DEV, SUBLANES, d), F32),
            pltpu.VMEM((N_CHIPS, SUBLANES, na), F32),
            pltpu.VMEM((SUBLANES, cq), F32), pltpu.VMEM((N_CHIPS, SUBLANES, cq), F32),
            _dma_sems(N_DEV - 1), _dma_sems(N_DEV - 1),
            _dma_sems(N_CHIPS - 1), _dma_sems(N_CHIPS - 1), _dma_sems(N_CHIPS - 1), _dma_sems(N_CHIPS - 1)]


def _mod_steps(d, na, cq):
    def steps(c_ref, w_ref, b_ref, cw_ref, mod_ref, cond_ref, conv_ref,
              csend, cbuf, mbuf, cvsend, cvbuf, s1, r1, s2, r2, s3, r3):
        x, y, c = _mesh_pos()
        me = 4 * x + 2 * y + c
        chip = 2 * x + y
        csend[...] = jnp.broadcast_to(c_ref[...], (SUBLANES, d))
        cbuf[me] = csend[...]
        first = []
        for m in range(1, N_DEV):
            peer = (_flip(x, m & 4), _flip(y, m & 2), _flip(c, m & 1))
            cp = pltpu.make_async_remote_copy(csend, cbuf.at[me], s1.at[m - 1], r1.at[m - 1],
                                              device_id=peer, device_id_type=MESH)
            cp.start()
            first.append(cp)
        cvsend[...] = jnp.zeros((SUBLANES, cq), F32)
        cvsend[0:3, :] = cw_ref[...]
        cvbuf[chip] = cvsend[...]
        for m in range(1, N_DEV):
            src = 4 * _flip(x, m & 4) + 2 * _flip(y, m & 2) + _flip(c, m & 1)
            pltpu.make_async_remote_copy(csend, cbuf.at[src], s1.at[m - 1], r1.at[m - 1],
                                         device_id=(x, y, c), device_id_type=MESH).wait_recv()
        rows = lax.broadcasted_iota(jnp.int32, (SUBLANES, d), 0)
        call = jnp.zeros((SUBLANES, d), F32)
        for b in range(N_DEV):
            call = jnp.where(rows == b, cbuf[b], call)
        cond = call * jax.nn.sigmoid(call)
        cond_ref[...] = cond
        part = jnp.dot(cond, w_ref[...], preferred_element_type=F32, precision=lax.Precision.HIGHEST)
        mbuf[chip] = part
        second = []
        for m in range(1, N_CHIPS):
            peer = (_flip(x, m & 2), _flip(y, m & 1), c)
            cp = pltpu.make_async_remote_copy(mbuf.at[chip], mbuf.at[chip], s2.at[m - 1], r2.at[m - 1],
                                              device_id=peer, device_id_type=MESH)
            cp.start()
            second.append(cp)
            cp = pltpu.make_async_remote_copy(cvsend, cvbuf.at[chip], s3.at[m - 1], r3.at[m - 1],
                                              device_id=peer, device_id_type=MESH)
            cp.start()
            second.append(cp)
        for m in range(1, N_CHIPS):
            src = 2 * _flip(x, m & 2) + _flip(y, m & 1)
            pltpu.make_async_remote_copy(mbuf.at[src], mbuf.at[src], s2.at[m - 1], r2.at[m - 1],
                                         device_id=(x, y, c), device_id_type=MESH).wait_recv()
            pltpu.make_async_remote_copy(cvsend, cvbuf.at[src], s3.at[m - 1], r3.at[m - 1],
                                         device_id=(x, y, c), device_id_type=MESH).wait_recv()
        for cp in first + second:
            cp.wait_send()
        rows_n = lax.broadcasted_iota(jnp.int32, (SUBLANES, na), 0)
        for k in range(N_CHIPS):
            mine = jnp.sum(jnp.where(rows_n == me, mbuf[k], 0.0), axis=0, keepdims=True)
            mod_ref[:, k * na:(k + 1) * na] = jnp.broadcast_to(mine + b_ref[:, k * na:(k + 1) * na], (SUBLANES, na))
            conv_ref[:, k * cq:(k + 1) * cq] = cvbuf[k]

    return steps


def _setup_exchange(c_row, w_ada_s, b_ada, conv_w_s, shards, n_now):
    n = len(shards)
    n_cp = n_now * (N_CHIPS - 1)
    d, na, cq = c_row.shape[1], w_ada_s.shape[1], conv_w_s.shape[1]
    mod_scratch = _mod_scratch(d, na, cq)
    mod_steps = _mod_steps(d, na, cq)

    def body(*refs):
        mod_in, ins = refs[:4], refs[4:4 + n]
        mod_out, outs = refs[4 + n:7 + n], refs[7 + n:7 + 2 * n]
        bufs = refs[7 + 2 * n:7 + 3 * n]
        mod_scr = refs[7 + 3 * n:7 + 3 * n + len(mod_scratch)]
        s_ici, r_ici, s_d2d, r_d2d, s_loc = refs[7 + 3 * n + len(mod_scratch):]
        x, y, _ = _mesh_pos()
        chip = 2 * x + y
        local = []
        for i in range(n):
            bufs[i][...] = ins[i][...].astype(BF16)
            cp = pltpu.make_async_copy(bufs[i], outs[i].at[chip], s_loc.at[i])
            cp.start()
            local.append(cp)
        for cp in local:
            cp.wait()
        start, forward, finish = _stack_gather(outs[:n_now], [s.shape[0] for s in shards[:n_now]],
                                               s_ici, r_ici, s_d2d, r_d2d)
        start()
        mod_steps(*mod_in, *mod_out, *mod_scr)
        forward()
        finish()

    return pl.pallas_call(
        body, name="setup_exchange",
        out_shape=(jax.ShapeDtypeStruct((SUBLANES, N_CHIPS * na), F32), jax.ShapeDtypeStruct((SUBLANES, d), F32),
                   jax.ShapeDtypeStruct((SUBLANES, N_CHIPS * cq), F32))
        + tuple(jax.ShapeDtypeStruct((N_CHIPS,) + s.shape, BF16) for s in shards),
        in_specs=[VMEM_SPEC] * (4 + n), out_specs=(VMEM_SPEC,) * 3 + (HBM_SPEC,) * n,
        scratch_shapes=[pltpu.VMEM(s.shape, BF16) for s in shards] + mod_scratch
        + [_dma_sems(n_cp), _dma_sems(n_cp), _dma_sems(n_cp), _dma_sems(n_cp), _dma_sems(n)],
        compiler_params=_params(),
    )(c_row, w_ada_s, b_ada, conv_w_s, *shards)


def _sibling_join(gs, tag):
    n = len(gs)

    def body(*refs):
        outs = refs[n:2 * n]
        s_sem, r_sem = refs[2 * n:]
        x, y, c = _mesh_pos()
        cps = []
        for i in range(n):
            rh = gs[i].shape[0] // 2
            mine = outs[i].at[pl.ds(c * rh, rh), :]
            cp = pltpu.make_async_remote_copy(mine, mine, s_sem.at[i], r_sem.at[i],
                                              device_id=(x, y, 1 - c), device_id_type=MESH)
            cp.start()
            cps.append(cp)
        for i in range(n):
            rh = gs[i].shape[0] // 2
            theirs = outs[i].at[pl.ds((1 - c) * rh, rh), :]
            pltpu.make_async_remote_copy(theirs, theirs, s_sem.at[i], r_sem.at[i],
                                         device_id=(x, y, c), device_id_type=MESH).wait_recv()
        for cp in cps:
            cp.wait_send()

    return pl.pallas_call(
        body, name="rs_sibling_join_" + tag,
        out_shape=tuple(jax.ShapeDtypeStruct(g.shape, g.dtype) for g in gs),
        in_specs=[HBM_SPEC] * n, out_specs=(HBM_SPEC,) * n,
        input_output_aliases={i: i for i in range(n)},
        scratch_shapes=[_dma_sems(n), _dma_sems(n)],
        compiler_params=_params(),
    )(*gs)


def _small_allreduce_steps(p_ref, sum_ref, all_out_ref, all_ref, s_sem, r_sem):
    x, y, c = _mesh_pos()
    me = 4 * x + 2 * y + c

    def send(m, to):
        return pltpu.make_async_remote_copy(p_ref, all_ref.at[me], s_sem.at[m - 1], r_sem.at[m - 1],
                                            device_id=to, device_id_type=MESH)

    def start():
        all_ref[me] = p_ref[...]
        for m in range(1, N_DEV):
            send(m, (_flip(x, m & 4), _flip(y, m & 2), _flip(c, m & 1))).start()

    def finish():
        for m in range(1, N_DEV):
            src = 4 * _flip(x, m & 4) + 2 * _flip(y, m & 2) + _flip(c, m & 1)
            pltpu.make_async_remote_copy(p_ref, all_ref.at[src], s_sem.at[m - 1], r_sem.at[m - 1],
                                         device_id=(x, y, c), device_id_type=MESH).wait_recv()
        for m in range(1, N_DEV):
            send(m, (x, y, c)).wait_send()
        sum_ref[...] = (((all_ref[0] + all_ref[1]) + (all_ref[2] + all_ref[3]))
                        + ((all_ref[4] + all_ref[5]) + (all_ref[6] + all_ref[7])))
        all_out_ref[...] = all_ref[...]

    return start, finish


def _row_block(rows, cap=256):
    rb = min(rows, cap)
    assert rows % rb == 0
    return rb


def _sum_chips(got, pos, name):
    _, rh, cc = got.shape
    rb = _row_block(rh)
    nb = rh // rb

    def body(pos_ref, a_ref, o_ref):
        a = a_ref[...].astype(F32)
        o_ref[...] = (a[0] + a[1]) + (a[2] + a[3])

    return pl.pallas_call(
        body, name=name,
        out_shape=jax.ShapeDtypeStruct((2 * rh, cc), F32),
        grid_spec=pltpu.PrefetchScalarGridSpec(
            num_scalar_prefetch=1, grid=(nb,),
            in_specs=[pl.BlockSpec((N_CHIPS, rb, cc), lambda i, p: (0, i, 0))],
            out_specs=pl.BlockSpec((rb, cc), lambda i, p: (p[0] * nb + i, 0))),
        compiler_params=_params(dimension_semantics=("arbitrary",)),
    )(pos, got)


def _adamw_math(w, g, m, v):
    m = ADAM_B1 * m + (1.0 - ADAM_B1) * g
    v = ADAM_B2 * v + (1.0 - ADAM_B2) * jnp.square(g)
    m_hat = m / (1.0 - ADAM_B1 ** ADAM_STEP)
    v_hat = v / (1.0 - ADAM_B2 ** ADAM_STEP)
    delta = -ADAM_LR * (m_hat / (jnp.sqrt(v_hat) + ADAM_EPS) + ADAM_WD * w)
    return delta, m, v


def _adamw_big(g, w, m, v, name):
    r, cc = w.shape
    rb = _row_block(r)

    def body(g_ref, w_ref, m_ref, v_ref, d_ref, mo_ref, vo_ref):
        d, mn, vn = _adamw_math(w_ref[...], g_ref[...], m_ref[...], v_ref[...])
        d_ref[...] = d
        mo_ref[...] = mn
        vo_ref[...] = vn

    spec = pl.BlockSpec((rb, cc), lambda i: (i, 0))
    return pl.pallas_call(
        body, name=name,
        out_shape=(jax.ShapeDtypeStruct((r, cc), F32),) * 3,
        grid=(r // rb,), in_specs=[spec] * 4, out_specs=(spec,) * 3,
        compiler_params=_params(dimension_semantics=("arbitrary",)),
    )(g, w, m, v)


def _adamw_tail(gs, ws, ms, vs, cond_t, dmod, ada, qs, gots):
    n = len(gs)
    n_q = len(qs)
    n_cp = n_q * (N_CHIPS - 1)
    rows = ada[0].shape[0]
    assert all(w.shape[0] == rows for w in ws)
    rb = _row_block(rows, 128)
    nb = rows // rb

    def body(*refs):
        g_refs, w_refs, m_refs, v_refs = refs[:n], refs[n:2 * n], refs[2 * n:3 * n], refs[3 * n:4 * n]
        ct_ref, dm_ref, wa_ref, ma_ref, va_ref = refs[4 * n:4 * n + 5]
        q_refs = refs[4 * n + 5:4 * n + 5 + n_q]
        outs = refs[4 * n + 5 + 2 * n_q:]
        got_refs, outs = outs[:n_q], outs[n_q:]
        big_outs, ada_outs = outs[:3 * n], outs[3 * n:3 * n + 4]
        s_sem, r_sem = outs[3 * n + 4:]
        i = pl.program_id(0)
        start, finish = _chunk_exchange(q_refs, got_refs, s_sem, r_sem)

        @pl.when(i == 0)
        def _():
            start()

        for k in range(n):
            dl, mn, vn = _adamw_math(w_refs[k][...], g_refs[k][...], m_refs[k][...], v_refs[k][...])
            big_outs[3 * k][...] = dl
            big_outs[3 * k + 1][...] = mn
            big_outs[3 * k + 2][...] = vn
        g = jnp.dot(ct_ref[...], dm_ref[...], preferred_element_type=F32, precision=lax.Precision.HIGHEST)
        dl, mn, vn = _adamw_math(wa_ref[...], g, ma_ref[...], va_ref[...])
        ada_outs[0][...] = g
        ada_outs[1][...] = dl
        ada_outs[2][...] = mn
        ada_outs[3][...] = vn

        @pl.when(i == nb - 1)
        def _():
            finish()

    blk = lambda a: pl.BlockSpec((rb, a.shape[1]), lambda i: (i, 0))
    shape = lambda a: jax.ShapeDtypeStruct(a.shape, F32)
    return pl.pallas_call(
        body, name="adamw_tail",
        out_shape=tuple(jax.ShapeDtypeStruct(g.shape, g.dtype) for g in gots)
        + tuple(shape(w) for w in ws for _ in range(3)) + (shape(ada[0]),) * 4,
        grid=(nb,),
        in_specs=[blk(a) for a in list(gs) + list(ws) + list(ms) + list(vs)]
        + [blk(cond_t), pl.BlockSpec(dmod.shape, lambda i: (0, 0))] + [blk(a) for a in ada]
        + [HBM_SPEC] * (2 * n_q),
        out_specs=(HBM_SPEC,) * n_q + tuple(blk(w) for w in ws for _ in range(3)) + (blk(ada[0]),) * 4,
        input_output_aliases={4 * n + 5 + n_q + k: k for k in range(n_q)},
        scratch_shapes=[_dma_sems(n_cp), _dma_sems(n_cp)],
        compiler_params=_params(dimension_semantics=("arbitrary",)),
    )(*gs, *ws, *ms, *vs, cond_t, dmod, *ada, *qs, *gots)


def _adamw_small(gs, ws, ms, vs):
    n = len(gs)

    def body(*refs):
        ins, outs = refs[:4 * n], refs[4 * n:]
        for i in range(n):
            d, mn, vn = _adamw_math(ins[n + i][...], ins[i][...], ins[2 * n + i][...], ins[3 * n + i][...])
            outs[i][...] = d
            outs[n + i][...] = mn
            outs[2 * n + i][...] = vn

    shapes = tuple(jax.ShapeDtypeStruct(w.shape, F32) for w in ws)
    return pl.pallas_call(
        body, name="adamw_small",
        out_shape=shapes * 3,
        in_specs=[VMEM_SPEC] * (4 * n), out_specs=(VMEM_SPEC,) * (3 * n),
        compiler_params=_params(),
    )(*gs, *ws, *ms, *vs)


def _layer_norm_fwd(r):
    mu = jnp.mean(r, axis=-1, keepdims=True)
    xc = r - mu
    var = jnp.mean(jnp.square(xc), axis=-1, keepdims=True)
    rstd = lax.rsqrt(var + LN_EPS)
    return xc * rstd, rstd


def _layer_norm_bwd(dy, xhat, rstd, gain):
    dxh = dy * gain
    m1 = jnp.mean(dxh, axis=-1, keepdims=True)
    m2 = jnp.mean(dxh * xhat, axis=-1, keepdims=True)
    return rstd * (dxh - m1 - xhat * m2)


def _inv_count(tile, tm, win):
    t = (tile * tm + lax.broadcasted_iota(jnp.int32, (tm, 1), 0) + 1).astype(F32)
    return 1.0 / jnp.minimum(t, float(win))


def _fwd_mix(x, mod, conv_w, w_in4, w_pool, pool_scale, w_out, ln_g, ln_b, late_stacks, tm):
    t_len, d = x.shape
    cw = w_in4.shape[2]
    dg = cw // N_GROUPS
    nt = t_len // tm
    n_late = len(late_stacks)
    n_cp = n_late * (N_CHIPS - 1)
    pass_step = max(nt - 3, 0)

    def body(x_ref, mod_ref, cw_ref, win_ref, wp_ref, ps_ref, wout_ref, g_ref, b_ref, *rest):
        late = rest[n_late:2 * n_late]
        x1_ref, zs_ref, mix_ref, h1_ref, cat_ref = rest[2 * n_late:2 * n_late + 5]
        ubuf, vbuf, s_ici, r_ici, s_d2d, r_d2d = rest[2 * n_late + 5:]
        i = pl.program_id(0)
        start, forward, finish = _stack_gather(late, [s.shape[1] for s in late_stacks], s_ici, r_ici, s_d2d, r_d2d)

        @pl.when(i == 0)
        def _():
            start()
            ubuf[0:HALO, :] = jnp.zeros((HALO, cw), F32)
            vbuf[0:HALO, :] = jnp.zeros((HALO, cw), F32)

        xv = x_ref[...]
        sh1, sc1, g1 = mod_ref[0:1, 0:d], mod_ref[0:1, d:2 * d], mod_ref[0:1, 2 * d:3 * d]
        h1 = (xv * (1.0 + sc1) + sh1).astype(BF16)
        h1_ref[...] = h1
        zb = _dot(h1, win_ref[0])
        zc = _dot(h1, win_ref[1])
        zv = _dot(h1, win_ref[2])
        zp = _dot(h1, win_ref[3])
        u = zc * zv
        ubuf[HALO:HALO + tm, :] = u
        u1 = ubuf[pl.ds(HALO - 1, tm), :]
        u2 = ubuf[pl.ds(HALO - 2, tm), :]
        conv = cw_ref[0:1, :] * u2 + cw_ref[1:2, :] * u1 + cw_ref[2:3, :] * u
        ubuf[0:HALO, :] = ubuf[tm:tm + HALO, :]
        yc = zb * conv
        vbuf[HALO:HALO + tm, :] = zp
        ps, pms = [], []
        for gi, win in enumerate(POOL_WINDOWS):
            sl = slice(gi * dg, (gi + 1) * dg)
            acc = zp[:, sl]
            for s in range(1, win):
                acc = acc + vbuf[pl.ds(HALO - s, tm), sl]
            p_g = acc * _inv_count(i, tm, win) - zp[:, sl]
            ps.append(p_g)
            pms.append(_dot(p_g.astype(BF16), wp_ref[gi]))
        vbuf[0:HALO, :] = vbuf[tm:tm + HALO, :]
        pooled = jnp.concatenate(ps, axis=1)
        yp = jnp.concatenate(pms, axis=1) * ps_ref[...]
        cat = jnp.concatenate([yc, yp], axis=1).astype(BF16)
        cat_ref[...] = cat
        mix = _dot(cat, wout_ref[...])
        mix_ref[...] = mix
        xhat, _ = _layer_norm_fwd(DEEPNORM_ALPHA * xv + (1.0 + g1) * mix)
        x1_ref[...] = xhat * g_ref[...] + b_ref[...]
        zs_ref[:, 0 * cw:1 * cw] = zb
        zs_ref[:, 1 * cw:2 * cw] = zc
        zs_ref[:, 2 * cw:3 * cw] = zv
        zs_ref[:, 3 * cw:4 * cw] = conv
        zs_ref[:, 4 * cw:5 * cw] = pooled

        @pl.when(i == pass_step)
        def _():
            forward()

        @pl.when(i == nt - 1)
        def _():
            finish()

    tile = lambda w: pl.BlockSpec((tm, w), lambda i: (i, 0))
    n_in = 9
    return pl.pallas_call(
        body, name="fwd_mix",
        out_shape=tuple(jax.ShapeDtypeStruct(s.shape, s.dtype) for s in late_stacks)
        + (jax.ShapeDtypeStruct((t_len, d), F32), jax.ShapeDtypeStruct((t_len, 5 * cw), F32),
           jax.ShapeDtypeStruct((t_len, d), F32), jax.ShapeDtypeStruct((t_len, d), BF16),
           jax.ShapeDtypeStruct((t_len, 2 * cw), BF16)),
        grid=(nt,),
        in_specs=[tile(d)] + [VMEM_SPEC] * 8 + [HBM_SPEC] * n_late,
        out_specs=(HBM_SPEC,) * n_late + (tile(d), tile(5 * cw), tile(d), tile(d), tile(2 * cw)),
        input_output_aliases={n_in + k: k for k in range(n_late)},
        scratch_shapes=[pltpu.VMEM((tm + HALO, cw), F32), pltpu.VMEM((tm + HALO, cw), F32),
                        _dma_sems(n_cp), _dma_sems(n_cp), _dma_sems(n_cp), _dma_sems(n_cp)],
        compiler_params=_params(dimension_semantics=("arbitrary",)),
    )(x, mod, conv_w, w_in4, w_pool, pool_scale, w_out, ln_g, ln_b, *late_stacks)


def _mlp_fwd_bwd(x1, tgt, mod, w_mi4, w_mo4, ln_g, ln_b, tm):
    t_len, d = x1.shape
    fq = w_mi4.shape[2]
    nt = t_len // tm

    def body(x1_ref, tgt_ref, mod_ref, wi_ref, wo_ref, g_ref, b_ref,
             dx1_ref, da_ref, s_ref, df_ref, h2_ref, st_ref, relu_buf, acc):
        i = pl.program_id(0)

        @pl.when(i == 0)
        def _():
            acc[...] = jnp.zeros(acc.shape, F32)

        x1v = x1_ref[...]
        sh2, sc2, g2 = mod_ref[0:1, 3 * d:4 * d], mod_ref[0:1, 4 * d:5 * d], mod_ref[0:1, 5 * d:6 * d]
        h2 = (x1v * (1.0 + sc2) + sh2).astype(BF16)
        h2_ref[...] = h2
        f = jnp.zeros((tm, d), F32)
        for j in range(N_CHIPS):
            a = jnp.maximum(_dot(h2, wi_ref[j]), 0.0)
            relu_buf[j] = a
            s = (a * a).astype(BF16)
            s_ref[j] = s
            f = f + _dot(s, wo_ref[j])
        xhat, rstd = _layer_norm_fwd(DEEPNORM_ALPHA * x1v + (1.0 + g2) * f)
        gain = g_ref[...]
        err = xhat * gain + b_ref[...] - tgt_ref[...]
        dy = err * (1.0 / d)
        dr2 = _layer_norm_bwd(dy, xhat, rstd, gain)
        df = ((1.0 + g2) * dr2).astype(BF16)
        df_ref[...] = df
        dh2 = jnp.zeros((tm, d), F32)
        for j in range(N_CHIPS):
            ds = _dot_nt(df, wo_ref[j])
            da = (ds * (2.0 * relu_buf[j])).astype(BF16)
            da_ref[j] = da
            dh2 = dh2 + _dot_nt(da, wi_ref[j])
        dx1_ref[...] = DEEPNORM_ALPHA * dr2 + dh2 * (1.0 + sc2)
        acc[0] += _fold8(dy * xhat)
        acc[1] += _fold8(dy)
        acc[2] += _fold8(dr2 * f)
        acc[3] += _fold8(dh2 * x1v)
        acc[4] += _fold8(dh2)
        acc[5] += _fold8(err * err)

        @pl.when(i == nt - 1)
        def _():
            for k in range(5):
                st_ref[k:k + 1, :] = jnp.sum(acc[k], axis=0, keepdims=True)
            loss = jnp.sum(acc[5]) * (0.5 / d)
            st_ref[5:6, :] = jnp.broadcast_to(loss, (1, d))
            st_ref[6:8, :] = jnp.zeros((2, d), F32)

    tile = lambda w: pl.BlockSpec((tm, w), lambda i: (i, 0))
    tile4 = pl.BlockSpec((N_CHIPS, tm, fq), lambda i: (0, i, 0))
    return pl.pallas_call(
        body, name="mlp_fwd_bwd",
        out_shape=(jax.ShapeDtypeStruct((t_len, d), F32),
                   jax.ShapeDtypeStruct((N_CHIPS, t_len, fq), BF16), jax.ShapeDtypeStruct((N_CHIPS, t_len, fq), BF16),
                   jax.ShapeDtypeStruct((t_len, d), BF16), jax.ShapeDtypeStruct((t_len, d), BF16),
                   jax.ShapeDtypeStruct((SUBLANES, d), F32)),
        grid=(nt,),
        in_specs=[tile(d), tile(d)] + [VMEM_SPEC] * 5,
        out_specs=(tile(d), tile4, tile4, tile(d), tile(d), pl.BlockSpec((SUBLANES, d), lambda i: (0, 0))),
        scratch_shapes=[pltpu.VMEM((N_CHIPS, tm, fq), F32), pltpu.VMEM((6, SUBLANES, d), F32)],
        compiler_params=_params(dimension_semantics=("arbitrary",)),
    )(x1, tgt, mod, w_mi4, w_mo4, ln_g, ln_b)


def _mlp_wgrad(h2, da4, s4, df, pos, tt):
    t_len, d = h2.shape
    fq = da4.shape[2]
    nt = t_len // tt
    assert nt >= 3
    rh = (d // 2, fq // 2)
    last = N_CHIPS - 1

    def body(pos_ref, h2_ref, da_ref, s_ref, df_ref, qi_ref, qo_ref, gi_ref, go_ref,
             acc_i, acc_o, rcv_i, rcv_o, s_sem, r_sem):
        j, t = pl.program_id(0), pl.program_id(1)
        slot = lax.rem(j, 2)
        x, y, c = _mesh_pos()
        accs, rcvs, q_refs, g_refs = (acc_i, acc_o), (rcv_i, rcv_o), (qi_ref, qo_ref), (gi_ref, go_ref)

        def to_sibling(a, sl):
            theirs = pl.multiple_of((1 - c) * rh[a], rh[a])
            return pltpu.make_async_remote_copy(accs[a].at[sl].at[pl.ds(theirs, rh[a]), :], rcvs[a].at[sl],
                                                s_sem.at[2 * sl + a], r_sem.at[2 * sl + a],
                                                device_id=(x, y, 1 - c), device_id_type=MESH)

        def emit(chunk, sl):
            for a in range(2):
                to_sibling(a, sl).wait()
                mine = pl.multiple_of(c * rh[a], rh[a])
                q = (accs[a][sl, pl.ds(mine, rh[a]), :] + rcvs[a][sl]).astype(BF16)
                q_refs[a][...] = q

                @pl.when(chunk == pos_ref[1])
                def _():
                    g_refs[a][...] = q

        @pl.when(t == 0)
        def _():
            acc_i[slot] = jnp.zeros((d, fq), F32)
            acc_o[slot] = jnp.zeros((fq, d), F32)

        acc_i[slot] += _dot_tn(h2_ref[...], da_ref[...])
        acc_o[slot] += _dot_tn(s_ref[...], df_ref[...])

        @pl.when((t == 1) & (j >= 1))
        def _():
            emit(j - 1, 1 - slot)

        @pl.when(t == nt - 1)
        def _():
            for a in range(2):
                to_sibling(a, slot).start()

        @pl.when((t == nt - 1) & (j == last))
        def _():
            emit(j, slot)

    q_index = lambda j, t, p: (jnp.where(t == nt - 1, j, jnp.maximum(j - 1, 0)), 0, 0)
    got_index = lambda j, t, p: (p[1], 0, 0)
    blocks = ((None, rh[0], fq), (None, rh[1], d))
    return pl.pallas_call(
        body, name="mlp_wgrad",
        out_shape=tuple(jax.ShapeDtypeStruct((N_CHIPS,) + b[1:], BF16) for b in blocks) * 2,
        grid_spec=pltpu.PrefetchScalarGridSpec(
            num_scalar_prefetch=1, grid=(N_CHIPS, nt),
            in_specs=[pl.BlockSpec((tt, d), lambda j, t, p: (t, 0)),
                      pl.BlockSpec((None, tt, fq), lambda j, t, p: (j, t, 0)),
                      pl.BlockSpec((None, tt, fq), lambda j, t, p: (j, t, 0)),
                      pl.BlockSpec((tt, d), lambda j, t, p: (t, 0))],
            out_specs=tuple(pl.BlockSpec(b, q_index) for b in blocks)
            + tuple(pl.BlockSpec(b, got_index) for b in blocks),
            scratch_shapes=[pltpu.VMEM((2, d, fq), F32), pltpu.VMEM((2, fq, d), F32),
                            pltpu.VMEM((2, rh[0], fq), F32), pltpu.VMEM((2, rh[1], d), F32),
                            _dma_sems(4), _dma_sems(4)]),
        compiler_params=_params(dimension_semantics=("arbitrary", "arbitrary")),
    )(pos, h2, da4, s4, df)


def _bwd_mix(x, mix, dx1, zs, mod, conv_w, w_in4, w_pool, pool_scale, w_out, ln_g, qs, gots, tm):
    t_len, d = x.shape
    cw = w_in4.shape[2]
    dg = cw // N_GROUPS
    nt = t_len // tm
    n_q = len(qs)
    n_cp = n_q * (N_CHIPS - 1)

    def body(x_ref, mix_ref, dx1_ref, zs_ref, mod_ref, cw_ref, win_ref, wp_ref, ps_ref, wout_ref, g_ref, *rest):
        q_refs = rest[:n_q]
        got_refs = rest[2 * n_q:3 * n_q]
        gx_ref, dz_ref, dmix_ref, sd_ref, sc_ref, dwp_ref = rest[3 * n_q:3 * n_q + 6]
        dbuf, qbuf, acc_d, acc_c, acc_p, s_sem, r_sem = rest[3 * n_q + 6:]
        i = pl.program_id(0)
        tile_idx = nt - 1 - i
        start, finish = _chunk_exchange(q_refs, got_refs, s_sem, r_sem)

        @pl.when(i == 0)
        def _():
            start()
            dbuf[tm:tm + HALO, :] = jnp.zeros((HALO, cw), F32)
            qbuf[tm:tm + HALO, :] = jnp.zeros((HALO, cw), F32)
            acc_d[...] = jnp.zeros(acc_d.shape, F32)
            acc_c[...] = jnp.zeros(acc_c.shape, F32)
            acc_p[...] = jnp.zeros(acc_p.shape, F32)

        xv, mixv, dx1v = x_ref[...], mix_ref[...], dx1_ref[...]
        sc1, g1 = mod_ref[0:1, d:2 * d], mod_ref[0:1, 2 * d:3 * d]
        xhat, rstd = _layer_norm_fwd(DEEPNORM_ALPHA * xv + (1.0 + g1) * mixv)
        dr1 = _layer_norm_bwd(dx1v, xhat, rstd, g_ref[...])
        dmix = ((1.0 + g1) * dr1).astype(BF16)
        dmix_ref[...] = dmix
        dcat = _dot_nt(dmix, wout_ref[...])
        dyc, dyp = dcat[:, 0:cw], dcat[:, cw:2 * cw]
        zb, zc, zv = zs_ref[:, 0:cw], zs_ref[:, cw:2 * cw], zs_ref[:, 2 * cw:3 * cw]
        conv, pooled = zs_ref[:, 3 * cw:4 * cw], zs_ref[:, 4 * cw:5 * cw]
        dzb = dyc * conv
        dcv = dyc * zb
        dbuf[0:tm, :] = dcv
        d1 = dbuf[pl.ds(1, tm), :]
        d2 = dbuf[pl.ds(2, tm), :]
        du = cw_ref[2:3, :] * dcv + cw_ref[1:2, :] * d1 + cw_ref[0:1, :] * d2
        dbuf[tm:tm + HALO, :] = dbuf[0:HALO, :]
        u = zc * zv
        acc_c[0] += _fold8(u * d2)
        acc_c[1] += _fold8(u * d1)
        acc_c[2] += _fold8(u * dcv)
        dzc = du * zv
        dzv = du * zc
        dpw = dyp * ps_ref[...]
        dps, pms = [], []
        for gi, win in enumerate(POOL_WINDOWS):
            sl = slice(gi * dg, (gi + 1) * dg)
            p_g = pooled[:, sl].astype(BF16)
            dpw_g = dpw[:, sl].astype(BF16)
            pms.append(_dot(p_g, wp_ref[gi]))
            acc_p[gi] += _dot_tn(p_g, dpw_g)
            dp_g = _dot_nt(dpw_g, wp_ref[gi])
            dps.append(dp_g)
            qbuf[0:tm, sl] = dp_g * _inv_count(tile_idx, tm, win)
        acc_c[3] += _fold8(dyp * jnp.concatenate(pms, axis=1))
        dzps = []
        for gi, win in enumerate(POOL_WINDOWS):
            sl = slice(gi * dg, (gi + 1) * dg)
            acc = qbuf[0:tm, sl]
            for s in range(1, win):
                acc = acc + qbuf[pl.ds(s, tm), sl]
            dzps.append(acc - dps[gi])
        qbuf[tm:tm + HALO, :] = qbuf[0:HALO, :]
        dz = [dzb.astype(BF16), dzc.astype(BF16), dzv.astype(BF16), jnp.concatenate(dzps, axis=1).astype(BF16)]
        dh1 = jnp.zeros((tm, d), F32)
        for j in range(N_CHIPS):
            dz_ref[j] = dz[j]
            dh1 = dh1 + _dot_nt(dz[j], win_ref[j])
        gx_ref[...] = DEEPNORM_ALPHA * dr1 + dh1 * (1.0 + sc1)
        acc_d[0] += _fold8(dx1v * xhat)
        acc_d[1] += _fold8(dx1v)
        acc_d[2] += _fold8(dr1 * mixv)
        acc_d[3] += _fold8(dh1 * xv)
        acc_d[4] += _fold8(dh1)

        @pl.when(i == nt - 1)
        def _():
            for k in range(5):
                sd_ref[k:k + 1, :] = jnp.sum(acc_d[k], axis=0, keepdims=True)
            sd_ref[5:8, :] = jnp.zeros((3, d), F32)
            for k in range(4):
                sc_ref[k:k + 1, :] = jnp.sum(acc_c[k], axis=0, keepdims=True)
            sc_ref[4:8, :] = jnp.zeros((4, cw), F32)
            dwp_ref[...] = acc_p[...]
            finish()

    rtile = lambda w: pl.BlockSpec((tm, w), lambda i: (nt - 1 - i, 0))
    whole = lambda shape: pl.BlockSpec(shape, lambda i: tuple(0 for _ in shape))
    n_in = 11
    return pl.pallas_call(
        body, name="bwd_mix",
        out_shape=tuple(jax.ShapeDtypeStruct(g.shape, g.dtype) for g in gots)
        + (jax.ShapeDtypeStruct((t_len, d), F32), jax.ShapeDtypeStruct((N_CHIPS, t_len, cw), BF16),
           jax.ShapeDtypeStruct((t_len, d), BF16), jax.ShapeDtypeStruct((SUBLANES, d), F32),
           jax.ShapeDtypeStruct((SUBLANES, cw), F32), jax.ShapeDtypeStruct((N_GROUPS, dg, dg), F32)),
        grid=(nt,),
        in_specs=[rtile(d), rtile(d), rtile(d), rtile(5 * cw)] + [VMEM_SPEC] * 7 + [HBM_SPEC] * (2 * n_q),
        out_specs=(HBM_SPEC,) * n_q
        + (rtile(d), pl.BlockSpec((N_CHIPS, tm, cw), lambda i: (0, nt - 1 - i, 0)), rtile(d),
           whole((SUBLANES, d)), whole((SUBLANES, cw)), whole((N_GROUPS, dg, dg))),
        input_output_aliases={n_in + n_q + k: k for k in range(n_q)},
        scratch_shapes=[pltpu.VMEM((tm + HALO, cw), F32), pltpu.VMEM((tm + HALO, cw), F32),
                        pltpu.VMEM((5, SUBLANES, d), F32), pltpu.VMEM((4, SUBLANES, cw), F32),
                        pltpu.VMEM((N_GROUPS, dg, dg), F32), _dma_sems(n_cp), _dma_sems(n_cp)],
        compiler_params=_params(dimension_semantics=("arbitrary",)),
    )(x, mix, dx1, zs, mod, conv_w, w_in4, w_pool, pool_scale, w_out, ln_g, *qs, *gots)


def _mix_wgrad(h1, dz4, cat, dmix, packed, tt):
    t_len, d = h1.shape
    cw = dz4.shape[2]
    nt = t_len // tt
    ro = (2 * cw) // N_CHIPS
    rh = (d // 2, ro // 2)
    r_small = packed.shape[0]

    def body(h1_ref, dz_ref, cat_ref, dmix_ref, p_ref, qi_ref, qo_ref, gi_ref, go_ref, sum_ref, all_ref,
             acc_i, acc_o, rcv_i, rcv_o, all_buf, s_sem, r_sem, s_small, r_small_sem):
        t = pl.program_id(0)
        x, y, c = _mesh_pos()
        chip = 2 * x + y
        small_start, small_finish = _small_allreduce_steps(p_ref, sum_ref, all_ref, all_buf, s_small, r_small_sem)
        accs, rcvs, q_refs, g_refs = (acc_i, acc_o), (rcv_i, rcv_o), (qi_ref, qo_ref), (gi_ref, go_ref)

        @pl.when(t == 0)
        def _():
            small_start()
            acc_i[...] = jnp.zeros(acc_i.shape, F32)
            acc_o[...] = jnp.zeros(acc_o.shape, F32)

        h1v, dmixv = h1_ref[...], dmix_ref[...]
        for j in range(N_CHIPS):
            acc_i[j] += _dot_tn(h1v, dz_ref[j])
            acc_o[j] += _dot_tn(cat_ref[:, j * ro:(j + 1) * ro], dmixv)

        @pl.when(t == nt - 1)
        def _():
            cps = []
            for a in range(2):
                theirs = pl.multiple_of((1 - c) * rh[a], rh[a])
                cp = pltpu.make_async_remote_copy(accs[a].at[:, pl.ds(theirs, rh[a]), :], rcvs[a],
                                                  s_sem.at[a], r_sem.at[a],
                                                  device_id=(x, y, 1 - c), device_id_type=MESH)
                cp.start()
                cps.append(cp)
            for a in range(2):
                cps[a].wait()
                mine = pl.multiple_of(c * rh[a], rh[a])
                for j in range(N_CHIPS):
                    q_refs[a][j] = (accs[a][j, pl.ds(mine, rh[a]), :] + rcvs[a][j]).astype(BF16)
                g_refs[a][chip] = q_refs[a][chip]
            small_finish()

    stacks = ((N_CHIPS, rh[0], cw), (N_CHIPS, rh[1], d))
    return pl.pallas_call(
        body, name="mix_wgrad",
        out_shape=tuple(jax.ShapeDtypeStruct(s, BF16) for s in stacks) * 2
        + (jax.ShapeDtypeStruct((r_small, LANES), F32), jax.ShapeDtypeStruct((N_DEV, r_small, LANES), F32)),
        grid=(nt,),
        in_specs=[pl.BlockSpec((tt, d), lambda t: (t, 0)), pl.BlockSpec((N_CHIPS, tt, cw), lambda t: (0, t, 0)),
                  pl.BlockSpec((tt, 2 * cw), lambda t: (t, 0)), pl.BlockSpec((tt, d), lambda t: (t, 0)), VMEM_SPEC],
        out_specs=(VMEM_SPEC,) * 6,
        scratch_shapes=[pltpu.VMEM((N_CHIPS, d, cw), F32), pltpu.VMEM((N_CHIPS, ro, d), F32),
                        pltpu.VMEM(stacks[0], F32), pltpu.VMEM(stacks[1], F32),
                        pltpu.VMEM((N_DEV, r_small, LANES), F32),
                        _dma_sems(2), _dma_sems(2), _dma_sems(N_DEV - 1), _dma_sems(N_DEV - 1)],
        compiler_params=_params(dimension_semantics=("arbitrary",)),
    )(h1, dz4, cat, dmix, packed)


def _rows128(v):
    v = v.reshape(-1, LANES)
    pad = (-v.shape[0]) % SUBLANES
    if pad:
        v = jnp.concatenate([v, jnp.zeros((pad, LANES), v.dtype)], axis=0)
    return v


def kernel(x, c, w_ada, b_ada, w_in, conv_w, w_pool, pool_scale, w_out, ln1_g, ln1_b, w_mlp_in, w_mlp_out, ln2_g, ln2_b, loss_target, m_w_ada, m_b_ada, m_w_in, m_conv_w, m_w_pool, m_pool_scale, m_w_out, m_ln1_g, m_ln1_b, m_w_mlp_in, m_w_mlp_out, m_ln2_g, m_ln2_b, v_w_ada, v_b_ada, v_w_in, v_conv_w, v_w_pool, v_pool_scale, v_w_out, v_ln1_g, v_ln1_b, v_w_mlp_in, v_w_mlp_out, v_ln2_g, v_ln2_b):
    t_len, d = x.shape[1], x.shape[2]
    cw = w_in.shape[2]
    cq = conv_w.shape[2]
    dg = w_pool.shape[2]
    assert cw == N_GROUPS * dg and cq * N_CHIPS == cw and LANES % cq == 0
    tm = min(TOKEN_TILE, t_len)
    tt = min(WGRAD_TILE, t_len // 4)
    chip = 2 * lax.axis_index("x") + lax.axis_index("y")
    pos = jnp.stack([lax.axis_index("c"), chip]).astype(jnp.int32)

    x2, tgt = x[0], loss_target[0]
    big_w = [w_in[0], w_out[0], w_mlp_in[0], w_mlp_out[0]]
    big_m = [m_w_in[0], m_w_out[0], m_w_mlp_in[0], m_w_mlp_out[0]]
    big_v = [v_w_in[0], v_w_out[0], v_w_mlp_in[0], v_w_mlp_out[0]]
    names = ["w_in", "w_out", "w_mlp_in", "w_mlp_out"]

    mod, cond_all, conv_full, w_in4, w_out4, w_mi_own, w_mo_own = _setup_exchange(
        c, w_ada[0], b_ada, conv_w[0], big_w, 2)
    w_out_full = w_out4.reshape(2 * cw, d)
    w_pool_bf = w_pool[0].astype(BF16)

    w_mi4, w_mo4, x1, zs, mix, h1, cat = _fwd_mix(x2, mod, conv_full, w_in4, w_pool_bf, pool_scale, w_out_full,
                                                  ln1_g, ln1_b, [w_mi_own, w_mo_own], tm)
    dx1, da4, s4, df, h2, st2 = _mlp_fwd_bwd(x1, tgt, mod, w_mi4, w_mo4, ln2_g, ln2_b, tm)
    mlp_qi, mlp_qo, mlp_gi, mlp_go = _mlp_wgrad(h2, da4, s4, df, pos, tt)
    out = _bwd_mix(x2, mix, dx1, zs, mod, conv_full, w_in4, w_pool_bf, pool_scale, w_out_full, ln1_g,
                   [mlp_qi, mlp_qo], [mlp_gi, mlp_go], tm)
    mlp_got, (grad_x, dz4, dmix, st1, stc, dw_pool) = out[:2], out[2:]

    conv_part = jnp.transpose(stc[0:3].reshape(3, N_CHIPS, cq), (1, 0, 2))
    conv_part = jnp.pad(conv_part, ((0, 0), (0, SUBLANES - 3), (0, LANES - cq)))
    dmod = jnp.concatenate([st1[4], st1[3], st1[2], st2[4], st2[3], st2[2]])
    pieces = [st2[5, 0:LANES], st1[0], st1[1], st2[0], st2[1], stc[3], conv_part, dw_pool, dmod]
    rows = [_rows128(p) for p in pieces]
    offs = [0]
    for r in rows:
        offs.append(offs[-1] + r.shape[0])
    mix_qi, mix_qo, mix_gi, mix_go, total, gathered = _mix_wgrad(h1, dz4, cat, dmix,
                                                                 jnp.concatenate(rows, axis=0), tt)

    def sec(k, shape):
        n = 1
        for s in shape:
            n *= s
        return total[offs[k]:offs[k] + n // LANES].reshape(shape)

    loss = total[0, 0]
    g_ln1_g, g_ln1_b, g_ln2_g, g_ln2_b = sec(1, (1, d)), sec(2, (1, d)), sec(3, (1, d)), sec(4, (1, d))
    g_pool_scale = sec(5, (1, cw))
    conv_blocks = total[offs[6]:offs[7]].reshape(N_CHIPS, SUBLANES, LANES)
    g_conv = lax.dynamic_index_in_dim(conv_blocks, chip, axis=0, keepdims=False)[0:3, 0:cq]
    g_w_pool = sec(7, (N_GROUPS, dg, dg))
    g_b_ada = sec(8, (1, 6 * d))
    dmod_all = gathered[:, offs[8]:offs[9], :].reshape(N_DEV, 6 * d)
    na = w_ada.shape[2]
    dmod_mine = lax.dynamic_slice_in_dim(dmod_all, chip * na, na, axis=1)

    mlp_halves = [_sum_chips(g, pos, "rs_total_" + n) for g, n in zip(mlp_got, names[2:])]
    mlp_g = _sibling_join(mlp_halves, "mlp")
    tail = _adamw_tail(mlp_g, big_w[2:], big_m[2:], big_v[2:], cond_all.T, dmod_mine,
                       [w_ada[0], m_w_ada[0], v_w_ada[0]], [mix_qi, mix_qo], [mix_gi, mix_go])
    mix_got, mlp_out, (g_ada, d_ada, nm_ada, nv_ada) = tail[:2], [tail[2:5], tail[5:8]], tail[8:12]
    mix_halves = [_sum_chips(g, pos, "rs_total_" + n) for g, n in zip(mix_got, names[:2])]
    mix_g = _sibling_join(mix_halves, "mix")
    mix_out = [_adamw_big(g, w, m, v, "adamw_" + n)
               for g, w, m, v, n in zip(mix_g, big_w[:2], big_m[:2], big_v[:2], names[:2])]
    big_g = list(mix_g) + list(mlp_g)
    big_out = mix_out + mlp_out

    small_g = [g_b_ada, g_conv, g_w_pool.reshape(-1, dg), g_pool_scale, g_ln1_g, g_ln1_b, g_ln2_g, g_ln2_b]
    small_w = [b_ada, conv_w[0], w_pool.reshape(-1, dg), pool_scale, ln1_g, ln1_b, ln2_g, ln2_b]
    small_m = [m_b_ada, m_conv_w[0], m_w_pool.reshape(-1, dg), m_pool_scale, m_ln1_g, m_ln1_b, m_ln2_g, m_ln2_b]
    small_v = [v_b_ada, v_conv_w[0], v_w_pool.reshape(-1, dg), v_pool_scale, v_ln1_g, v_ln1_b, v_ln2_g, v_ln2_b]
    sm = _adamw_small(small_g, small_w, small_m, small_v)
    ns = len(small_g)
    s_delta, s_m, s_v = sm[:ns], sm[ns:2 * ns], sm[2 * ns:]

    def assemble(ada, small, big):
        return [ada[None], small[0], big[0][None], small[1][None], small[2].reshape(w_pool.shape), small[3],
                big[1][None], small[4], small[5], big[2][None], big[3][None], small[6], small[7]]

    grads = assemble(g_ada, small_g, big_g)
    deltas = assemble(d_ada, s_delta, [o[0] for o in big_out])
    new_m = assemble(nm_ada, s_m, [o[1] for o in big_out])
    new_v = assemble(nv_ada, s_v, [o[2] for o in big_out])
    return (loss, grad_x[None], *grads, *deltas, *new_m, *new_v)
```

```python
import jax
import jax.numpy as jnp
from jax import lax
from jax.experimental import pallas as pl
from jax.experimental.pallas import tpu as pltpu

F32 = jnp.float32
BF16 = jnp.bfloat16
MESH = pl.DeviceIdType.MESH

LN_EPS = 1e-5
DEEPNORM_ALPHA = 2.0 ** 0.25
POOL_WINDOWS = (2, 4, 8, 16)
N_GROUPS = len(POOL_WINDOWS)
HALO = 16
N_CHIPS = 4
N_DEV = 8
LANES = 128
SUBLANES = 8
VMEM_LIMIT = 56 * 1024 * 1024
TOKEN_TILE = 256
WGRAD_TILE = 512

ADAM_LR = 0.001
ADAM_B1 = 0.9
ADAM_B2 = 0.999
ADAM_EPS = 1e-08
ADAM_WD = 0.01
ADAM_STEP = 10

VMEM_SPEC = pl.BlockSpec(memory_space=pltpu.VMEM)
HBM_SPEC = pl.BlockSpec(memory_space=pltpu.HBM)


def _dot(a, b):
    return jnp.dot(a, b, preferred_element_type=F32)


def _dot_nt(a, b):
    return lax.dot_general(a, b, (((1,), (1,)), ((), ())), preferred_element_type=F32)


def _dot_tn(a, b):
    return lax.dot_general(a, b, (((0,), (0,)), ((), ())), preferred_element_type=F32)


def _fold8(v):
    r, n = v.shape
    return jnp.sum(v.reshape(r // SUBLANES, SUBLANES, n), axis=0)


def _mesh_pos():
    return lax.axis_index("x"), lax.axis_index("y"), lax.axis_index("c")


def _flip(v, bit):
    return 1 - v if bit else v


def _params(**kw):
    return pltpu.CompilerParams(vmem_limit_bytes=VMEM_LIMIT, **kw)


def _dma_sems(n):
    return pltpu.SemaphoreType.DMA((n,))


def _stack_gather(stacks, rows, s_ici, r_ici, s_d2d, r_d2d):
    x, y, c = _mesh_pos()
    chip = 2 * x + y
    pairs = [(i, m) for i in range(len(stacks)) for m in range(1, N_CHIPS)]

    def blk(i, slot, which):
        rh = rows[i] // 2
        return stacks[i].at[slot].at[pl.ds(which * rh, rh), :]

    def other(m):
        return 2 * _flip(x, m & 2) + _flip(y, m & 1)

    def ici(i, m, slot, to):
        k = i * (N_CHIPS - 1) + m - 1
        return pltpu.make_async_remote_copy(blk(i, slot, c), blk(i, slot, c), s_ici.at[k], r_ici.at[k],
                                            device_id=to, device_id_type=MESH)

    def d2d(i, m, which, to):
        k = i * (N_CHIPS - 1) + m - 1
        return pltpu.make_async_remote_copy(blk(i, other(m), which), blk(i, other(m), which),
                                            s_d2d.at[k], r_d2d.at[k], device_id=to, device_id_type=MESH)

    def start():
        for i, m in pairs:
            ici(i, m, chip, (_flip(x, m & 2), _flip(y, m & 1), c)).start()

    def forward():
        for i, m in pairs:
            ici(i, m, other(m), (x, y, c)).wait_recv()
            d2d(i, m, c, (x, y, 1 - c)).start()

    def finish():
        for i, m in pairs:
            d2d(i, m, 1 - c, (x, y, c)).wait_recv()
        for i, m in pairs:
            ici(i, m, chip, (x, y, c)).wait_send()
            d2d(i, m, c, (x, y, c)).wait_send()

    return start, forward, finish


def _chunk_exchange(qs, gots, s_sem, r_sem):
    x, y, c = _mesh_pos()
    chip = 2 * x + y
    pairs = [(i, m) for i in range(len(qs)) for m in range(1, N_CHIPS)]

    def other(m):
        return 2 * _flip(x, m & 2) + _flip(y, m & 1)

    def send(i, m, to):
        k = i * (N_CHIPS - 1) + m - 1
        return pltpu.make_async_remote_copy(qs[i].at[other(m)], gots[i].at[chip], s_sem.at[k], r_sem.at[k],
                                            device_id=to, device_id_type=MESH)

    def arrival(i, m):
        k = i * (N_CHIPS - 1) + m - 1
        return pltpu.make_async_remote_copy(qs[i].at[other(m)], gots[i].at[other(m)], s_sem.at[k], r_sem.at[k],
                                            device_id=(x, y, c), device_id_type=MESH)

    def start():
        for i, m in pairs:
            send(i, m, (_flip(x, m & 2), _flip(y, m & 1), c)).start()

    def finish():
        for i, m in pairs:
            arrival(i, m).wait_recv()
        for i, m in pairs:
            send(i, m, (x, y, c)).wait_send()

    return start, finish


def _mod_scratch(d, na, cq):
    return [pltpu.VMEM((SUBLANES, d), F32), pltpu.VMEM((N_DEV, SUBLANES, d), F32),
            pltpu.VMEM((N_CHIPS, SUBLANES, na), F32),
            pltpu.VMEM((SUBLANES, cq), F32), pltpu.VMEM((N_CHIPS, SUBLANES, cq), F32),
            _dma_sems(N_DEV - 1), _dma_sems(N_DEV - 1),
            _dma_sems(N_CHIPS - 1), _dma_sems(N_CHIPS - 1), _dma_sems(N_CHIPS - 1), _dma_sems(N_CHIPS - 1)]


def _mod_steps(d, na, cq):
    def steps(c_ref, w_ref, b_ref, cw_ref, mod_ref, cond_ref, conv_ref,
              csend, cbuf, mbuf, cvsend, cvbuf, s1, r1, s2, r2, s3, r3):
        x, y, c = _mesh_pos()
        me = 4 * x + 2 * y + c
        chip = 2 * x + y
        csend[...] = jnp.broadcast_to(c_ref[...], (SUBLANES, d))
        cbuf[me] = csend[...]
        first = []
        for m in range(1, N_DEV):
            peer = (_flip(x, m & 4), _flip(y, m & 2), _flip(c, m & 1))
            cp = pltpu.make_async_remote_copy(csend, cbuf.at[me], s1.at[m - 1], r1.at[m - 1],
                                              device_id=peer, device_id_type=MESH)
            cp.start()
            first.append(cp)
        cvsend[...] = jnp.zeros((SUBLANES, cq), F32)
        cvsend[0:3, :] = cw_ref[...]
        cvbuf[chip] = cvsend[...]
        for m in range(1, N_DEV):
            src = 4 * _flip(x, m & 4) + 2 * _flip(y, m & 2) + _flip(c, m & 1)
            pltpu.make_async_remote_copy(csend, cbuf.at[src], s1.at[m - 1], r1.at[m - 1],
                                         device_id=(x, y, c), device_id_type=MESH).wait_recv()
        rows = lax.broadcasted_iota(jnp.int32, (SUBLANES, d), 0)
        call = jnp.zeros((SUBLANES, d), F32)
        for b in range(N_DEV):
            call = jnp.where(rows == b, cbuf[b], call)
        cond = call * jax.nn.sigmoid(call)
        cond_ref[...] = cond
        part = jnp.dot(cond, w_ref[...], preferred_element_type=F32, precision=lax.Precision.HIGHEST)
        mbuf[chip] = part
        second = []
        for m in range(1, N_CHIPS):
            peer = (_flip(x, m & 2), _flip(y, m & 1), c)
            cp = pltpu.make_async_remote_copy(mbuf.at[chip], mbuf.at[chip], s2.at[m - 1], r2.at[m - 1],
                                              device_id=peer, device_id_type=MESH)
            cp.start()
            second.append(cp)
            cp = pltpu.make_async_remote_copy(cvsend, cvbuf.at[chip], s3.at[m - 1], r3.at[m - 1],
                                              device_id=peer, device_id_type=MESH)
            cp.start()
            second.append(cp)
        for m in range(1, N_CHIPS):
            src = 2 * _flip(x, m & 2) + _flip(y, m & 1)
            pltpu.make_async_remote_copy(mbuf.at[src], mbuf.at[src], s2.at[m - 1], r2.at[m - 1],
                                         device_id=(x, y, c), device_id_type=MESH).wait_recv()
            pltpu.make_async_remote_copy(cvsend, cvbuf.at[src], s3.at[m - 1], r3.at[m - 1],
                                         device_id=(x, y, c), device_id_type=MESH).wait_recv()
        for cp in first + second:
            cp.wait_send()
        rows_n = lax.broadcasted_iota(jnp.int32, (SUBLANES, na), 0)
        for k in range(N_CHIPS):
            mine = jnp.sum(jnp.where(rows_n == me, mbuf[k], 0.0), axis=0, keepdims=True)
            mod_ref[:, k * na:(k + 1) * na] = jnp.broadcast_to(mine + b_ref[:, k * na:(k + 1) * na], (SUBLANES, na))
            conv_ref[:, k * cq:(k + 1) * cq] = cvbuf[k]

    return steps


def _setup_exchange(c_row, w_ada_s, b_ada, conv_w_s, shards, n_now):
    n = len(shards)
    n_cp = n_now * (N_CHIPS - 1)
    d, na, cq = c_row.shape[1], w_ada_s.shape[1], conv_w_s.shape[1]
    mod_scratch = _mod_scratch(d, na, cq)
    mod_steps = _mod_steps(d, na, cq)

    def body(*refs):
        mod_in, ins = refs[:4], refs[4:4 + n]
        mod_out, outs = refs[4 + n:7 + n], refs[7 + n:7 + 2 * n]
        bufs = refs[7 + 2 * n:7 + 3 * n]
        mod_scr = refs[7 + 3 * n:7 + 3 * n + len(mod_scratch)]
        s_ici, r_ici, s_d2d, r_d2d, s_loc = refs[7 + 3 * n + len(mod_scratch):]
        x, y, _ = _mesh_pos()
        chip = 2 * x + y
        local = []
        for i in range(n):
            bufs[i][...] = ins[i][...].astype(BF16)
            cp = pltpu.make_async_copy(bufs[i], outs[i].at[chip], s_loc.at[i])
            cp.start()
            local.append(cp)
        for cp in local:
            cp.wait()
        start, forward, finish = _stack_gather(outs[:n_now], [s.shape[0] for s in shards[:n_now]],
                                               s_ici, r_ici, s_d2d, r_d2d)
        start()
        mod_steps(*mod_in, *mod_out, *mod_scr)
        forward()
        finish()

    return pl.pallas_call(
        body, name="setup_exchange",
        out_shape=(jax.ShapeDtypeStruct((SUBLANES, N_CHIPS * na), F32), jax.ShapeDtypeStruct((SUBLANES, d), F32),
                   jax.ShapeDtypeStruct((SUBLANES, N_CHIPS * cq), F32))
        + tuple(jax.ShapeDtypeStruct((N_CHIPS,) + s.shape, BF16) for s in shards),
        in_specs=[VMEM_SPEC] * (4 + n), out_specs=(VMEM_SPEC,) * 3 + (HBM_SPEC,) * n,
        scratch_shapes=[pltpu.VMEM(s.shape, BF16) for s in shards] + mod_scratch
        + [_dma_sems(n_cp), _dma_sems(n_cp), _dma_sems(n_cp), _dma_sems(n_cp), _dma_sems(n)],
        compiler_params=_params(),
    )(c_row, w_ada_s, b_ada, conv_w_s, *shards)


def _sibling_join(gs, tag):
    n = len(gs)

    def body(*refs):
        outs = refs[n:2 * n]
        s_sem, r_sem = refs[2 * n:]
        x, y, c = _mesh_pos()
        cps = []
        for i in range(n):
            rh = gs[i].shape[0] // 2
            mine = outs[i].at[pl.ds(c * rh, rh), :]
            cp = pltpu.make_async_remote_copy(mine, mine, s_sem.at[i], r_sem.at[i],
                                              device_id=(x, y, 1 - c), device_id_type=MESH)
            cp.start()
            cps.append(cp)
        for i in range(n):
            rh = gs[i].shape[0] // 2
            theirs = outs[i].at[pl.ds((1 - c) * rh, rh), :]
            pltpu.make_async_remote_copy(theirs, theirs, s_sem.at[i], r_sem.at[i],
                                         device_id=(x, y, c), device_id_type=MESH).wait_recv()
        for cp in cps:
            cp.wait_send()

    return pl.pallas_call(
        body, name="rs_sibling_join_" + tag,
        out_shape=tuple(jax.ShapeDtypeStruct(g.shape, g.dtype) for g in gs),
        in_specs=[HBM_SPEC] * n, out_specs=(HBM_SPEC,) * n,
        input_output_aliases={i: i for i in range(n)},
        scratch_shapes=[_dma_sems(n), _dma_sems(n)],
        compiler_params=_params(),
    )(*gs)


SEM_SPEC = pl.BlockSpec(memory_space=pltpu.SEMAPHORE)
SIDE_EFFECT = pltpu.SideEffectType.DATAFLOW_SIDE_EFFECTING


def _exchange_start(qs, gots, tag):
    n = len(qs)
    n_cp = n * (N_CHIPS - 1)

    def body(*refs):
        send_sems, recv_sems = refs[2 * n], refs[2 * n + 1]
        q_thru, got_thru = refs[2 * n + 2:3 * n + 2], refs[3 * n + 2:4 * n + 2]
        token = refs[4 * n + 2]
        start, _ = _chunk_exchange(q_thru, got_thru, send_sems, recv_sems)
        start()
        token[...] = jnp.zeros_like(token)

    through = [pltpu.with_memory_space_constraint(a, pltpu.HBM) for a in list(qs) + list(gots)]
    return pl.pallas_call(
        body, name="exchange_start_" + tag,
        out_shape=(pltpu.SemaphoreType.DMA((n_cp,)), pltpu.SemaphoreType.DMA((n_cp,)))
        + tuple(pltpu.HBM(a.shape, a.dtype) for a in through) + (jax.ShapeDtypeStruct((SUBLANES, LANES), F32),),
        in_specs=[HBM_SPEC] * (2 * n), out_specs=(SEM_SPEC, SEM_SPEC) + (HBM_SPEC,) * (2 * n) + (VMEM_SPEC,),
        input_output_aliases={k: 2 + k for k in range(2 * n)},
        compiler_params=pltpu.CompilerParams(has_side_effects=SIDE_EFFECT),
    )(*through)


def _exchange_wait(started, after, tag):
    send_sems, recv_sems = started[0], started[1]
    n = (len(started) - 3) // 2
    q_thru, got_thru = started[2:2 + n], started[2 + n:2 + 2 * n]

    def body(*refs):
        q_refs, got_refs = refs[:n], refs[n:2 * n]
        s_sem, r_sem = refs[2 * n], refs[2 * n + 1]
        _, finish = _chunk_exchange(q_refs, got_refs, s_sem, r_sem)
        finish()

    return pl.pallas_call(
        body, name="exchange_wait_" + tag,
        out_shape=tuple(pltpu.HBM(a.shape, a.dtype) for a in list(q_thru) + list(got_thru)),
        in_specs=[HBM_SPEC] * (2 * n) + [SEM_SPEC, SEM_SPEC, pl.BlockSpec(memory_space=pl.ANY)],
        out_specs=(HBM_SPEC,) * (2 * n),
        input_output_aliases={k: k for k in range(2 * n)},
        compiler_params=pltpu.CompilerParams(has_side_effects=SIDE_EFFECT),
    )(*q_thru, *got_thru, send_sems, recv_sems, after)[n:]


def _small_allreduce_steps(p_ref, sum_ref, all_out_ref, all_ref, s_sem, r_sem):
    x, y, c = _mesh_pos()
    me = 4 * x + 2 * y + c

    def send(m, to):
        return pltpu.make_async_remote_copy(p_ref, all_ref.at[me], s_sem.at[m - 1], r_sem.at[m - 1],
                                            device_id=to, device_id_type=MESH)

    def start():
        all_ref[me] = p_ref[...]
        for m in range(1, N_DEV):
            send(m, (_flip(x, m & 4), _flip(y, m & 2), _flip(c, m & 1))).start()

    def finish():
        for m in range(1, N_DEV):
            src = 4 * _flip(x, m & 4) + 2 * _flip(y, m & 2) + _flip(c, m & 1)
            pltpu.make_async_remote_copy(p_ref, all_ref.at[src], s_sem.at[m - 1], r_sem.at[m - 1],
                                         device_id=(x, y, c), device_id_type=MESH).wait_recv()
        for m in range(1, N_DEV):
            send(m, (x, y, c)).wait_send()
        sum_ref[...] = (((all_ref[0] + all_ref[1]) + (all_ref[2] + all_ref[3]))
                        + ((all_ref[4] + all_ref[5]) + (all_ref[6] + all_ref[7])))
        all_out_ref[...] = all_ref[...]

    return start, finish


def _row_block(rows, cap=256):
    rb = min(rows, cap)
    assert rows % rb == 0
    return rb


def _sum_chips(got, pos, after, name):
    _, rh, cc = got.shape
    rb = _row_block(rh)
    nb = rh // rb

    def body(pos_ref, a_ref, after_ref, o_ref):
        a = a_ref[...].astype(F32)
        o_ref[...] = (a[0] + a[1]) + (a[2] + a[3])

    return pl.pallas_call(
        body, name=name,
        out_shape=jax.ShapeDtypeStruct((2 * rh, cc), F32),
        grid_spec=pltpu.PrefetchScalarGridSpec(
            num_scalar_prefetch=1, grid=(nb,),
            in_specs=[pl.BlockSpec((N_CHIPS, rb, cc), lambda i, p: (0, i, 0)),
                      pl.BlockSpec(memory_space=pl.ANY)],
            out_specs=pl.BlockSpec((rb, cc), lambda i, p: (p[0] * nb + i, 0))),
        compiler_params=_params(dimension_semantics=("arbitrary",)),
    )(pos, got, after)


def _adamw_math(w, g, m, v):
    m = ADAM_B1 * m + (1.0 - ADAM_B1) * g
    v = ADAM_B2 * v + (1.0 - ADAM_B2) * jnp.square(g)
    m_hat = m / (1.0 - ADAM_B1 ** ADAM_STEP)
    v_hat = v / (1.0 - ADAM_B2 ** ADAM_STEP)
    delta = -ADAM_LR * (m_hat / (jnp.sqrt(v_hat) + ADAM_EPS) + ADAM_WD * w)
    return delta, m, v


def _adamw_big(g, w, m, v, name):
    r, cc = w.shape
    rb = _row_block(r)

    def body(g_ref, w_ref, m_ref, v_ref, d_ref, mo_ref, vo_ref):
        d, mn, vn = _adamw_math(w_ref[...], g_ref[...], m_ref[...], v_ref[...])
        d_ref[...] = d
        mo_ref[...] = mn
        vo_ref[...] = vn

    spec = pl.BlockSpec((rb, cc), lambda i: (i, 0))
    return pl.pallas_call(
        body, name=name,
        out_shape=(jax.ShapeDtypeStruct((r, cc), F32),) * 3,
        grid=(r // rb,), in_specs=[spec] * 4, out_specs=(spec,) * 3,
        compiler_params=_params(dimension_semantics=("arbitrary",)),
    )(g, w, m, v)


def _adamw_ada(cond_t, dmod, w, m, v):
    d, na = w.shape
    cb = min(na, 256)
    assert na % cb == 0

    def body(ct_ref, dm_ref, w_ref, m_ref, v_ref, g_ref, d_ref, mo_ref, vo_ref):
        g = jnp.dot(ct_ref[...], dm_ref[...], preferred_element_type=F32, precision=lax.Precision.HIGHEST)
        dl, mn, vn = _adamw_math(w_ref[...], g, m_ref[...], v_ref[...])
        g_ref[...] = g
        d_ref[...] = dl
        mo_ref[...] = mn
        vo_ref[...] = vn

    spec = pl.BlockSpec((d, cb), lambda i: (0, i))
    return pl.pallas_call(
        body, name="adamw_w_ada",
        out_shape=(jax.ShapeDtypeStruct((d, na), F32),) * 4,
        grid=(na // cb,),
        in_specs=[pl.BlockSpec((d, N_DEV), lambda i: (0, 0)), pl.BlockSpec((N_DEV, cb), lambda i: (0, i)),
                  spec, spec, spec],
        out_specs=(spec,) * 4,
        compiler_params=_params(dimension_semantics=("arbitrary",)),
    )(cond_t, dmod, w, m, v)


def _adamw_small(gs, ws, ms, vs):
    n = len(gs)

    def body(*refs):
        ins, outs = refs[:4 * n], refs[4 * n:]
        for i in range(n):
            d, mn, vn = _adamw_math(ins[n + i][...], ins[i][...], ins[2 * n + i][...], ins[3 * n + i][...])
            outs[i][...] = d
            outs[n + i][...] = mn
            outs[2 * n + i][...] = vn

    shapes = tuple(jax.ShapeDtypeStruct(w.shape, F32) for w in ws)
    return pl.pallas_call(
        body, name="adamw_small",
        out_shape=shapes * 3,
        in_specs=[VMEM_SPEC] * (4 * n), out_specs=(VMEM_SPEC,) * (3 * n),
        compiler_params=_params(),
    )(*gs, *ws, *ms, *vs)


def _layer_norm_fwd(r):
    mu = jnp.mean(r, axis=-1, keepdims=True)
    xc = r - mu
    var = jnp.mean(jnp.square(xc), axis=-1, keepdims=True)
    rstd = lax.rsqrt(var + LN_EPS)
    return xc * rstd, rstd


def _layer_norm_bwd(dy, xhat, rstd, gain):
    dxh = dy * gain
    m1 = jnp.mean(dxh, axis=-1, keepdims=True)
    m2 = jnp.mean(dxh * xhat, axis=-1, keepdims=True)
    return rstd * (dxh - m1 - xhat * m2)


def _inv_count(tile, tm, win):
    t = (tile * tm + lax.broadcasted_iota(jnp.int32, (tm, 1), 0) + 1).astype(F32)
    return 1.0 / jnp.minimum(t, float(win))


def _fwd_mix(x, mod, conv_w, w_in4, w_pool, pool_scale, w_out, ln_g, ln_b, late_stacks, tm):
    t_len, d = x.shape
    cw = w_in4.shape[2]
    dg = cw // N_GROUPS
    nt = t_len // tm
    n_late = len(late_stacks)
    n_cp = n_late * (N_CHIPS - 1)
    pass_step = max(nt - 3, 0)

    def body(x_ref, mod_ref, cw_ref, win_ref, wp_ref, ps_ref, wout_ref, g_ref, b_ref, *rest):
        late = rest[n_late:2 * n_late]
        x1_ref, zs_ref, mix_ref, h1_ref, cat_ref = rest[2 * n_late:2 * n_late + 5]
        ubuf, vbuf, s_ici, r_ici, s_d2d, r_d2d = rest[2 * n_late + 5:]
        i = pl.program_id(0)
        start, forward, finish = _stack_gather(late, [s.shape[1] for s in late_stacks], s_ici, r_ici, s_d2d, r_d2d)

        @pl.when(i == 0)
        def _():
            start()
            ubuf[0:HALO, :] = jnp.zeros((HALO, cw), F32)
            vbuf[0:HALO, :] = jnp.zeros((HALO, cw), F32)

        xv = x_ref[...]
        sh1, sc1, g1 = mod_ref[0:1, 0:d], mod_ref[0:1, d:2 * d], mod_ref[0:1, 2 * d:3 * d]
        h1 = (xv * (1.0 + sc1) + sh1).astype(BF16)
        h1_ref[...] = h1
        zb = _dot(h1, win_ref[0])
        zc = _dot(h1, win_ref[1])
        zv = _dot(h1, win_ref[2])
        zp = _dot(h1, win_ref[3])
        u = zc * zv
        ubuf[HALO:HALO + tm, :] = u
        u1 = ubuf[pl.ds(HALO - 1, tm), :]
        u2 = ubuf[pl.ds(HALO - 2, tm), :]
        conv = cw_ref[0:1, :] * u2 + cw_ref[1:2, :] * u1 + cw_ref[2:3, :] * u
        ubuf[0:HALO, :] = ubuf[tm:tm + HALO, :]
        yc = zb * conv
        vbuf[HALO:HALO + tm, :] = zp
        ps, pms = [], []
        for gi, win in enumerate(POOL_WINDOWS):
            sl = slice(gi * dg, (gi + 1) * dg)
            acc = zp[:, sl]
            for s in range(1, win):
                acc = acc + vbuf[pl.ds(HALO - s, tm), sl]
            p_g = acc * _inv_count(i, tm, win) - zp[:, sl]
            ps.append(p_g)
            pms.append(_dot(p_g.astype(BF16), wp_ref[gi]))
        vbuf[0:HALO, :] = vbuf[tm:tm + HALO, :]
        pooled = jnp.concatenate(ps, axis=1)
        yp = jnp.concatenate(pms, axis=1) * ps_ref[...]
        cat = jnp.concatenate([yc, yp], axis=1).astype(BF16)
        cat_ref[...] = cat
        mix = _dot(cat, wout_ref[...])
        mix_ref[...] = mix
        xhat, _ = _layer_norm_fwd(DEEPNORM_ALPHA * xv + (1.0 + g1) * mix)
        x1_ref[...] = xhat * g_ref[...] + b_ref[...]
        zs_ref[:, 0 * cw:1 * cw] = zb.astype(BF16)
        zs_ref[:, 1 * cw:2 * cw] = zc.astype(BF16)
        zs_ref[:, 2 * cw:3 * cw] = zv.astype(BF16)
        zs_ref[:, 3 * cw:4 * cw] = conv.astype(BF16)
        zs_ref[:, 4 * cw:5 * cw] = pooled.astype(BF16)

        @pl.when(i == pass_step)
        def _():
            forward()

        @pl.when(i == nt - 1)
        def _():
            finish()

    tile = lambda w: pl.BlockSpec((tm, w), lambda i: (i, 0))
    n_in = 9
    return pl.pallas_call(
        body, name="fwd_mix",
        out_shape=tuple(jax.ShapeDtypeStruct(s.shape, s.dtype) for s in late_stacks)
        + (jax.ShapeDtypeStruct((t_len, d), F32), jax.ShapeDtypeStruct((t_len, 5 * cw), BF16),
           jax.ShapeDtypeStruct((t_len, d), F32), jax.ShapeDtypeStruct((t_len, d), BF16),
           jax.ShapeDtypeStruct((t_len, 2 * cw), BF16)),
        grid=(nt,),
        in_specs=[tile(d)] + [VMEM_SPEC] * 8 + [HBM_SPEC] * n_late,
        out_specs=(HBM_SPEC,) * n_late + (tile(d), tile(5 * cw), tile(d), tile(d), tile(2 * cw)),
        input_output_aliases={n_in + k: k for k in range(n_late)},
        scratch_shapes=[pltpu.VMEM((tm + HALO, cw), F32), pltpu.VMEM((tm + HALO, cw), F32),
                        _dma_sems(n_cp), _dma_sems(n_cp), _dma_sems(n_cp), _dma_sems(n_cp)],
        compiler_params=_params(dimension_semantics=("arbitrary",)),
    )(x, mod, conv_w, w_in4, w_pool, pool_scale, w_out, ln_g, ln_b, *late_stacks)


def _mlp_fwd_bwd(x1, tgt, mod, w_mi4, w_mo4, ln_g, ln_b, tm):
    t_len, d = x1.shape
    fq = w_mi4.shape[2]
    nt = t_len // tm

    def body(x1_ref, tgt_ref, mod_ref, wi_ref, wo_ref, g_ref, b_ref,
             dx1_ref, da_ref, s_ref, df_ref, h2_ref, st_ref, relu_buf, acc):
        i = pl.program_id(0)

        @pl.when(i == 0)
        def _():
            acc[...] = jnp.zeros(acc.shape, F32)

        x1v = x1_ref[...]
        sh2, sc2, g2 = mod_ref[0:1, 3 * d:4 * d], mod_ref[0:1, 4 * d:5 * d], mod_ref[0:1, 5 * d:6 * d]
        h2 = (x1v * (1.0 + sc2) + sh2).astype(BF16)
        h2_ref[...] = h2
        f = jnp.zeros((tm, d), F32)
        for j in range(N_CHIPS):
            a = jnp.maximum(_dot(h2, wi_ref[j]), 0.0)
            relu_buf[j] = a
            s = (a * a).astype(BF16)
            s_ref[j] = s
            f = f + _dot(s, wo_ref[j])
        xhat, rstd = _layer_norm_fwd(DEEPNORM_ALPHA * x1v + (1.0 + g2) * f)
        gain = g_ref[...]
        err = xhat * gain + b_ref[...] - tgt_ref[...]
        dy = err * (1.0 / d)
        dr2 = _layer_norm_bwd(dy, xhat, rstd, gain)
        df = ((1.0 + g2) * dr2).astype(BF16)
        df_ref[...] = df
        dh2 = jnp.zeros((tm, d), F32)
        for j in range(N_CHIPS):
            ds = _dot_nt(df, wo_ref[j])
            da = (ds * (2.0 * relu_buf[j])).astype(BF16)
            da_ref[j] = da
            dh2 = dh2 + _dot_nt(da, wi_ref[j])
        dx1_ref[...] = DEEPNORM_ALPHA * dr2 + dh2 * (1.0 + sc2)
        acc[0] += _fold8(dy * xhat)
        acc[1] += _fold8(dy)
        acc[2] += _fold8(dr2 * f)
        acc[3] += _fold8(dh2 * x1v)
        acc[4] += _fold8(dh2)
        acc[5] += _fold8(err * err)

        @pl.when(i == nt - 1)
        def _():
            for k in range(5):
                st_ref[k:k + 1, :] = jnp.sum(acc[k], axis=0, keepdims=True)
            loss = jnp.sum(acc[5]) * (0.5 / d)
            st_ref[5:6, :] = jnp.broadcast_to(loss, (1, d))
            st_ref[6:8, :] = jnp.zeros((2, d), F32)

    tile = lambda w: pl.BlockSpec((tm, w), lambda i: (i, 0))
    tile4 = pl.BlockSpec((N_CHIPS, tm, fq), lambda i: (0, i, 0))
    return pl.pallas_call(
        body, name="mlp_fwd_bwd",
        out_shape=(jax.ShapeDtypeStruct((t_len, d), F32),
                   jax.ShapeDtypeStruct((N_CHIPS, t_len, fq), BF16), jax.ShapeDtypeStruct((N_CHIPS, t_len, fq), BF16),
                   jax.ShapeDtypeStruct((t_len, d), BF16), jax.ShapeDtypeStruct((t_len, d), BF16),
                   jax.ShapeDtypeStruct((SUBLANES, d), F32)),
        grid=(nt,),
        in_specs=[tile(d), tile(d)] + [VMEM_SPEC] * 5,
        out_specs=(tile(d), tile4, tile4, tile(d), tile(d), pl.BlockSpec((SUBLANES, d), lambda i: (0, 0))),
        scratch_shapes=[pltpu.VMEM((N_CHIPS, tm, fq), F32), pltpu.VMEM((6, SUBLANES, d), F32)],
        compiler_params=_params(dimension_semantics=("arbitrary",)),
    )(x1, tgt, mod, w_mi4, w_mo4, ln_g, ln_b)


def _mlp_wgrad(h2, da4, s4, df, pos, tt):
    t_len, d = h2.shape
    fq = da4.shape[2]
    nt = t_len // tt
    assert nt >= 3
    rh = (d // 2, fq // 2)
    last = N_CHIPS - 1

    def body(pos_ref, h2_ref, da_ref, s_ref, df_ref, qi_ref, qo_ref, gi_ref, go_ref,
             acc_i, acc_o, rcv_i, rcv_o, s_sem, r_sem):
        j, t = pl.program_id(0), pl.program_id(1)
        slot = lax.rem(j, 2)
        x, y, c = _mesh_pos()
        accs, rcvs, q_refs, g_refs = (acc_i, acc_o), (rcv_i, rcv_o), (qi_ref, qo_ref), (gi_ref, go_ref)

        def to_sibling(a, sl):
            theirs = pl.multiple_of((1 - c) * rh[a], rh[a])
            return pltpu.make_async_remote_copy(accs[a].at[sl].at[pl.ds(theirs, rh[a]), :], rcvs[a].at[sl],
                                                s_sem.at[2 * sl + a], r_sem.at[2 * sl + a],
                                                device_id=(x, y, 1 - c), device_id_type=MESH)

        def emit(chunk, sl):
            for a in range(2):
                to_sibling(a, sl).wait()
                mine = pl.multiple_of(c * rh[a], rh[a])
                q = (accs[a][sl, pl.ds(mine, rh[a]), :] + rcvs[a][sl]).astype(BF16)
                q_refs[a][...] = q

                @pl.when(chunk == pos_ref[1])
                def _():
                    g_refs[a][...] = q

        @pl.when(t == 0)
        def _():
            acc_i[slot] = jnp.zeros((d, fq), F32)
            acc_o[slot] = jnp.zeros((fq, d), F32)

        acc_i[slot] += _dot_tn(h2_ref[...], da_ref[...])
        acc_o[slot] += _dot_tn(s_ref[...], df_ref[...])

        @pl.when((t == 1) & (j >= 1))
        def _():
            emit(j - 1, 1 - slot)

        @pl.when(t == nt - 1)
        def _():
            for a in range(2):
                to_sibling(a, slot).start()

        @pl.when((t == nt - 1) & (j == last))
        def _():
            emit(j, slot)

    q_index = lambda j, t, p: (jnp.where(t == nt - 1, j, jnp.maximum(j - 1, 0)), 0, 0)
    got_index = lambda j, t, p: (p[1], 0, 0)
    blocks = ((None, rh[0], fq), (None, rh[1], d))
    return pl.pallas_call(
        body, name="mlp_wgrad",
        out_shape=tuple(jax.ShapeDtypeStruct((N_CHIPS,) + b[1:], BF16) for b in blocks) * 2,
        grid_spec=pltpu.PrefetchScalarGridSpec(
            num_scalar_prefetch=1, grid=(N_CHIPS, nt),
            in_specs=[pl.BlockSpec((tt, d), lambda j, t, p: (t, 0)),
                      pl.BlockSpec((None, tt, fq), lambda j, t, p: (j, t, 0)),
                      pl.BlockSpec((None, tt, fq), lambda j, t, p: (j, t, 0)),
                      pl.BlockSpec((tt, d), lambda j, t, p: (t, 0))],
            out_specs=tuple(pl.BlockSpec(b, q_index) for b in blocks)
            + tuple(pl.BlockSpec(b, got_index) for b in blocks),
            scratch_shapes=[pltpu.VMEM((2, d, fq), F32), pltpu.VMEM((2, fq, d), F32),
                            pltpu.VMEM((2, rh[0], fq), F32), pltpu.VMEM((2, rh[1], d), F32),
                            _dma_sems(4), _dma_sems(4)]),
        compiler_params=_params(dimension_semantics=("arbitrary", "arbitrary")),
    )(pos, h2, da4, s4, df)


def _bwd_mix(x, mix, dx1, zs, mod, conv_w, w_in4, w_pool, pool_scale, w_out, ln_g, after, tm):
    t_len, d = x.shape
    cw = w_in4.shape[2]
    dg = cw // N_GROUPS
    nt = t_len // tm

    def body(x_ref, mix_ref, dx1_ref, zs_ref, mod_ref, cw_ref, win_ref, wp_ref, ps_ref, wout_ref, g_ref, after_ref,
             gx_ref, dz_ref, dmix_ref, sd_ref, sc_ref, dwp_ref, dbuf, qbuf, acc_d, acc_c, acc_p):
        i = pl.program_id(0)
        tile_idx = nt - 1 - i

        @pl.when(i == 0)
        def _():
            dbuf[tm:tm + HALO, :] = jnp.zeros((HALO, cw), F32)
            qbuf[tm:tm + HALO, :] = jnp.zeros((HALO, cw), F32)
            acc_d[...] = jnp.zeros(acc_d.shape, F32)
            acc_c[...] = jnp.zeros(acc_c.shape, F32)
            acc_p[...] = jnp.zeros(acc_p.shape, F32)

        xv, mixv, dx1v = x_ref[...], mix_ref[...], dx1_ref[...]
        sc1, g1 = mod_ref[0:1, d:2 * d], mod_ref[0:1, 2 * d:3 * d]
        xhat, rstd = _layer_norm_fwd(DEEPNORM_ALPHA * xv + (1.0 + g1) * mixv)
        dr1 = _layer_norm_bwd(dx1v, xhat, rstd, g_ref[...])
        dmix = ((1.0 + g1) * dr1).astype(BF16)
        dmix_ref[...] = dmix
        dcat = _dot_nt(dmix, wout_ref[...])
        dyc, dyp = dcat[:, 0:cw], dcat[:, cw:2 * cw]
        zb, zc, zv = (zs_ref[:, k * cw:(k + 1) * cw].astype(F32) for k in range(3))
        conv, pooled = zs_ref[:, 3 * cw:4 * cw].astype(F32), zs_ref[:, 4 * cw:5 * cw]
        dzb = dyc * conv
        dcv = dyc * zb
        dbuf[0:tm, :] = dcv
        d1 = dbuf[pl.ds(1, tm), :]
        d2 = dbuf[pl.ds(2, tm), :]
        du = cw_ref[2:3, :] * dcv + cw_ref[1:2, :] * d1 + cw_ref[0:1, :] * d2
        dbuf[tm:tm + HALO, :] = dbuf[0:HALO, :]
        u = zc * zv
        acc_c[0] += _fold8(u * d2)
        acc_c[1] += _fold8(u * d1)
        acc_c[2] += _fold8(u * dcv)
        dzc = du * zv
        dzv = du * zc
        dpw = dyp * ps_ref[...]
        dps, pms = [], []
        for gi, win in enumerate(POOL_WINDOWS):
            sl = slice(gi * dg, (gi + 1) * dg)
            p_g = pooled[:, sl].astype(BF16)
            dpw_g = dpw[:, sl].astype(BF16)
            pms.append(_dot(p_g, wp_ref[gi]))
            acc_p[gi] += _dot_tn(p_g, dpw_g)
            dp_g = _dot_nt(dpw_g, wp_ref[gi])
            dps.append(dp_g)
            qbuf[0:tm, sl] = dp_g * _inv_count(tile_idx, tm, win)
        acc_c[3] += _fold8(dyp * jnp.concatenate(pms, axis=1))
        dzps = []
        for gi, win in enumerate(POOL_WINDOWS):
            sl = slice(gi * dg, (gi + 1) * dg)
            acc = qbuf[0:tm, sl]
            for s in range(1, win):
                acc = acc + qbuf[pl.ds(s, tm), sl]
            dzps.append(acc - dps[gi])
        qbuf[tm:tm + HALO, :] = qbuf[0:HALO, :]
        dz = [dzb.astype(BF16), dzc.astype(BF16), dzv.astype(BF16), jnp.concatenate(dzps, axis=1).astype(BF16)]
        dh1 = jnp.zeros((tm, d), F32)
        for j in range(N_CHIPS):
            dz_ref[j] = dz[j]
            dh1 = dh1 + _dot_nt(dz[j], win_ref[j])
        gx_ref[...] = DEEPNORM_ALPHA * dr1 + dh1 * (1.0 + sc1)
        acc_d[0] += _fold8(dx1v * xhat)
        acc_d[1] += _fold8(dx1v)
        acc_d[2] += _fold8(dr1 * mixv)
        acc_d[3] += _fold8(dh1 * xv)
        acc_d[4] += _fold8(dh1)

        @pl.when(i == nt - 1)
        def _():
            for k in range(5):
                sd_ref[k:k + 1, :] = jnp.sum(acc_d[k], axis=0, keepdims=True)
            sd_ref[5:8, :] = jnp.zeros((3, d), F32)
            for k in range(4):
                sc_ref[k:k + 1, :] = jnp.sum(acc_c[k], axis=0, keepdims=True)
            sc_ref[4:8, :] = jnp.zeros((4, cw), F32)
            dwp_ref[...] = acc_p[...]

    rtile = lambda w: pl.BlockSpec((tm, w), lambda i: (nt - 1 - i, 0))
    whole = lambda shape: pl.BlockSpec(shape, lambda i: tuple(0 for _ in shape))
    return pl.pallas_call(
        body, name="bwd_mix",
        out_shape=(jax.ShapeDtypeStruct((t_len, d), F32), jax.ShapeDtypeStruct((N_CHIPS, t_len, cw), BF16),
                   jax.ShapeDtypeStruct((t_len, d), BF16), jax.ShapeDtypeStruct((SUBLANES, d), F32),
                   jax.ShapeDtypeStruct((SUBLANES, cw), F32), jax.ShapeDtypeStruct((N_GROUPS, dg, dg), F32)),
        grid=(nt,),
        in_specs=[rtile(d), rtile(d), rtile(d), rtile(5 * cw)] + [VMEM_SPEC] * 7
        + [pl.BlockSpec(memory_space=pl.ANY)],
        out_specs=(rtile(d), pl.BlockSpec((N_CHIPS, tm, cw), lambda i: (0, nt - 1 - i, 0)), rtile(d),
                   whole((SUBLANES, d)), whole((SUBLANES, cw)), whole((N_GROUPS, dg, dg))),
        scratch_shapes=[pltpu.VMEM((tm + HALO, cw), F32), pltpu.VMEM((tm + HALO, cw), F32),
                        pltpu.VMEM((5, SUBLANES, d), F32), pltpu.VMEM((4, SUBLANES, cw), F32),
                        pltpu.VMEM((N_GROUPS, dg, dg), F32)],
        compiler_params=_params(dimension_semantics=("arbitrary",)),
    )(x, mix, dx1, zs, mod, conv_w, w_in4, w_pool, pool_scale, w_out, ln_g, after)


def _mix_wgrad(h1, dz4, cat, dmix, packed, tt):
    t_len, d = h1.shape
    cw = dz4.shape[2]
    nt = t_len // tt
    ro = (2 * cw) // N_CHIPS
    rh = (d // 2, ro // 2)
    r_small = packed.shape[0]

    def body(h1_ref, dz_ref, cat_ref, dmix_ref, p_ref, qi_ref, qo_ref, gi_ref, go_ref, sum_ref, all_ref,
             acc_i, acc_o, rcv_i, rcv_o, all_buf, s_sem, r_sem, s_small, r_small_sem):
        t = pl.program_id(0)
        x, y, c = _mesh_pos()
        chip = 2 * x + y
        small_start, small_finish = _small_allreduce_steps(p_ref, sum_ref, all_ref, all_buf, s_small, r_small_sem)
        accs, rcvs, q_refs, g_refs = (acc_i, acc_o), (rcv_i, rcv_o), (qi_ref, qo_ref), (gi_ref, go_ref)

        @pl.when(t == 0)
        def _():
            small_start()
            acc_i[...] = jnp.zeros(acc_i.shape, F32)
            acc_o[...] = jnp.zeros(acc_o.shape, F32)

        h1v, dmixv = h1_ref[...], dmix_ref[...]
        for j in range(N_CHIPS):
            acc_i[j] += _dot_tn(h1v, dz_ref[j])
            acc_o[j] += _dot_tn(cat_ref[:, j * ro:(j + 1) * ro], dmixv)

        @pl.when(t == nt - 1)
        def _():
            cps = []
            for a in range(2):
                theirs = pl.multiple_of((1 - c) * rh[a], rh[a])
                cp = pltpu.make_async_remote_copy(accs[a].at[:, pl.ds(theirs, rh[a]), :], rcvs[a],
                                                  s_sem.at[a], r_sem.at[a],
                                                  device_id=(x, y, 1 - c), device_id_type=MESH)
                cp.start()
                cps.append(cp)
            for a in range(2):
                cps[a].wait()
                mine = pl.multiple_of(c * rh[a], rh[a])
                for j in range(N_CHIPS):
                    q_refs[a][j] = (accs[a][j, pl.ds(mine, rh[a]), :] + rcvs[a][j]).astype(BF16)
                g_refs[a][chip] = q_refs[a][chip]
            small_finish()

    stacks = ((N_CHIPS, rh[0], cw), (N_CHIPS, rh[1], d))
    return pl.pallas_call(
        body, name="mix_wgrad",
        out_shape=tuple(jax.ShapeDtypeStruct(s, BF16) for s in stacks) * 2
        + (jax.ShapeDtypeStruct((r_small, LANES), F32), jax.ShapeDtypeStruct((N_DEV, r_small, LANES), F32)),
        grid=(nt,),
        in_specs=[pl.BlockSpec((tt, d), lambda t: (t, 0)), pl.BlockSpec((N_CHIPS, tt, cw), lambda t: (0, t, 0)),
                  pl.BlockSpec((tt, 2 * cw), lambda t: (t, 0)), pl.BlockSpec((tt, d), lambda t: (t, 0)), VMEM_SPEC],
        out_specs=(VMEM_SPEC,) * 6,
        scratch_shapes=[pltpu.VMEM((N_CHIPS, d, cw), F32), pltpu.VMEM((N_CHIPS, ro, d), F32),
                        pltpu.VMEM(stacks[0], F32), pltpu.VMEM(stacks[1], F32),
                        pltpu.VMEM((N_DEV, r_small, LANES), F32),
                        _dma_sems(2), _dma_sems(2), _dma_sems(N_DEV - 1), _dma_sems(N_DEV - 1)],
        compiler_params=_params(dimension_semantics=("arbitrary",)),
    )(h1, dz4, cat, dmix, packed)


def _rows128(v):
    v = v.reshape(-1, LANES)
    pad = (-v.shape[0]) % SUBLANES
    if pad:
        v = jnp.concatenate([v, jnp.zeros((pad, LANES), v.dtype)], axis=0)
    return v


def kernel(x, c, w_ada, b_ada, w_in, conv_w, w_pool, pool_scale, w_out, ln1_g, ln1_b, w_mlp_in, w_mlp_out, ln2_g, ln2_b, loss_target, m_w_ada, m_b_ada, m_w_in, m_conv_w, m_w_pool, m_pool_scale, m_w_out, m_ln1_g, m_ln1_b, m_w_mlp_in, m_w_mlp_out, m_ln2_g, m_ln2_b, v_w_ada, v_b_ada, v_w_in, v_conv_w, v_w_pool, v_pool_scale, v_w_out, v_ln1_g, v_ln1_b, v_w_mlp_in, v_w_mlp_out, v_ln2_g, v_ln2_b):
    t_len, d = x.shape[1], x.shape[2]
    cw = w_in.shape[2]
    cq = conv_w.shape[2]
    dg = w_pool.shape[2]
    assert cw == N_GROUPS * dg and cq * N_CHIPS == cw and LANES % cq == 0
    tm = min(TOKEN_TILE, t_len)
    tt = min(WGRAD_TILE, t_len // 4)
    chip = 2 * lax.axis_index("x") + lax.axis_index("y")
    pos = jnp.stack([lax.axis_index("c"), chip]).astype(jnp.int32)

    x2, tgt = x[0], loss_target[0]
    big_w = [w_in[0], w_out[0], w_mlp_in[0], w_mlp_out[0]]
    big_m = [m_w_in[0], m_w_out[0], m_w_mlp_in[0], m_w_mlp_out[0]]
    big_v = [v_w_in[0], v_w_out[0], v_w_mlp_in[0], v_w_mlp_out[0]]
    names = ["w_in", "w_out", "w_mlp_in", "w_mlp_out"]

    mod, cond_all, conv_full, w_in4, w_out4, w_mi_own, w_mo_own = _setup_exchange(
        c, w_ada[0], b_ada, conv_w[0], big_w, 2)
    w_out_full = w_out4.reshape(2 * cw, d)
    w_pool_bf = w_pool[0].astype(BF16)

    w_mi4, w_mo4, x1, zs, mix, h1, cat = _fwd_mix(x2, mod, conv_full, w_in4, w_pool_bf, pool_scale, w_out_full,
                                                  ln1_g, ln1_b, [w_mi_own, w_mo_own], tm)
    dx1, da4, s4, df, h2, st2 = _mlp_fwd_bwd(x1, tgt, mod, w_mi4, w_mo4, ln2_g, ln2_b, tm)
    mlp_qi, mlp_qo, mlp_gi, mlp_go = _mlp_wgrad(h2, da4, s4, df, pos, tt)
    mlp_started = _exchange_start([mlp_qi, mlp_qo], [mlp_gi, mlp_go], "mlp")
    grad_x, dz4, dmix, st1, stc, dw_pool = _bwd_mix(x2, mix, dx1, zs, mod, conv_full, w_in4, w_pool_bf, pool_scale,
                                                    w_out_full, ln1_g, mlp_started[-1], tm)

    conv_part = jnp.transpose(stc[0:3].reshape(3, N_CHIPS, cq), (1, 0, 2))
    conv_part = jnp.pad(conv_part, ((0, 0), (0, SUBLANES - 3), (0, LANES - cq)))
    dmod = jnp.concatenate([st1[4], st1[3], st1[2], st2[4], st2[3], st2[2]])
    pieces = [st2[5, 0:LANES], st1[0], st1[1], st2[0], st2[1], stc[3], conv_part, dw_pool, dmod]
    rows = [_rows128(p) for p in pieces]
    offs = [0]
    for r in rows:
        offs.append(offs[-1] + r.shape[0])
    mix_qi, mix_qo, mix_gi, mix_go, total, gathered = _mix_wgrad(h1, dz4, cat, dmix,
                                                                 jnp.concatenate(rows, axis=0), tt)

    def sec(k, shape):
        n = 1
        for s in shape:
            n *= s
        return total[offs[k]:offs[k] + n // LANES].reshape(shape)

    loss = total[0, 0]
    g_ln1_g, g_ln1_b, g_ln2_g, g_ln2_b = sec(1, (1, d)), sec(2, (1, d)), sec(3, (1, d)), sec(4, (1, d))
    g_pool_scale = sec(5, (1, cw))
    conv_blocks = total[offs[6]:offs[7]].reshape(N_CHIPS, SUBLANES, LANES)
    g_conv = lax.dynamic_index_in_dim(conv_blocks, chip, axis=0, keepdims=False)[0:3, 0:cq]
    g_w_pool = sec(7, (N_GROUPS, dg, dg))
    g_b_ada = sec(8, (1, 6 * d))
    dmod_all = gathered[:, offs[8]:offs[9], :].reshape(N_DEV, 6 * d)
    na = w_ada.shape[2]
    dmod_mine = lax.dynamic_slice_in_dim(dmod_all, chip * na, na, axis=1)

    mlp_got = _exchange_wait(mlp_started, total, "mlp")
    mix_started = _exchange_start([mix_qi, mix_qo], [mix_gi, mix_go], "mix")
    mlp_halves = [_sum_chips(g, pos, mix_started[-1], "rs_total_" + n) for g, n in zip(mlp_got, names[2:])]
    mlp_g = _sibling_join(mlp_halves, "mlp")
    mlp_out = [_adamw_big(g, w, m, v, "adamw_" + n)
               for g, w, m, v, n in zip(mlp_g, big_w[2:], big_m[2:], big_v[2:], names[2:])]
    g_ada, d_ada, nm_ada, nv_ada = _adamw_ada(cond_all.T, dmod_mine, w_ada[0], m_w_ada[0], v_w_ada[0])
    mix_got = _exchange_wait(mix_started, d_ada, "mix")
    mix_halves = [_sum_chips(g, pos, d_ada, "rs_total_" + n) for g, n in zip(mix_got, names[:2])]
    mix_g = _sibling_join(mix_halves, "mix")
    mix_out = [_adamw_big(g, w, m, v, "adamw_" + n)
               for g, w, m, v, n in zip(mix_g, big_w[:2], big_m[:2], big_v[:2], names[:2])]
    big_g = list(mix_g) + list(mlp_g)
    big_out = mix_out + mlp_out

    small_g = [g_b_ada, g_conv, g_w_pool.reshape(-1, dg), g_pool_scale, g_ln1_g, g_ln1_b, g_ln2_g, g_ln2_b]
    small_w = [b_ada, conv_w[0], w_pool.reshape(-1, dg), pool_scale, ln1_g, ln1_b, ln2_g, ln2_b]
    small_m = [m_b_ada, m_conv_w[0], m_w_pool.reshape(-1, dg), m_pool_scale, m_ln1_g, m_ln1_b, m_ln2_g, m_ln2_b]
    small_v = [v_b_ada, v_conv_w[0], v_w_pool.reshape(-1, dg), v_pool_scale, v_ln1_g, v_ln1_b, v_ln2_g, v_ln2_b]
    sm = _adamw_small(small_g, small_w, small_m, small_v)
    ns = len(small_g)
    s_delta, s_m, s_v = sm[:ns], sm[ns:2 * ns], sm[2 * ns:]

    def assemble(ada, small, big):
        return [ada[None], small[0], big[0][None], small[1][None], small[2].reshape(w_pool.shape), small[3],
                big[1][None], small[4], small[5], big[2][None], big[3][None], small[6], small[7]]

    grads = assemble(g_ada, small_g, big_g)
    deltas = assemble(d_ada, s_delta, [o[0] for o in big_out])
    new_m = assemble(nm_ada, s_m, [o[1] for o in big_out])
    new_v = assemble(nv_ada, s_v, [o[2] for o in big_out])
    return (loss, grad_x[None], *grads, *deltas, *new_m, *new_v)
```

```python
import jax
import jax.numpy as jnp
from jax import lax
from jax.experimental import pallas as pl
from jax.experimental.pallas import tpu as pltpu

F32 = jnp.float32
BF16 = jnp.bfloat16
MESH = pl.DeviceIdType.MESH

LN_EPS = 1e-5
DEEPNORM_ALPHA = 2.0 ** 0.25
POOL_WINDOWS = (2, 4, 8, 16)
N_GROUPS = len(POOL_WINDOWS)
HALO = 16
N_CHIPS = 4
N_DEV = 8
LANES = 128
SUBLANES = 8
VMEM_LIMIT = 56 * 1024 * 1024
MIX_TOKEN_TILE = 512
MLP_TOKEN_TILE = 256
WGRAD_TILE = 1024

ADAM_LR = 0.001
ADAM_B1 = 0.9
ADAM_B2 = 0.999
ADAM_EPS = 1e-08
ADAM_WD = 0.01
ADAM_STEP = 10

VMEM_SPEC = pl.BlockSpec(memory_space=pltpu.VMEM)
HBM_SPEC = pl.BlockSpec(memory_space=pltpu.HBM)


def _dot(a, b):
    return jnp.dot(a, b, preferred_element_type=F32)


def _dot_nt(a, b):
    return lax.dot_general(a, b, (((1,), (1,)), ((), ())), preferred_element_type=F32)


def _dot_tn(a, b):
    return lax.dot_general(a, b, (((0,), (0,)), ((), ())), preferred_element_type=F32)


def _fold8(v):
    r, n = v.shape
    return jnp.sum(v.reshape(r // SUBLANES, SUBLANES, n), axis=0)


def _mesh_pos():
    return lax.axis_index("x"), lax.axis_index("y"), lax.axis_index("c")


def _flip(v, bit):
    return 1 - v if bit else v


def _params(**kw):
    return pltpu.CompilerParams(vmem_limit_bytes=VMEM_LIMIT, **kw)


def _dma_sems(n):
    return pltpu.SemaphoreType.DMA((n,))


def _stack_gather(stacks, rows, s_ici, r_ici, s_d2d, r_d2d):
    x, y, c = _mesh_pos()
    chip = 2 * x + y
    pairs = [(i, m) for i in range(len(stacks)) for m in range(1, N_CHIPS)]

    def blk(i, slot, which):
        rh = rows[i] // 2
        return stacks[i].at[slot].at[pl.ds(which * rh, rh), :]

    def other(m):
        return 2 * _flip(x, m & 2) + _flip(y, m & 1)

    def ici(i, m, slot, to):
        k = i * (N_CHIPS - 1) + m - 1
        return pltpu.make_async_remote_copy(blk(i, slot, c), blk(i, slot, c), s_ici.at[k], r_ici.at[k],
                                            device_id=to, device_id_type=MESH)

    def d2d(i, m, which, to):
        k = i * (N_CHIPS - 1) + m - 1
        return pltpu.make_async_remote_copy(blk(i, other(m), which), blk(i, other(m), which),
                                            s_d2d.at[k], r_d2d.at[k], device_id=to, device_id_type=MESH)

    def start():
        for i, m in pairs:
            ici(i, m, chip, (_flip(x, m & 2), _flip(y, m & 1), c)).start()

    def forward():
        for i, m in pairs:
            ici(i, m, other(m), (x, y, c)).wait_recv()
            d2d(i, m, c, (x, y, 1 - c)).start()

    def finish():
        for i, m in pairs:
            d2d(i, m, 1 - c, (x, y, c)).wait_recv()
        for i, m in pairs:
            ici(i, m, chip, (x, y, c)).wait_send()
            d2d(i, m, c, (x, y, c)).wait_send()

    return start, forward, finish


def _chunk_exchange(qs, gots, s_sem, r_sem):
    x, y, c = _mesh_pos()
    chip = 2 * x + y
    pairs = [(i, m) for i in range(len(qs)) for m in range(1, N_CHIPS)]

    def other(m):
        return 2 * _flip(x, m & 2) + _flip(y, m & 1)

    def send(i, m, to):
        k = i * (N_CHIPS - 1) + m - 1
        return pltpu.make_async_remote_copy(qs[i].at[other(m)], gots[i].at[chip], s_sem.at[k], r_sem.at[k],
                                            device_id=to, device_id_type=MESH)

    def arrival(i, m):
        k = i * (N_CHIPS - 1) + m - 1
        return pltpu.make_async_remote_copy(qs[i].at[other(m)], gots[i].at[other(m)], s_sem.at[k], r_sem.at[k],
                                            device_id=(x, y, c), device_id_type=MESH)

    def start():
        for i, m in pairs:
            send(i, m, (_flip(x, m & 2), _flip(y, m & 1), c)).start()

    def finish():
        for i, m in pairs:
            arrival(i, m).wait_recv()
        for i, m in pairs:
            send(i, m, (x, y, c)).wait_send()

    return start, finish


def _mod_scratch(d, na, cq):
    return [pltpu.VMEM((SUBLANES, d), F32), pltpu.VMEM((N_DEV, SUBLANES, d), F32),
            pltpu.VMEM((N_CHIPS, SUBLANES, na), F32),
            pltpu.VMEM((SUBLANES, cq), F32), pltpu.VMEM((N_CHIPS, SUBLANES, cq), F32),
            _dma_sems(N_DEV - 1), _dma_sems(N_DEV - 1),
            _dma_sems(N_CHIPS - 1), _dma_sems(N_CHIPS - 1), _dma_sems(N_CHIPS - 1), _dma_sems(N_CHIPS - 1)]


def _mod_steps(d, na, cq):
    def steps(c_ref, w_ref, b_ref, cw_ref, mod_ref, cond_ref, conv_ref,
              csend, cbuf, mbuf, cvsend, cvbuf, s1, r1, s2, r2, s3, r3):
        x, y, c = _mesh_pos()
        me = 4 * x + 2 * y + c
        chip = 2 * x + y
        csend[...] = jnp.broadcast_to(c_ref[...], (SUBLANES, d))
        cbuf[me] = csend[...]
        first = []
        for m in range(1, N_DEV):
            peer = (_flip(x, m & 4), _flip(y, m & 2), _flip(c, m & 1))
            cp = pltpu.make_async_remote_copy(csend, cbuf.at[me], s1.at[m - 1], r1.at[m - 1],
                                              device_id=peer, device_id_type=MESH)
            cp.start()
            first.append(cp)
        cvsend[...] = jnp.zeros((SUBLANES, cq), F32)
        cvsend[0:3, :] = cw_ref[...]
        cvbuf[chip] = cvsend[...]
        for m in range(1, N_DEV):
            src = 4 * _flip(x, m & 4) + 2 * _flip(y, m & 2) + _flip(c, m & 1)
            pltpu.make_async_remote_copy(csend, cbuf.at[src], s1.at[m - 1], r1.at[m - 1],
                                         device_id=(x, y, c), device_id_type=MESH).wait_recv()
        rows = lax.broadcasted_iota(jnp.int32, (SUBLANES, d), 0)
        call = jnp.zeros((SUBLANES, d), F32)
        for b in range(N_DEV):
            call = jnp.where(rows == b, cbuf[b], call)
        cond = call * jax.nn.sigmoid(call)
        cond_ref[...] = cond
        part = jnp.dot(cond, w_ref[...], preferred_element_type=F32, precision=lax.Precision.HIGHEST)
        mbuf[chip] = part
        second = []
        for m in range(1, N_CHIPS):
            peer = (_flip(x, m & 2), _flip(y, m & 1), c)
            cp = pltpu.make_async_remote_copy(mbuf.at[chip], mbuf.at[chip], s2.at[m - 1], r2.at[m - 1],
                                              device_id=peer, device_id_type=MESH)
            cp.start()
            second.append(cp)
            cp = pltpu.make_async_remote_copy(cvsend, cvbuf.at[chip], s3.at[m - 1], r3.at[m - 1],
                                              device_id=peer, device_id_type=MESH)
            cp.start()
            second.append(cp)
        for m in range(1, N_CHIPS):
            src = 2 * _flip(x, m & 2) + _flip(y, m & 1)
            pltpu.make_async_remote_copy(mbuf.at[src], mbuf.at[src], s2.at[m - 1], r2.at[m - 1],
                                         device_id=(x, y, c), device_id_type=MESH).wait_recv()
            pltpu.make_async_remote_copy(cvsend, cvbuf.at[src], s3.at[m - 1], r3.at[m - 1],
                                         device_id=(x, y, c), device_id_type=MESH).wait_recv()
        for cp in first + second:
            cp.wait_send()
        rows_n = lax.broadcasted_iota(jnp.int32, (SUBLANES, na), 0)
        for k in range(N_CHIPS):
            mine = jnp.sum(jnp.where(rows_n == me, mbuf[k], 0.0), axis=0, keepdims=True)
            mod_ref[:, k * na:(k + 1) * na] = jnp.broadcast_to(mine + b_ref[:, k * na:(k + 1) * na], (SUBLANES, na))
            conv_ref[:, k * cq:(k + 1) * cq] = cvbuf[k]

    return steps


def _setup_exchange(c_row, w_ada_s, b_ada, conv_w_s, shards, n_now):
    n = len(shards)
    n_cp = n_now * (N_CHIPS - 1)
    d, na, cq = c_row.shape[1], w_ada_s.shape[1], conv_w_s.shape[1]
    mod_scratch = _mod_scratch(d, na, cq)
    mod_steps = _mod_steps(d, na, cq)

    def body(*refs):
        mod_in, ins = refs[:4], refs[4:4 + n]
        mod_out, outs = refs[4 + n:7 + n], refs[7 + n:7 + 2 * n]
        bufs = refs[7 + 2 * n:7 + 3 * n]
        mod_scr = refs[7 + 3 * n:7 + 3 * n + len(mod_scratch)]
        s_ici, r_ici, s_d2d, r_d2d, s_loc = refs[7 + 3 * n + len(mod_scratch):]
        x, y, _ = _mesh_pos()
        chip = 2 * x + y
        def place(lo, hi):
            local = []
            for i in range(lo, hi):
                bufs[i][...] = ins[i][...].astype(BF16)
                cp = pltpu.make_async_copy(bufs[i], outs[i].at[chip], s_loc.at[i])
                cp.start()
                local.append(cp)
            return local

        start, forward, finish = _stack_gather(outs[:n_now], [s.shape[0] for s in shards[:n_now]],
                                               s_ici, r_ici, s_d2d, r_d2d)
        for cp in place(0, n_now):
            cp.wait()
        start()
        later = place(n_now, n)
        mod_steps(*mod_in, *mod_out, *mod_scr)
        forward()
        finish()
        for cp in later:
            cp.wait()

    return pl.pallas_call(
        body, name="setup_exchange",
        out_shape=(jax.ShapeDtypeStruct((SUBLANES, N_CHIPS * na), F32), jax.ShapeDtypeStruct((SUBLANES, d), F32),
                   jax.ShapeDtypeStruct((SUBLANES, N_CHIPS * cq), F32))
        + tuple(jax.ShapeDtypeStruct((N_CHIPS,) + s.shape, BF16) for s in shards),
        in_specs=[VMEM_SPEC] * (4 + n), out_specs=(VMEM_SPEC,) * 3 + (HBM_SPEC,) * n,
        scratch_shapes=[pltpu.VMEM(s.shape, BF16) for s in shards] + mod_scratch
        + [_dma_sems(n_cp), _dma_sems(n_cp), _dma_sems(n_cp), _dma_sems(n_cp), _dma_sems(n)],
        compiler_params=_params(),
    )(c_row, w_ada_s, b_ada, conv_w_s, *shards)


def _sibling_join(gs, tag):
    n = len(gs)

    def body(*refs):
        outs = refs[n:2 * n]
        s_sem, r_sem = refs[2 * n:]
        x, y, c = _mesh_pos()
        cps = []
        for i in range(n):
            rh = gs[i].shape[0] // 2
            mine = outs[i].at[pl.ds(c * rh, rh), :]
            cp = pltpu.make_async_remote_copy(mine, mine, s_sem.at[i], r_sem.at[i],
                                              device_id=(x, y, 1 - c), device_id_type=MESH)
            cp.start()
            cps.append(cp)
        for i in range(n):
            rh = gs[i].shape[0] // 2
            theirs = outs[i].at[pl.ds((1 - c) * rh, rh), :]
            pltpu.make_async_remote_copy(theirs, theirs, s_sem.at[i], r_sem.at[i],
                                         device_id=(x, y, c), device_id_type=MESH).wait_recv()
        for cp in cps:
            cp.wait_send()

    return pl.pallas_call(
        body, name="rs_sibling_join_" + tag,
        out_shape=tuple(jax.ShapeDtypeStruct(g.shape, g.dtype) for g in gs),
        in_specs=[HBM_SPEC] * n, out_specs=(HBM_SPEC,) * n,
        input_output_aliases={i: i for i in range(n)},
        scratch_shapes=[_dma_sems(n), _dma_sems(n)],
        compiler_params=_params(),
    )(*gs)


SEM_SPEC = pl.BlockSpec(memory_space=pltpu.SEMAPHORE)
SIDE_EFFECT = pltpu.SideEffectType.DATAFLOW_SIDE_EFFECTING


def _exchange_start(qs, gots, tag):
    n = len(qs)
    n_cp = n * (N_CHIPS - 1)

    def body(*refs):
        send_sems, recv_sems = refs[2 * n], refs[2 * n + 1]
        q_thru, got_thru = refs[2 * n + 2:3 * n + 2], refs[3 * n + 2:4 * n + 2]
        token = refs[4 * n + 2]
        start, _ = _chunk_exchange(q_thru, got_thru, send_sems, recv_sems)
        start()
        token[...] = jnp.zeros_like(token)

    through = [pltpu.with_memory_space_constraint(a, pltpu.HBM) for a in list(qs) + list(gots)]
    return pl.pallas_call(
        body, name="exchange_start_" + tag,
        out_shape=(pltpu.SemaphoreType.DMA((n_cp,)), pltpu.SemaphoreType.DMA((n_cp,)))
        + tuple(pltpu.HBM(a.shape, a.dtype) for a in through) + (jax.ShapeDtypeStruct((SUBLANES, LANES), F32),),
        in_specs=[HBM_SPEC] * (2 * n), out_specs=(SEM_SPEC, SEM_SPEC) + (HBM_SPEC,) * (2 * n) + (VMEM_SPEC,),
        input_output_aliases={k: 2 + k for k in range(2 * n)},
        compiler_params=pltpu.CompilerParams(has_side_effects=SIDE_EFFECT),
    )(*through)


def _exchange_wait(started, after, tag):
    send_sems, recv_sems = started[0], started[1]
    n = (len(started) - 3) // 2
    q_thru, got_thru = started[2:2 + n], started[2 + n:2 + 2 * n]

    def body(*refs):
        q_refs, got_refs = refs[:n], refs[n:2 * n]
        s_sem, r_sem = refs[2 * n], refs[2 * n + 1]
        _, finish = _chunk_exchange(q_refs, got_refs, s_sem, r_sem)
        finish()

    return pl.pallas_call(
        body, name="exchange_wait_" + tag,
        out_shape=tuple(pltpu.HBM(a.shape, a.dtype) for a in list(q_thru) + list(got_thru)),
        in_specs=[HBM_SPEC] * (2 * n) + [SEM_SPEC, SEM_SPEC, pl.BlockSpec(memory_space=pl.ANY)],
        out_specs=(HBM_SPEC,) * (2 * n),
        input_output_aliases={k: k for k in range(2 * n)},
        compiler_params=pltpu.CompilerParams(has_side_effects=SIDE_EFFECT),
    )(*q_thru, *got_thru, send_sems, recv_sems, after)[n:]


def _small_allreduce_steps(groups, s_sem, r_sem):
    x, y, c = _mesh_pos()
    me = 4 * x + 2 * y + c
    pairs = [(k, m) for k in range(len(groups)) for m in range(1, N_DEV)]

    def send(k, m, to):
        p_ref, all_ref = groups[k][0], groups[k][3]
        i = k * (N_DEV - 1) + m - 1
        return pltpu.make_async_remote_copy(p_ref, all_ref.at[me], s_sem.at[i], r_sem.at[i],
                                            device_id=to, device_id_type=MESH)

    def start():
        for p_ref, _, _, all_ref in groups:
            all_ref[me] = p_ref[...]
        for k, m in pairs:
            send(k, m, (_flip(x, m & 4), _flip(y, m & 2), _flip(c, m & 1))).start()

    def finish():
        for k, m in pairs:
            p_ref, all_ref = groups[k][0], groups[k][3]
            i = k * (N_DEV - 1) + m - 1
            src = 4 * _flip(x, m & 4) + 2 * _flip(y, m & 2) + _flip(c, m & 1)
            pltpu.make_async_remote_copy(p_ref, all_ref.at[src], s_sem.at[i], r_sem.at[i],
                                         device_id=(x, y, c), device_id_type=MESH).wait_recv()
        for k, m in pairs:
            send(k, m, (x, y, c)).wait_send()
        for _, sum_ref, all_out_ref, all_ref in groups:
            sum_ref[...] = (((all_ref[0] + all_ref[1]) + (all_ref[2] + all_ref[3]))
                            + ((all_ref[4] + all_ref[5]) + (all_ref[6] + all_ref[7])))
            if all_out_ref is not None:
                all_out_ref[...] = all_ref[...]

    return start, finish


def _row_block(rows, cap=256):
    rb = min(rows, cap)
    assert rows % rb == 0
    return rb


def _sum_chips(got, pos, after, name):
    _, rh, cc = got.shape
    rb = _row_block(rh)
    nb = rh // rb

    def body(pos_ref, a_ref, after_ref, o_ref):
        a = a_ref[...].astype(F32)
        o_ref[...] = (a[0] + a[1]) + (a[2] + a[3])

    return pl.pallas_call(
        body, name=name,
        out_shape=jax.ShapeDtypeStruct((2 * rh, cc), F32),
        grid_spec=pltpu.PrefetchScalarGridSpec(
            num_scalar_prefetch=1, grid=(nb,),
            in_specs=[pl.BlockSpec((N_CHIPS, rb, cc), lambda i, p: (0, i, 0)),
                      pl.BlockSpec(memory_space=pl.ANY)],
            out_specs=pl.BlockSpec((rb, cc), lambda i, p: (p[0] * nb + i, 0))),
        compiler_params=_params(dimension_semantics=("arbitrary",)),
    )(pos, got, after)


def _adamw_math(w, g, m, v):
    m = ADAM_B1 * m + (1.0 - ADAM_B1) * g
    v = ADAM_B2 * v + (1.0 - ADAM_B2) * jnp.square(g)
    m_hat = m / (1.0 - ADAM_B1 ** ADAM_STEP)
    v_hat = v / (1.0 - ADAM_B2 ** ADAM_STEP)
    delta = -ADAM_LR * (m_hat / (jnp.sqrt(v_hat) + ADAM_EPS) + ADAM_WD * w)
    return delta, m, v


def _adamw_big(g, w, m, v, name):
    r, cc = w.shape
    rb = _row_block(r)

    def body(g_ref, w_ref, m_ref, v_ref, d_ref, mo_ref, vo_ref):
        d, mn, vn = _adamw_math(w_ref[...], g_ref[...], m_ref[...], v_ref[...])
        d_ref[...] = d
        mo_ref[...] = mn
        vo_ref[...] = vn

    spec = pl.BlockSpec((rb, cc), lambda i: (i, 0))
    return pl.pallas_call(
        body, name=name,
        out_shape=(jax.ShapeDtypeStruct((r, cc), F32),) * 3,
        grid=(r // rb,), in_specs=[spec] * 4, out_specs=(spec,) * 3,
        compiler_params=_params(dimension_semantics=("arbitrary",)),
    )(g, w, m, v)


def _adamw_ada(cond_t, dmod, w, m, v):
    d, na = w.shape
    cb = min(na, 256)
    assert na % cb == 0

    def body(ct_ref, dm_ref, w_ref, m_ref, v_ref, g_ref, d_ref, mo_ref, vo_ref):
        g = jnp.dot(ct_ref[...], dm_ref[...], preferred_element_type=F32, precision=lax.Precision.HIGHEST)
        dl, mn, vn = _adamw_math(w_ref[...], g, m_ref[...], v_ref[...])
        g_ref[...] = g
        d_ref[...] = dl
        mo_ref[...] = mn
        vo_ref[...] = vn

    spec = pl.BlockSpec((d, cb), lambda i: (0, i))
    return pl.pallas_call(
        body, name="adamw_w_ada",
        out_shape=(jax.ShapeDtypeStruct((d, na), F32),) * 4,
        grid=(na // cb,),
        in_specs=[pl.BlockSpec((d, N_DEV), lambda i: (0, 0)), pl.BlockSpec((N_DEV, cb), lambda i: (0, i)),
                  spec, spec, spec],
        out_specs=(spec,) * 4,
        compiler_params=_params(dimension_semantics=("arbitrary",)),
    )(cond_t, dmod, w, m, v)


def _adamw_small(gs, ws, ms, vs):
    n = len(gs)

    def body(*refs):
        ins, outs = refs[:4 * n], refs[4 * n:]
        for i in range(n):
            d, mn, vn = _adamw_math(ins[n + i][...], ins[i][...], ins[2 * n + i][...], ins[3 * n + i][...])
            outs[i][...] = d
            outs[n + i][...] = mn
            outs[2 * n + i][...] = vn

    shapes = tuple(jax.ShapeDtypeStruct(w.shape, F32) for w in ws)
    return pl.pallas_call(
        body, name="adamw_small",
        out_shape=shapes * 3,
        in_specs=[VMEM_SPEC] * (4 * n), out_specs=(VMEM_SPEC,) * (3 * n),
        compiler_params=_params(),
    )(*gs, *ws, *ms, *vs)


def _layer_norm_fwd(r):
    mu = jnp.mean(r, axis=-1, keepdims=True)
    xc = r - mu
    var = jnp.mean(jnp.square(xc), axis=-1, keepdims=True)
    rstd = lax.rsqrt(var + LN_EPS)
    return xc * rstd, rstd


def _layer_norm_bwd(dy, xhat, rstd, gain):
    dxh = dy * gain
    m1 = jnp.mean(dxh, axis=-1, keepdims=True)
    m2 = jnp.mean(dxh * xhat, axis=-1, keepdims=True)
    return rstd * (dxh - m1 - xhat * m2)


def _inv_count(tile, tm, win):
    t = (tile * tm + lax.broadcasted_iota(jnp.int32, (tm, 1), 0) + 1).astype(F32)
    return 1.0 / jnp.minimum(t, float(win))


def _fwd_mix(x, mod, conv_w, w_in4, w_pool, pool_scale, w_out, ln_g, ln_b, late_stacks, tm):
    t_len, d = x.shape
    cw = w_in4.shape[2]
    dg = cw // N_GROUPS
    nt = t_len // tm
    n_late = len(late_stacks)
    n_cp = n_late * (N_CHIPS - 1)
    pass_step = max(nt - 3, 0)

    def body(x_ref, mod_ref, cw_ref, win_ref, wp_ref, ps_ref, wout_ref, g_ref, b_ref, *rest):
        late = rest[n_late:2 * n_late]
        x1_ref, zs_ref, mix_ref, h1_ref, cat_ref = rest[2 * n_late:2 * n_late + 5]
        ubuf, vbuf, s_ici, r_ici, s_d2d, r_d2d = rest[2 * n_late + 5:]
        i = pl.program_id(0)
        start, forward, finish = _stack_gather(late, [s.shape[1] for s in late_stacks], s_ici, r_ici, s_d2d, r_d2d)

        @pl.when(i == 0)
        def _():
            start()
            ubuf[0:HALO, :] = jnp.zeros((HALO, cw), F32)
            vbuf[0:HALO, :] = jnp.zeros((HALO, cw), F32)

        xv = x_ref[...]
        sh1, sc1, g1 = mod_ref[0:1, 0:d], mod_ref[0:1, d:2 * d], mod_ref[0:1, 2 * d:3 * d]
        h1 = (xv * (1.0 + sc1) + sh1).astype(BF16)
        h1_ref[...] = h1
        zb = _dot(h1, win_ref[0])
        zc = _dot(h1, win_ref[1])
        zv = _dot(h1, win_ref[2])
        zp = _dot(h1, win_ref[3])
        u = zc * zv
        ubuf[HALO:HALO + tm, :] = u
        u1 = ubuf[pl.ds(HALO - 1, tm), :]
        u2 = ubuf[pl.ds(HALO - 2, tm), :]
        conv = cw_ref[0:1, :] * u2 + cw_ref[1:2, :] * u1 + cw_ref[2:3, :] * u
        ubuf[0:HALO, :] = ubuf[tm:tm + HALO, :]
        yc = zb * conv
        vbuf[HALO:HALO + tm, :] = zp
        ps, pms = [], []
        for gi, win in enumerate(POOL_WINDOWS):
            sl = slice(gi * dg, (gi + 1) * dg)
            acc = zp[:, sl]
            for s in range(1, win):
                acc = acc + vbuf[pl.ds(HALO - s, tm), sl]
            p_g = acc * _inv_count(i, tm, win) - zp[:, sl]
            ps.append(p_g)
            pms.append(_dot(p_g.astype(BF16), wp_ref[gi]))
        vbuf[0:HALO, :] = vbuf[tm:tm + HALO, :]
        pooled = jnp.concatenate(ps, axis=1)
        yp = jnp.concatenate(pms, axis=1) * ps_ref[...]
        cat = jnp.concatenate([yc, yp], axis=1).astype(BF16)
        cat_ref[...] = cat
        mix = _dot(cat, wout_ref[...])
        mix_ref[...] = mix
        xhat, _ = _layer_norm_fwd(DEEPNORM_ALPHA * xv + (1.0 + g1) * mix)
        x1_ref[...] = xhat * g_ref[...] + b_ref[...]
        zs_ref[:, 0 * cw:1 * cw] = zb.astype(BF16)
        zs_ref[:, 1 * cw:2 * cw] = zc.astype(BF16)
        zs_ref[:, 2 * cw:3 * cw] = zv.astype(BF16)
        zs_ref[:, 3 * cw:4 * cw] = conv.astype(BF16)
        zs_ref[:, 4 * cw:5 * cw] = pooled.astype(BF16)

        @pl.when(i == pass_step)
        def _():
            forward()

        @pl.when(i == nt - 1)
        def _():
            finish()

    tile = lambda w: pl.BlockSpec((tm, w), lambda i: (i, 0))
    n_in = 9
    return pl.pallas_call(
        body, name="fwd_mix",
        out_shape=tuple(jax.ShapeDtypeStruct(s.shape, s.dtype) for s in late_stacks)
        + (jax.ShapeDtypeStruct((t_len, d), F32), jax.ShapeDtypeStruct((t_len, 5 * cw), BF16),
           jax.ShapeDtypeStruct((t_len, d), F32), jax.ShapeDtypeStruct((t_len, d), BF16),
           jax.ShapeDtypeStruct((t_len, 2 * cw), BF16)),
        grid=(nt,),
        in_specs=[tile(d)] + [VMEM_SPEC] * 8 + [HBM_SPEC] * n_late,
        out_specs=(HBM_SPEC,) * n_late + (tile(d), tile(5 * cw), tile(d), tile(d), tile(2 * cw)),
        input_output_aliases={n_in + k: k for k in range(n_late)},
        scratch_shapes=[pltpu.VMEM((tm + HALO, cw), F32), pltpu.VMEM((tm + HALO, cw), F32),
                        _dma_sems(n_cp), _dma_sems(n_cp), _dma_sems(n_cp), _dma_sems(n_cp)],
        compiler_params=_params(dimension_semantics=("arbitrary",)),
    )(x, mod, conv_w, w_in4, w_pool, pool_scale, w_out, ln_g, ln_b, *late_stacks)


def _mlp_fwd_bwd(x1, tgt, mod, w_mi4, w_mo4, ln_g, ln_b, tm):
    t_len, d = x1.shape
    fq = w_mi4.shape[2]
    nt = t_len // tm

    def body(x1_ref, tgt_ref, mod_ref, wi_ref, wo_ref, g_ref, b_ref,
             dx1_ref, da_ref, s_ref, df_ref, h2_ref, st_ref, relu_buf, acc):
        i = pl.program_id(0)

        @pl.when(i == 0)
        def _():
            acc[...] = jnp.zeros(acc.shape, F32)

        x1v = x1_ref[...]
        sh2, sc2, g2 = mod_ref[0:1, 3 * d:4 * d], mod_ref[0:1, 4 * d:5 * d], mod_ref[0:1, 5 * d:6 * d]
        h2 = (x1v * (1.0 + sc2) + sh2).astype(BF16)
        h2_ref[...] = h2
        f = jnp.zeros((tm, d), F32)
        for j in range(N_CHIPS):
            a = jnp.maximum(_dot(h2, wi_ref[j]), 0.0)
            relu_buf[j] = a
            s = (a * a).astype(BF16)
            s_ref[j] = s
            f = f + _dot(s, wo_ref[j])
        xhat, rstd = _layer_norm_fwd(DEEPNORM_ALPHA * x1v + (1.0 + g2) * f)
        gain = g_ref[...]
        err = xhat * gain + b_ref[...] - tgt_ref[...]
        dy = err * (1.0 / d)
        dr2 = _layer_norm_bwd(dy, xhat, rstd, gain)
        df = ((1.0 + g2) * dr2).astype(BF16)
        df_ref[...] = df
        dh2 = jnp.zeros((tm, d), F32)
        for j in range(N_CHIPS):
            ds = _dot_nt(df, wo_ref[j])
            da = (ds * (2.0 * relu_buf[j])).astype(BF16)
            da_ref[j] = da
            dh2 = dh2 + _dot_nt(da, wi_ref[j])
        dx1_ref[...] = DEEPNORM_ALPHA * dr2 + dh2 * (1.0 + sc2)
        acc[0] += _fold8(dy * xhat)
        acc[1] += _fold8(dy)
        acc[2] += _fold8(dr2 * f)
        acc[3] += _fold8(dh2 * x1v)
        acc[4] += _fold8(dh2)
        acc[5] += _fold8(err * err)

        @pl.when(i == nt - 1)
        def _():
            for k in range(5):
                st_ref[k:k + 1, :] = jnp.sum(acc[k], axis=0, keepdims=True)
            loss = jnp.sum(acc[5]) * (0.5 / d)
            st_ref[5:6, :] = jnp.broadcast_to(loss, (1, d))
            st_ref[6:8, :] = jnp.zeros((2, d), F32)

    tile = lambda w: pl.BlockSpec((tm, w), lambda i: (i, 0))
    tile4 = pl.BlockSpec((N_CHIPS, tm, fq), lambda i: (0, i, 0))
    return pl.pallas_call(
        body, name="mlp_fwd_bwd",
        out_shape=(jax.ShapeDtypeStruct((t_len, d), F32),
                   jax.ShapeDtypeStruct((N_CHIPS, t_len, fq), BF16), jax.ShapeDtypeStruct((N_CHIPS, t_len, fq), BF16),
                   jax.ShapeDtypeStruct((t_len, d), BF16), jax.ShapeDtypeStruct((t_len, d), BF16),
                   jax.ShapeDtypeStruct((SUBLANES, d), F32)),
        grid=(nt,),
        in_specs=[tile(d), tile(d)] + [VMEM_SPEC] * 5,
        out_specs=(tile(d), tile4, tile4, tile(d), tile(d), pl.BlockSpec((SUBLANES, d), lambda i: (0, 0))),
        scratch_shapes=[pltpu.VMEM((N_CHIPS, tm, fq), F32), pltpu.VMEM((6, SUBLANES, d), F32)],
        compiler_params=_params(dimension_semantics=("arbitrary",)),
    )(x1, tgt, mod, w_mi4, w_mo4, ln_g, ln_b)


def _mlp_wgrad(h2, da4, s4, df, pos, tt):
    t_len, d = h2.shape
    fq = da4.shape[2]
    nt = t_len // tt
    assert nt >= 3
    rh = (d // 2, fq // 2)
    last = N_CHIPS - 1

    def body(pos_ref, h2_ref, da_ref, s_ref, df_ref, qi_ref, qo_ref, gi_ref, go_ref,
             acc_i, acc_o, rcv_i, rcv_o, s_sem, r_sem):
        j, t = pl.program_id(0), pl.program_id(1)
        slot = lax.rem(j, 2)
        x, y, c = _mesh_pos()
        accs, rcvs, q_refs, g_refs = (acc_i, acc_o), (rcv_i, rcv_o), (qi_ref, qo_ref), (gi_ref, go_ref)

        def to_sibling(a, sl):
            theirs = pl.multiple_of((1 - c) * rh[a], rh[a])
            return pltpu.make_async_remote_copy(accs[a].at[sl].at[pl.ds(theirs, rh[a]), :], rcvs[a].at[sl],
                                                s_sem.at[2 * sl + a], r_sem.at[2 * sl + a],
                                                device_id=(x, y, 1 - c), device_id_type=MESH)

        def emit(chunk, sl):
            for a in range(2):
                to_sibling(a, sl).wait()
                mine = pl.multiple_of(c * rh[a], rh[a])
                q = (accs[a][sl, pl.ds(mine, rh[a]), :] + rcvs[a][sl]).astype(BF16)
                q_refs[a][...] = q

                @pl.when(chunk == pos_ref[1])
                def _():
                    g_refs[a][...] = q

        @pl.when(t == 0)
        def _():
            acc_i[slot] = jnp.zeros((d, fq), F32)
            acc_o[slot] = jnp.zeros((fq, d), F32)

        acc_i[slot] += _dot_tn(h2_ref[...], da_ref[...])
        acc_o[slot] += _dot_tn(s_ref[...], df_ref[...])

        @pl.when((t == 1) & (j >= 1))
        def _():
            emit(j - 1, 1 - slot)

        @pl.when(t == nt - 1)
        def _():
            for a in range(2):
                to_sibling(a, slot).start()

        @pl.when((t == nt - 1) & (j == last))
        def _():
            emit(j, slot)

    q_index = lambda j, t, p: (jnp.where(t == nt - 1, j, jnp.maximum(j - 1, 0)), 0, 0)
    got_index = lambda j, t, p: (p[1], 0, 0)
    blocks = ((None, rh[0], fq), (None, rh[1], d))
    return pl.pallas_call(
        body, name="mlp_wgrad",
        out_shape=tuple(jax.ShapeDtypeStruct((N_CHIPS,) + b[1:], BF16) for b in blocks) * 2,
        grid_spec=pltpu.PrefetchScalarGridSpec(
            num_scalar_prefetch=1, grid=(N_CHIPS, nt),
            in_specs=[pl.BlockSpec((tt, d), lambda j, t, p: (t, 0)),
                      pl.BlockSpec((None, tt, fq), lambda j, t, p: (j, t, 0)),
                      pl.BlockSpec((None, tt, fq), lambda j, t, p: (j, t, 0)),
                      pl.BlockSpec((tt, d), lambda j, t, p: (t, 0))],
            out_specs=tuple(pl.BlockSpec(b, q_index) for b in blocks)
            + tuple(pl.BlockSpec(b, got_index) for b in blocks),
            scratch_shapes=[pltpu.VMEM((2, d, fq), F32), pltpu.VMEM((2, fq, d), F32),
                            pltpu.VMEM((2, rh[0], fq), F32), pltpu.VMEM((2, rh[1], d), F32),
                            _dma_sems(4), _dma_sems(4)]),
        compiler_params=_params(dimension_semantics=("arbitrary", "arbitrary")),
    )(pos, h2, da4, s4, df)


def _bwd_mix(x, mix, dx1, zs, mod, conv_w, w_in4, w_pool, pool_scale, w_out, ln_g, after, tm):
    t_len, d = x.shape
    cw = w_in4.shape[2]
    dg = cw // N_GROUPS
    nt = t_len // tm

    def body(x_ref, mix_ref, dx1_ref, zs_ref, mod_ref, cw_ref, win_ref, wp_ref, ps_ref, wout_ref, g_ref, after_ref,
             gx_ref, dz_ref, dmix_ref, sd_ref, sc_ref, dwp_ref, dbuf, qbuf, acc_d, acc_c, acc_p):
        i = pl.program_id(0)
        tile_idx = nt - 1 - i

        @pl.when(i == 0)
        def _():
            dbuf[tm:tm + HALO, :] = jnp.zeros((HALO, cw), F32)
            qbuf[tm:tm + HALO, :] = jnp.zeros((HALO, cw), F32)
            acc_d[...] = jnp.zeros(acc_d.shape, F32)
            acc_c[...] = jnp.zeros(acc_c.shape, F32)
            acc_p[...] = jnp.zeros(acc_p.shape, F32)

        xv, mixv, dx1v = x_ref[...], mix_ref[...], dx1_ref[...]
        sc1, g1 = mod_ref[0:1, d:2 * d], mod_ref[0:1, 2 * d:3 * d]
        xhat, rstd = _layer_norm_fwd(DEEPNORM_ALPHA * xv + (1.0 + g1) * mixv)
        dr1 = _layer_norm_bwd(dx1v, xhat, rstd, g_ref[...])
        dmix = ((1.0 + g1) * dr1).astype(BF16)
        dmix_ref[...] = dmix
        dcat = _dot_nt(dmix, wout_ref[...])
        dyc, dyp = dcat[:, 0:cw], dcat[:, cw:2 * cw]
        zb, zc, zv = (zs_ref[:, k * cw:(k + 1) * cw].astype(F32) for k in range(3))
        conv, pooled = zs_ref[:, 3 * cw:4 * cw].astype(F32), zs_ref[:, 4 * cw:5 * cw]
        dzb = dyc * conv
        dcv = dyc * zb
        dbuf[0:tm, :] = dcv
        d1 = dbuf[pl.ds(1, tm), :]
        d2 = dbuf[pl.ds(2, tm), :]
        du = cw_ref[2:3, :] * dcv + cw_ref[1:2, :] * d1 + cw_ref[0:1, :] * d2
        dbuf[tm:tm + HALO, :] = dbuf[0:HALO, :]
        u = zc * zv
        acc_c[0] += _fold8(u * d2)
        acc_c[1] += _fold8(u * d1)
        acc_c[2] += _fold8(u * dcv)
        dzc = du * zv
        dzv = du * zc
        dpw = dyp * ps_ref[...]
        dps, pms = [], []
        for gi, win in enumerate(POOL_WINDOWS):
            sl = slice(gi * dg, (gi + 1) * dg)
            p_g = pooled[:, sl].astype(BF16)
            dpw_g = dpw[:, sl].astype(BF16)
            pms.append(_dot(p_g, wp_ref[gi]))
            acc_p[gi] += _dot_tn(p_g, dpw_g)
            dp_g = _dot_nt(dpw_g, wp_ref[gi])
            dps.append(dp_g)
            qbuf[0:tm, sl] = dp_g * _inv_count(tile_idx, tm, win)
        acc_c[3] += _fold8(dyp * jnp.concatenate(pms, axis=1))
        dzps = []
        for gi, win in enumerate(POOL_WINDOWS):
            sl = slice(gi * dg, (gi + 1) * dg)
            acc = qbuf[0:tm, sl]
            for s in range(1, win):
                acc = acc + qbuf[pl.ds(s, tm), sl]
            dzps.append(acc - dps[gi])
        qbuf[tm:tm + HALO, :] = qbuf[0:HALO, :]
        dz = [dzb.astype(BF16), dzc.astype(BF16), dzv.astype(BF16), jnp.concatenate(dzps, axis=1).astype(BF16)]
        dh1 = jnp.zeros((tm, d), F32)
        for j in range(N_CHIPS):
            dz_ref[j] = dz[j]
            dh1 = dh1 + _dot_nt(dz[j], win_ref[j])
        gx_ref[...] = DEEPNORM_ALPHA * dr1 + dh1 * (1.0 + sc1)
        acc_d[0] += _fold8(dx1v * xhat)
        acc_d[1] += _fold8(dx1v)
        acc_d[2] += _fold8(dr1 * mixv)
        acc_d[3] += _fold8(dh1 * xv)
        acc_d[4] += _fold8(dh1)

        @pl.when(i == nt - 1)
        def _():
            for k in range(5):
                sd_ref[k:k + 1, :] = jnp.sum(acc_d[k], axis=0, keepdims=True)
            sd_ref[5:8, :] = jnp.zeros((3, d), F32)
            for k in range(4):
                sc_ref[k:k + 1, :] = jnp.sum(acc_c[k], axis=0, keepdims=True)
            sc_ref[4:8, :] = jnp.zeros((4, cw), F32)
            dwp_ref[...] = acc_p[...]

    rtile = lambda w: pl.BlockSpec((tm, w), lambda i: (nt - 1 - i, 0))
    whole = lambda shape: pl.BlockSpec(shape, lambda i: tuple(0 for _ in shape))
    return pl.pallas_call(
        body, name="bwd_mix",
        out_shape=(jax.ShapeDtypeStruct((t_len, d), F32), jax.ShapeDtypeStruct((N_CHIPS, t_len, cw), BF16),
                   jax.ShapeDtypeStruct((t_len, d), BF16), jax.ShapeDtypeStruct((SUBLANES, d), F32),
                   jax.ShapeDtypeStruct((SUBLANES, cw), F32), jax.ShapeDtypeStruct((N_GROUPS, dg, dg), F32)),
        grid=(nt,),
        in_specs=[rtile(d), rtile(d), rtile(d), rtile(5 * cw)] + [VMEM_SPEC] * 7
        + [pl.BlockSpec(memory_space=pl.ANY)],
        out_specs=(rtile(d), pl.BlockSpec((N_CHIPS, tm, cw), lambda i: (0, nt - 1 - i, 0)), rtile(d),
                   whole((SUBLANES, d)), whole((SUBLANES, cw)), whole((N_GROUPS, dg, dg))),
        scratch_shapes=[pltpu.VMEM((tm + HALO, cw), F32), pltpu.VMEM((tm + HALO, cw), F32),
                        pltpu.VMEM((5, SUBLANES, d), F32), pltpu.VMEM((4, SUBLANES, cw), F32),
                        pltpu.VMEM((N_GROUPS, dg, dg), F32)],
        compiler_params=_params(dimension_semantics=("arbitrary",)),
    )(x, mix, dx1, zs, mod, conv_w, w_in4, w_pool, pool_scale, w_out, ln_g, after)


def _mix_wgrad(h1, dz4, cat, dmix, small, n_all, tt):
    t_len, d = h1.shape
    cw = dz4.shape[2]
    nt = t_len // tt
    ro = (2 * cw) // N_CHIPS
    rh = (d // 2, ro // 2)
    n_small = len(small)

    def body(h1_ref, dz_ref, cat_ref, dmix_ref, *rest):
        p_refs = rest[:n_small]
        qi_ref, qo_ref, gi_ref, go_ref = rest[n_small:n_small + 4]
        sum_refs = rest[n_small + 4:2 * n_small + 4]
        all_refs = rest[2 * n_small + 4:2 * n_small + 4 + n_all]
        scratch = rest[2 * n_small + 4 + n_all:]
        acc_i, acc_o, rcv_i, rcv_o = scratch[:4]
        all_bufs = scratch[4:4 + n_small]
        s_sem, r_sem, s_small, r_small_sem = scratch[4 + n_small:]
        t = pl.program_id(0)
        x, y, c = _mesh_pos()
        chip = 2 * x + y
        groups = [(p_refs[k], sum_refs[k], all_refs[k] if k < n_all else None, all_bufs[k]) for k in range(n_small)]
        small_start, small_finish = _small_allreduce_steps(groups, s_small, r_small_sem)
        accs, rcvs, q_refs, g_refs = (acc_i, acc_o), (rcv_i, rcv_o), (qi_ref, qo_ref), (gi_ref, go_ref)

        @pl.when(t == 0)
        def _():
            small_start()
            acc_i[...] = jnp.zeros(acc_i.shape, F32)
            acc_o[...] = jnp.zeros(acc_o.shape, F32)

        h1v, dmixv = h1_ref[...], dmix_ref[...]
        for j in range(N_CHIPS):
            acc_i[j] += _dot_tn(h1v, dz_ref[j])
            acc_o[j] += _dot_tn(cat_ref[:, j * ro:(j + 1) * ro], dmixv)

        @pl.when(t == nt - 1)
        def _():
            cps = []
            for a in range(2):
                theirs = pl.multiple_of((1 - c) * rh[a], rh[a])
                cp = pltpu.make_async_remote_copy(accs[a].at[:, pl.ds(theirs, rh[a]), :], rcvs[a],
                                                  s_sem.at[a], r_sem.at[a],
                                                  device_id=(x, y, 1 - c), device_id_type=MESH)
                cp.start()
                cps.append(cp)
            for a in range(2):
                cps[a].wait()
                mine = pl.multiple_of(c * rh[a], rh[a])
                for j in range(N_CHIPS):
                    q_refs[a][j] = (accs[a][j, pl.ds(mine, rh[a]), :] + rcvs[a][j]).astype(BF16)
                g_refs[a][chip] = q_refs[a][chip]
            small_finish()

    stacks = ((N_CHIPS, rh[0], cw), (N_CHIPS, rh[1], d))
    n_cp = n_small * (N_DEV - 1)
    return pl.pallas_call(
        body, name="mix_wgrad",
        out_shape=tuple(jax.ShapeDtypeStruct(s, BF16) for s in stacks) * 2
        + tuple(jax.ShapeDtypeStruct(a.shape, F32) for a in small)
        + tuple(jax.ShapeDtypeStruct((N_DEV,) + a.shape, F32) for a in small[:n_all]),
        grid=(nt,),
        in_specs=[pl.BlockSpec((tt, d), lambda t: (t, 0)), pl.BlockSpec((N_CHIPS, tt, cw), lambda t: (0, t, 0)),
                  pl.BlockSpec((tt, 2 * cw), lambda t: (t, 0)), pl.BlockSpec((tt, d), lambda t: (t, 0))]
        + [VMEM_SPEC] * n_small,
        out_specs=(VMEM_SPEC,) * (4 + n_small + n_all),
        scratch_shapes=[pltpu.VMEM((N_CHIPS, d, cw), F32), pltpu.VMEM((N_CHIPS, ro, d), F32),
                        pltpu.VMEM(stacks[0], F32), pltpu.VMEM(stacks[1], F32)]
        + [pltpu.VMEM((N_DEV,) + a.shape, F32) for a in small]
        + [_dma_sems(2), _dma_sems(2), _dma_sems(n_cp), _dma_sems(n_cp)],
        compiler_params=_params(dimension_semantics=("arbitrary",)),
    )(h1, dz4, cat, dmix, *small)


def kernel(x, c, w_ada, b_ada, w_in, conv_w, w_pool, pool_scale, w_out, ln1_g, ln1_b, w_mlp_in, w_mlp_out, ln2_g, ln2_b, loss_target, m_w_ada, m_b_ada, m_w_in, m_conv_w, m_w_pool, m_pool_scale, m_w_out, m_ln1_g, m_ln1_b, m_w_mlp_in, m_w_mlp_out, m_ln2_g, m_ln2_b, v_w_ada, v_b_ada, v_w_in, v_conv_w, v_w_pool, v_pool_scale, v_w_out, v_ln1_g, v_ln1_b, v_w_mlp_in, v_w_mlp_out, v_ln2_g, v_ln2_b):
    t_len, d = x.shape[1], x.shape[2]
    cw = w_in.shape[2]
    cq = conv_w.shape[2]
    dg = w_pool.shape[2]
    assert cw == N_GROUPS * dg and cq * N_CHIPS == cw and LANES % cq == 0
    tm_mix = min(MIX_TOKEN_TILE, t_len)
    tm_mlp = min(MLP_TOKEN_TILE, t_len)
    tt = min(WGRAD_TILE, t_len // 4)
    chip = 2 * lax.axis_index("x") + lax.axis_index("y")
    pos = jnp.stack([lax.axis_index("c"), chip]).astype(jnp.int32)

    x2, tgt = x[0], loss_target[0]
    big_w = [w_in[0], w_out[0], w_mlp_in[0], w_mlp_out[0]]
    big_m = [m_w_in[0], m_w_out[0], m_w_mlp_in[0], m_w_mlp_out[0]]
    big_v = [v_w_in[0], v_w_out[0], v_w_mlp_in[0], v_w_mlp_out[0]]
    names = ["w_in", "w_out", "w_mlp_in", "w_mlp_out"]

    mod, cond_all, conv_full, w_in4, w_out4, w_mi_own, w_mo_own = _setup_exchange(
        c, w_ada[0], b_ada, conv_w[0], big_w, 2)
    w_out_full = w_out4.reshape(2 * cw, d)
    w_pool_bf = w_pool[0].astype(BF16)

    w_mi4, w_mo4, x1, zs, mix, h1, cat = _fwd_mix(x2, mod, conv_full, w_in4, w_pool_bf, pool_scale, w_out_full,
                                                  ln1_g, ln1_b, [w_mi_own, w_mo_own], tm_mix)
    dx1, da4, s4, df, h2, st2 = _mlp_fwd_bwd(x1, tgt, mod, w_mi4, w_mo4, ln2_g, ln2_b, tm_mlp)
    mlp_qi, mlp_qo, mlp_gi, mlp_go = _mlp_wgrad(h2, da4, s4, df, pos, tt)
    mlp_started = _exchange_start([mlp_qi, mlp_qo], [mlp_gi, mlp_go], "mlp")
    grad_x, dz4, dmix, st1, stc, dw_pool = _bwd_mix(x2, mix, dx1, zs, mod, conv_full, w_in4, w_pool_bf, pool_scale,
                                                    w_out_full, ln1_g, mlp_started[-1], tm_mix)

    mix_qi, mix_qo, mix_gi, mix_go, t1, t2, tc, g_w_pool, all1, all2 = _mix_wgrad(
        h1, dz4, cat, dmix, [st1, st2, stc, dw_pool], 2, tt)
    loss = t2[5, 0]
    g_ln1_g, g_ln1_b, g_ln2_g, g_ln2_b = t1[0:1], t1[1:2], t2[0:1], t2[1:2]
    g_pool_scale = tc[3:4]
    g_conv = lax.dynamic_slice_in_dim(tc[0:3], chip * cq, cq, axis=1)
    g_b_ada = jnp.concatenate([t1[4:5], t1[3:4], t1[2:3], t2[4:5], t2[3:4], t2[2:3]], axis=1)
    dmod_all = jnp.concatenate([all1[:, 4], all1[:, 3], all1[:, 2], all2[:, 4], all2[:, 3], all2[:, 2]], axis=1)
    na = w_ada.shape[2]
    dmod_mine = lax.dynamic_slice_in_dim(dmod_all, chip * na, na, axis=1)

    mlp_got = _exchange_wait(mlp_started, t1, "mlp")
    mix_started = _exchange_start([mix_qi, mix_qo], [mix_gi, mix_go], "mix")
    mlp_halves = [_sum_chips(g, pos, mix_started[-1], "rs_total_" + n) for g, n in zip(mlp_got, names[2:])]
    mlp_g = _sibling_join(mlp_halves, "mlp")
    mlp_out = [_adamw_big(g, w, m, v, "adamw_" + n)
               for g, w, m, v, n in zip(mlp_g, big_w[2:], big_m[2:], big_v[2:], names[2:])]
    g_ada, d_ada, nm_ada, nv_ada = _adamw_ada(cond_all.T, dmod_mine, w_ada[0], m_w_ada[0], v_w_ada[0])
    mix_got = _exchange_wait(mix_started, d_ada, "mix")
    mix_halves = [_sum_chips(g, pos, d_ada, "rs_total_" + n) for g, n in zip(mix_got, names[:2])]
    mix_g = _sibling_join(mix_halves, "mix")
    mix_out = [_adamw_big(g, w, m, v, "adamw_" + n)
               for g, w, m, v, n in zip(mix_g, big_w[:2], big_m[:2], big_v[:2], names[:2])]
    big_g = list(mix_g) + list(mlp_g)
    big_out = mix_out + mlp_out

    small_g = [g_b_ada, g_conv, g_w_pool.reshape(-1, dg), g_pool_scale, g_ln1_g, g_ln1_b, g_ln2_g, g_ln2_b]
    small_w = [b_ada, conv_w[0], w_pool.reshape(-1, dg), pool_scale, ln1_g, ln1_b, ln2_g, ln2_b]
    small_m = [m_b_ada, m_conv_w[0], m_w_pool.reshape(-1, dg), m_pool_scale, m_ln1_g, m_ln1_b, m_ln2_g, m_ln2_b]
    small_v = [v_b_ada, v_conv_w[0], v_w_pool.reshape(-1, dg), v_pool_scale, v_ln1_g, v_ln1_b, v_ln2_g, v_ln2_b]
    sm = _adamw_small(small_g, small_w, small_m, small_v)
    ns = len(small_g)
    s_delta, s_m, s_v = sm[:ns], sm[ns:2 * ns], sm[2 * ns:]

    def assemble(ada, small, big):
        return [ada[None], small[0], big[0][None], small[1][None], small[2].reshape(w_pool.shape), small[3],
                big[1][None], small[4], small[5], big[2][None], big[3][None], small[6], small[7]]

    grads = assemble(g_ada, small_g, big_g)
    deltas = assemble(d_ada, s_delta, [o[0] for o in big_out])
    new_m = assemble(nm_ada, s_m, [o[1] for o in big_out])
    new_v = assemble(nv_ada, s_v, [o[2] for o in big_out])
    return (loss, grad_x[None], *grads, *deltas, *new_m, *new_v)
```

```python
import jax
import jax.numpy as jnp
from jax import lax
from jax.experimental import pallas as pl
from jax.experimental.pallas import tpu as pltpu

F32 = jnp.float32
BF16 = jnp.bfloat16
MESH = pl.DeviceIdType.MESH

LN_EPS = 1e-5
DEEPNORM_ALPHA = 2.0 ** 0.25
POOL_WINDOWS = (2, 4, 8, 16)
N_GROUPS = len(POOL_WINDOWS)
HALO = 16
N_CHIPS = 4
N_DEV = 8
LANES = 128
SUBLANES = 8
VMEM_LIMIT = 56 * 1024 * 1024
MIX_TOKEN_TILE = 512
MLP_TOKEN_TILE = 256
WGRAD_TILE = 1024

ADAM_LR = 0.001
ADAM_B1 = 0.9
ADAM_B2 = 0.999
ADAM_EPS = 1e-08
ADAM_WD = 0.01
ADAM_STEP = 10

VMEM_SPEC = pl.BlockSpec(memory_space=pltpu.VMEM)
HBM_SPEC = pl.BlockSpec(memory_space=pltpu.HBM)


def _dot(a, b):
    return jnp.dot(a, b, preferred_element_type=F32)


def _dot_nt(a, b):
    return lax.dot_general(a, b, (((1,), (1,)), ((), ())), preferred_element_type=F32)


def _dot_tn(a, b):
    return lax.dot_general(a, b, (((0,), (0,)), ((), ())), preferred_element_type=F32)


def _fold8(v):
    r, n = v.shape
    return jnp.sum(v.reshape(r // SUBLANES, SUBLANES, n), axis=0)


def _mesh_pos():
    return lax.axis_index("x"), lax.axis_index("y"), lax.axis_index("c")


def _flip(v, bit):
    return 1 - v if bit else v


def _params(**kw):
    return pltpu.CompilerParams(vmem_limit_bytes=VMEM_LIMIT, **kw)


def _dma_sems(n):
    return pltpu.SemaphoreType.DMA((n,))


def _stack_gather(stacks, rows, s_ici, r_ici, s_d2d, r_d2d):
    x, y, c = _mesh_pos()
    chip = 2 * x + y
    pairs = [(i, m) for i in range(len(stacks)) for m in range(1, N_CHIPS)]

    def blk(i, slot, which):
        rh = rows[i] // 2
        return stacks[i].at[slot].at[pl.ds(which * rh, rh), :]

    def other(m):
        return 2 * _flip(x, m & 2) + _flip(y, m & 1)

    def ici(i, m, slot, to):
        k = i * (N_CHIPS - 1) + m - 1
        return pltpu.make_async_remote_copy(blk(i, slot, c), blk(i, slot, c), s_ici.at[k], r_ici.at[k],
                                            device_id=to, device_id_type=MESH)

    def d2d(i, m, which, to):
        k = i * (N_CHIPS - 1) + m - 1
        return pltpu.make_async_remote_copy(blk(i, other(m), which), blk(i, other(m), which),
                                            s_d2d.at[k], r_d2d.at[k], device_id=to, device_id_type=MESH)

    def start():
        for i, m in pairs:
            ici(i, m, chip, (_flip(x, m & 2), _flip(y, m & 1), c)).start()

    def forward(which=None):
        for i, m in pairs:
            if which is None or i == which:
                ici(i, m, other(m), (x, y, c)).wait_recv()
                d2d(i, m, c, (x, y, 1 - c)).start()

    def finish():
        for i, m in pairs:
            d2d(i, m, 1 - c, (x, y, c)).wait_recv()
        for i, m in pairs:
            ici(i, m, chip, (x, y, c)).wait_send()
            d2d(i, m, c, (x, y, c)).wait_send()

    return start, forward, finish


def _chunk_exchange(qs, gots, s_sem, r_sem):
    x, y, c = _mesh_pos()
    chip = 2 * x + y
    pairs = [(i, m) for i in range(len(qs)) for m in range(1, N_CHIPS)]

    def other(m):
        return 2 * _flip(x, m & 2) + _flip(y, m & 1)

    def send(i, m, to):
        k = i * (N_CHIPS - 1) + m - 1
        return pltpu.make_async_remote_copy(qs[i].at[other(m)], gots[i].at[chip], s_sem.at[k], r_sem.at[k],
                                            device_id=to, device_id_type=MESH)

    def arrival(i, m):
        k = i * (N_CHIPS - 1) + m - 1
        return pltpu.make_async_remote_copy(qs[i].at[other(m)], gots[i].at[other(m)], s_sem.at[k], r_sem.at[k],
                                            device_id=(x, y, c), device_id_type=MESH)

    def start():
        for i, m in pairs:
            send(i, m, (_flip(x, m & 2), _flip(y, m & 1), c)).start()

    def finish():
        for i, m in pairs:
            arrival(i, m).wait_recv()
        for i, m in pairs:
            send(i, m, (x, y, c)).wait_send()

    return start, finish


def _mod_scratch(d, na, cq):
    return [pltpu.VMEM((SUBLANES, d), F32), pltpu.VMEM((N_DEV, SUBLANES, d), F32),
            pltpu.VMEM((N_CHIPS, SUBLANES, na), F32),
            pltpu.VMEM((SUBLANES, cq), F32), pltpu.VMEM((N_CHIPS, SUBLANES, cq), F32),
            _dma_sems(N_DEV - 1), _dma_sems(N_DEV - 1),
            _dma_sems(N_CHIPS - 1), _dma_sems(N_CHIPS - 1), _dma_sems(N_CHIPS - 1), _dma_sems(N_CHIPS - 1)]


def _mod_steps(d, na, cq):
    def steps(c_ref, w_ref, b_ref, cw_ref, mod_ref, cond_ref, conv_ref,
              csend, cbuf, mbuf, cvsend, cvbuf, s1, r1, s2, r2, s3, r3):
        x, y, c = _mesh_pos()
        me = 4 * x + 2 * y + c
        chip = 2 * x + y
        csend[...] = jnp.broadcast_to(c_ref[...], (SUBLANES, d))
        cbuf[me] = csend[...]
        first = []
        for m in range(1, N_DEV):
            peer = (_flip(x, m & 4), _flip(y, m & 2), _flip(c, m & 1))
            cp = pltpu.make_async_remote_copy(csend, cbuf.at[me], s1.at[m - 1], r1.at[m - 1],
                                              device_id=peer, device_id_type=MESH)
            cp.start()
            first.append(cp)
        cvsend[...] = jnp.zeros((SUBLANES, cq), F32)
        cvsend[0:3, :] = cw_ref[...]
        cvbuf[chip] = cvsend[...]
        for m in range(1, N_DEV):
            src = 4 * _flip(x, m & 4) + 2 * _flip(y, m & 2) + _flip(c, m & 1)
            pltpu.make_async_remote_copy(csend, cbuf.at[src], s1.at[m - 1], r1.at[m - 1],
                                         device_id=(x, y, c), device_id_type=MESH).wait_recv()
        rows = lax.broadcasted_iota(jnp.int32, (SUBLANES, d), 0)
        call = jnp.zeros((SUBLANES, d), F32)
        for b in range(N_DEV):
            call = jnp.where(rows == b, cbuf[b], call)
        cond = call * jax.nn.sigmoid(call)
        cond_ref[...] = cond
        part = jnp.dot(cond, w_ref[...], preferred_element_type=F32, precision=lax.Precision.HIGHEST)
        mbuf[chip] = part
        second = []
        for m in range(1, N_CHIPS):
            peer = (_flip(x, m & 2), _flip(y, m & 1), c)
            cp = pltpu.make_async_remote_copy(mbuf.at[chip], mbuf.at[chip], s2.at[m - 1], r2.at[m - 1],
                                              device_id=peer, device_id_type=MESH)
            cp.start()
            second.append(cp)
            cp = pltpu.make_async_remote_copy(cvsend, cvbuf.at[chip], s3.at[m - 1], r3.at[m - 1],
                                              device_id=peer, device_id_type=MESH)
            cp.start()
            second.append(cp)
        for m in range(1, N_CHIPS):
            src = 2 * _flip(x, m & 2) + _flip(y, m & 1)
            pltpu.make_async_remote_copy(mbuf.at[src], mbuf.at[src], s2.at[m - 1], r2.at[m - 1],
                                         device_id=(x, y, c), device_id_type=MESH).wait_recv()
            pltpu.make_async_remote_copy(cvsend, cvbuf.at[src], s3.at[m - 1], r3.at[m - 1],
                                         device_id=(x, y, c), device_id_type=MESH).wait_recv()
        for cp in first + second:
            cp.wait_send()
        rows_n = lax.broadcasted_iota(jnp.int32, (SUBLANES, na), 0)
        for k in range(N_CHIPS):
            mine = jnp.sum(jnp.where(rows_n == me, mbuf[k], 0.0), axis=0, keepdims=True)
            mod_ref[:, k * na:(k + 1) * na] = jnp.broadcast_to(mine + b_ref[:, k * na:(k + 1) * na], (SUBLANES, na))
            conv_ref[:, k * cq:(k + 1) * cq] = cvbuf[k]

    return steps


def _setup_exchange(c_row, w_ada_s, b_ada, conv_w_s, shards, n_now):
    n = len(shards)
    n_cp = n_now * (N_CHIPS - 1)
    d, na, cq = c_row.shape[1], w_ada_s.shape[1], conv_w_s.shape[1]
    mod_scratch = _mod_scratch(d, na, cq)
    mod_steps = _mod_steps(d, na, cq)

    def body(*refs):
        mod_in, ins = refs[:4], refs[4:4 + n]
        mod_out, outs = refs[4 + n:7 + n], refs[7 + n:7 + 2 * n]
        bufs = refs[7 + 2 * n:7 + 3 * n]
        mod_scr = refs[7 + 3 * n:7 + 3 * n + len(mod_scratch)]
        s_ici, r_ici, s_d2d, r_d2d, s_loc = refs[7 + 3 * n + len(mod_scratch):]
        x, y, _ = _mesh_pos()
        chip = 2 * x + y
        def place(lo, hi):
            local = []
            for i in range(lo, hi):
                bufs[i][...] = ins[i][...].astype(BF16)
                cp = pltpu.make_async_copy(bufs[i], outs[i].at[chip], s_loc.at[i])
                cp.start()
                local.append(cp)
            return local

        start, forward, finish = _stack_gather(outs[:n_now], [s.shape[0] for s in shards[:n_now]],
                                               s_ici, r_ici, s_d2d, r_d2d)
        for cp in place(0, n_now):
            cp.wait()
        start()
        later = place(n_now, n)
        mod_steps(*mod_in, *mod_out, *mod_scr)
        forward()
        finish()
        for cp in later:
            cp.wait()

    return pl.pallas_call(
        body, name="setup_exchange",
        out_shape=(jax.ShapeDtypeStruct((SUBLANES, N_CHIPS * na), F32), jax.ShapeDtypeStruct((SUBLANES, d), F32),
                   jax.ShapeDtypeStruct((SUBLANES, N_CHIPS * cq), F32))
        + tuple(jax.ShapeDtypeStruct((N_CHIPS,) + s.shape, BF16) for s in shards),
        in_specs=[VMEM_SPEC] * (4 + n), out_specs=(VMEM_SPEC,) * 3 + (HBM_SPEC,) * n,
        scratch_shapes=[pltpu.VMEM(s.shape, BF16) for s in shards] + mod_scratch
        + [_dma_sems(n_cp), _dma_sems(n_cp), _dma_sems(n_cp), _dma_sems(n_cp), _dma_sems(n)],
        compiler_params=_params(),
    )(c_row, w_ada_s, b_ada, conv_w_s, *shards)


def _sibling_join_plan(shapes):
    def plan(refs, s_sem, r_sem):
        x, y, c = _mesh_pos()

        def half(i, which):
            rh = shapes[i][0] // 2
            return refs[i].at[pl.ds(which * rh, rh), :]

        def start():
            for i in range(len(refs)):
                pltpu.make_async_remote_copy(half(i, c), half(i, c), s_sem.at[i], r_sem.at[i],
                                             device_id=(x, y, 1 - c), device_id_type=MESH).start()

        def finish():
            for i in range(len(refs)):
                pltpu.make_async_remote_copy(half(i, 1 - c), half(i, 1 - c), s_sem.at[i], r_sem.at[i],
                                             device_id=(x, y, c), device_id_type=MESH).wait_recv()
            for i in range(len(refs)):
                pltpu.make_async_remote_copy(half(i, c), half(i, c), s_sem.at[i], r_sem.at[i],
                                             device_id=(x, y, c), device_id_type=MESH).wait_send()

        return start, finish

    return plan


def _chunk_exchange_plan(n):
    return lambda refs, s_sem, r_sem: _chunk_exchange(refs[:n], refs[n:], s_sem, r_sem)


def _sibling_join(gs, tag):
    n = len(gs)
    plan = _sibling_join_plan([g.shape for g in gs])

    def body(*refs):
        start, finish = plan(refs[n:2 * n], refs[2 * n], refs[2 * n + 1])
        start()
        finish()

    return pl.pallas_call(
        body, name="rs_sibling_join_" + tag,
        out_shape=tuple(jax.ShapeDtypeStruct(g.shape, g.dtype) for g in gs),
        in_specs=[HBM_SPEC] * n, out_specs=(HBM_SPEC,) * n,
        input_output_aliases={i: i for i in range(n)},
        scratch_shapes=[_dma_sems(n), _dma_sems(n)],
        compiler_params=_params(),
    )(*gs)


SEM_SPEC = pl.BlockSpec(memory_space=pltpu.SEMAPHORE)
SIDE_EFFECT = pltpu.SideEffectType.DATAFLOW_SIDE_EFFECTING


def _split_start(arrays, plan, n_cp, tag):
    n = len(arrays)

    def body(*refs):
        start, _ = plan(refs[n + 2:2 * n + 2], refs[n], refs[n + 1])
        start()
        token = refs[2 * n + 2]
        token[...] = jnp.zeros_like(token)

    through = [pltpu.with_memory_space_constraint(a, pltpu.HBM) for a in arrays]
    return pl.pallas_call(
        body, name=tag + "_start",
        out_shape=(pltpu.SemaphoreType.DMA((n_cp,)), pltpu.SemaphoreType.DMA((n_cp,)))
        + tuple(pltpu.HBM(a.shape, a.dtype) for a in through) + (jax.ShapeDtypeStruct((SUBLANES, LANES), F32),),
        in_specs=[HBM_SPEC] * n, out_specs=(SEM_SPEC, SEM_SPEC) + (HBM_SPEC,) * n + (VMEM_SPEC,),
        input_output_aliases={k: 2 + k for k in range(n)},
        compiler_params=pltpu.CompilerParams(has_side_effects=SIDE_EFFECT),
    )(*through)


def _split_wait(started, plan, after, tag):
    send_sems, recv_sems, arrays = started[0], started[1], started[2:-1]
    n = len(arrays)

    def body(*refs):
        _, finish = plan(refs[:n], refs[n], refs[n + 1])
        finish()

    return pl.pallas_call(
        body, name=tag + "_wait",
        out_shape=tuple(pltpu.HBM(a.shape, a.dtype) for a in arrays),
        in_specs=[HBM_SPEC] * n + [SEM_SPEC, SEM_SPEC, pl.BlockSpec(memory_space=pl.ANY)],
        out_specs=(HBM_SPEC,) * n,
        input_output_aliases={k: k for k in range(n)},
        compiler_params=pltpu.CompilerParams(has_side_effects=SIDE_EFFECT),
    )(*arrays, send_sems, recv_sems, after)


def _small_allreduce_steps(groups, s_sem, r_sem):
    x, y, c = _mesh_pos()
    me = 4 * x + 2 * y + c
    pairs = [(k, m) for k in range(len(groups)) for m in range(1, N_DEV)]

    def send(k, m, to):
        p_ref, all_ref = groups[k][0], groups[k][3]
        i = k * (N_DEV - 1) + m - 1
        return pltpu.make_async_remote_copy(p_ref, all_ref.at[me], s_sem.at[i], r_sem.at[i],
                                            device_id=to, device_id_type=MESH)

    def start():
        for p_ref, _, _, all_ref in groups:
            all_ref[me] = p_ref[...]
        for k, m in pairs:
            send(k, m, (_flip(x, m & 4), _flip(y, m & 2), _flip(c, m & 1))).start()

    def finish():
        for k, m in pairs:
            p_ref, all_ref = groups[k][0], groups[k][3]
            i = k * (N_DEV - 1) + m - 1
            src = 4 * _flip(x, m & 4) + 2 * _flip(y, m & 2) + _flip(c, m & 1)
            pltpu.make_async_remote_copy(p_ref, all_ref.at[src], s_sem.at[i], r_sem.at[i],
                                         device_id=(x, y, c), device_id_type=MESH).wait_recv()
        for k, m in pairs:
            send(k, m, (x, y, c)).wait_send()
        for _, sum_ref, all_out_ref, all_ref in groups:
            sum_ref[...] = (((all_ref[0] + all_ref[1]) + (all_ref[2] + all_ref[3]))
                            + ((all_ref[4] + all_ref[5]) + (all_ref[6] + all_ref[7])))
            if all_out_ref is not None:
                all_out_ref[...] = all_ref[...]

    return start, finish


def _row_block(rows, cap=256):
    rb = min(rows, cap)
    assert rows % rb == 0
    return rb


def _sum_chips(got, pos, after, name):
    _, rh, cc = got.shape
    rb = _row_block(rh)
    nb = rh // rb

    def body(pos_ref, a_ref, after_ref, o_ref):
        a = a_ref[...].astype(F32)
        o_ref[...] = (a[0] + a[1]) + (a[2] + a[3])

    return pl.pallas_call(
        body, name=name,
        out_shape=jax.ShapeDtypeStruct((2 * rh, cc), F32),
        grid_spec=pltpu.PrefetchScalarGridSpec(
            num_scalar_prefetch=1, grid=(nb,),
            in_specs=[pl.BlockSpec((N_CHIPS, rb, cc), lambda i, p: (0, i, 0)),
                      pl.BlockSpec(memory_space=pl.ANY)],
            out_specs=pl.BlockSpec((rb, cc), lambda i, p: (p[0] * nb + i, 0))),
        compiler_params=_params(dimension_semantics=("arbitrary",)),
    )(pos, got, after)


def _adamw_math(w, g, m, v):
    m = ADAM_B1 * m + (1.0 - ADAM_B1) * g
    v = ADAM_B2 * v + (1.0 - ADAM_B2) * jnp.square(g)
    m_hat = m / (1.0 - ADAM_B1 ** ADAM_STEP)
    v_hat = v / (1.0 - ADAM_B2 ** ADAM_STEP)
    delta = -ADAM_LR * (m_hat / (jnp.sqrt(v_hat) + ADAM_EPS) + ADAM_WD * w)
    return delta, m, v


def _adamw_big(g, w, m, v, name):
    r, cc = w.shape
    rb = _row_block(r)

    def body(g_ref, w_ref, m_ref, v_ref, d_ref, mo_ref, vo_ref):
        d, mn, vn = _adamw_math(w_ref[...], g_ref[...], m_ref[...], v_ref[...])
        d_ref[...] = d
        mo_ref[...] = mn
        vo_ref[...] = vn

    spec = pl.BlockSpec((rb, cc), lambda i: (i, 0))
    return pl.pallas_call(
        body, name=name,
        out_shape=(jax.ShapeDtypeStruct((r, cc), F32),) * 3,
        grid=(r // rb,), in_specs=[spec] * 4, out_specs=(spec,) * 3,
        compiler_params=_params(dimension_semantics=("arbitrary",)),
    )(g, w, m, v)


def _adamw_ada(cond_t, dmod, w, m, v, after):
    d, na = w.shape
    rb = _row_block(d, 128)

    def body(ct_ref, dm_ref, w_ref, m_ref, v_ref, after_ref, g_ref, d_ref, mo_ref, vo_ref):
        g = jnp.dot(ct_ref[...], dm_ref[...], preferred_element_type=F32, precision=lax.Precision.HIGHEST)
        dl, mn, vn = _adamw_math(w_ref[...], g, m_ref[...], v_ref[...])
        g_ref[...] = g
        d_ref[...] = dl
        mo_ref[...] = mn
        vo_ref[...] = vn

    spec = pl.BlockSpec((rb, na), lambda i: (i, 0))
    return pl.pallas_call(
        body, name="adamw_w_ada",
        out_shape=(jax.ShapeDtypeStruct((d, na), F32),) * 4,
        grid=(d // rb,),
        in_specs=[pl.BlockSpec((rb, N_DEV), lambda i: (i, 0)), pl.BlockSpec((N_DEV, na), lambda i: (0, 0)),
                  spec, spec, spec, pl.BlockSpec(memory_space=pl.ANY)],
        out_specs=(spec,) * 4,
        compiler_params=_params(dimension_semantics=("arbitrary",)),
    )(cond_t, dmod, w, m, v, after)


def _adamw_small(gs, ws, ms, vs):
    n = len(gs)

    def body(*refs):
        ins, outs = refs[:4 * n], refs[4 * n:]
        for i in range(n):
            d, mn, vn = _adamw_math(ins[n + i][...], ins[i][...], ins[2 * n + i][...], ins[3 * n + i][...])
            outs[i][...] = d
            outs[n + i][...] = mn
            outs[2 * n + i][...] = vn

    shapes = tuple(jax.ShapeDtypeStruct(w.shape, F32) for w in ws)
    return pl.pallas_call(
        body, name="adamw_small",
        out_shape=shapes * 3,
        in_specs=[VMEM_SPEC] * (4 * n), out_specs=(VMEM_SPEC,) * (3 * n),
        compiler_params=_params(),
    )(*gs, *ws, *ms, *vs)


def _layer_norm_fwd(r):
    mu = jnp.mean(r, axis=-1, keepdims=True)
    xc = r - mu
    var = jnp.mean(jnp.square(xc), axis=-1, keepdims=True)
    rstd = lax.rsqrt(var + LN_EPS)
    return xc * rstd, rstd


def _layer_norm_bwd(dy, xhat, rstd, gain):
    dxh = dy * gain
    m1 = jnp.mean(dxh, axis=-1, keepdims=True)
    m2 = jnp.mean(dxh * xhat, axis=-1, keepdims=True)
    return rstd * (dxh - m1 - xhat * m2)


def _inv_count(tile, tm, win):
    t = (tile * tm + lax.broadcasted_iota(jnp.int32, (tm, 1), 0) + 1).astype(F32)
    return 1.0 / jnp.minimum(t, float(win))


def _fwd_mix(x, mod, conv_w, w_in4, w_pool, pool_scale, w_out, ln_g, ln_b, late_stacks, tm):
    t_len, d = x.shape
    cw = w_in4.shape[2]
    dg = cw // N_GROUPS
    nt = t_len // tm
    n_late = len(late_stacks)
    n_cp = n_late * (N_CHIPS - 1)
    pass_steps = [min(((k + 1) * nt) // n_late + (1 if k + 1 < n_late else -2), nt - 1) if nt > 2 else nt - 1
                  for k in range(n_late)]
    pass_steps = [max(p, 0) for p in pass_steps]

    def body(x_ref, mod_ref, cw_ref, win_ref, wp_ref, ps_ref, wout_ref, g_ref, b_ref, *rest):
        late = rest[n_late:2 * n_late]
        x1_ref, zs_ref, mix_ref, h1_ref, cat_ref = rest[2 * n_late:2 * n_late + 5]
        ubuf, vbuf, s_ici, r_ici, s_d2d, r_d2d = rest[2 * n_late + 5:]
        i = pl.program_id(0)
        start, forward, finish = _stack_gather(late, [s.shape[1] for s in late_stacks], s_ici, r_ici, s_d2d, r_d2d)

        @pl.when(i == 0)
        def _():
            start()
            ubuf[0:HALO, :] = jnp.zeros((HALO, cw), F32)
            vbuf[0:HALO, :] = jnp.zeros((HALO, cw), F32)

        xv = x_ref[...]
        sh1, sc1, g1 = mod_ref[0:1, 0:d], mod_ref[0:1, d:2 * d], mod_ref[0:1, 2 * d:3 * d]
        h1 = (xv * (1.0 + sc1) + sh1).astype(BF16)
        h1_ref[...] = h1
        zb = _dot(h1, win_ref[0])
        zc = _dot(h1, win_ref[1])
        zv = _dot(h1, win_ref[2])
        zp = _dot(h1, win_ref[3])
        u = zc * zv
        ubuf[HALO:HALO + tm, :] = u
        u1 = ubuf[pl.ds(HALO - 1, tm), :]
        u2 = ubuf[pl.ds(HALO - 2, tm), :]
        conv = cw_ref[0:1, :] * u2 + cw_ref[1:2, :] * u1 + cw_ref[2:3, :] * u
        ubuf[0:HALO, :] = ubuf[tm:tm + HALO, :]
        yc = zb * conv
        vbuf[HALO:HALO + tm, :] = zp
        ps, pms = [], []
        for gi, win in enumerate(POOL_WINDOWS):
            sl = slice(gi * dg, (gi + 1) * dg)
            acc = zp[:, sl]
            for s in range(1, win):
                acc = acc + vbuf[pl.ds(HALO - s, tm), sl]
            p_g = acc * _inv_count(i, tm, win) - zp[:, sl]
            ps.append(p_g)
            pms.append(_dot(p_g.astype(BF16), wp_ref[gi]))
        vbuf[0:HALO, :] = vbuf[tm:tm + HALO, :]
        pooled = jnp.concatenate(ps, axis=1)
        yp = jnp.concatenate(pms, axis=1) * ps_ref[...]
        cat = jnp.concatenate([yc, yp], axis=1).astype(BF16)
        cat_ref[...] = cat
        mix = _dot(cat, wout_ref[...])
        mix_ref[...] = mix
        xhat, _ = _layer_norm_fwd(DEEPNORM_ALPHA * xv + (1.0 + g1) * mix)
        x1_ref[...] = xhat * g_ref[...] + b_ref[...]
        zs_ref[:, 0 * cw:1 * cw] = zb.astype(BF16)
        zs_ref[:, 1 * cw:2 * cw] = zc.astype(BF16)
        zs_ref[:, 2 * cw:3 * cw] = zv.astype(BF16)
        zs_ref[:, 3 * cw:4 * cw] = conv.astype(BF16)
        zs_ref[:, 4 * cw:5 * cw] = pooled.astype(BF16)

        for k in range(n_late):
            @pl.when(i == pass_steps[k])
            def _():
                forward(k)

        @pl.when(i == nt - 1)
        def _():
            finish()

    tile = lambda w: pl.BlockSpec((tm, w), lambda i: (i, 0))
    n_in = 9
    return pl.pallas_call(
        body, name="fwd_mix",
        out_shape=tuple(jax.ShapeDtypeStruct(s.shape, s.dtype) for s in late_stacks)
        + (jax.ShapeDtypeStruct((t_len, d), F32), jax.ShapeDtypeStruct((t_len, 5 * cw), BF16),
           jax.ShapeDtypeStruct((t_len, d), F32), jax.ShapeDtypeStruct((t_len, d), BF16),
           jax.ShapeDtypeStruct((t_len, 2 * cw), BF16)),
        grid=(nt,),
        in_specs=[tile(d)] + [VMEM_SPEC] * 8 + [HBM_SPEC] * n_late,
        out_specs=(HBM_SPEC,) * n_late + (tile(d), tile(5 * cw), tile(d), tile(d), tile(2 * cw)),
        input_output_aliases={n_in + k: k for k in range(n_late)},
        scratch_shapes=[pltpu.VMEM((tm + HALO, cw), F32), pltpu.VMEM((tm + HALO, cw), F32),
                        _dma_sems(n_cp), _dma_sems(n_cp), _dma_sems(n_cp), _dma_sems(n_cp)],
        compiler_params=_params(dimension_semantics=("arbitrary",)),
    )(x, mod, conv_w, w_in4, w_pool, pool_scale, w_out, ln_g, ln_b, *late_stacks)


def _mlp_fwd_bwd(x1, tgt, mod, w_mi4, w_mo4, ln_g, ln_b, tm):
    t_len, d = x1.shape
    fq = w_mi4.shape[2]
    nt = t_len // tm

    def body(x1_ref, tgt_ref, mod_ref, wi_ref, wo_ref, g_ref, b_ref,
             dx1_ref, da_ref, s_ref, df_ref, h2_ref, st_ref, relu_buf, acc):
        i = pl.program_id(0)

        @pl.when(i == 0)
        def _():
            acc[...] = jnp.zeros(acc.shape, F32)

        x1v = x1_ref[...]
        sh2, sc2, g2 = mod_ref[0:1, 3 * d:4 * d], mod_ref[0:1, 4 * d:5 * d], mod_ref[0:1, 5 * d:6 * d]
        h2 = (x1v * (1.0 + sc2) + sh2).astype(BF16)
        h2_ref[...] = h2
        f = jnp.zeros((tm, d), F32)
        for j in range(N_CHIPS):
            a = jnp.maximum(_dot(h2, wi_ref[j]), 0.0)
            relu_buf[j] = a
            s = (a * a).astype(BF16)
            s_ref[j] = s
            f = f + _dot(s, wo_ref[j])
        xhat, rstd = _layer_norm_fwd(DEEPNORM_ALPHA * x1v + (1.0 + g2) * f)
        gain = g_ref[...]
        err = xhat * gain + b_ref[...] - tgt_ref[...]
        dy = err * (1.0 / d)
        dr2 = _layer_norm_bwd(dy, xhat, rstd, gain)
        df = ((1.0 + g2) * dr2).astype(BF16)
        df_ref[...] = df
        dh2 = jnp.zeros((tm, d), F32)
        for j in range(N_CHIPS):
            ds = _dot_nt(df, wo_ref[j])
            da = (ds * (2.0 * relu_buf[j])).astype(BF16)
            da_ref[j] = da
            dh2 = dh2 + _dot_nt(da, wi_ref[j])
        dx1_ref[...] = DEEPNORM_ALPHA * dr2 + dh2 * (1.0 + sc2)
        acc[0] += _fold8(dy * xhat)
        acc[1] += _fold8(dy)
        acc[2] += _fold8(dr2 * f)
        acc[3] += _fold8(dh2 * x1v)
        acc[4] += _fold8(dh2)
        acc[5] += _fold8(err * err)

        @pl.when(i == nt - 1)
        def _():
            for k in range(5):
                st_ref[k:k + 1, :] = jnp.sum(acc[k], axis=0, keepdims=True)
            loss = jnp.sum(acc[5]) * (0.5 / d)
            st_ref[5:6, :] = jnp.broadcast_to(loss, (1, d))
            st_ref[6:8, :] = jnp.zeros((2, d), F32)

    tile = lambda w: pl.BlockSpec((tm, w), lambda i: (i, 0))
    tile4 = pl.BlockSpec((N_CHIPS, tm, fq), lambda i: (0, i, 0))
    return pl.pallas_call(
        body, name="mlp_fwd_bwd",
        out_shape=(jax.ShapeDtypeStruct((t_len, d), F32),
                   jax.ShapeDtypeStruct((N_CHIPS, t_len, fq), BF16), jax.ShapeDtypeStruct((N_CHIPS, t_len, fq), BF16),
                   jax.ShapeDtypeStruct((t_len, d), BF16), jax.ShapeDtypeStruct((t_len, d), BF16),
                   jax.ShapeDtypeStruct((SUBLANES, d), F32)),
        grid=(nt,),
        in_specs=[tile(d), tile(d)] + [VMEM_SPEC] * 5,
        out_specs=(tile(d), tile4, tile4, tile(d), tile(d), pl.BlockSpec((SUBLANES, d), lambda i: (0, 0))),
        scratch_shapes=[pltpu.VMEM((N_CHIPS, tm, fq), F32), pltpu.VMEM((6, SUBLANES, d), F32)],
        compiler_params=_params(dimension_semantics=("arbitrary",)),
    )(x1, tgt, mod, w_mi4, w_mo4, ln_g, ln_b)


def _mlp_wgrad(h2, da4, s4, df, pos, tt):
    t_len, d = h2.shape
    fq = da4.shape[2]
    nt = t_len // tt
    assert nt >= 3
    rh = (d // 2, fq // 2)
    last = N_CHIPS - 1

    def body(pos_ref, h2_ref, da_ref, s_ref, df_ref, qi_ref, qo_ref, gi_ref, go_ref,
             acc_i, acc_o, rcv_i, rcv_o, s_sem, r_sem):
        j, t = pl.program_id(0), pl.program_id(1)
        slot = lax.rem(j, 2)
        x, y, c = _mesh_pos()
        accs, rcvs, q_refs, g_refs = (acc_i, acc_o), (rcv_i, rcv_o), (qi_ref, qo_ref), (gi_ref, go_ref)

        def to_sibling(a, sl):
            theirs = pl.multiple_of((1 - c) * rh[a], rh[a])
            return pltpu.make_async_remote_copy(accs[a].at[sl].at[pl.ds(theirs, rh[a]), :], rcvs[a].at[sl],
                                                s_sem.at[2 * sl + a], r_sem.at[2 * sl + a],
                                                device_id=(x, y, 1 - c), device_id_type=MESH)

        def emit(chunk, sl):
            for a in range(2):
                to_sibling(a, sl).wait()
                mine = pl.multiple_of(c * rh[a], rh[a])
                q = (accs[a][sl, pl.ds(mine, rh[a]), :] + rcvs[a][sl]).astype(BF16)
                q_refs[a][...] = q

                @pl.when(chunk == pos_ref[1])
                def _():
                    g_refs[a][...] = q

        @pl.when(t == 0)
        def _():
            acc_i[slot] = jnp.zeros((d, fq), F32)
            acc_o[slot] = jnp.zeros((fq, d), F32)

        acc_i[slot] += _dot_tn(h2_ref[...], da_ref[...])
        acc_o[slot] += _dot_tn(s_ref[...], df_ref[...])

        @pl.when((t == 1) & (j >= 1))
        def _():
            emit(j - 1, 1 - slot)

        @pl.when(t == nt - 1)
        def _():
            for a in range(2):
                to_sibling(a, slot).start()

        @pl.when((t == nt - 1) & (j == last))
        def _():
            emit(j, slot)

    q_index = lambda j, t, p: (jnp.where(t == nt - 1, j, jnp.maximum(j - 1, 0)), 0, 0)
    got_index = lambda j, t, p: (p[1], 0, 0)
    blocks = ((None, rh[0], fq), (None, rh[1], d))
    return pl.pallas_call(
        body, name="mlp_wgrad",
        out_shape=tuple(jax.ShapeDtypeStruct((N_CHIPS,) + b[1:], BF16) for b in blocks) * 2,
        grid_spec=pltpu.PrefetchScalarGridSpec(
            num_scalar_prefetch=1, grid=(N_CHIPS, nt),
            in_specs=[pl.BlockSpec((tt, d), lambda j, t, p: (t, 0)),
                      pl.BlockSpec((None, tt, fq), lambda j, t, p: (j, t, 0)),
                      pl.BlockSpec((None, tt, fq), lambda j, t, p: (j, t, 0)),
                      pl.BlockSpec((tt, d), lambda j, t, p: (t, 0))],
            out_specs=tuple(pl.BlockSpec(b, q_index) for b in blocks)
            + tuple(pl.BlockSpec(b, got_index) for b in blocks),
            scratch_shapes=[pltpu.VMEM((2, d, fq), F32), pltpu.VMEM((2, fq, d), F32),
                            pltpu.VMEM((2, rh[0], fq), F32), pltpu.VMEM((2, rh[1], d), F32),
                            _dma_sems(4), _dma_sems(4)]),
        compiler_params=_params(dimension_semantics=("arbitrary", "arbitrary")),
    )(pos, h2, da4, s4, df)


def _bwd_mix(x, mix, dx1, zs, mod, conv_w, w_in4, w_pool, pool_scale, w_out, ln_g, after, tm):
    t_len, d = x.shape
    cw = w_in4.shape[2]
    dg = cw // N_GROUPS
    nt = t_len // tm

    def body(x_ref, mix_ref, dx1_ref, zs_ref, mod_ref, cw_ref, win_ref, wp_ref, ps_ref, wout_ref, g_ref, after_ref,
             gx_ref, dz_ref, dmix_ref, sd_ref, sc_ref, dwp_ref, dbuf, qbuf, acc_d, acc_c, acc_p):
        i = pl.program_id(0)
        tile_idx = nt - 1 - i

        @pl.when(i == 0)
        def _():
            dbuf[tm:tm + HALO, :] = jnp.zeros((HALO, cw), F32)
            qbuf[tm:tm + HALO, :] = jnp.zeros((HALO, cw), F32)
            acc_d[...] = jnp.zeros(acc_d.shape, F32)
            acc_c[...] = jnp.zeros(acc_c.shape, F32)
            acc_p[...] = jnp.zeros(acc_p.shape, F32)

        xv, mixv, dx1v = x_ref[...], mix_ref[...], dx1_ref[...]
        sc1, g1 = mod_ref[0:1, d:2 * d], mod_ref[0:1, 2 * d:3 * d]
        xhat, rstd = _layer_norm_fwd(DEEPNORM_ALPHA * xv + (1.0 + g1) * mixv)
        dr1 = _layer_norm_bwd(dx1v, xhat, rstd, g_ref[...])
        dmix = ((1.0 + g1) * dr1).astype(BF16)
        dmix_ref[...] = dmix
        dcat = _dot_nt(dmix, wout_ref[...])
        dyc, dyp = dcat[:, 0:cw], dcat[:, cw:2 * cw]
        zb, zc, zv = (zs_ref[:, k * cw:(k + 1) * cw].astype(F32) for k in range(3))
        conv, pooled = zs_ref[:, 3 * cw:4 * cw].astype(F32), zs_ref[:, 4 * cw:5 * cw]
        dzb = dyc * conv
        dcv = dyc * zb
        dbuf[0:tm, :] = dcv
        d1 = dbuf[pl.ds(1, tm), :]
        d2 = dbuf[pl.ds(2, tm), :]
        du = cw_ref[2:3, :] * dcv + cw_ref[1:2, :] * d1 + cw_ref[0:1, :] * d2
        dbuf[tm:tm + HALO, :] = dbuf[0:HALO, :]
        u = zc * zv
        acc_c[0] += _fold8(u * d2)
        acc_c[1] += _fold8(u * d1)
        acc_c[2] += _fold8(u * dcv)
        dzc = du * zv
        dzv = du * zc
        dpw = dyp * ps_ref[...]
        dps, pms = [], []
        for gi, win in enumerate(POOL_WINDOWS):
            sl = slice(gi * dg, (gi + 1) * dg)
            p_g = pooled[:, sl].astype(BF16)
            dpw_g = dpw[:, sl].astype(BF16)
            pms.append(_dot(p_g, wp_ref[gi]))
            acc_p[gi] += _dot_tn(p_g, dpw_g)
            dp_g = _dot_nt(dpw_g, wp_ref[gi])
            dps.append(dp_g)
            qbuf[0:tm, sl] = dp_g * _inv_count(tile_idx, tm, win)
        acc_c[3] += _fold8(dyp * jnp.concatenate(pms, axis=1))
        dzps = []
        for gi, win in enumerate(POOL_WINDOWS):
            sl = slice(gi * dg, (gi + 1) * dg)
            acc = qbuf[0:tm, sl]
            for s in range(1, win):
                acc = acc + qbuf[pl.ds(s, tm), sl]
            dzps.append(acc - dps[gi])
        qbuf[tm:tm + HALO, :] = qbuf[0:HALO, :]
        dz = [dzb.astype(BF16), dzc.astype(BF16), dzv.astype(BF16), jnp.concatenate(dzps, axis=1).astype(BF16)]
        dh1 = jnp.zeros((tm, d), F32)
        for j in range(N_CHIPS):
            dz_ref[j] = dz[j]
            dh1 = dh1 + _dot_nt(dz[j], win_ref[j])
        gx_ref[...] = DEEPNORM_ALPHA * dr1 + dh1 * (1.0 + sc1)
        acc_d[0] += _fold8(dx1v * xhat)
        acc_d[1] += _fold8(dx1v)
        acc_d[2] += _fold8(dr1 * mixv)
        acc_d[3] += _fold8(dh1 * xv)
        acc_d[4] += _fold8(dh1)

        @pl.when(i == nt - 1)
        def _():
            for k in range(5):
                sd_ref[k:k + 1, :] = jnp.sum(acc_d[k], axis=0, keepdims=True)
            sd_ref[5:8, :] = jnp.zeros((3, d), F32)
            for k in range(4):
                sc_ref[k:k + 1, :] = jnp.sum(acc_c[k], axis=0, keepdims=True)
            sc_ref[4:8, :] = jnp.zeros((4, cw), F32)
            dwp_ref[...] = acc_p[...]

    rtile = lambda w: pl.BlockSpec((tm, w), lambda i: (nt - 1 - i, 0))
    whole = lambda shape: pl.BlockSpec(shape, lambda i: tuple(0 for _ in shape))
    return pl.pallas_call(
        body, name="bwd_mix",
        out_shape=(jax.ShapeDtypeStruct((t_len, d), F32), jax.ShapeDtypeStruct((N_CHIPS, t_len, cw), BF16),
                   jax.ShapeDtypeStruct((t_len, d), BF16), jax.ShapeDtypeStruct((SUBLANES, d), F32),
                   jax.ShapeDtypeStruct((SUBLANES, cw), F32), jax.ShapeDtypeStruct((N_GROUPS, dg, dg), F32)),
        grid=(nt,),
        in_specs=[rtile(d), rtile(d), rtile(d), rtile(5 * cw)] + [VMEM_SPEC] * 7
        + [pl.BlockSpec(memory_space=pl.ANY)],
        out_specs=(rtile(d), pl.BlockSpec((N_CHIPS, tm, cw), lambda i: (0, nt - 1 - i, 0)), rtile(d),
                   whole((SUBLANES, d)), whole((SUBLANES, cw)), whole((N_GROUPS, dg, dg))),
        scratch_shapes=[pltpu.VMEM((tm + HALO, cw), F32), pltpu.VMEM((tm + HALO, cw), F32),
                        pltpu.VMEM((5, SUBLANES, d), F32), pltpu.VMEM((4, SUBLANES, cw), F32),
                        pltpu.VMEM((N_GROUPS, dg, dg), F32)],
        compiler_params=_params(dimension_semantics=("arbitrary",)),
    )(x, mix, dx1, zs, mod, conv_w, w_in4, w_pool, pool_scale, w_out, ln_g, after)


def _mix_wgrad(h1, dz4, cat, dmix, small, n_all, tt):
    t_len, d = h1.shape
    cw = dz4.shape[2]
    nt = t_len // tt
    ro = (2 * cw) // N_CHIPS
    rh = (d // 2, ro // 2)
    n_small = len(small)

    def body(h1_ref, dz_ref, cat_ref, dmix_ref, *rest):
        p_refs = rest[:n_small]
        qi_ref, qo_ref, gi_ref, go_ref = rest[n_small:n_small + 4]
        sum_refs = rest[n_small + 4:2 * n_small + 4]
        all_refs = rest[2 * n_small + 4:2 * n_small + 4 + n_all]
        scratch = rest[2 * n_small + 4 + n_all:]
        acc_i, acc_o, rcv_i, rcv_o = scratch[:4]
        all_bufs = scratch[4:4 + n_small]
        s_sem, r_sem, s_small, r_small_sem = scratch[4 + n_small:]
        t = pl.program_id(0)
        x, y, c = _mesh_pos()
        chip = 2 * x + y
        groups = [(p_refs[k], sum_refs[k], all_refs[k] if k < n_all else None, all_bufs[k]) for k in range(n_small)]
        small_start, small_finish = _small_allreduce_steps(groups, s_small, r_small_sem)
        accs, rcvs, q_refs, g_refs = (acc_i, acc_o), (rcv_i, rcv_o), (qi_ref, qo_ref), (gi_ref, go_ref)

        @pl.when(t == 0)
        def _():
            small_start()
            acc_i[...] = jnp.zeros(acc_i.shape, F32)
            acc_o[...] = jnp.zeros(acc_o.shape, F32)

        h1v, dmixv = h1_ref[...], dmix_ref[...]
        for j in range(N_CHIPS):
            acc_i[j] += _dot_tn(h1v, dz_ref[j])
            acc_o[j] += _dot_tn(cat_ref[:, j * ro:(j + 1) * ro], dmixv)

        @pl.when(t == nt - 1)
        def _():
            cps = []
            for a in range(2):
                theirs = pl.multiple_of((1 - c) * rh[a], rh[a])
                cp = pltpu.make_async_remote_copy(accs[a].at[:, pl.ds(theirs, rh[a]), :], rcvs[a],
                                                  s_sem.at[a], r_sem.at[a],
                                                  device_id=(x, y, 1 - c), device_id_type=MESH)
                cp.start()
                cps.append(cp)
            for a in range(2):
                cps[a].wait()
                mine = pl.multiple_of(c * rh[a], rh[a])
                for j in range(N_CHIPS):
                    q_refs[a][j] = (accs[a][j, pl.ds(mine, rh[a]), :] + rcvs[a][j]).astype(BF16)
                g_refs[a][chip] = q_refs[a][chip]
            small_finish()

    stacks = ((N_CHIPS, rh[0], cw), (N_CHIPS, rh[1], d))
    n_cp = n_small * (N_DEV - 1)
    return pl.pallas_call(
        body, name="mix_wgrad",
        out_shape=tuple(jax.ShapeDtypeStruct(s, BF16) for s in stacks) * 2
        + tuple(jax.ShapeDtypeStruct(a.shape, F32) for a in small)
        + tuple(jax.ShapeDtypeStruct((N_DEV,) + a.shape, F32) for a in small[:n_all]),
        grid=(nt,),
        in_specs=[pl.BlockSpec((tt, d), lambda t: (t, 0)), pl.BlockSpec((N_CHIPS, tt, cw), lambda t: (0, t, 0)),
                  pl.BlockSpec((tt, 2 * cw), lambda t: (t, 0)), pl.BlockSpec((tt, d), lambda t: (t, 0))]
        + [VMEM_SPEC] * n_small,
        out_specs=(VMEM_SPEC,) * (4 + n_small + n_all),
        scratch_shapes=[pltpu.VMEM((N_CHIPS, d, cw), F32), pltpu.VMEM((N_CHIPS, ro, d), F32),
                        pltpu.VMEM(stacks[0], F32), pltpu.VMEM(stacks[1], F32)]
        + [pltpu.VMEM((N_DEV,) + a.shape, F32) for a in small]
        + [_dma_sems(2), _dma_sems(2), _dma_sems(n_cp), _dma_sems(n_cp)],
        compiler_params=_params(dimension_semantics=("arbitrary",)),
    )(h1, dz4, cat, dmix, *small)


def kernel(x, c, w_ada, b_ada, w_in, conv_w, w_pool, pool_scale, w_out, ln1_g, ln1_b, w_mlp_in, w_mlp_out, ln2_g, ln2_b, loss_target, m_w_ada, m_b_ada, m_w_in, m_conv_w, m_w_pool, m_pool_scale, m_w_out, m_ln1_g, m_ln1_b, m_w_mlp_in, m_w_mlp_out, m_ln2_g, m_ln2_b, v_w_ada, v_b_ada, v_w_in, v_conv_w, v_w_pool, v_pool_scale, v_w_out, v_ln1_g, v_ln1_b, v_w_mlp_in, v_w_mlp_out, v_ln2_g, v_ln2_b):
    t_len, d = x.shape[1], x.shape[2]
    cw = w_in.shape[2]
    cq = conv_w.shape[2]
    dg = w_pool.shape[2]
    assert cw == N_GROUPS * dg and cq * N_CHIPS == cw and LANES % cq == 0
    tm_mix = min(MIX_TOKEN_TILE, t_len)
    tm_mlp = min(MLP_TOKEN_TILE, t_len)
    tt = min(WGRAD_TILE, t_len // 4)
    chip = 2 * lax.axis_index("x") + lax.axis_index("y")
    pos = jnp.stack([lax.axis_index("c"), chip]).astype(jnp.int32)

    x2, tgt = x[0], loss_target[0]
    big_w = [w_in[0], w_out[0], w_mlp_in[0], w_mlp_out[0]]
    big_m = [m_w_in[0], m_w_out[0], m_w_mlp_in[0], m_w_mlp_out[0]]
    big_v = [v_w_in[0], v_w_out[0], v_w_mlp_in[0], v_w_mlp_out[0]]
    names = ["w_in", "w_out", "w_mlp_in", "w_mlp_out"]

    mod, cond_all, conv_full, w_in4, w_out4, w_mi_own, w_mo_own = _setup_exchange(
        c, w_ada[0], b_ada, conv_w[0], big_w, 2)
    w_out_full = w_out4.reshape(2 * cw, d)
    w_pool_bf = w_pool[0].astype(BF16)

    w_mi4, w_mo4, x1, zs, mix, h1, cat = _fwd_mix(x2, mod, conv_full, w_in4, w_pool_bf, pool_scale, w_out_full,
                                                  ln1_g, ln1_b, [w_mi_own, w_mo_own], tm_mix)
    dx1, da4, s4, df, h2, st2 = _mlp_fwd_bwd(x1, tgt, mod, w_mi4, w_mo4, ln2_g, ln2_b, tm_mlp)
    mlp_qi, mlp_qo, mlp_gi, mlp_go = _mlp_wgrad(h2, da4, s4, df, pos, tt)
    mlp_started = _split_start([mlp_qi, mlp_qo, mlp_gi, mlp_go], _chunk_exchange_plan(2), 2 * (N_CHIPS - 1),
                               "exchange_mlp")
    grad_x, dz4, dmix, st1, stc, dw_pool = _bwd_mix(x2, mix, dx1, zs, mod, conv_full, w_in4, w_pool_bf, pool_scale,
                                                    w_out_full, ln1_g, mlp_started[-1], tm_mix)

    mix_qi, mix_qo, mix_gi, mix_go, t1, t2, tc, g_w_pool, all1, all2 = _mix_wgrad(
        h1, dz4, cat, dmix, [st1, st2, stc, dw_pool], 2, tt)
    loss = t2[5, 0]
    g_ln1_g, g_ln1_b, g_ln2_g, g_ln2_b = t1[0:1], t1[1:2], t2[0:1], t2[1:2]
    g_pool_scale = tc[3:4]
    g_conv = lax.dynamic_slice_in_dim(tc[0:3], chip * cq, cq, axis=1)
    g_b_ada = jnp.concatenate([t1[4:5], t1[3:4], t1[2:3], t2[4:5], t2[3:4], t2[2:3]], axis=1)
    dmod_all = jnp.concatenate([all1[:, 4], all1[:, 3], all1[:, 2], all2[:, 4], all2[:, 3], all2[:, 2]], axis=1)
    na = w_ada.shape[2]
    dmod_mine = lax.dynamic_slice_in_dim(dmod_all, chip * na, na, axis=1)

    exchange, n_cp = _chunk_exchange_plan(2), 2 * (N_CHIPS - 1)
    mlp_got = _split_wait(mlp_started, exchange, t1, "exchange_mlp")[2:]
    mix_started = _split_start([mix_qi, mix_qo, mix_gi, mix_go], exchange, n_cp, "exchange_mix")
    mlp_halves = [_sum_chips(g, pos, mix_started[-1], "rs_total_" + n) for g, n in zip(mlp_got, names[2:])]
    join = _sibling_join_plan([h.shape for h in mlp_halves])
    join_started = _split_start(mlp_halves, join, 2, "join_mlp")
    g_ada, d_ada, nm_ada, nv_ada = _adamw_ada(cond_all.T, dmod_mine, w_ada[0], m_w_ada[0], v_w_ada[0],
                                              join_started[-1])
    mlp_g = _split_wait(join_started, join, d_ada, "join_mlp")
    mlp_out = [_adamw_big(g, w, m, v, "adamw_" + n)
               for g, w, m, v, n in zip(mlp_g, big_w[2:], big_m[2:], big_v[2:], names[2:])]
    mix_got = _split_wait(mix_started, exchange, mlp_out[1][0], "exchange_mix")[2:]
    mix_halves = [_sum_chips(g, pos, d_ada, "rs_total_" + n) for g, n in zip(mix_got, names[:2])]
    mix_g = _sibling_join(mix_halves, "mix")
    mix_out = [_adamw_big(g, w, m, v, "adamw_" + n)
               for g, w, m, v, n in zip(mix_g, big_w[:2], big_m[:2], big_v[:2], names[:2])]
    big_g = list(mix_g) + list(mlp_g)
    big_out = mix_out + mlp_out

    small_g = [g_b_ada, g_conv, g_w_pool.reshape(-1, dg), g_pool_scale, g_ln1_g, g_ln1_b, g_ln2_g, g_ln2_b]
    small_w = [b_ada, conv_w[0], w_pool.reshape(-1, dg), pool_scale, ln1_g, ln1_b, ln2_g, ln2_b]
    small_m = [m_b_ada, m_conv_w[0], m_w_pool.reshape(-1, dg), m_pool_scale, m_ln1_g, m_ln1_b, m_ln2_g, m_ln2_b]
    small_v = [v_b_ada, v_conv_w[0], v_w_pool.reshape(-1, dg), v_pool_scale, v_ln1_g, v_ln1_b, v_ln2_g, v_ln2_b]
    sm = _adamw_small(small_g, small_w, small_m, small_v)
    ns = len(small_g)
    s_delta, s_m, s_v = sm[:ns], sm[ns:2 * ns], sm[2 * ns:]

    def assemble(ada, small, big):
        return [ada[None], small[0], big[0][None], small[1][None], small[2].reshape(w_pool.shape), small[3],
                big[1][None], small[4], small[5], big[2][None], big[3][None], small[6], small[7]]

    grads = assemble(g_ada, small_g, big_g)
    deltas = assemble(d_ada, s_delta, [o[0] for o in big_out])
    new_m = assemble(nm_ada, s_m, [o[1] for o in big_out])
    new_v = assemble(nv_ada, s_v, [o[2] for o in big_out])
    return (loss, grad_x[None], *grads, *deltas, *new_m, *new_v)
```

```python
import jax
import jax.numpy as jnp
from jax import lax
from jax.experimental import pallas as pl
from jax.experimental.pallas import tpu as pltpu

F32 = jnp.float32
BF16 = jnp.bfloat16
MESH = pl.DeviceIdType.MESH

LN_EPS = 1e-5
DEEPNORM_ALPHA = 2.0 ** 0.25
POOL_WINDOWS = (2, 4, 8, 16)
N_GROUPS = len(POOL_WINDOWS)
HALO = 16
N_CHIPS = 4
N_DEV = 8
LANES = 128
SUBLANES = 8
VMEM_LIMIT = 56 * 1024 * 1024
MIX_TOKEN_TILE = 512
MLP_TOKEN_TILE = 256
WGRAD_TILE = 1024

ADAM_LR = 0.001
ADAM_B1 = 0.9
ADAM_B2 = 0.999
ADAM_EPS = 1e-08
ADAM_WD = 0.01
ADAM_STEP = 10

VMEM_SPEC = pl.BlockSpec(memory_space=pltpu.VMEM)
HBM_SPEC = pl.BlockSpec(memory_space=pltpu.HBM)


def _dot(a, b):
    return jnp.dot(a, b, preferred_element_type=F32)


def _dot_nt(a, b):
    return lax.dot_general(a, b, (((1,), (1,)), ((), ())), preferred_element_type=F32)


def _dot_tn(a, b):
    return lax.dot_general(a, b, (((0,), (0,)), ((), ())), preferred_element_type=F32)


def _fold8(v):
    r, n = v.shape
    return jnp.sum(v.reshape(r // SUBLANES, SUBLANES, n), axis=0)


def _mesh_pos():
    return lax.axis_index("x"), lax.axis_index("y"), lax.axis_index("c")


def _flip(v, bit):
    return 1 - v if bit else v


def _params(**kw):
    return pltpu.CompilerParams(vmem_limit_bytes=VMEM_LIMIT, **kw)


def _dma_sems(n):
    return pltpu.SemaphoreType.DMA((n,))


def _stack_gather(stacks, rows, s_ici, r_ici, s_d2d, r_d2d):
    x, y, c = _mesh_pos()
    chip = 2 * x + y
    pairs = [(i, m) for i in range(len(stacks)) for m in range(1, N_CHIPS)]

    def blk(i, slot, which):
        rh = rows[i] // 2
        return stacks[i].at[slot].at[pl.ds(which * rh, rh), :]

    def other(m):
        return 2 * _flip(x, m & 2) + _flip(y, m & 1)

    def ici(i, m, slot, to):
        k = i * (N_CHIPS - 1) + m - 1
        return pltpu.make_async_remote_copy(blk(i, slot, c), blk(i, slot, c), s_ici.at[k], r_ici.at[k],
                                            device_id=to, device_id_type=MESH)

    def d2d(i, m, which, to):
        k = i * (N_CHIPS - 1) + m - 1
        return pltpu.make_async_remote_copy(blk(i, other(m), which), blk(i, other(m), which),
                                            s_d2d.at[k], r_d2d.at[k], device_id=to, device_id_type=MESH)

    def start():
        for i, m in pairs:
            ici(i, m, chip, (_flip(x, m & 2), _flip(y, m & 1), c)).start()

    def forward(which=None):
        for i, m in pairs:
            if which is None or i == which:
                ici(i, m, other(m), (x, y, c)).wait_recv()
                d2d(i, m, c, (x, y, 1 - c)).start()

    def finish():
        for i, m in pairs:
            d2d(i, m, 1 - c, (x, y, c)).wait_recv()
        for i, m in pairs:
            ici(i, m, chip, (x, y, c)).wait_send()
            d2d(i, m, c, (x, y, c)).wait_send()

    return start, forward, finish


def _chunk_exchange(qs, gots, s_sem, r_sem):
    x, y, c = _mesh_pos()
    chip = 2 * x + y
    pairs = [(i, m) for i in range(len(qs)) for m in range(1, N_CHIPS)]

    def other(m):
        return 2 * _flip(x, m & 2) + _flip(y, m & 1)

    def send(i, m, to):
        k = i * (N_CHIPS - 1) + m - 1
        return pltpu.make_async_remote_copy(qs[i].at[other(m)], gots[i].at[chip], s_sem.at[k], r_sem.at[k],
                                            device_id=to, device_id_type=MESH)

    def arrival(i, m):
        k = i * (N_CHIPS - 1) + m - 1
        return pltpu.make_async_remote_copy(qs[i].at[other(m)], gots[i].at[other(m)], s_sem.at[k], r_sem.at[k],
                                            device_id=(x, y, c), device_id_type=MESH)

    def start():
        for i, m in pairs:
            send(i, m, (_flip(x, m & 2), _flip(y, m & 1), c)).start()

    def finish():
        for i, m in pairs:
            arrival(i, m).wait_recv()
        for i, m in pairs:
            send(i, m, (x, y, c)).wait_send()

    return start, finish


def _mod_scratch(d, na, cq):
    return [pltpu.VMEM((SUBLANES, d), F32), pltpu.VMEM((N_DEV, SUBLANES, d), F32),
            pltpu.VMEM((N_CHIPS, SUBLANES, na), F32),
            pltpu.VMEM((SUBLANES, cq), F32), pltpu.VMEM((N_CHIPS, SUBLANES, cq), F32),
            _dma_sems(N_DEV - 1), _dma_sems(N_DEV - 1),
            _dma_sems(N_CHIPS - 1), _dma_sems(N_CHIPS - 1), _dma_sems(N_CHIPS - 1), _dma_sems(N_CHIPS - 1)]


def _mod_steps(d, na, cq):
    def steps(c_ref, w_ref, b_ref, cw_ref, mod_ref, cond_ref, conv_ref,
              csend, cbuf, mbuf, cvsend, cvbuf, s1, r1, s2, r2, s3, r3):
        x, y, c = _mesh_pos()
        me = 4 * x + 2 * y + c
        chip = 2 * x + y
        csend[...] = jnp.broadcast_to(c_ref[...], (SUBLANES, d))
        cbuf[me] = csend[...]
        first = []
        for m in range(1, N_DEV):
            peer = (_flip(x, m & 4), _flip(y, m & 2), _flip(c, m & 1))
            cp = pltpu.make_async_remote_copy(csend, cbuf.at[me], s1.at[m - 1], r1.at[m - 1],
                                              device_id=peer, device_id_type=MESH)
            cp.start()
            first.append(cp)
        cvsend[...] = jnp.zeros((SUBLANES, cq), F32)
        cvsend[0:3, :] = cw_ref[...]
        cvbuf[chip] = cvsend[...]
        for m in range(1, N_DEV):
            src = 4 * _flip(x, m & 4) + 2 * _flip(y, m & 2) + _flip(c, m & 1)
            pltpu.make_async_remote_copy(csend, cbuf.at[src], s1.at[m - 1], r1.at[m - 1],
                                         device_id=(x, y, c), device_id_type=MESH).wait_recv()
        rows = lax.broadcasted_iota(jnp.int32, (SUBLANES, d), 0)
        call = jnp.zeros((SUBLANES, d), F32)
        for b in range(N_DEV):
            call = jnp.where(rows == b, cbuf[b], call)
        cond = call * jax.nn.sigmoid(call)
        cond_ref[...] = cond
        part = jnp.dot(cond, w_ref[...], preferred_element_type=F32, precision=lax.Precision.HIGHEST)
        mbuf[chip] = part
        second = []
        for m in range(1, N_CHIPS):
            peer = (_flip(x, m & 2), _flip(y, m & 1), c)
            cp = pltpu.make_async_remote_copy(mbuf.at[chip], mbuf.at[chip], s2.at[m - 1], r2.at[m - 1],
                                              device_id=peer, device_id_type=MESH)
            cp.start()
            second.append(cp)
            cp = pltpu.make_async_remote_copy(cvsend, cvbuf.at[chip], s3.at[m - 1], r3.at[m - 1],
                                              device_id=peer, device_id_type=MESH)
            cp.start()
            second.append(cp)
        for m in range(1, N_CHIPS):
            src = 2 * _flip(x, m & 2) + _flip(y, m & 1)
            pltpu.make_async_remote_copy(mbuf.at[src], mbuf.at[src], s2.at[m - 1], r2.at[m - 1],
                                         device_id=(x, y, c), device_id_type=MESH).wait_recv()
            pltpu.make_async_remote_copy(cvsend, cvbuf.at[src], s3.at[m - 1], r3.at[m - 1],
                                         device_id=(x, y, c), device_id_type=MESH).wait_recv()
        for cp in first + second:
            cp.wait_send()
        rows_n = lax.broadcasted_iota(jnp.int32, (SUBLANES, na), 0)
        for k in range(N_CHIPS):
            mine = jnp.sum(jnp.where(rows_n == me, mbuf[k], 0.0), axis=0, keepdims=True)
            mod_ref[:, k * na:(k + 1) * na] = jnp.broadcast_to(mine + b_ref[:, k * na:(k + 1) * na], (SUBLANES, na))
            conv_ref[:, k * cq:(k + 1) * cq] = cvbuf[k]

    return steps


def _setup_exchange(c_row, w_ada_s, b_ada, conv_w_s, shards, n_now):
    n = len(shards)
    n_cp = n_now * (N_CHIPS - 1)
    d, na, cq = c_row.shape[1], w_ada_s.shape[1], conv_w_s.shape[1]
    mod_scratch = _mod_scratch(d, na, cq)
    mod_steps = _mod_steps(d, na, cq)

    def body(*refs):
        mod_in, ins = refs[:4], refs[4:4 + n]
        mod_out, outs = refs[4 + n:7 + n], refs[7 + n:7 + 2 * n]
        bufs = refs[7 + 2 * n:7 + 3 * n]
        mod_scr = refs[7 + 3 * n:7 + 3 * n + len(mod_scratch)]
        s_ici, r_ici, s_d2d, r_d2d, s_loc = refs[7 + 3 * n + len(mod_scratch):]
        x, y, _ = _mesh_pos()
        chip = 2 * x + y
        def place(lo, hi):
            local = []
            for i in range(lo, hi):
                bufs[i][...] = ins[i][...].astype(BF16)
                cp = pltpu.make_async_copy(bufs[i], outs[i].at[chip], s_loc.at[i])
                cp.start()
                local.append(cp)
            return local

        start, forward, finish = _stack_gather(outs[:n_now], [s.shape[0] for s in shards[:n_now]],
                                               s_ici, r_ici, s_d2d, r_d2d)
        for cp in place(0, n_now):
            cp.wait()
        start()
        later = place(n_now, n)
        mod_steps(*mod_in, *mod_out, *mod_scr)
        forward()
        finish()
        for cp in later:
            cp.wait()

    return pl.pallas_call(
        body, name="setup_exchange",
        out_shape=(jax.ShapeDtypeStruct((SUBLANES, N_CHIPS * na), F32), jax.ShapeDtypeStruct((SUBLANES, d), F32),
                   jax.ShapeDtypeStruct((SUBLANES, N_CHIPS * cq), F32))
        + tuple(jax.ShapeDtypeStruct((N_CHIPS,) + s.shape, BF16) for s in shards),
        in_specs=[VMEM_SPEC] * (4 + n), out_specs=(VMEM_SPEC,) * 3 + (HBM_SPEC,) * n,
        scratch_shapes=[pltpu.VMEM(s.shape, BF16) for s in shards] + mod_scratch
        + [_dma_sems(n_cp), _dma_sems(n_cp), _dma_sems(n_cp), _dma_sems(n_cp), _dma_sems(n)],
        compiler_params=_params(),
    )(c_row, w_ada_s, b_ada, conv_w_s, *shards)


def _sibling_join_plan(shapes):
    def plan(refs, s_sem, r_sem):
        x, y, c = _mesh_pos()

        def half(i, which):
            rh = shapes[i][0] // 2
            return refs[i].at[pl.ds(which * rh, rh), :]

        def start():
            for i in range(len(refs)):
                pltpu.make_async_remote_copy(half(i, c), half(i, c), s_sem.at[i], r_sem.at[i],
                                             device_id=(x, y, 1 - c), device_id_type=MESH).start()

        def finish():
            for i in range(len(refs)):
                pltpu.make_async_remote_copy(half(i, 1 - c), half(i, 1 - c), s_sem.at[i], r_sem.at[i],
                                             device_id=(x, y, c), device_id_type=MESH).wait_recv()
            for i in range(len(refs)):
                pltpu.make_async_remote_copy(half(i, c), half(i, c), s_sem.at[i], r_sem.at[i],
                                             device_id=(x, y, c), device_id_type=MESH).wait_send()

        return start, finish

    return plan


def _chunk_exchange_plan(n):
    return lambda refs, s_sem, r_sem: _chunk_exchange(refs[:n], refs[n:], s_sem, r_sem)


def _sibling_join(gs, tag):
    n = len(gs)
    plan = _sibling_join_plan([g.shape for g in gs])

    def body(*refs):
        start, finish = plan(refs[n:2 * n], refs[2 * n], refs[2 * n + 1])
        start()
        finish()

    return pl.pallas_call(
        body, name="rs_sibling_join_" + tag,
        out_shape=tuple(jax.ShapeDtypeStruct(g.shape, g.dtype) for g in gs),
        in_specs=[HBM_SPEC] * n, out_specs=(HBM_SPEC,) * n,
        input_output_aliases={i: i for i in range(n)},
        scratch_shapes=[_dma_sems(n), _dma_sems(n)],
        compiler_params=_params(),
    )(*gs)


SEM_SPEC = pl.BlockSpec(memory_space=pltpu.SEMAPHORE)
SIDE_EFFECT = pltpu.SideEffectType.DATAFLOW_SIDE_EFFECTING


def _split_start(arrays, plan, n_cp, tag):
    n = len(arrays)

    def body(*refs):
        start, _ = plan(refs[n + 2:2 * n + 2], refs[n], refs[n + 1])
        start()
        token = refs[2 * n + 2]
        token[...] = jnp.zeros_like(token)

    through = [pltpu.with_memory_space_constraint(a, pltpu.HBM) for a in arrays]
    return pl.pallas_call(
        body, name=tag + "_start",
        out_shape=(pltpu.SemaphoreType.DMA((n_cp,)), pltpu.SemaphoreType.DMA((n_cp,)))
        + tuple(pltpu.HBM(a.shape, a.dtype) for a in through) + (jax.ShapeDtypeStruct((SUBLANES, LANES), F32),),
        in_specs=[HBM_SPEC] * n, out_specs=(SEM_SPEC, SEM_SPEC) + (HBM_SPEC,) * n + (VMEM_SPEC,),
        input_output_aliases={k: 2 + k for k in range(n)},
        compiler_params=pltpu.CompilerParams(has_side_effects=SIDE_EFFECT),
    )(*through)


def _split_wait(started, plan, after, tag):
    send_sems, recv_sems, arrays = started[0], started[1], started[2:-1]
    n = len(arrays)

    def body(*refs):
        _, finish = plan(refs[:n], refs[n], refs[n + 1])
        finish()

    return pl.pallas_call(
        body, name=tag + "_wait",
        out_shape=tuple(pltpu.HBM(a.shape, a.dtype) for a in arrays),
        in_specs=[HBM_SPEC] * n + [SEM_SPEC, SEM_SPEC, pl.BlockSpec(memory_space=pl.ANY)],
        out_specs=(HBM_SPEC,) * n,
        input_output_aliases={k: k for k in range(n)},
        compiler_params=pltpu.CompilerParams(has_side_effects=SIDE_EFFECT),
    )(*arrays, send_sems, recv_sems, after)


def _small_allreduce_steps(groups, s_sem, r_sem):
    x, y, c = _mesh_pos()
    me = 4 * x + 2 * y + c
    pairs = [(k, m) for k in range(len(groups)) for m in range(1, N_DEV)]

    def send(k, m, to):
        p_ref, all_ref = groups[k][0], groups[k][3]
        i = k * (N_DEV - 1) + m - 1
        return pltpu.make_async_remote_copy(p_ref, all_ref.at[me], s_sem.at[i], r_sem.at[i],
                                            device_id=to, device_id_type=MESH)

    def start():
        for p_ref, _, _, all_ref in groups:
            all_ref[me] = p_ref[...]
        for k, m in pairs:
            send(k, m, (_flip(x, m & 4), _flip(y, m & 2), _flip(c, m & 1))).start()

    def finish():
        for k, m in pairs:
            p_ref, all_ref = groups[k][0], groups[k][3]
            i = k * (N_DEV - 1) + m - 1
            src = 4 * _flip(x, m & 4) + 2 * _flip(y, m & 2) + _flip(c, m & 1)
            pltpu.make_async_remote_copy(p_ref, all_ref.at[src], s_sem.at[i], r_sem.at[i],
                                         device_id=(x, y, c), device_id_type=MESH).wait_recv()
        for k, m in pairs:
            send(k, m, (x, y, c)).wait_send()
        for _, sum_ref, all_out_ref, all_ref in groups:
            sum_ref[...] = (((all_ref[0] + all_ref[1]) + (all_ref[2] + all_ref[3]))
                            + ((all_ref[4] + all_ref[5]) + (all_ref[6] + all_ref[7])))
            if all_out_ref is not None:
                all_out_ref[...] = all_ref[...]

    return start, finish


def _row_block(rows, cap=256):
    rb = min(rows, cap)
    assert rows % rb == 0
    return rb


def _sum_chips(got, pos, after, name):
    _, rh, cc = got.shape
    rb = _row_block(rh)
    nb = rh // rb

    def body(pos_ref, a_ref, after_ref, o_ref):
        a = a_ref[...].astype(F32)
        o_ref[...] = (a[0] + a[1]) + (a[2] + a[3])

    return pl.pallas_call(
        body, name=name,
        out_shape=jax.ShapeDtypeStruct((2 * rh, cc), F32),
        grid_spec=pltpu.PrefetchScalarGridSpec(
            num_scalar_prefetch=1, grid=(nb,),
            in_specs=[pl.BlockSpec((N_CHIPS, rb, cc), lambda i, p: (0, i, 0)),
                      pl.BlockSpec(memory_space=pl.ANY)],
            out_specs=pl.BlockSpec((rb, cc), lambda i, p: (p[0] * nb + i, 0))),
        compiler_params=_params(dimension_semantics=("arbitrary",)),
    )(pos, got, after)


def _adamw_math(w, g, m, v):
    m = ADAM_B1 * m + (1.0 - ADAM_B1) * g
    v = ADAM_B2 * v + (1.0 - ADAM_B2) * jnp.square(g)
    m_hat = m / (1.0 - ADAM_B1 ** ADAM_STEP)
    v_hat = v / (1.0 - ADAM_B2 ** ADAM_STEP)
    delta = -ADAM_LR * (m_hat / (jnp.sqrt(v_hat) + ADAM_EPS) + ADAM_WD * w)
    return delta, m, v


def _adamw_big(g, w, m, v, name):
    r, cc = w.shape
    rb = _row_block(r)

    def body(g_ref, w_ref, m_ref, v_ref, d_ref, mo_ref, vo_ref):
        d, mn, vn = _adamw_math(w_ref[...], g_ref[...], m_ref[...], v_ref[...])
        d_ref[...] = d
        mo_ref[...] = mn
        vo_ref[...] = vn

    spec = pl.BlockSpec((rb, cc), lambda i: (i, 0))
    return pl.pallas_call(
        body, name=name,
        out_shape=(jax.ShapeDtypeStruct((r, cc), F32),) * 3,
        grid=(r // rb,), in_specs=[spec] * 4, out_specs=(spec,) * 3,
        compiler_params=_params(dimension_semantics=("arbitrary",)),
    )(g, w, m, v)


def _adamw_ada(cond_t, dmod, w, m, v, after):
    d, na = w.shape
    rb = _row_block(d, 128)

    def body(ct_ref, dm_ref, w_ref, m_ref, v_ref, after_ref, g_ref, d_ref, mo_ref, vo_ref):
        g = jnp.dot(ct_ref[...], dm_ref[...], preferred_element_type=F32, precision=lax.Precision.HIGHEST)
        dl, mn, vn = _adamw_math(w_ref[...], g, m_ref[...], v_ref[...])
        g_ref[...] = g
        d_ref[...] = dl
        mo_ref[...] = mn
        vo_ref[...] = vn

    spec = pl.BlockSpec((rb, na), lambda i: (i, 0))
    return pl.pallas_call(
        body, name="adamw_w_ada",
        out_shape=(jax.ShapeDtypeStruct((d, na), F32),) * 4,
        grid=(d // rb,),
        in_specs=[pl.BlockSpec((rb, N_DEV), lambda i: (i, 0)), pl.BlockSpec((N_DEV, na), lambda i: (0, 0)),
                  spec, spec, spec, pl.BlockSpec(memory_space=pl.ANY)],
        out_specs=(spec,) * 4,
        compiler_params=_params(dimension_semantics=("arbitrary",)),
    )(cond_t, dmod, w, m, v, after)


def _adamw_small(gs, ws, ms, vs):
    n = len(gs)

    def body(*refs):
        ins, outs = refs[:4 * n], refs[4 * n:]
        for i in range(n):
            d, mn, vn = _adamw_math(ins[n + i][...], ins[i][...], ins[2 * n + i][...], ins[3 * n + i][...])
            outs[i][...] = d
            outs[n + i][...] = mn
            outs[2 * n + i][...] = vn

    shapes = tuple(jax.ShapeDtypeStruct(w.shape, F32) for w in ws)
    return pl.pallas_call(
        body, name="adamw_small",
        out_shape=shapes * 3,
        in_specs=[VMEM_SPEC] * (4 * n), out_specs=(VMEM_SPEC,) * (3 * n),
        compiler_params=_params(),
    )(*gs, *ws, *ms, *vs)


def _layer_norm_fwd(r):
    mu = jnp.mean(r, axis=-1, keepdims=True)
    xc = r - mu
    var = jnp.mean(jnp.square(xc), axis=-1, keepdims=True)
    rstd = lax.rsqrt(var + LN_EPS)
    return xc * rstd, rstd


def _layer_norm_bwd(dy, xhat, rstd, gain):
    dxh = dy * gain
    m1 = jnp.mean(dxh, axis=-1, keepdims=True)
    m2 = jnp.mean(dxh * xhat, axis=-1, keepdims=True)
    return rstd * (dxh - m1 - xhat * m2)


def _inv_count(tile, tm, win):
    t = (tile * tm + lax.broadcasted_iota(jnp.int32, (tm, 1), 0) + 1).astype(F32)
    return 1.0 / jnp.minimum(t, float(win))


def _window_sums(v, bufs, dg, tm, offsets):
    base, sign = offsets
    level = v
    sums = []
    for k in range(N_GROUPS):
        bufs[k][base:base + tm, :] = level
        level = level + bufs[k][pl.ds(base + sign * (1 << k), tm), :]
        sums.append(level[:, 0:dg])
        level = level[:, dg:]
    return sums


def _fwd_mix(x, mod, conv_w, w_in4, w_pool, pool_scale, w_out, late_stacks, tm):
    t_len, d = x.shape
    cw = w_in4.shape[2]
    dg = cw // N_GROUPS
    nt = t_len // tm
    n_late = len(late_stacks)
    n_cp = n_late * (N_CHIPS - 1)
    pass_steps = [min(((k + 1) * nt) // n_late + (1 if k + 1 < n_late else -2), nt - 1) if nt > 2 else nt - 1
                  for k in range(n_late)]
    pass_steps = [max(p, 0) for p in pass_steps]

    def body(x_ref, mod_ref, cw_ref, win_ref, wp_ref, ps_ref, wout_ref, *rest):
        late = rest[n_late:2 * n_late]
        zs_ref, mix_ref, h1_ref, cat_ref = rest[2 * n_late:2 * n_late + 4]
        ubuf = rest[2 * n_late + 4]
        vbufs = rest[2 * n_late + 5:2 * n_late + 5 + N_GROUPS]
        s_ici, r_ici, s_d2d, r_d2d = rest[2 * n_late + 5 + N_GROUPS:]
        i = pl.program_id(0)
        start, forward, finish = _stack_gather(late, [s.shape[1] for s in late_stacks], s_ici, r_ici, s_d2d, r_d2d)

        @pl.when(i == 0)
        def _():
            start()
            for buf in (ubuf,) + tuple(vbufs):
                buf[0:HALO, :] = jnp.zeros((HALO, buf.shape[1]), F32)

        xv = x_ref[...]
        sh1, sc1 = mod_ref[0:1, 0:d], mod_ref[0:1, d:2 * d]
        h1 = (xv * (1.0 + sc1) + sh1).astype(BF16)
        h1_ref[...] = h1
        zb = _dot(h1, win_ref[0])
        zc = _dot(h1, win_ref[1])
        zv = _dot(h1, win_ref[2])
        zp = _dot(h1, win_ref[3])
        u = zc * zv
        ubuf[HALO:HALO + tm, :] = u
        u1 = ubuf[pl.ds(HALO - 1, tm), :]
        u2 = ubuf[pl.ds(HALO - 2, tm), :]
        conv = cw_ref[0:1, :] * u2 + cw_ref[1:2, :] * u1 + cw_ref[2:3, :] * u
        ubuf[0:HALO, :] = ubuf[tm:tm + HALO, :]
        yc = zb * conv
        sums = _window_sums(zp, vbufs, dg, tm, (HALO, -1))
        ps, pms = [], []
        for gi, win in enumerate(POOL_WINDOWS):
            p_g = sums[gi] * _inv_count(i, tm, win) - zp[:, gi * dg:(gi + 1) * dg]
            ps.append(p_g)
            pms.append(_dot(p_g.astype(BF16), wp_ref[gi]))
        for buf in vbufs:
            buf[0:HALO, :] = buf[tm:tm + HALO, :]
        pooled = jnp.concatenate(ps, axis=1)
        yp = jnp.concatenate(pms, axis=1) * ps_ref[...]
        cat = jnp.concatenate([yc, yp], axis=1).astype(BF16)
        cat_ref[...] = cat
        mix_ref[...] = _dot(cat, wout_ref[...])
        zs_ref[:, 0 * cw:1 * cw] = zb.astype(BF16)
        zs_ref[:, 1 * cw:2 * cw] = zc.astype(BF16)
        zs_ref[:, 2 * cw:3 * cw] = zv.astype(BF16)
        zs_ref[:, 3 * cw:4 * cw] = conv.astype(BF16)
        zs_ref[:, 4 * cw:5 * cw] = pooled.astype(BF16)

        for k in range(n_late):
            @pl.when(i == pass_steps[k])
            def _():
                forward(k)

        @pl.when(i == nt - 1)
        def _():
            finish()

    tile = lambda w: pl.BlockSpec((tm, w), lambda i: (i, 0))
    n_in = 7
    return pl.pallas_call(
        body, name="fwd_mix",
        out_shape=tuple(jax.ShapeDtypeStruct(s.shape, s.dtype) for s in late_stacks)
        + (jax.ShapeDtypeStruct((t_len, 5 * cw), BF16), jax.ShapeDtypeStruct((t_len, d), F32),
           jax.ShapeDtypeStruct((t_len, d), BF16), jax.ShapeDtypeStruct((t_len, 2 * cw), BF16)),
        grid=(nt,),
        in_specs=[tile(d)] + [VMEM_SPEC] * 6 + [HBM_SPEC] * n_late,
        out_specs=(HBM_SPEC,) * n_late + (tile(5 * cw), tile(d), tile(d), tile(2 * cw)),
        input_output_aliases={n_in + k: k for k in range(n_late)},
        scratch_shapes=[pltpu.VMEM((tm + HALO, cw), F32)]
        + [pltpu.VMEM((tm + HALO, (N_GROUPS - k) * dg), F32) for k in range(N_GROUPS)]
        + [_dma_sems(n_cp), _dma_sems(n_cp), _dma_sems(n_cp), _dma_sems(n_cp)],
        compiler_params=_params(dimension_semantics=("arbitrary",)),
    )(x, mod, conv_w, w_in4, w_pool, pool_scale, w_out, *late_stacks)


def _mlp_fwd_bwd(x, mix, tgt, mod, w_mi4, w_mo4, ln1_g, ln1_b, ln_g, ln_b, tm):
    t_len, d = x.shape
    fq = w_mi4.shape[2]
    nt = t_len // tm
    n_acc = 9

    def body(x_ref, mix_ref, tgt_ref, mod_ref, wi_ref, wo_ref, g1_ref, b1_ref, g_ref, b_ref,
             adr1_ref, dmix_ref, da_ref, s_ref, df_ref, h2_ref, st_ref, relu_buf, acc):
        i = pl.program_id(0)

        @pl.when(i == 0)
        def _():
            acc[...] = jnp.zeros(acc.shape, F32)

        mixv = mix_ref[...]
        g1 = mod_ref[0:1, 2 * d:3 * d]
        xhat1, rstd1 = _layer_norm_fwd(DEEPNORM_ALPHA * x_ref[...] + (1.0 + g1) * mixv)
        gain1 = g1_ref[...]
        x1v = xhat1 * gain1 + b1_ref[...]
        sh2, sc2, g2 = mod_ref[0:1, 3 * d:4 * d], mod_ref[0:1, 4 * d:5 * d], mod_ref[0:1, 5 * d:6 * d]
        h2 = (x1v * (1.0 + sc2) + sh2).astype(BF16)
        h2_ref[...] = h2
        f = jnp.zeros((tm, d), F32)
        for j in range(N_CHIPS):
            a = jnp.maximum(_dot(h2, wi_ref[j]), 0.0)
            relu_buf[j] = a
            s = (a * a).astype(BF16)
            s_ref[j] = s
            f = f + _dot(s, wo_ref[j])
        xhat, rstd = _layer_norm_fwd(DEEPNORM_ALPHA * x1v + (1.0 + g2) * f)
        gain = g_ref[...]
        err = xhat * gain + b_ref[...] - tgt_ref[...]
        dy = err * (1.0 / d)
        dr2 = _layer_norm_bwd(dy, xhat, rstd, gain)
        df = ((1.0 + g2) * dr2).astype(BF16)
        df_ref[...] = df
        dh2 = jnp.zeros((tm, d), F32)
        for j in range(N_CHIPS):
            ds = _dot_nt(df, wo_ref[j])
            da = (ds * (2.0 * relu_buf[j])).astype(BF16)
            da_ref[j] = da
            dh2 = dh2 + _dot_nt(da, wi_ref[j])
        dx1 = DEEPNORM_ALPHA * dr2 + dh2 * (1.0 + sc2)
        dr1 = _layer_norm_bwd(dx1, xhat1, rstd1, gain1)
        adr1_ref[...] = DEEPNORM_ALPHA * dr1
        dmix_ref[...] = ((1.0 + g1) * dr1).astype(BF16)
        acc[0] += _fold8(dy * xhat)
        acc[1] += _fold8(dy)
        acc[2] += _fold8(dr2 * f)
        acc[3] += _fold8(dh2 * x1v)
        acc[4] += _fold8(dh2)
        acc[5] += _fold8(err * err)
        acc[6] += _fold8(dx1 * xhat1)
        acc[7] += _fold8(dx1)
        acc[8] += _fold8(dr1 * mixv)

        @pl.when(i == nt - 1)
        def _():
            for k in (0, 1, 2, 3, 4, 6, 7, 8):
                st_ref[k:k + 1, :] = jnp.sum(acc[k], axis=0, keepdims=True)
            loss = jnp.sum(acc[5]) * (0.5 / d)
            st_ref[5:6, :] = jnp.broadcast_to(loss, (1, d))
            st_ref[n_acc:2 * SUBLANES, :] = jnp.zeros((2 * SUBLANES - n_acc, d), F32)

    tile = lambda w: pl.BlockSpec((tm, w), lambda i: (i, 0))
    tile4 = pl.BlockSpec((N_CHIPS, tm, fq), lambda i: (0, i, 0))
    return pl.pallas_call(
        body, name="mlp_fwd_bwd",
        out_shape=(jax.ShapeDtypeStruct((t_len, d), F32), jax.ShapeDtypeStruct((t_len, d), BF16),
                   jax.ShapeDtypeStruct((N_CHIPS, t_len, fq), BF16), jax.ShapeDtypeStruct((N_CHIPS, t_len, fq), BF16),
                   jax.ShapeDtypeStruct((t_len, d), BF16), jax.ShapeDtypeStruct((t_len, d), BF16),
                   jax.ShapeDtypeStruct((2 * SUBLANES, d), F32)),
        grid=(nt,),
        in_specs=[tile(d), tile(d), tile(d)] + [VMEM_SPEC] * 7,
        out_specs=(tile(d), tile(d), tile4, tile4, tile(d), tile(d),
                   pl.BlockSpec((2 * SUBLANES, d), lambda i: (0, 0))),
        scratch_shapes=[pltpu.VMEM((N_CHIPS, tm, fq), F32), pltpu.VMEM((n_acc, SUBLANES, d), F32)],
        compiler_params=_params(dimension_semantics=("arbitrary",)),
    )(x, mix, tgt, mod, w_mi4, w_mo4, ln1_g, ln1_b, ln_g, ln_b)


def _mlp_wgrad(h2, da4, s4, df, pos, tt):
    t_len, d = h2.shape
    fq = da4.shape[2]
    nt = t_len // tt
    assert nt >= 3
    rh = (d // 2, fq // 2)
    last = N_CHIPS - 1

    def body(pos_ref, h2_ref, da_ref, s_ref, df_ref, qi_ref, qo_ref, gi_ref, go_ref,
             acc_i, acc_o, rcv_i, rcv_o, s_sem, r_sem):
        j, t = pl.program_id(0), pl.program_id(1)
        slot = lax.rem(j, 2)
        x, y, c = _mesh_pos()
        accs, rcvs, q_refs, g_refs = (acc_i, acc_o), (rcv_i, rcv_o), (qi_ref, qo_ref), (gi_ref, go_ref)

        def to_sibling(a, sl):
            theirs = pl.multiple_of((1 - c) * rh[a], rh[a])
            return pltpu.make_async_remote_copy(accs[a].at[sl].at[pl.ds(theirs, rh[a]), :], rcvs[a].at[sl],
                                                s_sem.at[2 * sl + a], r_sem.at[2 * sl + a],
                                                device_id=(x, y, 1 - c), device_id_type=MESH)

        def emit(chunk, sl):
            for a in range(2):
                to_sibling(a, sl).wait()
                mine = pl.multiple_of(c * rh[a], rh[a])
                q = (accs[a][sl, pl.ds(mine, rh[a]), :] + rcvs[a][sl]).astype(BF16)
                q_refs[a][...] = q

                @pl.when(chunk == pos_ref[1])
                def _():
                    g_refs[a][...] = q

        @pl.when(t == 0)
        def _():
            acc_i[slot] = jnp.zeros((d, fq), F32)
            acc_o[slot] = jnp.zeros((fq, d), F32)

        acc_i[slot] += _dot_tn(h2_ref[...], da_ref[...])
        acc_o[slot] += _dot_tn(s_ref[...], df_ref[...])

        @pl.when((t == 1) & (j >= 1))
        def _():
            emit(j - 1, 1 - slot)

        @pl.when(t == nt - 1)
        def _():
            for a in range(2):
                to_sibling(a, slot).start()

        @pl.when((t == nt - 1) & (j == last))
        def _():
            emit(j, slot)

    q_index = lambda j, t, p: (jnp.where(t == nt - 1, j, jnp.maximum(j - 1, 0)), 0, 0)
    got_index = lambda j, t, p: (p[1], 0, 0)
    blocks = ((None, rh[0], fq), (None, rh[1], d))
    return pl.pallas_call(
        body, name="mlp_wgrad",
        out_shape=tuple(jax.ShapeDtypeStruct((N_CHIPS,) + b[1:], BF16) for b in blocks) * 2,
        grid_spec=pltpu.PrefetchScalarGridSpec(
            num_scalar_prefetch=1, grid=(N_CHIPS, nt),
            in_specs=[pl.BlockSpec((tt, d), lambda j, t, p: (t, 0)),
                      pl.BlockSpec((None, tt, fq), lambda j, t, p: (j, t, 0)),
                      pl.BlockSpec((None, tt, fq), lambda j, t, p: (j, t, 0)),
                      pl.BlockSpec((tt, d), lambda j, t, p: (t, 0))],
            out_specs=tuple(pl.BlockSpec(b, q_index) for b in blocks)
            + tuple(pl.BlockSpec(b, got_index) for b in blocks),
            scratch_shapes=[pltpu.VMEM((2, d, fq), F32), pltpu.VMEM((2, fq, d), F32),
                            pltpu.VMEM((2, rh[0], fq), F32), pltpu.VMEM((2, rh[1], d), F32),
                            _dma_sems(4), _dma_sems(4)]),
        compiler_params=_params(dimension_semantics=("arbitrary", "arbitrary")),
    )(pos, h2, da4, s4, df)


def _bwd_mix(x, adr1, dmix, zs, mod, conv_w, w_in4, w_pool, pool_scale, w_out, after, tm):
    t_len, d = x.shape
    cw = w_in4.shape[2]
    dg = cw // N_GROUPS
    nt = t_len // tm

    def body(x_ref, adr1_ref, dmix_ref, zs_ref, mod_ref, cw_ref, win_ref, wp_ref, ps_ref, wout_ref, after_ref,
             gx_ref, dz_ref, sd_ref, sc_ref, dwp_ref, dbuf, *rest):
        qbufs = rest[:N_GROUPS]
        acc_d, acc_c, acc_p = rest[N_GROUPS:]
        i = pl.program_id(0)
        tile_idx = nt - 1 - i

        @pl.when(i == 0)
        def _():
            for buf in (dbuf,) + tuple(qbufs):
                buf[tm:tm + HALO, :] = jnp.zeros((HALO, buf.shape[1]), F32)
            acc_d[...] = jnp.zeros(acc_d.shape, F32)
            acc_c[...] = jnp.zeros(acc_c.shape, F32)
            acc_p[...] = jnp.zeros(acc_p.shape, F32)

        xv = x_ref[...]
        sc1 = mod_ref[0:1, d:2 * d]
        dcat = _dot_nt(dmix_ref[...], wout_ref[...])
        dyc, dyp = dcat[:, 0:cw], dcat[:, cw:2 * cw]
        zb, zc, zv = (zs_ref[:, k * cw:(k + 1) * cw].astype(F32) for k in range(3))
        conv, pooled = zs_ref[:, 3 * cw:4 * cw].astype(F32), zs_ref[:, 4 * cw:5 * cw]
        dzb = dyc * conv
        dcv = dyc * zb
        dbuf[0:tm, :] = dcv
        d1 = dbuf[pl.ds(1, tm), :]
        d2 = dbuf[pl.ds(2, tm), :]
        du = cw_ref[2:3, :] * dcv + cw_ref[1:2, :] * d1 + cw_ref[0:1, :] * d2
        dbuf[tm:tm + HALO, :] = dbuf[0:HALO, :]
        u = zc * zv
        acc_c[0] += _fold8(u * d2)
        acc_c[1] += _fold8(u * d1)
        acc_c[2] += _fold8(u * dcv)
        dzc = du * zv
        dzv = du * zc
        dpw = dyp * ps_ref[...]
        dps, pms, qs = [], [], []
        for gi, win in enumerate(POOL_WINDOWS):
            sl = slice(gi * dg, (gi + 1) * dg)
            p_g = pooled[:, sl]
            dpw_g = dpw[:, sl].astype(BF16)
            pms.append(_dot(p_g, wp_ref[gi]))
            acc_p[gi] += _dot_tn(p_g, dpw_g)
            dp_g = _dot_nt(dpw_g, wp_ref[gi])
            dps.append(dp_g)
            qs.append(dp_g * _inv_count(tile_idx, tm, win))
        acc_c[3] += _fold8(dyp * jnp.concatenate(pms, axis=1))
        sums = _window_sums(jnp.concatenate(qs, axis=1), qbufs, dg, tm, (0, 1))
        for buf in qbufs:
            buf[tm:tm + HALO, :] = buf[0:HALO, :]
        dzp = jnp.concatenate([sums[gi] - dps[gi] for gi in range(N_GROUPS)], axis=1)
        dz = [dzb.astype(BF16), dzc.astype(BF16), dzv.astype(BF16), dzp.astype(BF16)]
        dh1 = jnp.zeros((tm, d), F32)
        for j in range(N_CHIPS):
            dz_ref[j] = dz[j]
            dh1 = dh1 + _dot_nt(dz[j], win_ref[j])
        gx_ref[...] = adr1_ref[...] + dh1 * (1.0 + sc1)
        acc_d[0] += _fold8(dh1 * xv)
        acc_d[1] += _fold8(dh1)

        @pl.when(i == nt - 1)
        def _():
            for k in range(2):
                sd_ref[k:k + 1, :] = jnp.sum(acc_d[k], axis=0, keepdims=True)
            sd_ref[2:8, :] = jnp.zeros((6, d), F32)
            for k in range(4):
                sc_ref[k:k + 1, :] = jnp.sum(acc_c[k], axis=0, keepdims=True)
            sc_ref[4:8, :] = jnp.zeros((4, cw), F32)
            dwp_ref[...] = acc_p[...]

    rtile = lambda w: pl.BlockSpec((tm, w), lambda i: (nt - 1 - i, 0))
    whole = lambda shape: pl.BlockSpec(shape, lambda i: tuple(0 for _ in shape))
    return pl.pallas_call(
        body, name="bwd_mix",
        out_shape=(jax.ShapeDtypeStruct((t_len, d), F32), jax.ShapeDtypeStruct((N_CHIPS, t_len, cw), BF16),
                   jax.ShapeDtypeStruct((SUBLANES, d), F32),
                   jax.ShapeDtypeStruct((SUBLANES, cw), F32), jax.ShapeDtypeStruct((N_GROUPS, dg, dg), F32)),
        grid=(nt,),
        in_specs=[rtile(d), rtile(d), rtile(d), rtile(5 * cw)] + [VMEM_SPEC] * 6
        + [pl.BlockSpec(memory_space=pl.ANY)],
        out_specs=(rtile(d), pl.BlockSpec((N_CHIPS, tm, cw), lambda i: (0, nt - 1 - i, 0)),
                   whole((SUBLANES, d)), whole((SUBLANES, cw)), whole((N_GROUPS, dg, dg))),
        scratch_shapes=[pltpu.VMEM((tm + HALO, cw), F32)]
        + [pltpu.VMEM((tm + HALO, (N_GROUPS - k) * dg), F32) for k in range(N_GROUPS)]
        + [pltpu.VMEM((2, SUBLANES, d), F32), pltpu.VMEM((4, SUBLANES, cw), F32),
           pltpu.VMEM((N_GROUPS, dg, dg), F32)],
        compiler_params=_params(dimension_semantics=("arbitrary",)),
    )(x, adr1, dmix, zs, mod, conv_w, w_in4, w_pool, pool_scale, w_out, after)


def _mix_wgrad(h1, dz4, cat, dmix, small, n_all, tt):
    t_len, d = h1.shape
    cw = dz4.shape[2]
    nt = t_len // tt
    ro = (2 * cw) // N_CHIPS
    rh = (d // 2, ro // 2)
    n_small = len(small)

    def body(h1_ref, dz_ref, cat_ref, dmix_ref, *rest):
        p_refs = rest[:n_small]
        qi_ref, qo_ref, gi_ref, go_ref = rest[n_small:n_small + 4]
        sum_refs = rest[n_small + 4:2 * n_small + 4]
        all_refs = rest[2 * n_small + 4:2 * n_small + 4 + n_all]
        scratch = rest[2 * n_small + 4 + n_all:]
        acc_i, acc_o, rcv_i, rcv_o = scratch[:4]
        all_bufs = scratch[4:4 + n_small]
        s_sem, r_sem, s_small, r_small_sem = scratch[4 + n_small:]
        t = pl.program_id(0)
        x, y, c = _mesh_pos()
        chip = 2 * x + y
        groups = [(p_refs[k], sum_refs[k], all_refs[k] if k < n_all else None, all_bufs[k]) for k in range(n_small)]
        small_start, small_finish = _small_allreduce_steps(groups, s_small, r_small_sem)
        accs, rcvs, q_refs, g_refs = (acc_i, acc_o), (rcv_i, rcv_o), (qi_ref, qo_ref), (gi_ref, go_ref)

        @pl.when(t == 0)
        def _():
            small_start()
            acc_i[...] = jnp.zeros(acc_i.shape, F32)
            acc_o[...] = jnp.zeros(acc_o.shape, F32)

        h1v, dmixv = h1_ref[...], dmix_ref[...]
        for j in range(N_CHIPS):
            acc_i[j] += _dot_tn(h1v, dz_ref[j])
            acc_o[j] += _dot_tn(cat_ref[:, j * ro:(j + 1) * ro], dmixv)

        @pl.when(t == nt - 1)
        def _():
            cps = []
            for a in range(2):
                theirs = pl.multiple_of((1 - c) * rh[a], rh[a])
                cp = pltpu.make_async_remote_copy(accs[a].at[:, pl.ds(theirs, rh[a]), :], rcvs[a],
                                                  s_sem.at[a], r_sem.at[a],
                                                  device_id=(x, y, 1 - c), device_id_type=MESH)
                cp.start()
                cps.append(cp)
            for a in range(2):
                cps[a].wait()
                mine = pl.multiple_of(c * rh[a], rh[a])
                for j in range(N_CHIPS):
                    q_refs[a][j] = (accs[a][j, pl.ds(mine, rh[a]), :] + rcvs[a][j]).astype(BF16)
                g_refs[a][chip] = q_refs[a][chip]
            small_finish()

    stacks = ((N_CHIPS, rh[0], cw), (N_CHIPS, rh[1], d))
    n_cp = n_small * (N_DEV - 1)
    return pl.pallas_call(
        body, name="mix_wgrad",
        out_shape=tuple(jax.ShapeDtypeStruct(s, BF16) for s in stacks) * 2
        + tuple(jax.ShapeDtypeStruct(a.shape, F32) for a in small)
        + tuple(jax.ShapeDtypeStruct((N_DEV,) + a.shape, F32) for a in small[:n_all]),
        grid=(nt,),
        in_specs=[pl.BlockSpec((tt, d), lambda t: (t, 0)), pl.BlockSpec((N_CHIPS, tt, cw), lambda t: (0, t, 0)),
                  pl.BlockSpec((tt, 2 * cw), lambda t: (t, 0)), pl.BlockSpec((tt, d), lambda t: (t, 0))]
        + [VMEM_SPEC] * n_small,
        out_specs=(VMEM_SPEC,) * (4 + n_small + n_all),
        scratch_shapes=[pltpu.VMEM((N_CHIPS, d, cw), F32), pltpu.VMEM((N_CHIPS, ro, d), F32),
                        pltpu.VMEM(stacks[0], F32), pltpu.VMEM(stacks[1], F32)]
        + [pltpu.VMEM((N_DEV,) + a.shape, F32) for a in small]
        + [_dma_sems(2), _dma_sems(2), _dma_sems(n_cp), _dma_sems(n_cp)],
        compiler_params=_params(dimension_semantics=("arbitrary",)),
    )(h1, dz4, cat, dmix, *small)


def kernel(x, c, w_ada, b_ada, w_in, conv_w, w_pool, pool_scale, w_out, ln1_g, ln1_b, w_mlp_in, w_mlp_out, ln2_g, ln2_b, loss_target, m_w_ada, m_b_ada, m_w_in, m_conv_w, m_w_pool, m_pool_scale, m_w_out, m_ln1_g, m_ln1_b, m_w_mlp_in, m_w_mlp_out, m_ln2_g, m_ln2_b, v_w_ada, v_b_ada, v_w_in, v_conv_w, v_w_pool, v_pool_scale, v_w_out, v_ln1_g, v_ln1_b, v_w_mlp_in, v_w_mlp_out, v_ln2_g, v_ln2_b):
    t_len, d = x.shape[1], x.shape[2]
    cw = w_in.shape[2]
    cq = conv_w.shape[2]
    dg = w_pool.shape[2]
    assert cw == N_GROUPS * dg and cq * N_CHIPS == cw and LANES % cq == 0
    tm_mix = min(MIX_TOKEN_TILE, t_len)
    tm_mlp = min(MLP_TOKEN_TILE, t_len)
    tt = min(WGRAD_TILE, t_len // 4)
    chip = 2 * lax.axis_index("x") + lax.axis_index("y")
    pos = jnp.stack([lax.axis_index("c"), chip]).astype(jnp.int32)

    x2, tgt = x[0], loss_target[0]
    big_w = [w_in[0], w_out[0], w_mlp_in[0], w_mlp_out[0]]
    big_m = [m_w_in[0], m_w_out[0], m_w_mlp_in[0], m_w_mlp_out[0]]
    big_v = [v_w_in[0], v_w_out[0], v_w_mlp_in[0], v_w_mlp_out[0]]
    names = ["w_in", "w_out", "w_mlp_in", "w_mlp_out"]

    mod, cond_all, conv_full, w_in4, w_out4, w_mi_own, w_mo_own = _setup_exchange(
        c, w_ada[0], b_ada, conv_w[0], big_w, 2)
    w_out_full = w_out4.reshape(2 * cw, d)
    w_pool_bf = w_pool[0].astype(BF16)

    w_mi4, w_mo4, zs, mix, h1, cat = _fwd_mix(x2, mod, conv_full, w_in4, w_pool_bf, pool_scale, w_out_full,
                                              [w_mi_own, w_mo_own], tm_mix)
    adr1, dmix, da4, s4, df, h2, st2 = _mlp_fwd_bwd(x2, mix, tgt, mod, w_mi4, w_mo4, ln1_g, ln1_b, ln2_g, ln2_b,
                                                    tm_mlp)
    mlp_qi, mlp_qo, mlp_gi, mlp_go = _mlp_wgrad(h2, da4, s4, df, pos, tt)
    mlp_started = _split_start([mlp_qi, mlp_qo, mlp_gi, mlp_go], _chunk_exchange_plan(2), 2 * (N_CHIPS - 1),
                               "exchange_mlp")
    grad_x, dz4, st1, stc, dw_pool = _bwd_mix(x2, adr1, dmix, zs, mod, conv_full, w_in4, w_pool_bf, pool_scale,
                                              w_out_full, mlp_started[-1], tm_mix)

    mix_qi, mix_qo, mix_gi, mix_go, t1, t2, tc, g_w_pool, all1, all2 = _mix_wgrad(
        h1, dz4, cat, dmix, [st1, st2, stc, dw_pool], 2, tt)
    loss = t2[5, 0]
    g_ln1_g, g_ln1_b, g_ln2_g, g_ln2_b = t2[6:7], t2[7:8], t2[0:1], t2[1:2]
    g_pool_scale = tc[3:4]
    g_conv = lax.dynamic_slice_in_dim(tc[0:3], chip * cq, cq, axis=1)
    g_b_ada = jnp.concatenate([t1[1:2], t1[0:1], t2[8:9], t2[4:5], t2[3:4], t2[2:3]], axis=1)
    dmod_all = jnp.concatenate([all1[:, 1], all1[:, 0], all2[:, 8], all2[:, 4], all2[:, 3], all2[:, 2]], axis=1)
    na = w_ada.shape[2]
    dmod_mine = lax.dynamic_slice_in_dim(dmod_all, chip * na, na, axis=1)

    exchange, n_cp = _chunk_exchange_plan(2), 2 * (N_CHIPS - 1)
    mlp_got = _split_wait(mlp_started, exchange, t1, "exchange_mlp")[2:]
    mix_started = _split_start([mix_qi, mix_qo, mix_gi, mix_go], exchange, n_cp, "exchange_mix")
    mlp_halves = [_sum_chips(g, pos, mix_started[-1], "rs_total_" + n) for g, n in zip(mlp_got, names[2:])]
    join = _sibling_join_plan([h.shape for h in mlp_halves])
    join_started = _split_start(mlp_halves, join, 2, "join_mlp")
    g_ada, d_ada, nm_ada, nv_ada = _adamw_ada(cond_all.T, dmod_mine, w_ada[0], m_w_ada[0], v_w_ada[0],
                                              join_started[-1])
    mlp_g = _split_wait(join_started, join, d_ada, "join_mlp")
    mlp_out = [_adamw_big(g, w, m, v, "adamw_" + n)
               for g, w, m, v, n in zip(mlp_g, big_w[2:], big_m[2:], big_v[2:], names[2:])]
    mix_got = _split_wait(mix_started, exchange, mlp_out[1][0], "exchange_mix")[2:]
    mix_halves = [_sum_chips(g, pos, d_ada, "rs_total_" + n) for g, n in zip(mix_got, names[:2])]
    mix_g = _sibling_join(mix_halves, "mix")
    mix_out = [_adamw_big(g, w, m, v, "adamw_" + n)
               for g, w, m, v, n in zip(mix_g, big_w[:2], big_m[:2], big_v[:2], names[:2])]
    big_g = list(mix_g) + list(mlp_g)
    big_out = mix_out + mlp_out

    small_g = [g_b_ada, g_conv, g_w_pool.reshape(-1, dg), g_pool_scale, g_ln1_g, g_ln1_b, g_ln2_g, g_ln2_b]
    small_w = [b_ada, conv_w[0], w_pool.reshape(-1, dg), pool_scale, ln1_g, ln1_b, ln2_g, ln2_b]
    small_m = [m_b_ada, m_conv_w[0], m_w_pool.reshape(-1, dg), m_pool_scale, m_ln1_g, m_ln1_b, m_ln2_g, m_ln2_b]
    small_v = [v_b_ada, v_conv_w[0], v_w_pool.reshape(-1, dg), v_pool_scale, v_ln1_g, v_ln1_b, v_ln2_g, v_ln2_b]
    sm = _adamw_small(small_g, small_w, small_m, small_v)
    ns = len(small_g)
    s_delta, s_m, s_v = sm[:ns], sm[ns:2 * ns], sm[2 * ns:]

    def assemble(ada, small, big):
        return [ada[None], small[0], big[0][None], small[1][None], small[2].reshape(w_pool.shape), small[3],
                big[1][None], small[4], small[5], big[2][None], big[3][None], small[6], small[7]]

    grads = assemble(g_ada, small_g, big_g)
    deltas = assemble(d_ada, s_delta, [o[0] for o in big_out])
    new_m = assemble(nm_ada, s_m, [o[1] for o in big_out])
    new_v = assemble(nv_ada, s_v, [o[2] for o in big_out])
    return (loss, grad_x[None], *grads, *deltas, *new_m, *new_v)
```

```python
import jax
import jax.numpy as jnp
from jax import lax
from jax.experimental import pallas as pl
from jax.experimental.pallas import tpu as pltpu

F32 = jnp.float32
BF16 = jnp.bfloat16
MESH = pl.DeviceIdType.MESH

LN_EPS = 1e-5
DEEPNORM_ALPHA = 2.0 ** 0.25
POOL_WINDOWS = (2, 4, 8, 16)
N_GROUPS = len(POOL_WINDOWS)
HALO = 16
N_CHIPS = 4
N_DEV = 8
LANES = 128
SUBLANES = 8
VMEM_LIMIT = 56 * 1024 * 1024
MIX_TOKEN_TILE = 512
MLP_TOKEN_TILE = 512
WGRAD_TILE = 1024

ADAM_LR = 0.001
ADAM_B1 = 0.9
ADAM_B2 = 0.999
ADAM_EPS = 1e-08
ADAM_WD = 0.01
ADAM_STEP = 10

VMEM_SPEC = pl.BlockSpec(memory_space=pltpu.VMEM)
HBM_SPEC = pl.BlockSpec(memory_space=pltpu.HBM)


def _dot(a, b):
    return jnp.dot(a, b, preferred_element_type=F32)


def _dot_nt(a, b):
    return lax.dot_general(a, b, (((1,), (1,)), ((), ())), preferred_element_type=F32)


def _dot_tn(a, b):
    return lax.dot_general(a, b, (((0,), (0,)), ((), ())), preferred_element_type=F32)


def _fold8(v):
    r, n = v.shape
    return jnp.sum(v.reshape(r // SUBLANES, SUBLANES, n), axis=0)


def _mesh_pos():
    return lax.axis_index("x"), lax.axis_index("y"), lax.axis_index("c")


def _flip(v, bit):
    return 1 - v if bit else v


def _params(**kw):
    return pltpu.CompilerParams(vmem_limit_bytes=VMEM_LIMIT, **kw)


def _dma_sems(n):
    return pltpu.SemaphoreType.DMA((n,))


def _stack_gather(stacks, rows, s_ici, r_ici, s_d2d, r_d2d):
    x, y, c = _mesh_pos()
    chip = 2 * x + y
    pairs = [(i, m) for i in range(len(stacks)) for m in range(1, N_CHIPS)]

    def blk(i, slot, which):
        rh = rows[i] // 2
        return stacks[i].at[slot].at[pl.ds(which * rh, rh), :]

    def other(m):
        return 2 * _flip(x, m & 2) + _flip(y, m & 1)

    def ici(i, m, slot, to):
        k = i * (N_CHIPS - 1) + m - 1
        return pltpu.make_async_remote_copy(blk(i, slot, c), blk(i, slot, c), s_ici.at[k], r_ici.at[k],
                                            device_id=to, device_id_type=MESH)

    def d2d(i, m, which, to):
        k = i * (N_CHIPS - 1) + m - 1
        return pltpu.make_async_remote_copy(blk(i, other(m), which), blk(i, other(m), which),
                                            s_d2d.at[k], r_d2d.at[k], device_id=to, device_id_type=MESH)

    def start():
        for i, m in pairs:
            ici(i, m, chip, (_flip(x, m & 2), _flip(y, m & 1), c)).start()

    def forward(which=None):
        for i, m in pairs:
            if which is None or i == which:
                ici(i, m, other(m), (x, y, c)).wait_recv()
                d2d(i, m, c, (x, y, 1 - c)).start()

    def finish():
        for i, m in pairs:
            d2d(i, m, 1 - c, (x, y, c)).wait_recv()
        for i, m in pairs:
            ici(i, m, chip, (x, y, c)).wait_send()
            d2d(i, m, c, (x, y, c)).wait_send()

    return start, forward, finish


def _chunk_exchange(qs, gots, s_sem, r_sem):
    x, y, c = _mesh_pos()
    chip = 2 * x + y
    pairs = [(i, m) for i in range(len(qs)) for m in range(1, N_CHIPS)]

    def other(m):
        return 2 * _flip(x, m & 2) + _flip(y, m & 1)

    def send(i, m, to):
        k = i * (N_CHIPS - 1) + m - 1
        return pltpu.make_async_remote_copy(qs[i].at[other(m)], gots[i].at[chip], s_sem.at[k], r_sem.at[k],
                                            device_id=to, device_id_type=MESH)

    def arrival(i, m):
        k = i * (N_CHIPS - 1) + m - 1
        return pltpu.make_async_remote_copy(qs[i].at[other(m)], gots[i].at[other(m)], s_sem.at[k], r_sem.at[k],
                                            device_id=(x, y, c), device_id_type=MESH)

    def start():
        for i, m in pairs:
            send(i, m, (_flip(x, m & 2), _flip(y, m & 1), c)).start()

    def finish():
        for i, m in pairs:
            arrival(i, m).wait_recv()
        for i, m in pairs:
            send(i, m, (x, y, c)).wait_send()

    return start, finish


def _mod_scratch(d, na, cq):
    return [pltpu.VMEM((SUBLANES, d), F32), pltpu.VMEM((N_DEV, SUBLANES, d), F32),
            pltpu.VMEM((N_CHIPS, SUBLANES, na), F32),
            pltpu.VMEM((SUBLANES, cq), F32), pltpu.VMEM((N_CHIPS, SUBLANES, cq), F32),
            _dma_sems(N_DEV - 1), _dma_sems(N_DEV - 1),
            _dma_sems(N_CHIPS - 1), _dma_sems(N_CHIPS - 1), _dma_sems(N_CHIPS - 1), _dma_sems(N_CHIPS - 1)]


def _mod_steps(d, na, cq):
    def steps(c_ref, w_ref, b_ref, cw_ref, mod_ref, cond_ref, conv_ref,
              csend, cbuf, mbuf, cvsend, cvbuf, s1, r1, s2, r2, s3, r3):
        x, y, c = _mesh_pos()
        me = 4 * x + 2 * y + c
        chip = 2 * x + y
        csend[...] = jnp.broadcast_to(c_ref[...], (SUBLANES, d))
        cbuf[me] = csend[...]
        first = []
        for m in range(1, N_DEV):
            peer = (_flip(x, m & 4), _flip(y, m & 2), _flip(c, m & 1))
            cp = pltpu.make_async_remote_copy(csend, cbuf.at[me], s1.at[m - 1], r1.at[m - 1],
                                              device_id=peer, device_id_type=MESH)
            cp.start()
            first.append(cp)
        cvsend[...] = jnp.zeros((SUBLANES, cq), F32)
        cvsend[0:3, :] = cw_ref[...]
        cvbuf[chip] = cvsend[...]
        for m in range(1, N_DEV):
            src = 4 * _flip(x, m & 4) + 2 * _flip(y, m & 2) + _flip(c, m & 1)
            pltpu.make_async_remote_copy(csend, cbuf.at[src], s1.at[m - 1], r1.at[m - 1],
                                         device_id=(x, y, c), device_id_type=MESH).wait_recv()
        rows = lax.broadcasted_iota(jnp.int32, (SUBLANES, d), 0)
        call = jnp.zeros((SUBLANES, d), F32)
        for b in range(N_DEV):
            call = jnp.where(rows == b, cbuf[b], call)
        cond = call * jax.nn.sigmoid(call)
        cond_ref[...] = cond
        part = jnp.dot(cond, w_ref[...], preferred_element_type=F32, precision=lax.Precision.HIGHEST)
        mbuf[chip] = part
        second = []
        for m in range(1, N_CHIPS):
            peer = (_flip(x, m & 2), _flip(y, m & 1), c)
            cp = pltpu.make_async_remote_copy(mbuf.at[chip], mbuf.at[chip], s2.at[m - 1], r2.at[m - 1],
                                              device_id=peer, device_id_type=MESH)
            cp.start()
            second.append(cp)
            cp = pltpu.make_async_remote_copy(cvsend, cvbuf.at[chip], s3.at[m - 1], r3.at[m - 1],
                                              device_id=peer, device_id_type=MESH)
            cp.start()
            second.append(cp)
        for m in range(1, N_CHIPS):
            src = 2 * _flip(x, m & 2) + _flip(y, m & 1)
            pltpu.make_async_remote_copy(mbuf.at[src], mbuf.at[src], s2.at[m - 1], r2.at[m - 1],
                                         device_id=(x, y, c), device_id_type=MESH).wait_recv()
            pltpu.make_async_remote_copy(cvsend, cvbuf.at[src], s3.at[m - 1], r3.at[m - 1],
                                         device_id=(x, y, c), device_id_type=MESH).wait_recv()
        for cp in first + second:
            cp.wait_send()
        rows_n = lax.broadcasted_iota(jnp.int32, (SUBLANES, na), 0)
        for k in range(N_CHIPS):
            mine = jnp.sum(jnp.where(rows_n == me, mbuf[k], 0.0), axis=0, keepdims=True)
            mod_ref[:, k * na:(k + 1) * na] = jnp.broadcast_to(mine + b_ref[:, k * na:(k + 1) * na], (SUBLANES, na))
            conv_ref[:, k * cq:(k + 1) * cq] = cvbuf[k]

    return steps


def _setup_exchange(c_row, w_ada_s, b_ada, conv_w_s, shards, n_now):
    n = len(shards)
    n_cp = n_now * (N_CHIPS - 1)
    d, na, cq = c_row.shape[1], w_ada_s.shape[1], conv_w_s.shape[1]
    mod_scratch = _mod_scratch(d, na, cq)
    mod_steps = _mod_steps(d, na, cq)

    def body(*refs):
        mod_in, ins = refs[:4], refs[4:4 + n]
        mod_out, outs = refs[4 + n:7 + n], refs[7 + n:7 + 2 * n]
        bufs = refs[7 + 2 * n:7 + 3 * n]
        mod_scr = refs[7 + 3 * n:7 + 3 * n + len(mod_scratch)]
        s_ici, r_ici, s_d2d, r_d2d, s_loc = refs[7 + 3 * n + len(mod_scratch):]
        x, y, _ = _mesh_pos()
        chip = 2 * x + y
        def place(lo, hi):
            local = []
            for i in range(lo, hi):
                bufs[i][...] = ins[i][...].astype(BF16)
                cp = pltpu.make_async_copy(bufs[i], outs[i].at[chip], s_loc.at[i])
                cp.start()
                local.append(cp)
            return local

        start, forward, finish = _stack_gather(outs[:n_now], [s.shape[0] for s in shards[:n_now]],
                                               s_ici, r_ici, s_d2d, r_d2d)
        for cp in place(0, n_now):
            cp.wait()
        start()
        later = place(n_now, n)
        mod_steps(*mod_in, *mod_out, *mod_scr)
        forward()
        finish()
        for cp in later:
            cp.wait()

    return pl.pallas_call(
        body, name="setup_exchange",
        out_shape=(jax.ShapeDtypeStruct((SUBLANES, N_CHIPS * na), F32), jax.ShapeDtypeStruct((SUBLANES, d), F32),
                   jax.ShapeDtypeStruct((SUBLANES, N_CHIPS * cq), F32))
        + tuple(jax.ShapeDtypeStruct((N_CHIPS,) + s.shape, BF16) for s in shards),
        in_specs=[VMEM_SPEC] * (4 + n), out_specs=(VMEM_SPEC,) * 3 + (HBM_SPEC,) * n,
        scratch_shapes=[pltpu.VMEM(s.shape, BF16) for s in shards] + mod_scratch
        + [_dma_sems(n_cp), _dma_sems(n_cp), _dma_sems(n_cp), _dma_sems(n_cp), _dma_sems(n)],
        compiler_params=_params(),
    )(c_row, w_ada_s, b_ada, conv_w_s, *shards)


def _sibling_join_plan(shapes):
    def plan(refs, s_sem, r_sem):
        x, y, c = _mesh_pos()

        def half(i, which):
            rh = shapes[i][0] // 2
            return refs[i].at[pl.ds(which * rh, rh), :]

        def start():
            for i in range(len(refs)):
                pltpu.make_async_remote_copy(half(i, c), half(i, c), s_sem.at[i], r_sem.at[i],
                                             device_id=(x, y, 1 - c), device_id_type=MESH).start()

        def finish():
            for i in range(len(refs)):
                pltpu.make_async_remote_copy(half(i, 1 - c), half(i, 1 - c), s_sem.at[i], r_sem.at[i],
                                             device_id=(x, y, c), device_id_type=MESH).wait_recv()
            for i in range(len(refs)):
                pltpu.make_async_remote_copy(half(i, c), half(i, c), s_sem.at[i], r_sem.at[i],
                                             device_id=(x, y, c), device_id_type=MESH).wait_send()

        return start, finish

    return plan


def _chunk_exchange_plan(n):
    return lambda refs, s_sem, r_sem: _chunk_exchange(refs[:n], refs[n:], s_sem, r_sem)


def _sibling_join(gs, tag):
    n = len(gs)
    plan = _sibling_join_plan([g.shape for g in gs])

    def body(*refs):
        start, finish = plan(refs[n:2 * n], refs[2 * n], refs[2 * n + 1])
        start()
        finish()

    return pl.pallas_call(
        body, name="rs_sibling_join_" + tag,
        out_shape=tuple(jax.ShapeDtypeStruct(g.shape, g.dtype) for g in gs),
        in_specs=[HBM_SPEC] * n, out_specs=(HBM_SPEC,) * n,
        input_output_aliases={i: i for i in range(n)},
        scratch_shapes=[_dma_sems(n), _dma_sems(n)],
        compiler_params=_params(),
    )(*gs)


SEM_SPEC = pl.BlockSpec(memory_space=pltpu.SEMAPHORE)
SIDE_EFFECT = pltpu.SideEffectType.DATAFLOW_SIDE_EFFECTING


def _split_start(arrays, plan, n_cp, tag):
    n = len(arrays)

    def body(*refs):
        start, _ = plan(refs[n + 2:2 * n + 2], refs[n], refs[n + 1])
        start()
        token = refs[2 * n + 2]
        token[...] = jnp.zeros_like(token)

    through = [pltpu.with_memory_space_constraint(a, pltpu.HBM) for a in arrays]
    return pl.pallas_call(
        body, name=tag + "_start",
        out_shape=(pltpu.SemaphoreType.DMA((n_cp,)), pltpu.SemaphoreType.DMA((n_cp,)))
        + tuple(pltpu.HBM(a.shape, a.dtype) for a in through) + (jax.ShapeDtypeStruct((SUBLANES, LANES), F32),),
        in_specs=[HBM_SPEC] * n, out_specs=(SEM_SPEC, SEM_SPEC) + (HBM_SPEC,) * n + (VMEM_SPEC,),
        input_output_aliases={k: 2 + k for k in range(n)},
        compiler_params=pltpu.CompilerParams(has_side_effects=SIDE_EFFECT),
    )(*through)


def _split_wait(started, plan, after, tag):
    send_sems, recv_sems, arrays = started[0], started[1], started[2:-1]
    n = len(arrays)

    def body(*refs):
        _, finish = plan(refs[:n], refs[n], refs[n + 1])
        finish()

    return pl.pallas_call(
        body, name=tag + "_wait",
        out_shape=tuple(pltpu.HBM(a.shape, a.dtype) for a in arrays),
        in_specs=[HBM_SPEC] * n + [SEM_SPEC, SEM_SPEC, pl.BlockSpec(memory_space=pl.ANY)],
        out_specs=(HBM_SPEC,) * n,
        input_output_aliases={k: k for k in range(n)},
        compiler_params=pltpu.CompilerParams(has_side_effects=SIDE_EFFECT),
    )(*arrays, send_sems, recv_sems, after)


def _small_allreduce_steps(groups, s_sem, r_sem):
    x, y, c = _mesh_pos()
    me = 4 * x + 2 * y + c
    pairs = [(k, m) for k in range(len(groups)) for m in range(1, N_DEV)]

    def send(k, m, to):
        p_ref, all_ref = groups[k][0], groups[k][3]
        i = k * (N_DEV - 1) + m - 1
        return pltpu.make_async_remote_copy(p_ref, all_ref.at[me], s_sem.at[i], r_sem.at[i],
                                            device_id=to, device_id_type=MESH)

    def start():
        for p_ref, _, _, all_ref in groups:
            all_ref[me] = p_ref[...]
        for k, m in pairs:
            send(k, m, (_flip(x, m & 4), _flip(y, m & 2), _flip(c, m & 1))).start()

    def finish():
        for k, m in pairs:
            p_ref, all_ref = groups[k][0], groups[k][3]
            i = k * (N_DEV - 1) + m - 1
            src = 4 * _flip(x, m & 4) + 2 * _flip(y, m & 2) + _flip(c, m & 1)
            pltpu.make_async_remote_copy(p_ref, all_ref.at[src], s_sem.at[i], r_sem.at[i],
                                         device_id=(x, y, c), device_id_type=MESH).wait_recv()
        for k, m in pairs:
            send(k, m, (x, y, c)).wait_send()
        for _, sum_ref, all_out_ref, all_ref in groups:
            sum_ref[...] = (((all_ref[0] + all_ref[1]) + (all_ref[2] + all_ref[3]))
                            + ((all_ref[4] + all_ref[5]) + (all_ref[6] + all_ref[7])))
            if all_out_ref is not None:
                all_out_ref[...] = all_ref[...]

    return start, finish


def _row_block(rows, cap=256):
    rb = min(rows, cap)
    assert rows % rb == 0
    return rb


def _sum_chips(got, pos, after, name):
    _, rh, cc = got.shape
    rb = _row_block(rh)
    nb = rh // rb

    def body(pos_ref, a_ref, after_ref, o_ref):
        a = a_ref[...].astype(F32)
        o_ref[...] = (a[0] + a[1]) + (a[2] + a[3])

    return pl.pallas_call(
        body, name=name,
        out_shape=jax.ShapeDtypeStruct((2 * rh, cc), F32),
        grid_spec=pltpu.PrefetchScalarGridSpec(
            num_scalar_prefetch=1, grid=(nb,),
            in_specs=[pl.BlockSpec((N_CHIPS, rb, cc), lambda i, p: (0, i, 0)),
                      pl.BlockSpec(memory_space=pl.ANY)],
            out_specs=pl.BlockSpec((rb, cc), lambda i, p: (p[0] * nb + i, 0))),
        compiler_params=_params(dimension_semantics=("arbitrary",)),
    )(pos, got, after)


def _adamw_math(w, g, m, v):
    m = ADAM_B1 * m + (1.0 - ADAM_B1) * g
    v = ADAM_B2 * v + (1.0 - ADAM_B2) * jnp.square(g)
    m_hat = m / (1.0 - ADAM_B1 ** ADAM_STEP)
    v_hat = v / (1.0 - ADAM_B2 ** ADAM_STEP)
    delta = -ADAM_LR * (m_hat / (jnp.sqrt(v_hat) + ADAM_EPS) + ADAM_WD * w)
    return delta, m, v


def _adamw_big(g, w, m, v, name):
    r, cc = w.shape
    rb = _row_block(r)

    def body(g_ref, w_ref, m_ref, v_ref, d_ref, mo_ref, vo_ref):
        d, mn, vn = _adamw_math(w_ref[...], g_ref[...], m_ref[...], v_ref[...])
        d_ref[...] = d
        mo_ref[...] = mn
        vo_ref[...] = vn

    spec = pl.BlockSpec((rb, cc), lambda i: (i, 0))
    return pl.pallas_call(
        body, name=name,
        out_shape=(jax.ShapeDtypeStruct((r, cc), F32),) * 3,
        grid=(r // rb,), in_specs=[spec] * 4, out_specs=(spec,) * 3,
        compiler_params=_params(dimension_semantics=("arbitrary",)),
    )(g, w, m, v)


def _adamw_ada(cond_t, dmod, w, m, v, after):
    d, na = w.shape
    rb = _row_block(d, 128)

    def body(ct_ref, dm_ref, w_ref, m_ref, v_ref, after_ref, g_ref, d_ref, mo_ref, vo_ref):
        g = jnp.dot(ct_ref[...], dm_ref[...], preferred_element_type=F32, precision=lax.Precision.HIGHEST)
        dl, mn, vn = _adamw_math(w_ref[...], g, m_ref[...], v_ref[...])
        g_ref[...] = g
        d_ref[...] = dl
        mo_ref[...] = mn
        vo_ref[...] = vn

    spec = pl.BlockSpec((rb, na), lambda i: (i, 0))
    return pl.pallas_call(
        body, name="adamw_w_ada",
        out_shape=(jax.ShapeDtypeStruct((d, na), F32),) * 4,
        grid=(d // rb,),
        in_specs=[pl.BlockSpec((rb, N_DEV), lambda i: (i, 0)), pl.BlockSpec((N_DEV, na), lambda i: (0, 0)),
                  spec, spec, spec, pl.BlockSpec(memory_space=pl.ANY)],
        out_specs=(spec,) * 4,
        compiler_params=_params(dimension_semantics=("arbitrary",)),
    )(cond_t, dmod, w, m, v, after)


def _adamw_small(gs, ws, ms, vs):
    n = len(gs)

    def body(*refs):
        ins, outs = refs[:4 * n], refs[4 * n:]
        for i in range(n):
            d, mn, vn = _adamw_math(ins[n + i][...], ins[i][...], ins[2 * n + i][...], ins[3 * n + i][...])
            outs[i][...] = d
            outs[n + i][...] = mn
            outs[2 * n + i][...] = vn

    shapes = tuple(jax.ShapeDtypeStruct(w.shape, F32) for w in ws)
    return pl.pallas_call(
        body, name="adamw_small",
        out_shape=shapes * 3,
        in_specs=[VMEM_SPEC] * (4 * n), out_specs=(VMEM_SPEC,) * (3 * n),
        compiler_params=_params(),
    )(*gs, *ws, *ms, *vs)


def _layer_norm_fwd(r):
    mu = jnp.mean(r, axis=-1, keepdims=True)
    xc = r - mu
    var = jnp.mean(jnp.square(xc), axis=-1, keepdims=True)
    rstd = lax.rsqrt(var + LN_EPS)
    return xc * rstd, rstd


def _layer_norm_bwd(dy, xhat, rstd, gain):
    dxh = dy * gain
    m1 = jnp.mean(dxh, axis=-1, keepdims=True)
    m2 = jnp.mean(dxh * xhat, axis=-1, keepdims=True)
    return rstd * (dxh - m1 - xhat * m2)


def _inv_count(tile, tm, win):
    t = (tile * tm + lax.broadcasted_iota(jnp.int32, (tm, 1), 0) + 1).astype(F32)
    return 1.0 / jnp.minimum(t, float(win))


def _window_sums(v, bufs, dg, tm, offsets):
    base, sign = offsets
    level = v
    sums = []
    for k in range(N_GROUPS):
        bufs[k][base:base + tm, :] = level
        level = level + bufs[k][pl.ds(base + sign * (1 << k), tm), :]
        sums.append(level[:, 0:dg])
        level = level[:, dg:]
    return sums


def _fwd_mix(x, mod, conv_w, w_in4, w_pool, pool_scale, w_out, ln_g, ln_b, late_stacks, tm):
    t_len, d = x.shape
    cw = w_in4.shape[2]
    dg = cw // N_GROUPS
    nt = t_len // tm
    n_late = len(late_stacks)
    n_cp = n_late * (N_CHIPS - 1)
    pass_steps = [min(((k + 1) * nt) // n_late + (1 if k + 1 < n_late else -2), nt - 1) if nt > 2 else nt - 1
                  for k in range(n_late)]
    pass_steps = [max(p, 0) for p in pass_steps]

    def body(x_ref, mod_ref, cw_ref, win_ref, wp_ref, ps_ref, wout_ref, g_ref, b_ref, *rest):
        late = rest[n_late:2 * n_late]
        x1_ref, zs_ref, mix_ref, h1_ref, cat_ref = rest[2 * n_late:2 * n_late + 5]
        ubuf = rest[2 * n_late + 5]
        vbufs = rest[2 * n_late + 6:2 * n_late + 6 + N_GROUPS]
        s_ici, r_ici, s_d2d, r_d2d = rest[2 * n_late + 6 + N_GROUPS:]
        i = pl.program_id(0)
        start, forward, finish = _stack_gather(late, [s.shape[1] for s in late_stacks], s_ici, r_ici, s_d2d, r_d2d)

        @pl.when(i == 0)
        def _():
            start()
            for buf in (ubuf,) + tuple(vbufs):
                buf[0:HALO, :] = jnp.zeros((HALO, buf.shape[1]), F32)

        xv = x_ref[...]
        sh1, sc1, g1 = mod_ref[0:1, 0:d], mod_ref[0:1, d:2 * d], mod_ref[0:1, 2 * d:3 * d]
        h1 = (xv * (1.0 + sc1) + sh1).astype(BF16)
        h1_ref[...] = h1
        zb = _dot(h1, win_ref[0])
        zc = _dot(h1, win_ref[1])
        zv = _dot(h1, win_ref[2])
        zp = _dot(h1, win_ref[3])
        u = zc * zv
        ubuf[HALO:HALO + tm, :] = u
        u1 = ubuf[pl.ds(HALO - 1, tm), :]
        u2 = ubuf[pl.ds(HALO - 2, tm), :]
        conv = cw_ref[0:1, :] * u2 + cw_ref[1:2, :] * u1 + cw_ref[2:3, :] * u
        ubuf[0:HALO, :] = ubuf[tm:tm + HALO, :]
        yc = zb * conv
        sums = _window_sums(zp, vbufs, dg, tm, (HALO, -1))
        ps, pms = [], []
        for gi, win in enumerate(POOL_WINDOWS):
            p_g = sums[gi] * _inv_count(i, tm, win) - zp[:, gi * dg:(gi + 1) * dg]
            ps.append(p_g)
            pms.append(_dot(p_g.astype(BF16), wp_ref[gi]))
        for buf in vbufs:
            buf[0:HALO, :] = buf[tm:tm + HALO, :]
        pooled = jnp.concatenate(ps, axis=1)
        yp = jnp.concatenate(pms, axis=1) * ps_ref[...]
        cat = jnp.concatenate([yc, yp], axis=1).astype(BF16)
        cat_ref[...] = cat
        mix = _dot(cat, wout_ref[...])
        mix_ref[...] = mix
        xhat, _ = _layer_norm_fwd(DEEPNORM_ALPHA * xv + (1.0 + g1) * mix)
        x1_ref[...] = xhat * g_ref[...] + b_ref[...]
        zs_ref[:, 0 * cw:1 * cw] = zb.astype(BF16)
        zs_ref[:, 1 * cw:2 * cw] = zc.astype(BF16)
        zs_ref[:, 2 * cw:3 * cw] = zv.astype(BF16)
        zs_ref[:, 3 * cw:4 * cw] = conv.astype(BF16)
        zs_ref[:, 4 * cw:5 * cw] = pooled.astype(BF16)

        for k in range(n_late):
            @pl.when(i == pass_steps[k])
            def _():
                forward(k)

        @pl.when(i == nt - 1)
        def _():
            finish()

    tile = lambda w: pl.BlockSpec((tm, w), lambda i: (i, 0))
    n_in = 9
    return pl.pallas_call(
        body, name="fwd_mix",
        out_shape=tuple(jax.ShapeDtypeStruct(s.shape, s.dtype) for s in late_stacks)
        + (jax.ShapeDtypeStruct((t_len, d), F32), jax.ShapeDtypeStruct((t_len, 5 * cw), BF16),
           jax.ShapeDtypeStruct((t_len, d), F32), jax.ShapeDtypeStruct((t_len, d), BF16),
           jax.ShapeDtypeStruct((t_len, 2 * cw), BF16)),
        grid=(nt,),
        in_specs=[tile(d)] + [VMEM_SPEC] * 8 + [HBM_SPEC] * n_late,
        out_specs=(HBM_SPEC,) * n_late + (tile(d), tile(5 * cw), tile(d), tile(d), tile(2 * cw)),
        input_output_aliases={n_in + k: k for k in range(n_late)},
        scratch_shapes=[pltpu.VMEM((tm + HALO, cw), F32)]
        + [pltpu.VMEM((tm + HALO, (N_GROUPS - k) * dg), F32) for k in range(N_GROUPS)]
        + [_dma_sems(n_cp), _dma_sems(n_cp), _dma_sems(n_cp), _dma_sems(n_cp)],
        compiler_params=_params(dimension_semantics=("arbitrary",)),
    )(x, mod, conv_w, w_in4, w_pool, pool_scale, w_out, ln_g, ln_b, *late_stacks)


def _mlp_fwd_bwd(x1, tgt, mod, w_mi4, w_mo4, ln_g, ln_b, tm):
    t_len, d = x1.shape
    fq = w_mi4.shape[2]
    nt = t_len // tm

    def body(x1_ref, tgt_ref, mod_ref, wi_ref, wo_ref, g_ref, b_ref,
             dx1_ref, da_ref, r_ref, df_ref, h2_ref, st_ref, acc):
        i = pl.program_id(0)

        @pl.when(i == 0)
        def _():
            acc[...] = jnp.zeros(acc.shape, F32)

        x1v = x1_ref[...]
        sh2, sc2, g2 = mod_ref[0:1, 3 * d:4 * d], mod_ref[0:1, 4 * d:5 * d], mod_ref[0:1, 5 * d:6 * d]
        h2 = (x1v * (1.0 + sc2) + sh2).astype(BF16)
        h2_ref[...] = h2
        f = jnp.zeros((tm, d), F32)
        for j in range(N_CHIPS):
            a = jnp.maximum(_dot(h2, wi_ref[j]), 0.0)
            r_ref[j] = a.astype(BF16)
            f = f + _dot((a * a).astype(BF16), wo_ref[j])
        xhat, rstd = _layer_norm_fwd(DEEPNORM_ALPHA * x1v + (1.0 + g2) * f)
        gain = g_ref[...]
        err = xhat * gain + b_ref[...] - tgt_ref[...]
        dy = err * (1.0 / d)
        dr2 = _layer_norm_bwd(dy, xhat, rstd, gain)
        df = ((1.0 + g2) * dr2).astype(BF16)
        df_ref[...] = df
        dh2 = jnp.zeros((tm, d), F32)
        for j in range(N_CHIPS):
            ds = _dot_nt(df, wo_ref[j])
            da = (ds * (2.0 * r_ref[j].astype(F32))).astype(BF16)
            da_ref[j] = da
            dh2 = dh2 + _dot_nt(da, wi_ref[j])
        dx1_ref[...] = DEEPNORM_ALPHA * dr2 + dh2 * (1.0 + sc2)
        acc[0] += _fold8(dy * xhat)
        acc[1] += _fold8(dy)
        acc[2] += _fold8(dr2 * f)
        acc[3] += _fold8(dh2 * x1v)
        acc[4] += _fold8(dh2)
        acc[5] += _fold8(err * err)

        @pl.when(i == nt - 1)
        def _():
            for k in range(5):
                st_ref[k:k + 1, :] = jnp.sum(acc[k], axis=0, keepdims=True)
            loss = jnp.sum(acc[5]) * (0.5 / d)
            st_ref[5:6, :] = jnp.broadcast_to(loss, (1, d))
            st_ref[6:8, :] = jnp.zeros((2, d), F32)

    tile = lambda w: pl.BlockSpec((tm, w), lambda i: (i, 0))
    tile4 = pl.BlockSpec((N_CHIPS, tm, fq), lambda i: (0, i, 0))
    return pl.pallas_call(
        body, name="mlp_fwd_bwd",
        out_shape=(jax.ShapeDtypeStruct((t_len, d), F32),
                   jax.ShapeDtypeStruct((N_CHIPS, t_len, fq), BF16), jax.ShapeDtypeStruct((N_CHIPS, t_len, fq), BF16),
                   jax.ShapeDtypeStruct((t_len, d), BF16), jax.ShapeDtypeStruct((t_len, d), BF16),
                   jax.ShapeDtypeStruct((SUBLANES, d), F32)),
        grid=(nt,),
        in_specs=[tile(d), tile(d)] + [VMEM_SPEC] * 5,
        out_specs=(tile(d), tile4, tile4, tile(d), tile(d), pl.BlockSpec((SUBLANES, d), lambda i: (0, 0))),
        scratch_shapes=[pltpu.VMEM((6, SUBLANES, d), F32)],
        compiler_params=_params(dimension_semantics=("arbitrary",)),
    )(x1, tgt, mod, w_mi4, w_mo4, ln_g, ln_b)


def _mlp_wgrad(h2, da4, r4, df, pos, tt):
    t_len, d = h2.shape
    fq = da4.shape[2]
    nt = t_len // tt
    assert nt >= 3
    rh = (d // 2, fq // 2)
    last = N_CHIPS - 1

    def body(pos_ref, h2_ref, da_ref, r_ref, df_ref, qi_ref, qo_ref, gi_ref, go_ref,
             acc_i, acc_o, rcv_i, rcv_o, s_sem, r_sem):
        j, t = pl.program_id(0), pl.program_id(1)
        slot = lax.rem(j, 2)
        x, y, c = _mesh_pos()
        accs, rcvs, q_refs, g_refs = (acc_i, acc_o), (rcv_i, rcv_o), (qi_ref, qo_ref), (gi_ref, go_ref)

        def to_sibling(a, sl):
            theirs = pl.multiple_of((1 - c) * rh[a], rh[a])
            return pltpu.make_async_remote_copy(accs[a].at[sl].at[pl.ds(theirs, rh[a]), :], rcvs[a].at[sl],
                                                s_sem.at[2 * sl + a], r_sem.at[2 * sl + a],
                                                device_id=(x, y, 1 - c), device_id_type=MESH)

        def emit(chunk, sl):
            for a in range(2):
                to_sibling(a, sl).wait()
                mine = pl.multiple_of(c * rh[a], rh[a])
                q = (accs[a][sl, pl.ds(mine, rh[a]), :] + rcvs[a][sl]).astype(BF16)
                q_refs[a][...] = q

                @pl.when(chunk == pos_ref[1])
                def _():
                    g_refs[a][...] = q

        @pl.when(t == 0)
        def _():
            acc_i[slot] = jnp.zeros((d, fq), F32)
            acc_o[slot] = jnp.zeros((fq, d), F32)

        r = r_ref[...]
        acc_i[slot] += _dot_tn(h2_ref[...], da_ref[...])
        acc_o[slot] += _dot_tn(r * r, df_ref[...])

        @pl.when((t == 1) & (j >= 1))
        def _():
            emit(j - 1, 1 - slot)

        @pl.when(t == nt - 1)
        def _():
            for a in range(2):
                to_sibling(a, slot).start()

        @pl.when((t == nt - 1) & (j == last))
        def _():
            emit(j, slot)

    q_index = lambda j, t, p: (jnp.where(t == nt - 1, j, jnp.maximum(j - 1, 0)), 0, 0)
    got_index = lambda j, t, p: (p[1], 0, 0)
    blocks = ((None, rh[0], fq), (None, rh[1], d))
    return pl.pallas_call(
        body, name="mlp_wgrad",
        out_shape=tuple(jax.ShapeDtypeStruct((N_CHIPS,) + b[1:], BF16) for b in blocks) * 2,
        grid_spec=pltpu.PrefetchScalarGridSpec(
            num_scalar_prefetch=1, grid=(N_CHIPS, nt),
            in_specs=[pl.BlockSpec((tt, d), lambda j, t, p: (t, 0)),
                      pl.BlockSpec((None, tt, fq), lambda j, t, p: (j, t, 0)),
                      pl.BlockSpec((None, tt, fq), lambda j, t, p: (j, t, 0)),
                      pl.BlockSpec((tt, d), lambda j, t, p: (t, 0))],
            out_specs=tuple(pl.BlockSpec(b, q_index) for b in blocks)
            + tuple(pl.BlockSpec(b, got_index) for b in blocks),
            scratch_shapes=[pltpu.VMEM((2, d, fq), F32), pltpu.VMEM((2, fq, d), F32),
                            pltpu.VMEM((2, rh[0], fq), F32), pltpu.VMEM((2, rh[1], d), F32),
                            _dma_sems(4), _dma_sems(4)]),
        compiler_params=_params(dimension_semantics=("arbitrary", "arbitrary")),
    )(pos, h2, da4, r4, df)


def _bwd_mix(x, mix, dx1, zs, mod, conv_w, w_in4, w_pool, pool_scale, w_out, ln_g, after, tm):
    t_len, d = x.shape
    cw = w_in4.shape[2]
    dg = cw // N_GROUPS
    nt = t_len // tm

    def body(x_ref, mix_ref, dx1_ref, zs_ref, mod_ref, cw_ref, win_ref, wp_ref, ps_ref, wout_ref, g_ref, after_ref,
             gx_ref, dz_ref, dmix_ref, sd_ref, sc_ref, dwp_ref, dbuf, *rest):
        qbufs = rest[:N_GROUPS]
        acc_d, acc_c, acc_p = rest[N_GROUPS:]
        i = pl.program_id(0)
        tile_idx = nt - 1 - i

        @pl.when(i == 0)
        def _():
            for buf in (dbuf,) + tuple(qbufs):
                buf[tm:tm + HALO, :] = jnp.zeros((HALO, buf.shape[1]), F32)
            acc_d[...] = jnp.zeros(acc_d.shape, F32)
            acc_c[...] = jnp.zeros(acc_c.shape, F32)
            acc_p[...] = jnp.zeros(acc_p.shape, F32)

        xv, mixv, dx1v = x_ref[...], mix_ref[...], dx1_ref[...]
        sc1, g1 = mod_ref[0:1, d:2 * d], mod_ref[0:1, 2 * d:3 * d]
        xhat, rstd = _layer_norm_fwd(DEEPNORM_ALPHA * xv + (1.0 + g1) * mixv)
        dr1 = _layer_norm_bwd(dx1v, xhat, rstd, g_ref[...])
        dmix = ((1.0 + g1) * dr1).astype(BF16)
        dmix_ref[...] = dmix
        dcat = _dot_nt(dmix, wout_ref[...])
        dyc, dyp = dcat[:, 0:cw], dcat[:, cw:2 * cw]
        zb, zc, zv = (zs_ref[:, k * cw:(k + 1) * cw].astype(F32) for k in range(3))
        conv, pooled = zs_ref[:, 3 * cw:4 * cw].astype(F32), zs_ref[:, 4 * cw:5 * cw]
        dzb = dyc * conv
        dcv = dyc * zb
        dbuf[0:tm, :] = dcv
        d1 = dbuf[pl.ds(1, tm), :]
        d2 = dbuf[pl.ds(2, tm), :]
        du = cw_ref[2:3, :] * dcv + cw_ref[1:2, :] * d1 + cw_ref[0:1, :] * d2
        dbuf[tm:tm + HALO, :] = dbuf[0:HALO, :]
        u = zc * zv
        acc_c[0] += _fold8(u * d2)
        acc_c[1] += _fold8(u * d1)
        acc_c[2] += _fold8(u * dcv)
        dzc = du * zv
        dzv = du * zc
        dpw = dyp * ps_ref[...]
        dps, pms, qs = [], [], []
        for gi, win in enumerate(POOL_WINDOWS):
            sl = slice(gi * dg, (gi + 1) * dg)
            p_g = pooled[:, sl]
            dpw_g = dpw[:, sl].astype(BF16)
            pms.append(_dot(p_g, wp_ref[gi]))
            acc_p[gi] += _dot_tn(p_g, dpw_g)
            dp_g = _dot_nt(dpw_g, wp_ref[gi])
            dps.append(dp_g)
            qs.append(dp_g * _inv_count(tile_idx, tm, win))
        acc_c[3] += _fold8(dyp * jnp.concatenate(pms, axis=1))
        sums = _window_sums(jnp.concatenate(qs, axis=1), qbufs, dg, tm, (0, 1))
        for buf in qbufs:
            buf[tm:tm + HALO, :] = buf[0:HALO, :]
        dzp = jnp.concatenate([sums[gi] - dps[gi] for gi in range(N_GROUPS)], axis=1)
        dz = [dzb.astype(BF16), dzc.astype(BF16), dzv.astype(BF16), dzp.astype(BF16)]
        dh1 = jnp.zeros((tm, d), F32)
        for j in range(N_CHIPS):
            dz_ref[j] = dz[j]
            dh1 = dh1 + _dot_nt(dz[j], win_ref[j])
        gx_ref[...] = DEEPNORM_ALPHA * dr1 + dh1 * (1.0 + sc1)
        acc_d[0] += _fold8(dx1v * xhat)
        acc_d[1] += _fold8(dx1v)
        acc_d[2] += _fold8(dr1 * mixv)
        acc_d[3] += _fold8(dh1 * xv)
        acc_d[4] += _fold8(dh1)

        @pl.when(i == nt - 1)
        def _():
            for k in range(5):
                sd_ref[k:k + 1, :] = jnp.sum(acc_d[k], axis=0, keepdims=True)
            sd_ref[5:8, :] = jnp.zeros((3, d), F32)
            for k in range(4):
                sc_ref[k:k + 1, :] = jnp.sum(acc_c[k], axis=0, keepdims=True)
            sc_ref[4:8, :] = jnp.zeros((4, cw), F32)
            dwp_ref[...] = acc_p[...]

    rtile = lambda w: pl.BlockSpec((tm, w), lambda i: (nt - 1 - i, 0))
    whole = lambda shape: pl.BlockSpec(shape, lambda i: tuple(0 for _ in shape))
    return pl.pallas_call(
        body, name="bwd_mix",
        out_shape=(jax.ShapeDtypeStruct((t_len, d), F32), jax.ShapeDtypeStruct((N_CHIPS, t_len, cw), BF16),
                   jax.ShapeDtypeStruct((t_len, d), BF16), jax.ShapeDtypeStruct((SUBLANES, d), F32),
                   jax.ShapeDtypeStruct((SUBLANES, cw), F32), jax.ShapeDtypeStruct((N_GROUPS, dg, dg), F32)),
        grid=(nt,),
        in_specs=[rtile(d), rtile(d), rtile(d), rtile(5 * cw)] + [VMEM_SPEC] * 7
        + [pl.BlockSpec(memory_space=pl.ANY)],
        out_specs=(rtile(d), pl.BlockSpec((N_CHIPS, tm, cw), lambda i: (0, nt - 1 - i, 0)), rtile(d),
                   whole((SUBLANES, d)), whole((SUBLANES, cw)), whole((N_GROUPS, dg, dg))),
        scratch_shapes=[pltpu.VMEM((tm + HALO, cw), F32)]
        + [pltpu.VMEM((tm + HALO, (N_GROUPS - k) * dg), F32) for k in range(N_GROUPS)]
        + [pltpu.VMEM((5, SUBLANES, d), F32), pltpu.VMEM((4, SUBLANES, cw), F32),
           pltpu.VMEM((N_GROUPS, dg, dg), F32)],
        compiler_params=_params(dimension_semantics=("arbitrary",)),
    )(x, mix, dx1, zs, mod, conv_w, w_in4, w_pool, pool_scale, w_out, ln_g, after)


def _mix_wgrad(h1, dz4, cat, dmix, small, n_all, tt):
    t_len, d = h1.shape
    cw = dz4.shape[2]
    nt = t_len // tt
    ro = (2 * cw) // N_CHIPS
    rh = (d // 2, ro // 2)
    n_small = len(small)

    def body(h1_ref, dz_ref, cat_ref, dmix_ref, *rest):
        p_refs = rest[:n_small]
        qi_ref, qo_ref, gi_ref, go_ref = rest[n_small:n_small + 4]
        sum_refs = rest[n_small + 4:2 * n_small + 4]
        all_refs = rest[2 * n_small + 4:2 * n_small + 4 + n_all]
        scratch = rest[2 * n_small + 4 + n_all:]
        acc_i, acc_o, rcv_i, rcv_o = scratch[:4]
        all_bufs = scratch[4:4 + n_small]
        s_sem, r_sem, s_small, r_small_sem = scratch[4 + n_small:]
        t = pl.program_id(0)
        x, y, c = _mesh_pos()
        chip = 2 * x + y
        groups = [(p_refs[k], sum_refs[k], all_refs[k] if k < n_all else None, all_bufs[k]) for k in range(n_small)]
        small_start, small_finish = _small_allreduce_steps(groups, s_small, r_small_sem)
        accs, rcvs, q_refs, g_refs = (acc_i, acc_o), (rcv_i, rcv_o), (qi_ref, qo_ref), (gi_ref, go_ref)

        @pl.when(t == 0)
        def _():
            small_start()
            acc_i[...] = jnp.zeros(acc_i.shape, F32)
            acc_o[...] = jnp.zeros(acc_o.shape, F32)

        h1v, dmixv = h1_ref[...], dmix_ref[...]
        for j in range(N_CHIPS):
            acc_i[j] += _dot_tn(h1v, dz_ref[j])
            acc_o[j] += _dot_tn(cat_ref[:, j * ro:(j + 1) * ro], dmixv)

        @pl.when(t == nt - 1)
        def _():
            cps = []
            for a in range(2):
                theirs = pl.multiple_of((1 - c) * rh[a], rh[a])
                cp = pltpu.make_async_remote_copy(accs[a].at[:, pl.ds(theirs, rh[a]), :], rcvs[a],
                                                  s_sem.at[a], r_sem.at[a],
                                                  device_id=(x, y, 1 - c), device_id_type=MESH)
                cp.start()
                cps.append(cp)
            for a in range(2):
                cps[a].wait()
                mine = pl.multiple_of(c * rh[a], rh[a])
                for j in range(N_CHIPS):
                    q_refs[a][j] = (accs[a][j, pl.ds(mine, rh[a]), :] + rcvs[a][j]).astype(BF16)
                g_refs[a][chip] = q_refs[a][chip]
            small_finish()

    stacks = ((N_CHIPS, rh[0], cw), (N_CHIPS, rh[1], d))
    n_cp = n_small * (N_DEV - 1)
    return pl.pallas_call(
        body, name="mix_wgrad",
        out_shape=tuple(jax.ShapeDtypeStruct(s, BF16) for s in stacks) * 2
        + tuple(jax.ShapeDtypeStruct(a.shape, F32) for a in small)
        + tuple(jax.ShapeDtypeStruct((N_DEV,) + a.shape, F32) for a in small[:n_all]),
        grid=(nt,),
        in_specs=[pl.BlockSpec((tt, d), lambda t: (t, 0)), pl.BlockSpec((N_CHIPS, tt, cw), lambda t: (0, t, 0)),
                  pl.BlockSpec((tt, 2 * cw), lambda t: (t, 0)), pl.BlockSpec((tt, d), lambda t: (t, 0))]
        + [VMEM_SPEC] * n_small,
        out_specs=(VMEM_SPEC,) * (4 + n_small + n_all),
        scratch_shapes=[pltpu.VMEM((N_CHIPS, d, cw), F32), pltpu.VMEM((N_CHIPS, ro, d), F32),
                        pltpu.VMEM(stacks[0], F32), pltpu.VMEM(stacks[1], F32)]
        + [pltpu.VMEM((N_DEV,) + a.shape, F32) for a in small]
        + [_dma_sems(2), _dma_sems(2), _dma_sems(n_cp), _dma_sems(n_cp)],
        compiler_params=_params(dimension_semantics=("arbitrary",)),
    )(h1, dz4, cat, dmix, *small)


def kernel(x, c, w_ada, b_ada, w_in, conv_w, w_pool, pool_scale, w_out, ln1_g, ln1_b, w_mlp_in, w_mlp_out, ln2_g, ln2_b, loss_target, m_w_ada, m_b_ada, m_w_in, m_conv_w, m_w_pool, m_pool_scale, m_w_out, m_ln1_g, m_ln1_b, m_w_mlp_in, m_w_mlp_out, m_ln2_g, m_ln2_b, v_w_ada, v_b_ada, v_w_in, v_conv_w, v_w_pool, v_pool_scale, v_w_out, v_ln1_g, v_ln1_b, v_w_mlp_in, v_w_mlp_out, v_ln2_g, v_ln2_b):
    t_len, d = x.shape[1], x.shape[2]
    cw = w_in.shape[2]
    cq = conv_w.shape[2]
    dg = w_pool.shape[2]
    assert cw == N_GROUPS * dg and cq * N_CHIPS == cw and LANES % cq == 0
    tm_mix = min(MIX_TOKEN_TILE, t_len)
    tm_mlp = min(MLP_TOKEN_TILE, t_len)
    tt = min(WGRAD_TILE, t_len // 4)
    chip = 2 * lax.axis_index("x") + lax.axis_index("y")
    pos = jnp.stack([lax.axis_index("c"), chip]).astype(jnp.int32)

    x2, tgt = x[0], loss_target[0]
    big_w = [w_in[0], w_out[0], w_mlp_in[0], w_mlp_out[0]]
    big_m = [m_w_in[0], m_w_out[0], m_w_mlp_in[0], m_w_mlp_out[0]]
    big_v = [v_w_in[0], v_w_out[0], v_w_mlp_in[0], v_w_mlp_out[0]]
    names = ["w_in", "w_out", "w_mlp_in", "w_mlp_out"]

    mod, cond_all, conv_full, w_in4, w_out4, w_mi_own, w_mo_own = _setup_exchange(
        c, w_ada[0], b_ada, conv_w[0], big_w, 2)
    w_out_full = w_out4.reshape(2 * cw, d)
    w_pool_bf = w_pool[0].astype(BF16)

    w_mi4, w_mo4, x1, zs, mix, h1, cat = _fwd_mix(x2, mod, conv_full, w_in4, w_pool_bf, pool_scale, w_out_full,
                                                  ln1_g, ln1_b, [w_mi_own, w_mo_own], tm_mix)
    dx1, da4, r4, df, h2, st2 = _mlp_fwd_bwd(x1, tgt, mod, w_mi4, w_mo4, ln2_g, ln2_b, tm_mlp)
    mlp_qi, mlp_qo, mlp_gi, mlp_go = _mlp_wgrad(h2, da4, r4, df, pos, tt)
    mlp_started = _split_start([mlp_qi, mlp_qo, mlp_gi, mlp_go], _chunk_exchange_plan(2), 2 * (N_CHIPS - 1),
                               "exchange_mlp")
    grad_x, dz4, dmix, st1, stc, dw_pool = _bwd_mix(x2, mix, dx1, zs, mod, conv_full, w_in4, w_pool_bf, pool_scale,
                                                    w_out_full, ln1_g, mlp_started[-1], tm_mix)

    mix_qi, mix_qo, mix_gi, mix_go, t1, t2, tc, g_w_pool, all1, all2 = _mix_wgrad(
        h1, dz4, cat, dmix, [st1, st2, stc, dw_pool], 2, tt)
    loss = t2[5, 0]
    g_ln1_g, g_ln1_b, g_ln2_g, g_ln2_b = t1[0:1], t1[1:2], t2[0:1], t2[1:2]
    g_pool_scale = tc[3:4]
    g_conv = lax.dynamic_slice_in_dim(tc[0:3], chip * cq, cq, axis=1)
    g_b_ada = jnp.concatenate([t1[4:5], t1[3:4], t1[2:3], t2[4:5], t2[3:4], t2[2:3]], axis=1)
    dmod_all = jnp.concatenate([all1[:, 4], all1[:, 3], all1[:, 2], all2[:, 4], all2[:, 3], all2[:, 2]], axis=1)
    na = w_ada.shape[2]
    dmod_mine = lax.dynamic_slice_in_dim(dmod_all, chip * na, na, axis=1)

    exchange, n_cp = _chunk_exchange_plan(2), 2 * (N_CHIPS - 1)
    mlp_got = _split_wait(mlp_started, exchange, t1, "exchange_mlp")[2:]
    mix_started = _split_start([mix_qi, mix_qo, mix_gi, mix_go], exchange, n_cp, "exchange_mix")
    mlp_halves = [_sum_chips(g, pos, mix_started[-1], "rs_total_" + n) for g, n in zip(mlp_got, names[2:])]
    join = _sibling_join_plan([h.shape for h in mlp_halves])
    join_started = _split_start(mlp_halves, join, 2, "join_mlp")
    g_ada, d_ada, nm_ada, nv_ada = _adamw_ada(cond_all.T, dmod_mine, w_ada[0], m_w_ada[0], v_w_ada[0],
                                              join_started[-1])
    mlp_g = _split_wait(join_started, join, d_ada, "join_mlp")
    mlp_out = [_adamw_big(g, w, m, v, "adamw_" + n)
               for g, w, m, v, n in zip(mlp_g, big_w[2:], big_m[2:], big_v[2:], names[2:])]
    mix_got = _split_wait(mix_started, exchange, mlp_out[1][0], "exchange_mix")[2:]
    mix_halves = [_sum_chips(g, pos, d_ada, "rs_total_" + n) for g, n in zip(mix_got, names[:2])]
    mix_g = _sibling_join(mix_halves, "mix")
    mix_out = [_adamw_big(g, w, m, v, "adamw_" + n)
               for g, w, m, v, n in zip(mix_g, big_w[:2], big_m[:2], big_v[:2], names[:2])]
    big_g = list(mix_g) + list(mlp_g)
    big_out = mix_out + mlp_out

    small_g = [g_b_ada, g_conv, g_w_pool.reshape(-1, dg), g_pool_scale, g_ln1_g, g_ln1_b, g_ln2_g, g_ln2_b]
    small_w = [b_ada, conv_w[0], w_pool.reshape(-1, dg), pool_scale, ln1_g, ln1_b, ln2_g, ln2_b]
    small_m = [m_b_ada, m_conv_w[0], m_w_pool.reshape(-1, dg), m_pool_scale, m_ln1_g, m_ln1_b, m_ln2_g, m_ln2_b]
    small_v = [v_b_ada, v_conv_w[0], v_w_pool.reshape(-1, dg), v_pool_scale, v_ln1_g, v_ln1_b, v_ln2_g, v_ln2_b]
    sm = _adamw_small(small_g, small_w, small_m, small_v)
    ns = len(small_g)
    s_delta, s_m, s_v = sm[:ns], sm[ns:2 * ns], sm[2 * ns:]

    def assemble(ada, small, big):
        return [ada[None], small[0], big[0][None], small[1][None], small[2].reshape(w_pool.shape), small[3],
                big[1][None], small[4], small[5], big[2][None], big[3][None], small[6], small[7]]

    grads = assemble(g_ada, small_g, big_g)
    deltas = assemble(d_ada, s_delta, [o[0] for o in big_out])
    new_m = assemble(nm_ada, s_m, [o[1] for o in big_out])
    new_v = assemble(nv_ada, s_v, [o[2] for o in big_out])
    return (loss, grad_x[None], *grads, *deltas, *new_m, *new_v)
```

```python
import jax
import jax.numpy as jnp
from jax import lax
from jax.experimental import pallas as pl
from jax.experimental.pallas import tpu as pltpu

F32 = jnp.float32
BF16 = jnp.bfloat16
MESH = pl.DeviceIdType.MESH

LN_EPS = 1e-5
DEEPNORM_ALPHA = 2.0 ** 0.25
POOL_WINDOWS = (2, 4, 8, 16)
N_GROUPS = len(POOL_WINDOWS)
HALO = 16
N_CHIPS = 4
N_DEV = 8
LANES = 128
SUBLANES = 8
VMEM_LIMIT = 56 * 1024 * 1024
MIX_TOKEN_TILE = 512
MLP_TOKEN_TILE = 256
WGRAD_TILE = 1024

ADAM_LR = 0.001
ADAM_B1 = 0.9
ADAM_B2 = 0.999
ADAM_EPS = 1e-08
ADAM_WD = 0.01
ADAM_STEP = 10

VMEM_SPEC = pl.BlockSpec(memory_space=pltpu.VMEM)
HBM_SPEC = pl.BlockSpec(memory_space=pltpu.HBM)


def _dot(a, b):
    return jnp.dot(a, b, preferred_element_type=F32)


def _dot_nt(a, b):
    return lax.dot_general(a, b, (((1,), (1,)), ((), ())), preferred_element_type=F32)


def _dot_tn(a, b):
    return lax.dot_general(a, b, (((0,), (0,)), ((), ())), preferred_element_type=F32)


def _fold8(v):
    r, n = v.shape
    return jnp.sum(v.reshape(r // SUBLANES, SUBLANES, n), axis=0)


def _mesh_pos():
    return lax.axis_index("x"), lax.axis_index("y"), lax.axis_index("c")


def _flip(v, bit):
    return 1 - v if bit else v


def _params(**kw):
    return pltpu.CompilerParams(vmem_limit_bytes=VMEM_LIMIT, **kw)


def _dma_sems(n):
    return pltpu.SemaphoreType.DMA((n,))


def _stack_gather(stacks, rows, s_ici, r_ici, s_d2d, r_d2d):
    x, y, c = _mesh_pos()
    chip = 2 * x + y
    pairs = [(i, m) for i in range(len(stacks)) for m in range(1, N_CHIPS)]

    def blk(i, slot, which):
        rh = rows[i] // 2
        return stacks[i].at[slot].at[pl.ds(which * rh, rh), :]

    def other(m):
        return 2 * _flip(x, m & 2) + _flip(y, m & 1)

    def ici(i, m, slot, to):
        k = i * (N_CHIPS - 1) + m - 1
        return pltpu.make_async_remote_copy(blk(i, slot, c), blk(i, slot, c), s_ici.at[k], r_ici.at[k],
                                            device_id=to, device_id_type=MESH)

    def d2d(i, m, which, to):
        k = i * (N_CHIPS - 1) + m - 1
        return pltpu.make_async_remote_copy(blk(i, other(m), which), blk(i, other(m), which),
                                            s_d2d.at[k], r_d2d.at[k], device_id=to, device_id_type=MESH)

    def start():
        for i, m in pairs:
            ici(i, m, chip, (_flip(x, m & 2), _flip(y, m & 1), c)).start()

    def forward(which=None):
        for i, m in pairs:
            if which is None or i == which:
                ici(i, m, other(m), (x, y, c)).wait_recv()
                d2d(i, m, c, (x, y, 1 - c)).start()

    def finish():
        for i, m in pairs:
            d2d(i, m, 1 - c, (x, y, c)).wait_recv()
        for i, m in pairs:
            ici(i, m, chip, (x, y, c)).wait_send()
            d2d(i, m, c, (x, y, c)).wait_send()

    return start, forward, finish


def _chunk_exchange(qs, gots, s_sem, r_sem):
    x, y, c = _mesh_pos()
    chip = 2 * x + y
    pairs = [(i, m) for i in range(len(qs)) for m in range(1, N_CHIPS)]

    def other(m):
        return 2 * _flip(x, m & 2) + _flip(y, m & 1)

    def send(i, m, to):
        k = i * (N_CHIPS - 1) + m - 1
        return pltpu.make_async_remote_copy(qs[i].at[other(m)], gots[i].at[chip], s_sem.at[k], r_sem.at[k],
                                            device_id=to, device_id_type=MESH)

    def arrival(i, m):
        k = i * (N_CHIPS - 1) + m - 1
        return pltpu.make_async_remote_copy(qs[i].at[other(m)], gots[i].at[other(m)], s_sem.at[k], r_sem.at[k],
                                            device_id=(x, y, c), device_id_type=MESH)

    def start():
        for i, m in pairs:
            send(i, m, (_flip(x, m & 2), _flip(y, m & 1), c)).start()

    def finish():
        for i, m in pairs:
            arrival(i, m).wait_recv()
        for i, m in pairs:
            send(i, m, (x, y, c)).wait_send()

    return start, finish


def _mod_scratch(d, na, cq):
    return [pltpu.VMEM((SUBLANES, d), F32), pltpu.VMEM((N_DEV, SUBLANES, d), F32),
            pltpu.VMEM((N_CHIPS, SUBLANES, na), F32),
            pltpu.VMEM((SUBLANES, cq), F32), pltpu.VMEM((N_CHIPS, SUBLANES, cq), F32),
            _dma_sems(N_DEV - 1), _dma_sems(N_DEV - 1),
            _dma_sems(N_CHIPS - 1), _dma_sems(N_CHIPS - 1), _dma_sems(N_CHIPS - 1), _dma_sems(N_CHIPS - 1)]


def _mod_steps(d, na, cq):
    def steps(c_ref, w_ref, b_ref, cw_ref, mod_ref, cond_ref, conv_ref,
              csend, cbuf, mbuf, cvsend, cvbuf, s1, r1, s2, r2, s3, r3):
        x, y, c = _mesh_pos()
        me = 4 * x + 2 * y + c
        chip = 2 * x + y
        csend[...] = jnp.broadcast_to(c_ref[...], (SUBLANES, d))
        cbuf[me] = csend[...]
        first = []
        for m in range(1, N_DEV):
            peer = (_flip(x, m & 4), _flip(y, m & 2), _flip(c, m & 1))
            cp = pltpu.make_async_remote_copy(csend, cbuf.at[me], s1.at[m - 1], r1.at[m - 1],
                                              device_id=peer, device_id_type=MESH)
            cp.start()
            first.append(cp)
        cvsend[...] = jnp.zeros((SUBLANES, cq), F32)
        cvsend[0:3, :] = cw_ref[...]
        cvbuf[chip] = cvsend[...]
        for m in range(1, N_DEV):
            src = 4 * _flip(x, m & 4) + 2 * _flip(y, m & 2) + _flip(c, m & 1)
            pltpu.make_async_remote_copy(csend, cbuf.at[src], s1.at[m - 1], r1.at[m - 1],
                                         device_id=(x, y, c), device_id_type=MESH).wait_recv()
        rows = lax.broadcasted_iota(jnp.int32, (SUBLANES, d), 0)
        call = jnp.zeros((SUBLANES, d), F32)
        for b in range(N_DEV):
            call = jnp.where(rows == b, cbuf[b], call)
        cond = call * jax.nn.sigmoid(call)
        cond_ref[...] = cond
        part = jnp.dot(cond, w_ref[...], preferred_element_type=F32, precision=lax.Precision.HIGHEST)
        mbuf[chip] = part
        second = []
        for m in range(1, N_CHIPS):
            peer = (_flip(x, m & 2), _flip(y, m & 1), c)
            cp = pltpu.make_async_remote_copy(mbuf.at[chip], mbuf.at[chip], s2.at[m - 1], r2.at[m - 1],
                                              device_id=peer, device_id_type=MESH)
            cp.start()
            second.append(cp)
            cp = pltpu.make_async_remote_copy(cvsend, cvbuf.at[chip], s3.at[m - 1], r3.at[m - 1],
                                              device_id=peer, device_id_type=MESH)
            cp.start()
            second.append(cp)
        for m in range(1, N_CHIPS):
            src = 2 * _flip(x, m & 2) + _flip(y, m & 1)
            pltpu.make_async_remote_copy(mbuf.at[src], mbuf.at[src], s2.at[m - 1], r2.at[m - 1],
                                         device_id=(x, y, c), device_id_type=MESH).wait_recv()
            pltpu.make_async_remote_copy(cvsend, cvbuf.at[src], s3.at[m - 1], r3.at[m - 1],
                                         device_id=(x, y, c), device_id_type=MESH).wait_recv()
        for cp in first + second:
            cp.wait_send()
        rows_n = lax.broadcasted_iota(jnp.int32, (SUBLANES, na), 0)
        for k in range(N_CHIPS):
            mine = jnp.sum(jnp.where(rows_n == me, mbuf[k], 0.0), axis=0, keepdims=True)
            mod_ref[:, k * na:(k + 1) * na] = jnp.broadcast_to(mine + b_ref[:, k * na:(k + 1) * na], (SUBLANES, na))
            conv_ref[:, k * cq:(k + 1) * cq] = cvbuf[k]

    return steps


def _setup_exchange(c_row, w_ada_s, b_ada, conv_w_s, shards, n_now):
    n = len(shards)
    n_cp = n_now * (N_CHIPS - 1)
    d, na, cq = c_row.shape[1], w_ada_s.shape[1], conv_w_s.shape[1]
    mod_scratch = _mod_scratch(d, na, cq)
    mod_steps = _mod_steps(d, na, cq)

    def body(*refs):
        mod_in, ins = refs[:4], refs[4:4 + n]
        mod_out, outs = refs[4 + n:7 + n], refs[7 + n:7 + 2 * n]
        bufs = refs[7 + 2 * n:7 + 3 * n]
        mod_scr = refs[7 + 3 * n:7 + 3 * n + len(mod_scratch)]
        s_ici, r_ici, s_d2d, r_d2d, s_loc = refs[7 + 3 * n + len(mod_scratch):7 + 3 * n + len(mod_scratch) + 5]
        w_ada_buf, s_in = refs[7 + 3 * n + len(mod_scratch) + 5:7 + 3 * n + len(mod_scratch) + 7]
        late_bufs = refs[7 + 3 * n + len(mod_scratch) + 7:]
        x, y, _ = _mesh_pos()
        chip = 2 * x + y
        loads = [pltpu.make_async_copy(mod_in[1], w_ada_buf, s_in.at[0])]
        loads += [pltpu.make_async_copy(ins[i], late_bufs[i - n_now], s_in.at[1 + i - n_now]) for i in range(n_now, n)]
        for cp in loads:
            cp.start()

        def place(i, src):
            bufs[i][...] = src[...].astype(BF16)
            cp = pltpu.make_async_copy(bufs[i], outs[i].at[chip], s_loc.at[i])
            cp.start()
            return cp

        start, forward, finish = _stack_gather(outs[:n_now], [s.shape[0] for s in shards[:n_now]],
                                               s_ici, r_ici, s_d2d, r_d2d)
        for cp in [place(i, ins[i]) for i in range(n_now)]:
            cp.wait()
        start()
        later = []
        for i in range(n_now, n):
            loads[1 + i - n_now].wait()
            later.append(place(i, late_bufs[i - n_now]))
        loads[0].wait()
        mod_steps(mod_in[0], w_ada_buf, mod_in[2], mod_in[3], *mod_out, *mod_scr)
        forward()
        finish()
        for cp in later:
            cp.wait()

    any_spec = pl.BlockSpec(memory_space=pl.ANY)
    return pl.pallas_call(
        body, name="setup_exchange",
        out_shape=(jax.ShapeDtypeStruct((SUBLANES, N_CHIPS * na), F32), jax.ShapeDtypeStruct((SUBLANES, d), F32),
                   jax.ShapeDtypeStruct((SUBLANES, N_CHIPS * cq), F32))
        + tuple(jax.ShapeDtypeStruct((N_CHIPS,) + s.shape, BF16) for s in shards),
        in_specs=[VMEM_SPEC, any_spec, VMEM_SPEC, VMEM_SPEC] + [VMEM_SPEC] * n_now + [any_spec] * (n - n_now),
        out_specs=(VMEM_SPEC,) * 3 + (HBM_SPEC,) * n,
        scratch_shapes=[pltpu.VMEM(s.shape, BF16) for s in shards] + mod_scratch
        + [_dma_sems(n_cp), _dma_sems(n_cp), _dma_sems(n_cp), _dma_sems(n_cp), _dma_sems(n)]
        + [pltpu.VMEM(w_ada_s.shape, F32), _dma_sems(1 + n - n_now)]
        + [pltpu.VMEM(s.shape, F32) for s in shards[n_now:]],
        compiler_params=_params(),
    )(c_row, w_ada_s, b_ada, conv_w_s, *shards)


def _sibling_join_plan(shapes):
    def plan(refs, s_sem, r_sem):
        x, y, c = _mesh_pos()

        def half(i, which):
            rh = shapes[i][0] // 2
            return refs[i].at[pl.ds(which * rh, rh), :]

        def start():
            for i in range(len(refs)):
                pltpu.make_async_remote_copy(half(i, c), half(i, c), s_sem.at[i], r_sem.at[i],
                                             device_id=(x, y, 1 - c), device_id_type=MESH).start()

        def finish():
            for i in range(len(refs)):
                pltpu.make_async_remote_copy(half(i, 1 - c), half(i, 1 - c), s_sem.at[i], r_sem.at[i],
                                             device_id=(x, y, c), device_id_type=MESH).wait_recv()
            for i in range(len(refs)):
                pltpu.make_async_remote_copy(half(i, c), half(i, c), s_sem.at[i], r_sem.at[i],
                                             device_id=(x, y, c), device_id_type=MESH).wait_send()

        return start, finish

    return plan


def _chunk_exchange_plan(n):
    return lambda refs, s_sem, r_sem: _chunk_exchange(refs[:n], refs[n:], s_sem, r_sem)


def _sibling_join(gs, tag):
    n = len(gs)
    plan = _sibling_join_plan([g.shape for g in gs])

    def body(*refs):
        start, finish = plan(refs[n:2 * n], refs[2 * n], refs[2 * n + 1])
        start()
        finish()

    return pl.pallas_call(
        body, name="rs_sibling_join_" + tag,
        out_shape=tuple(jax.ShapeDtypeStruct(g.shape, g.dtype) for g in gs),
        in_specs=[HBM_SPEC] * n, out_specs=(HBM_SPEC,) * n,
        input_output_aliases={i: i for i in range(n)},
        scratch_shapes=[_dma_sems(n), _dma_sems(n)],
        compiler_params=_params(),
    )(*gs)


SEM_SPEC = pl.BlockSpec(memory_space=pltpu.SEMAPHORE)
SIDE_EFFECT = pltpu.SideEffectType.DATAFLOW_SIDE_EFFECTING


def _split_start(arrays, plan, n_cp, tag):
    n = len(arrays)

    def body(*refs):
        start, _ = plan(refs[n + 2:2 * n + 2], refs[n], refs[n + 1])
        start()
        token = refs[2 * n + 2]
        token[...] = jnp.zeros_like(token)

    through = [pltpu.with_memory_space_constraint(a, pltpu.HBM) for a in arrays]
    return pl.pallas_call(
        body, name=tag + "_start",
        out_shape=(pltpu.SemaphoreType.DMA((n_cp,)), pltpu.SemaphoreType.DMA((n_cp,)))
        + tuple(pltpu.HBM(a.shape, a.dtype) for a in through) + (jax.ShapeDtypeStruct((SUBLANES, LANES), F32),),
        in_specs=[HBM_SPEC] * n, out_specs=(SEM_SPEC, SEM_SPEC) + (HBM_SPEC,) * n + (VMEM_SPEC,),
        input_output_aliases={k: 2 + k for k in range(n)},
        compiler_params=pltpu.CompilerParams(has_side_effects=SIDE_EFFECT),
    )(*through)


def _split_wait(started, plan, after, tag):
    send_sems, recv_sems, arrays = started[0], started[1], started[2:-1]
    n = len(arrays)

    def body(*refs):
        _, finish = plan(refs[:n], refs[n], refs[n + 1])
        finish()

    return pl.pallas_call(
        body, name=tag + "_wait",
        out_shape=tuple(pltpu.HBM(a.shape, a.dtype) for a in arrays),
        in_specs=[HBM_SPEC] * n + [SEM_SPEC, SEM_SPEC, pl.BlockSpec(memory_space=pl.ANY)],
        out_specs=(HBM_SPEC,) * n,
        input_output_aliases={k: k for k in range(n)},
        compiler_params=pltpu.CompilerParams(has_side_effects=SIDE_EFFECT),
    )(*arrays, send_sems, recv_sems, after)


def _small_allreduce_steps(groups, s_sem, r_sem):
    x, y, c = _mesh_pos()
    me = 4 * x + 2 * y + c
    pairs = [(k, m) for k in range(len(groups)) for m in range(1, N_DEV)]

    def send(k, m, to):
        p_ref, all_ref = groups[k][0], groups[k][3]
        i = k * (N_DEV - 1) + m - 1
        return pltpu.make_async_remote_copy(p_ref, all_ref.at[me], s_sem.at[i], r_sem.at[i],
                                            device_id=to, device_id_type=MESH)

    def start():
        for p_ref, _, _, all_ref in groups:
            all_ref[me] = p_ref[...]
        for k, m in pairs:
            send(k, m, (_flip(x, m & 4), _flip(y, m & 2), _flip(c, m & 1))).start()

    def finish():
        for k, m in pairs:
            p_ref, all_ref = groups[k][0], groups[k][3]
            i = k * (N_DEV - 1) + m - 1
            src = 4 * _flip(x, m & 4) + 2 * _flip(y, m & 2) + _flip(c, m & 1)
            pltpu.make_async_remote_copy(p_ref, all_ref.at[src], s_sem.at[i], r_sem.at[i],
                                         device_id=(x, y, c), device_id_type=MESH).wait_recv()
        for k, m in pairs:
            send(k, m, (x, y, c)).wait_send()
        for _, sum_ref, all_out_ref, all_ref in groups:
            sum_ref[...] = (((all_ref[0] + all_ref[1]) + (all_ref[2] + all_ref[3]))
                            + ((all_ref[4] + all_ref[5]) + (all_ref[6] + all_ref[7])))
            if all_out_ref is not None:
                all_out_ref[...] = all_ref[...]

    return start, finish


def _row_block(rows, cap=256):
    rb = min(rows, cap)
    assert rows % rb == 0
    return rb


def _sum_chips(got, pos, after, name):
    _, rh, cc = got.shape
    rb = _row_block(rh)
    nb = rh // rb

    def body(pos_ref, a_ref, after_ref, o_ref):
        a = a_ref[...].astype(F32)
        o_ref[...] = (a[0] + a[1]) + (a[2] + a[3])

    return pl.pallas_call(
        body, name=name,
        out_shape=jax.ShapeDtypeStruct((2 * rh, cc), F32),
        grid_spec=pltpu.PrefetchScalarGridSpec(
            num_scalar_prefetch=1, grid=(nb,),
            in_specs=[pl.BlockSpec((N_CHIPS, rb, cc), lambda i, p: (0, i, 0)),
                      pl.BlockSpec(memory_space=pl.ANY)],
            out_specs=pl.BlockSpec((rb, cc), lambda i, p: (p[0] * nb + i, 0))),
        compiler_params=_params(dimension_semantics=("arbitrary",)),
    )(pos, got, after)


def _adamw_math(w, g, m, v):
    m = ADAM_B1 * m + (1.0 - ADAM_B1) * g
    v = ADAM_B2 * v + (1.0 - ADAM_B2) * jnp.square(g)
    m_hat = m / (1.0 - ADAM_B1 ** ADAM_STEP)
    v_hat = v / (1.0 - ADAM_B2 ** ADAM_STEP)
    delta = -ADAM_LR * (m_hat / (jnp.sqrt(v_hat) + ADAM_EPS) + ADAM_WD * w)
    return delta, m, v


def _adamw_big(g, w, m, v, name):
    r, cc = w.shape
    rb = _row_block(r)

    def body(g_ref, w_ref, m_ref, v_ref, d_ref, mo_ref, vo_ref):
        d, mn, vn = _adamw_math(w_ref[...], g_ref[...], m_ref[...], v_ref[...])
        d_ref[...] = d
        mo_ref[...] = mn
        vo_ref[...] = vn

    spec = pl.BlockSpec((rb, cc), lambda i: (i, 0))
    return pl.pallas_call(
        body, name=name,
        out_shape=(jax.ShapeDtypeStruct((r, cc), F32),) * 3,
        grid=(r // rb,), in_specs=[spec] * 4, out_specs=(spec,) * 3,
        compiler_params=_params(dimension_semantics=("arbitrary",)),
    )(g, w, m, v)


def _adamw_ada(cond_t, dmod, w, m, v, after):
    d, na = w.shape
    rb = _row_block(d, 128)

    def body(ct_ref, dm_ref, w_ref, m_ref, v_ref, after_ref, g_ref, d_ref, mo_ref, vo_ref):
        g = jnp.dot(ct_ref[...], dm_ref[...], preferred_element_type=F32, precision=lax.Precision.HIGHEST)
        dl, mn, vn = _adamw_math(w_ref[...], g, m_ref[...], v_ref[...])
        g_ref[...] = g
        d_ref[...] = dl
        mo_ref[...] = mn
        vo_ref[...] = vn

    spec = pl.BlockSpec((rb, na), lambda i: (i, 0))
    return pl.pallas_call(
        body, name="adamw_w_ada",
        out_shape=(jax.ShapeDtypeStruct((d, na), F32),) * 4,
        grid=(d // rb,),
        in_specs=[pl.BlockSpec((rb, N_DEV), lambda i: (i, 0)), pl.BlockSpec((N_DEV, na), lambda i: (0, 0)),
                  spec, spec, spec, pl.BlockSpec(memory_space=pl.ANY)],
        out_specs=(spec,) * 4,
        compiler_params=_params(dimension_semantics=("arbitrary",)),
    )(cond_t, dmod, w, m, v, after)


def _adamw_small(gs, ws, ms, vs):
    n = len(gs)

    def body(*refs):
        ins, outs = refs[:4 * n], refs[4 * n:]
        for i in range(n):
            d, mn, vn = _adamw_math(ins[n + i][...], ins[i][...], ins[2 * n + i][...], ins[3 * n + i][...])
            outs[i][...] = d
            outs[n + i][...] = mn
            outs[2 * n + i][...] = vn

    shapes = tuple(jax.ShapeDtypeStruct(w.shape, F32) for w in ws)
    return pl.pallas_call(
        body, name="adamw_small",
        out_shape=shapes * 3,
        in_specs=[VMEM_SPEC] * (4 * n), out_specs=(VMEM_SPEC,) * (3 * n),
        compiler_params=_params(),
    )(*gs, *ws, *ms, *vs)


def _layer_norm_fwd(r):
    mu = jnp.mean(r, axis=-1, keepdims=True)
    xc = r - mu
    var = jnp.mean(jnp.square(xc), axis=-1, keepdims=True)
    rstd = lax.rsqrt(var + LN_EPS)
    return xc * rstd, rstd


def _layer_norm_bwd(dy, xhat, rstd, gain):
    dxh = dy * gain
    m1 = jnp.mean(dxh, axis=-1, keepdims=True)
    m2 = jnp.mean(dxh * xhat, axis=-1, keepdims=True)
    return rstd * (dxh - m1 - xhat * m2)


def _inv_count(tile, tm, win):
    t = (tile * tm + lax.broadcasted_iota(jnp.int32, (tm, 1), 0) + 1).astype(F32)
    return 1.0 / jnp.minimum(t, float(win))


def _fwd_mix(x, mod, conv_w, w_in4, w_pool, pool_scale, w_out, ln_g, ln_b, late_stacks, tm):
    t_len, d = x.shape
    cw = w_in4.shape[2]
    dg = cw // N_GROUPS
    nt = t_len // tm
    n_late = len(late_stacks)
    n_cp = n_late * (N_CHIPS - 1)
    pass_steps = [min(((k + 1) * nt) // n_late + (1 if k + 1 < n_late else -2), nt - 1) if nt > 2 else nt - 1
                  for k in range(n_late)]
    pass_steps = [max(p, 0) for p in pass_steps]

    def body(x_ref, mod_ref, cw_ref, win_ref, wp_ref, ps_ref, wout_ref, g_ref, b_ref, *rest):
        late = rest[n_late:2 * n_late]
        x1_ref, zs_ref, mix_ref, h1_ref, cat_ref = rest[2 * n_late:2 * n_late + 5]
        ubuf, vbuf, s_ici, r_ici, s_d2d, r_d2d = rest[2 * n_late + 5:]
        i = pl.program_id(0)
        start, forward, finish = _stack_gather(late, [s.shape[1] for s in late_stacks], s_ici, r_ici, s_d2d, r_d2d)

        @pl.when(i == 0)
        def _():
            start()
            ubuf[0:HALO, :] = jnp.zeros((HALO, cw), F32)
            vbuf[0:HALO, :] = jnp.zeros((HALO, cw), F32)

        xv = x_ref[...]
        sh1, sc1, g1 = mod_ref[0:1, 0:d], mod_ref[0:1, d:2 * d], mod_ref[0:1, 2 * d:3 * d]
        h1 = (xv * (1.0 + sc1) + sh1).astype(BF16)
        h1_ref[...] = h1
        zb = _dot(h1, win_ref[0])
        zc = _dot(h1, win_ref[1])
        zv = _dot(h1, win_ref[2])
        zp = _dot(h1, win_ref[3])
        u = zc * zv
        ubuf[HALO:HALO + tm, :] = u
        u1 = ubuf[pl.ds(HALO - 1, tm), :]
        u2 = ubuf[pl.ds(HALO - 2, tm), :]
        conv = cw_ref[0:1, :] * u2 + cw_ref[1:2, :] * u1 + cw_ref[2:3, :] * u
        ubuf[0:HALO, :] = ubuf[tm:tm + HALO, :]
        yc = zb * conv
        vbuf[HALO:HALO + tm, :] = zp
        ps, pms = [], []
        for gi, win in enumerate(POOL_WINDOWS):
            sl = slice(gi * dg, (gi + 1) * dg)
            acc = zp[:, sl]
            for s in range(1, win):
                acc = acc + vbuf[pl.ds(HALO - s, tm), sl]
            p_g = acc * _inv_count(i, tm, win) - zp[:, sl]
            ps.append(p_g)
            pms.append(_dot(p_g.astype(BF16), wp_ref[gi]))
        vbuf[0:HALO, :] = vbuf[tm:tm + HALO, :]
        pooled = jnp.concatenate(ps, axis=1)
        yp = jnp.concatenate(pms, axis=1) * ps_ref[...]
        cat = jnp.concatenate([yc, yp], axis=1).astype(BF16)
        cat_ref[...] = cat
        mix = _dot(cat, wout_ref[...])
        mix_ref[...] = mix
        xhat, _ = _layer_norm_fwd(DEEPNORM_ALPHA * xv + (1.0 + g1) * mix)
        x1_ref[...] = xhat * g_ref[...] + b_ref[...]
        zs_ref[:, 0 * cw:1 * cw] = zb.astype(BF16)
        zs_ref[:, 1 * cw:2 * cw] = zc.astype(BF16)
        zs_ref[:, 2 * cw:3 * cw] = zv.astype(BF16)
        zs_ref[:, 3 * cw:4 * cw] = conv.astype(BF16)
        zs_ref[:, 4 * cw:5 * cw] = pooled.astype(BF16)

        for k in range(n_late):
            @pl.when(i == pass_steps[k])
            def _():
                forward(k)

        @pl.when(i == nt - 1)
        def _():
            finish()

    tile = lambda w: pl.BlockSpec((tm, w), lambda i: (i, 0))
    n_in = 9
    return pl.pallas_call(
        body, name="fwd_mix",
        out_shape=tuple(jax.ShapeDtypeStruct(s.shape, s.dtype) for s in late_stacks)
        + (jax.ShapeDtypeStruct((t_len, d), F32), jax.ShapeDtypeStruct((t_len, 5 * cw), BF16),
           jax.ShapeDtypeStruct((t_len, d), F32), jax.ShapeDtypeStruct((t_len, d), BF16),
           jax.ShapeDtypeStruct((t_len, 2 * cw), BF16)),
        grid=(nt,),
        in_specs=[tile(d)] + [VMEM_SPEC] * 8 + [HBM_SPEC] * n_late,
        out_specs=(HBM_SPEC,) * n_late + (tile(d), tile(5 * cw), tile(d), tile(d), tile(2 * cw)),
        input_output_aliases={n_in + k: k for k in range(n_late)},
        scratch_shapes=[pltpu.VMEM((tm + HALO, cw), F32), pltpu.VMEM((tm + HALO, cw), F32),
                        _dma_sems(n_cp), _dma_sems(n_cp), _dma_sems(n_cp), _dma_sems(n_cp)],
        compiler_params=_params(dimension_semantics=("arbitrary",)),
    )(x, mod, conv_w, w_in4, w_pool, pool_scale, w_out, ln_g, ln_b, *late_stacks)


def _mlp_fwd_bwd(x1, tgt, mod, w_mi4, w_mo4, ln_g, ln_b, tm):
    t_len, d = x1.shape
    fq = w_mi4.shape[2]
    nt = t_len // tm

    def body(x1_ref, tgt_ref, mod_ref, wi_ref, wo_ref, g_ref, b_ref,
             dx1_ref, da_ref, s_ref, df_ref, h2_ref, st_ref, relu_buf, acc):
        i = pl.program_id(0)

        @pl.when(i == 0)
        def _():
            acc[...] = jnp.zeros(acc.shape, F32)

        x1v = x1_ref[...]
        sh2, sc2, g2 = mod_ref[0:1, 3 * d:4 * d], mod_ref[0:1, 4 * d:5 * d], mod_ref[0:1, 5 * d:6 * d]
        h2 = (x1v * (1.0 + sc2) + sh2).astype(BF16)
        h2_ref[...] = h2
        f = jnp.zeros((tm, d), F32)
        for j in range(N_CHIPS):
            a = jnp.maximum(_dot(h2, wi_ref[j]), 0.0)
            relu_buf[j] = a
            s = (a * a).astype(BF16)
            s_ref[j] = s
            f = f + _dot(s, wo_ref[j])
        xhat, rstd = _layer_norm_fwd(DEEPNORM_ALPHA * x1v + (1.0 + g2) * f)
        gain = g_ref[...]
        err = xhat * gain + b_ref[...] - tgt_ref[...]
        dy = err * (1.0 / d)
        dr2 = _layer_norm_bwd(dy, xhat, rstd, gain)
        df = ((1.0 + g2) * dr2).astype(BF16)
        df_ref[...] = df
        dh2 = jnp.zeros((tm, d), F32)
        for j in range(N_CHIPS):
            ds = _dot_nt(df, wo_ref[j])
            da = (ds * (2.0 * relu_buf[j])).astype(BF16)
            da_ref[j] = da
            dh2 = dh2 + _dot_nt(da, wi_ref[j])
        dx1_ref[...] = DEEPNORM_ALPHA * dr2 + dh2 * (1.0 + sc2)
        acc[0] += _fold8(dy * xhat)
        acc[1] += _fold8(dy)
        acc[2] += _fold8(dr2 * f)
        acc[3] += _fold8(dh2 * x1v)
        acc[4] += _fold8(dh2)
        acc[5] += _fold8(err * err)

        @pl.when(i == nt - 1)
        def _():
            for k in range(5):
                st_ref[k:k + 1, :] = jnp.sum(acc[k], axis=0, keepdims=True)
            loss = jnp.sum(acc[5]) * (0.5 / d)
            st_ref[5:6, :] = jnp.broadcast_to(loss, (1, d))
            st_ref[6:8, :] = jnp.zeros((2, d), F32)

    tile = lambda w: pl.BlockSpec((tm, w), lambda i: (i, 0))
    tile4 = pl.BlockSpec((N_CHIPS, tm, fq), lambda i: (0, i, 0))
    return pl.pallas_call(
        body, name="mlp_fwd_bwd",
        out_shape=(jax.ShapeDtypeStruct((t_len, d), F32),
                   jax.ShapeDtypeStruct((N_CHIPS, t_len, fq), BF16), jax.ShapeDtypeStruct((N_CHIPS, t_len, fq), BF16),
                   jax.ShapeDtypeStruct((t_len, d), BF16), jax.ShapeDtypeStruct((t_len, d), BF16),
                   jax.ShapeDtypeStruct((SUBLANES, d), F32)),
        grid=(nt,),
        in_specs=[tile(d), tile(d)] + [VMEM_SPEC] * 5,
        out_specs=(tile(d), tile4, tile4, tile(d), tile(d), pl.BlockSpec((SUBLANES, d), lambda i: (0, 0))),
        scratch_shapes=[pltpu.VMEM((N_CHIPS, tm, fq), F32), pltpu.VMEM((6, SUBLANES, d), F32)],
        compiler_params=_params(dimension_semantics=("arbitrary",)),
    )(x1, tgt, mod, w_mi4, w_mo4, ln_g, ln_b)


def _mlp_wgrad(h2, da4, s4, df, pos, tt):
    t_len, d = h2.shape
    fq = da4.shape[2]
    nt = t_len // tt
    assert nt >= 3
    rh = (d // 2, fq // 2)
    last = N_CHIPS - 1

    def body(pos_ref, h2_ref, da_ref, s_ref, df_ref, qi_ref, qo_ref, gi_ref, go_ref,
             acc_i, acc_o, rcv_i, rcv_o, s_sem, r_sem):
        j, t = pl.program_id(0), pl.program_id(1)
        slot = lax.rem(j, 2)
        x, y, c = _mesh_pos()
        accs, rcvs, q_refs, g_refs = (acc_i, acc_o), (rcv_i, rcv_o), (qi_ref, qo_ref), (gi_ref, go_ref)

        def to_sibling(a, sl):
            theirs = pl.multiple_of((1 - c) * rh[a], rh[a])
            return pltpu.make_async_remote_copy(accs[a].at[sl].at[pl.ds(theirs, rh[a]), :], rcvs[a].at[sl],
                                                s_sem.at[2 * sl + a], r_sem.at[2 * sl + a],
                                                device_id=(x, y, 1 - c), device_id_type=MESH)

        def emit(chunk, sl):
            for a in range(2):
                to_sibling(a, sl).wait()
                mine = pl.multiple_of(c * rh[a], rh[a])
                q = (accs[a][sl, pl.ds(mine, rh[a]), :] + rcvs[a][sl]).astype(BF16)
                q_refs[a][...] = q

                @pl.when(chunk == pos_ref[1])
                def _():
                    g_refs[a][...] = q

        @pl.when(t == 0)
        def _():
            acc_i[slot] = jnp.zeros((d, fq), F32)
            acc_o[slot] = jnp.zeros((fq, d), F32)

        acc_i[slot] += _dot_tn(h2_ref[...], da_ref[...])
        acc_o[slot] += _dot_tn(s_ref[...], df_ref[...])

        @pl.when((t == 1) & (j >= 1))
        def _():
            emit(j - 1, 1 - slot)

        @pl.when(t == nt - 1)
        def _():
            for a in range(2):
                to_sibling(a, slot).start()

        @pl.when((t == nt - 1) & (j == last))
        def _():
            emit(j, slot)

    q_index = lambda j, t, p: (jnp.where(t == nt - 1, j, jnp.maximum(j - 1, 0)), 0, 0)
    got_index = lambda j, t, p: (p[1], 0, 0)
    blocks = ((None, rh[0], fq), (None, rh[1], d))
    return pl.pallas_call(
        body, name="mlp_wgrad",
        out_shape=tuple(jax.ShapeDtypeStruct((N_CHIPS,) + b[1:], BF16) for b in blocks) * 2,
        grid_spec=pltpu.PrefetchScalarGridSpec(
            num_scalar_prefetch=1, grid=(N_CHIPS, nt),
            in_specs=[pl.BlockSpec((tt, d), lambda j, t, p: (t, 0)),
                      pl.BlockSpec((None, tt, fq), lambda j, t, p: (j, t, 0)),
                      pl.BlockSpec((None, tt, fq), lambda j, t, p: (j, t, 0)),
                      pl.BlockSpec((tt, d), lambda j, t, p: (t, 0))],
            out_specs=tuple(pl.BlockSpec(b, q_index) for b in blocks)
            + tuple(pl.BlockSpec(b, got_index) for b in blocks),
            scratch_shapes=[pltpu.VMEM((2, d, fq), F32), pltpu.VMEM((2, fq, d), F32),
                            pltpu.VMEM((2, rh[0], fq), F32), pltpu.VMEM((2, rh[1], d), F32),
                            _dma_sems(4), _dma_sems(4)]),
        compiler_params=_params(dimension_semantics=("arbitrary", "arbitrary")),
    )(pos, h2, da4, s4, df)


def _bwd_mix(x, mix, dx1, zs, mod, conv_w, w_in4, w_pool, pool_scale, w_out, ln_g, after, tm):
    t_len, d = x.shape
    cw = w_in4.shape[2]
    dg = cw // N_GROUPS
    nt = t_len // tm

    def body(x_ref, mix_ref, dx1_ref, zs_ref, mod_ref, cw_ref, win_ref, wp_ref, ps_ref, wout_ref, g_ref, after_ref,
             gx_ref, dz_ref, dmix_ref, sd_ref, sc_ref, dwp_ref, dbuf, qbuf, acc_d, acc_c, acc_p):
        i = pl.program_id(0)
        tile_idx = nt - 1 - i

        @pl.when(i == 0)
        def _():
            dbuf[tm:tm + HALO, :] = jnp.zeros((HALO, cw), F32)
            qbuf[tm:tm + HALO, :] = jnp.zeros((HALO, cw), F32)
            acc_d[...] = jnp.zeros(acc_d.shape, F32)
            acc_c[...] = jnp.zeros(acc_c.shape, F32)
            acc_p[...] = jnp.zeros(acc_p.shape, F32)

        xv, mixv, dx1v = x_ref[...], mix_ref[...], dx1_ref[...]
        sc1, g1 = mod_ref[0:1, d:2 * d], mod_ref[0:1, 2 * d:3 * d]
        xhat, rstd = _layer_norm_fwd(DEEPNORM_ALPHA * xv + (1.0 + g1) * mixv)
        dr1 = _layer_norm_bwd(dx1v, xhat, rstd, g_ref[...])
        dmix = ((1.0 + g1) * dr1).astype(BF16)
        dmix_ref[...] = dmix
        dcat = _dot_nt(dmix, wout_ref[...])
        dyc, dyp = dcat[:, 0:cw], dcat[:, cw:2 * cw]
        zb, zc, zv = (zs_ref[:, k * cw:(k + 1) * cw].astype(F32) for k in range(3))
        conv, pooled = zs_ref[:, 3 * cw:4 * cw].astype(F32), zs_ref[:, 4 * cw:5 * cw]
        dzb = dyc * conv
        dcv = dyc * zb
        dbuf[0:tm, :] = dcv
        d1 = dbuf[pl.ds(1, tm), :]
        d2 = dbuf[pl.ds(2, tm), :]
        du = cw_ref[2:3, :] * dcv + cw_ref[1:2, :] * d1 + cw_ref[0:1, :] * d2
        dbuf[tm:tm + HALO, :] = dbuf[0:HALO, :]
        u = zc * zv
        acc_c[0] += _fold8(u * d2)
        acc_c[1] += _fold8(u * d1)
        acc_c[2] += _fold8(u * dcv)
        dzc = du * zv
        dzv = du * zc
        dpw = dyp * ps_ref[...]
        dps, pms = [], []
        for gi, win in enumerate(POOL_WINDOWS):
            sl = slice(gi * dg, (gi + 1) * dg)
            p_g = pooled[:, sl].astype(BF16)
            dpw_g = dpw[:, sl].astype(BF16)
            pms.append(_dot(p_g, wp_ref[gi]))
            acc_p[gi] += _dot_tn(p_g, dpw_g)
            dp_g = _dot_nt(dpw_g, wp_ref[gi])
            dps.append(dp_g)
            qbuf[0:tm, sl] = dp_g * _inv_count(tile_idx, tm, win)
        acc_c[3] += _fold8(dyp * jnp.concatenate(pms, axis=1))
        dzps = []
        for gi, win in enumerate(POOL_WINDOWS):
            sl = slice(gi * dg, (gi + 1) * dg)
            acc = qbuf[0:tm, sl]
            for s in range(1, win):
                acc = acc + qbuf[pl.ds(s, tm), sl]
            dzps.append(acc - dps[gi])
        qbuf[tm:tm + HALO, :] = qbuf[0:HALO, :]
        dz = [dzb.astype(BF16), dzc.astype(BF16), dzv.astype(BF16), jnp.concatenate(dzps, axis=1).astype(BF16)]
        dh1 = jnp.zeros((tm, d), F32)
        for j in range(N_CHIPS):
            dz_ref[j] = dz[j]
            dh1 = dh1 + _dot_nt(dz[j], win_ref[j])
        gx_ref[...] = DEEPNORM_ALPHA * dr1 + dh1 * (1.0 + sc1)
        acc_d[0] += _fold8(dx1v * xhat)
        acc_d[1] += _fold8(dx1v)
        acc_d[2] += _fold8(dr1 * mixv)
        acc_d[3] += _fold8(dh1 * xv)
        acc_d[4] += _fold8(dh1)

        @pl.when(i == nt - 1)
        def _():
            for k in range(5):
                sd_ref[k:k + 1, :] = jnp.sum(acc_d[k], axis=0, keepdims=True)
            sd_ref[5:8, :] = jnp.zeros((3, d), F32)
            for k in range(4):
                sc_ref[k:k + 1, :] = jnp.sum(acc_c[k], axis=0, keepdims=True)
            sc_ref[4:8, :] = jnp.zeros((4, cw), F32)
            dwp_ref[...] = acc_p[...]

    rtile = lambda w: pl.BlockSpec((tm, w), lambda i: (nt - 1 - i, 0))
    whole = lambda shape: pl.BlockSpec(shape, lambda i: tuple(0 for _ in shape))
    return pl.pallas_call(
        body, name="bwd_mix",
        out_shape=(jax.ShapeDtypeStruct((t_len, d), F32), jax.ShapeDtypeStruct((N_CHIPS, t_len, cw), BF16),
                   jax.ShapeDtypeStruct((t_len, d), BF16), jax.ShapeDtypeStruct((SUBLANES, d), F32),
                   jax.ShapeDtypeStruct((SUBLANES, cw), F32), jax.ShapeDtypeStruct((N_GROUPS, dg, dg), F32)),
        grid=(nt,),
        in_specs=[rtile(d), rtile(d), rtile(d), rtile(5 * cw)] + [VMEM_SPEC] * 7
        + [pl.BlockSpec(memory_space=pl.ANY)],
        out_specs=(rtile(d), pl.BlockSpec((N_CHIPS, tm, cw), lambda i: (0, nt - 1 - i, 0)), rtile(d),
                   whole((SUBLANES, d)), whole((SUBLANES, cw)), whole((N_GROUPS, dg, dg))),
        scratch_shapes=[pltpu.VMEM((tm + HALO, cw), F32), pltpu.VMEM((tm + HALO, cw), F32),
                        pltpu.VMEM((5, SUBLANES, d), F32), pltpu.VMEM((4, SUBLANES, cw), F32),
                        pltpu.VMEM((N_GROUPS, dg, dg), F32)],
        compiler_params=_params(dimension_semantics=("arbitrary",)),
    )(x, mix, dx1, zs, mod, conv_w, w_in4, w_pool, pool_scale, w_out, ln_g, after)


def _mix_wgrad(h1, dz4, cat, dmix, small, n_all, tt):
    t_len, d = h1.shape
    cw = dz4.shape[2]
    nt = t_len // tt
    ro = (2 * cw) // N_CHIPS
    rh = (d // 2, ro // 2)
    n_small = len(small)

    def body(h1_ref, dz_ref, cat_ref, dmix_ref, *rest):
        p_refs = rest[:n_small]
        qi_ref, qo_ref, gi_ref, go_ref = rest[n_small:n_small + 4]
        sum_refs = rest[n_small + 4:2 * n_small + 4]
        all_refs = rest[2 * n_small + 4:2 * n_small + 4 + n_all]
        scratch = rest[2 * n_small + 4 + n_all:]
        acc_i, acc_o, rcv_i, rcv_o = scratch[:4]
        all_bufs = scratch[4:4 + n_small]
        s_sem, r_sem, s_small, r_small_sem = scratch[4 + n_small:]
        t = pl.program_id(0)
        x, y, c = _mesh_pos()
        chip = 2 * x + y
        groups = [(p_refs[k], sum_refs[k], all_refs[k] if k < n_all else None, all_bufs[k]) for k in range(n_small)]
        small_start, small_finish = _small_allreduce_steps(groups, s_small, r_small_sem)
        accs, rcvs, q_refs, g_refs = (acc_i, acc_o), (rcv_i, rcv_o), (qi_ref, qo_ref), (gi_ref, go_ref)

        @pl.when(t == 0)
        def _():
            small_start()
            acc_i[...] = jnp.zeros(acc_i.shape, F32)
            acc_o[...] = jnp.zeros(acc_o.shape, F32)

        def accumulate(j):
            acc_i[j] += _dot_tn(h1_ref[...], dz_ref[j])
            acc_o[j] += _dot_tn(cat_ref[:, j * ro:(j + 1) * ro], dmix_ref[...])

        def to_sibling(a, j):
            theirs = pl.multiple_of((1 - c) * rh[a], rh[a])
            return pltpu.make_async_remote_copy(accs[a].at[j].at[pl.ds(theirs, rh[a]), :], rcvs[a].at[j],
                                                s_sem.at[a * N_CHIPS + j], r_sem.at[a * N_CHIPS + j],
                                                device_id=(x, y, 1 - c), device_id_type=MESH)

        @pl.when(t < nt - 1)
        def _():
            for j in range(N_CHIPS):
                accumulate(j)

        @pl.when(t == nt - 1)
        def _():
            for j in range(N_CHIPS):
                accumulate(j)
                for a in range(2):
                    to_sibling(a, j).start()
            for a in range(2):
                mine = pl.multiple_of(c * rh[a], rh[a])
                for j in range(N_CHIPS):
                    to_sibling(a, j).wait()
                    q_refs[a][j] = (accs[a][j, pl.ds(mine, rh[a]), :] + rcvs[a][j]).astype(BF16)
                g_refs[a][chip] = q_refs[a][chip]
            small_finish()

    stacks = ((N_CHIPS, rh[0], cw), (N_CHIPS, rh[1], d))
    n_cp = n_small * (N_DEV - 1)
    return pl.pallas_call(
        body, name="mix_wgrad",
        out_shape=tuple(jax.ShapeDtypeStruct(s, BF16) for s in stacks) * 2
        + tuple(jax.ShapeDtypeStruct(a.shape, F32) for a in small)
        + tuple(jax.ShapeDtypeStruct((N_DEV,) + a.shape, F32) for a in small[:n_all]),
        grid=(nt,),
        in_specs=[pl.BlockSpec((tt, d), lambda t: (t, 0)), pl.BlockSpec((N_CHIPS, tt, cw), lambda t: (0, t, 0)),
                  pl.BlockSpec((tt, 2 * cw), lambda t: (t, 0)), pl.BlockSpec((tt, d), lambda t: (t, 0))]
        + [VMEM_SPEC] * n_small,
        out_specs=(VMEM_SPEC,) * (4 + n_small + n_all),
        scratch_shapes=[pltpu.VMEM((N_CHIPS, d, cw), F32), pltpu.VMEM((N_CHIPS, ro, d), F32),
                        pltpu.VMEM(stacks[0], F32), pltpu.VMEM(stacks[1], F32)]
        + [pltpu.VMEM((N_DEV,) + a.shape, F32) for a in small]
        + [_dma_sems(2 * N_CHIPS), _dma_sems(2 * N_CHIPS), _dma_sems(n_cp), _dma_sems(n_cp)],
        compiler_params=_params(dimension_semantics=("arbitrary",)),
    )(h1, dz4, cat, dmix, *small)


def kernel(x, c, w_ada, b_ada, w_in, conv_w, w_pool, pool_scale, w_out, ln1_g, ln1_b, w_mlp_in, w_mlp_out, ln2_g, ln2_b, loss_target, m_w_ada, m_b_ada, m_w_in, m_conv_w, m_w_pool, m_pool_scale, m_w_out, m_ln1_g, m_ln1_b, m_w_mlp_in, m_w_mlp_out, m_ln2_g, m_ln2_b, v_w_ada, v_b_ada, v_w_in, v_conv_w, v_w_pool, v_pool_scale, v_w_out, v_ln1_g, v_ln1_b, v_w_mlp_in, v_w_mlp_out, v_ln2_g, v_ln2_b):
    t_len, d = x.shape[1], x.shape[2]
    cw = w_in.shape[2]
    cq = conv_w.shape[2]
    dg = w_pool.shape[2]
    assert cw == N_GROUPS * dg and cq * N_CHIPS == cw and LANES % cq == 0
    tm_mix = min(MIX_TOKEN_TILE, t_len)
    tm_mlp = min(MLP_TOKEN_TILE, t_len)
    tt = min(WGRAD_TILE, t_len // 4)
    chip = 2 * lax.axis_index("x") + lax.axis_index("y")
    pos = jnp.stack([lax.axis_index("c"), chip]).astype(jnp.int32)

    x2, tgt = x[0], loss_target[0]
    big_w = [w_in[0], w_out[0], w_mlp_in[0], w_mlp_out[0]]
    big_m = [m_w_in[0], m_w_out[0], m_w_mlp_in[0], m_w_mlp_out[0]]
    big_v = [v_w_in[0], v_w_out[0], v_w_mlp_in[0], v_w_mlp_out[0]]
    names = ["w_in", "w_out", "w_mlp_in", "w_mlp_out"]

    mod, cond_all, conv_full, w_in4, w_out4, w_mi_own, w_mo_own = _setup_exchange(
        c, w_ada[0], b_ada, conv_w[0], big_w, 2)
    w_out_full = w_out4.reshape(2 * cw, d)
    w_pool_bf = w_pool[0].astype(BF16)

    w_mi4, w_mo4, x1, zs, mix, h1, cat = _fwd_mix(x2, mod, conv_full, w_in4, w_pool_bf, pool_scale, w_out_full,
                                                  ln1_g, ln1_b, [w_mi_own, w_mo_own], tm_mix)
    dx1, da4, s4, df, h2, st2 = _mlp_fwd_bwd(x1, tgt, mod, w_mi4, w_mo4, ln2_g, ln2_b, tm_mlp)
    mlp_qi, mlp_qo, mlp_gi, mlp_go = _mlp_wgrad(h2, da4, s4, df, pos, tt)
    mlp_started = _split_start([mlp_qi, mlp_qo, mlp_gi, mlp_go], _chunk_exchange_plan(2), 2 * (N_CHIPS - 1),
                               "exchange_mlp")
    grad_x, dz4, dmix, st1, stc, dw_pool = _bwd_mix(x2, mix, dx1, zs, mod, conv_full, w_in4, w_pool_bf, pool_scale,
                                                    w_out_full, ln1_g, mlp_started[-1], tm_mix)

    mix_qi, mix_qo, mix_gi, mix_go, t1, t2, tc, g_w_pool, all1, all2 = _mix_wgrad(
        h1, dz4, cat, dmix, [st1, st2, stc, dw_pool], 2, tt)
    loss = t2[5, 0]
    g_ln1_g, g_ln1_b, g_ln2_g, g_ln2_b = t1[0:1], t1[1:2], t2[0:1], t2[1:2]
    g_pool_scale = tc[3:4]
    g_conv = lax.dynamic_slice_in_dim(tc[0:3], chip * cq, cq, axis=1)
    g_b_ada = jnp.concatenate([t1[4:5], t1[3:4], t1[2:3], t2[4:5], t2[3:4], t2[2:3]], axis=1)
    dmod_all = jnp.concatenate([all1[:, 4], all1[:, 3], all1[:, 2], all2[:, 4], all2[:, 3], all2[:, 2]], axis=1)
    na = w_ada.shape[2]
    dmod_mine = lax.dynamic_slice_in_dim(dmod_all, chip * na, na, axis=1)

    exchange, n_cp = _chunk_exchange_plan(2), 2 * (N_CHIPS - 1)
    mlp_got = _split_wait(mlp_started, exchange, t1, "exchange_mlp")[2:]
    mix_started = _split_start([mix_qi, mix_qo, mix_gi, mix_go], exchange, n_cp, "exchange_mix")
    mlp_halves = [_sum_chips(g, pos, mix_started[-1], "rs_total_" + n) for g, n in zip(mlp_got, names[2:])]
    join = _sibling_join_plan([h.shape for h in mlp_halves])
    join_started = _split_start(mlp_halves, join, 2, "join_mlp")
    g_ada, d_ada, nm_ada, nv_ada = _adamw_ada(cond_all.T, dmod_mine, w_ada[0], m_w_ada[0], v_w_ada[0],
                                              join_started[-1])
    mlp_g = _split_wait(join_started, join, d_ada, "join_mlp")
    mlp_out = [_adamw_big(g, w, m, v, "adamw_" + n)
               for g, w, m, v, n in zip(mlp_g, big_w[2:], big_m[2:], big_v[2:], names[2:])]
    mix_got = _split_wait(mix_started, exchange, mlp_out[1][0], "exchange_mix")[2:]
    mix_halves = [_sum_chips(g, pos, d_ada, "rs_total_" + n) for g, n in zip(mix_got, names[:2])]
    mix_g = _sibling_join(mix_halves, "mix")
    mix_out = [_adamw_big(g, w, m, v, "adamw_" + n)
               for g, w, m, v, n in zip(mix_g, big_w[:2], big_m[:2], big_v[:2], names[:2])]
    big_g = list(mix_g) + list(mlp_g)
    big_out = mix_out + mlp_out

    small_g = [g_b_ada, g_conv, g_w_pool.reshape(-1, dg), g_pool_scale, g_ln1_g, g_ln1_b, g_ln2_g, g_ln2_b]
    small_w = [b_ada, conv_w[0], w_pool.reshape(-1, dg), pool_scale, ln1_g, ln1_b, ln2_g, ln2_b]
    small_m = [m_b_ada, m_conv_w[0], m_w_pool.reshape(-1, dg), m_pool_scale, m_ln1_g, m_ln1_b, m_ln2_g, m_ln2_b]
    small_v = [v_b_ada, v_conv_w[0], v_w_pool.reshape(-1, dg), v_pool_scale, v_ln1_g, v_ln1_b, v_ln2_g, v_ln2_b]
    sm = _adamw_small(small_g, small_w, small_m, small_v)
    ns = len(small_g)
    s_delta, s_m, s_v = sm[:ns], sm[ns:2 * ns], sm[2 * ns:]

    def assemble(ada, small, big):
        return [ada[None], small[0], big[0][None], small[1][None], small[2].reshape(w_pool.shape), small[3],
                big[1][None], small[4], small[5], big[2][None], big[3][None], small[6], small[7]]

    grads = assemble(g_ada, small_g, big_g)
    deltas = assemble(d_ada, s_delta, [o[0] for o in big_out])
    new_m = assemble(nm_ada, s_m, [o[1] for o in big_out])
    new_v = assemble(nv_ada, s_v, [o[2] for o in big_out])
    return (loss, grad_x[None], *grads, *deltas, *new_m, *new_v)
```

```python
import jax
import jax.numpy as jnp
from jax import lax
from jax.experimental import pallas as pl
from jax.experimental.pallas import tpu as pltpu

F32 = jnp.float32
BF16 = jnp.bfloat16
MESH = pl.DeviceIdType.MESH

LN_EPS = 1e-5
DEEPNORM_ALPHA = 2.0 ** 0.25
POOL_WINDOWS = (2, 4, 8, 16)
N_GROUPS = len(POOL_WINDOWS)
HALO = 16
N_CHIPS = 4
N_DEV = 8
LANES = 128
SUBLANES = 8
VMEM_LIMIT = 56 * 1024 * 1024
MIX_TOKEN_TILE = 512
MLP_TOKEN_TILE = 256
WGRAD_TILE = 1024

ADAM_LR = 0.001
ADAM_B1 = 0.9
ADAM_B2 = 0.999
ADAM_EPS = 1e-08
ADAM_WD = 0.01
ADAM_STEP = 10

VMEM_SPEC = pl.BlockSpec(memory_space=pltpu.VMEM)
HBM_SPEC = pl.BlockSpec(memory_space=pltpu.HBM)


def _dot(a, b):
    return jnp.dot(a, b, preferred_element_type=F32)


def _dot_nt(a, b):
    return lax.dot_general(a, b, (((1,), (1,)), ((), ())), preferred_element_type=F32)


def _dot_tn(a, b):
    return lax.dot_general(a, b, (((0,), (0,)), ((), ())), preferred_element_type=F32)


def _fold8(v):
    r, n = v.shape
    return jnp.sum(v.reshape(r // SUBLANES, SUBLANES, n), axis=0)


def _mesh_pos():
    return lax.axis_index("x"), lax.axis_index("y"), lax.axis_index("c")


def _flip(v, bit):
    return 1 - v if bit else v


def _params(**kw):
    return pltpu.CompilerParams(vmem_limit_bytes=VMEM_LIMIT, **kw)


def _dma_sems(n):
    return pltpu.SemaphoreType.DMA((n,))


def _stack_gather(stacks, rows, s_ici, r_ici, s_d2d, r_d2d):
    x, y, c = _mesh_pos()
    chip = 2 * x + y
    pairs = [(i, m) for i in range(len(stacks)) for m in range(1, N_CHIPS)]

    def blk(i, slot, which):
        rh = rows[i] // 2
        return stacks[i].at[slot].at[pl.ds(which * rh, rh), :]

    def other(m):
        return 2 * _flip(x, m & 2) + _flip(y, m & 1)

    def ici(i, m, slot, to):
        k = i * (N_CHIPS - 1) + m - 1
        return pltpu.make_async_remote_copy(blk(i, slot, c), blk(i, slot, c), s_ici.at[k], r_ici.at[k],
                                            device_id=to, device_id_type=MESH)

    def d2d(i, m, which, to):
        k = i * (N_CHIPS - 1) + m - 1
        return pltpu.make_async_remote_copy(blk(i, other(m), which), blk(i, other(m), which),
                                            s_d2d.at[k], r_d2d.at[k], device_id=to, device_id_type=MESH)

    def start():
        for i, m in pairs:
            ici(i, m, chip, (_flip(x, m & 2), _flip(y, m & 1), c)).start()

    def forward(which=None):
        for i, m in pairs:
            if which is None or i == which:
                ici(i, m, other(m), (x, y, c)).wait_recv()
                d2d(i, m, c, (x, y, 1 - c)).start()

    def finish():
        for i, m in pairs:
            d2d(i, m, 1 - c, (x, y, c)).wait_recv()
        for i, m in pairs:
            ici(i, m, chip, (x, y, c)).wait_send()
            d2d(i, m, c, (x, y, c)).wait_send()

    return start, forward, finish


def _chunk_exchange(qs, gots, s_sem, r_sem):
    x, y, c = _mesh_pos()
    chip = 2 * x + y
    pairs = [(i, m) for i in range(len(qs)) for m in range(1, N_CHIPS)]

    def other(m):
        return 2 * _flip(x, m & 2) + _flip(y, m & 1)

    def send(i, m, to):
        k = i * (N_CHIPS - 1) + m - 1
        return pltpu.make_async_remote_copy(qs[i].at[other(m)], gots[i].at[chip], s_sem.at[k], r_sem.at[k],
                                            device_id=to, device_id_type=MESH)

    def arrival(i, m):
        k = i * (N_CHIPS - 1) + m - 1
        return pltpu.make_async_remote_copy(qs[i].at[other(m)], gots[i].at[other(m)], s_sem.at[k], r_sem.at[k],
                                            device_id=(x, y, c), device_id_type=MESH)

    def start():
        for i, m in pairs:
            send(i, m, (_flip(x, m & 2), _flip(y, m & 1), c)).start()

    def finish():
        for i, m in pairs:
            arrival(i, m).wait_recv()
        for i, m in pairs:
            send(i, m, (x, y, c)).wait_send()

    return start, finish


def _mod_scratch(d, na, cq):
    return [pltpu.VMEM((SUBLANES, d), F32), pltpu.VMEM((N_DEV, SUBLANES, d), F32),
            pltpu.VMEM((N_CHIPS, SUBLANES, na), F32),
            pltpu.VMEM((SUBLANES, cq), F32), pltpu.VMEM((N_CHIPS, SUBLANES, cq), F32),
            _dma_sems(N_DEV - 1), _dma_sems(N_DEV - 1),
            _dma_sems(N_CHIPS - 1), _dma_sems(N_CHIPS - 1), _dma_sems(N_CHIPS - 1), _dma_sems(N_CHIPS - 1)]


def _mod_steps(d, na, cq):
    def steps(c_ref, w_ref, b_ref, cw_ref, mod_ref, cond_ref, conv_ref,
              csend, cbuf, mbuf, cvsend, cvbuf, s1, r1, s2, r2, s3, r3):
        x, y, c = _mesh_pos()
        me = 4 * x + 2 * y + c
        chip = 2 * x + y
        csend[...] = jnp.broadcast_to(c_ref[...], (SUBLANES, d))
        cbuf[me] = csend[...]
        first = []
        for m in range(1, N_DEV):
            peer = (_flip(x, m & 4), _flip(y, m & 2), _flip(c, m & 1))
            cp = pltpu.make_async_remote_copy(csend, cbuf.at[me], s1.at[m - 1], r1.at[m - 1],
                                              device_id=peer, device_id_type=MESH)
            cp.start()
            first.append(cp)
        cvsend[...] = jnp.zeros((SUBLANES, cq), F32)
        cvsend[0:3, :] = cw_ref[...]
        cvbuf[chip] = cvsend[...]
        for m in range(1, N_DEV):
            src = 4 * _flip(x, m & 4) + 2 * _flip(y, m & 2) + _flip(c, m & 1)
            pltpu.make_async_remote_copy(csend, cbuf.at[src], s1.at[m - 1], r1.at[m - 1],
                                         device_id=(x, y, c), device_id_type=MESH).wait_recv()
        rows = lax.broadcasted_iota(jnp.int32, (SUBLANES, d), 0)
        call = jnp.zeros((SUBLANES, d), F32)
        for b in range(N_DEV):
            call = jnp.where(rows == b, cbuf[b], call)
        cond = call * jax.nn.sigmoid(call)
        cond_ref[...] = cond
        part = jnp.dot(cond, w_ref[...], preferred_element_type=F32, precision=lax.Precision.HIGHEST)
        mbuf[chip] = part
        second = []
        for m in range(1, N_CHIPS):
            peer = (_flip(x, m & 2), _flip(y, m & 1), c)
            cp = pltpu.make_async_remote_copy(mbuf.at[chip], mbuf.at[chip], s2.at[m - 1], r2.at[m - 1],
                                              device_id=peer, device_id_type=MESH)
            cp.start()
            second.append(cp)
            cp = pltpu.make_async_remote_copy(cvsend, cvbuf.at[chip], s3.at[m - 1], r3.at[m - 1],
                                              device_id=peer, device_id_type=MESH)
            cp.start()
            second.append(cp)
        for m in range(1, N_CHIPS):
            src = 2 * _flip(x, m & 2) + _flip(y, m & 1)
            pltpu.make_async_remote_copy(mbuf.at[src], mbuf.at[src], s2.at[m - 1], r2.at[m - 1],
                                         device_id=(x, y, c), device_id_type=MESH).wait_recv()
            pltpu.make_async_remote_copy(cvsend, cvbuf.at[src], s3.at[m - 1], r3.at[m - 1],
                                         device_id=(x, y, c), device_id_type=MESH).wait_recv()
        for cp in first + second:
            cp.wait_send()
        rows_n = lax.broadcasted_iota(jnp.int32, (SUBLANES, na), 0)
        for k in range(N_CHIPS):
            mine = jnp.sum(jnp.where(rows_n == me, mbuf[k], 0.0), axis=0, keepdims=True)
            mod_ref[:, k * na:(k + 1) * na] = jnp.broadcast_to(mine + b_ref[:, k * na:(k + 1) * na], (SUBLANES, na))
            conv_ref[:, k * cq:(k + 1) * cq] = cvbuf[k]

    return steps


def _setup_exchange(c_row, w_ada_s, b_ada, conv_w_s, shards, n_now):
    n = len(shards)
    n_cp = n_now * (N_CHIPS - 1)
    d, na, cq = c_row.shape[1], w_ada_s.shape[1], conv_w_s.shape[1]
    mod_scratch = _mod_scratch(d, na, cq)
    mod_steps = _mod_steps(d, na, cq)

    def body(*refs):
        mod_in, ins = refs[:4], refs[4:4 + n]
        mod_out, outs = refs[4 + n:7 + n], refs[7 + n:7 + 2 * n]
        bufs = refs[7 + 2 * n:7 + 3 * n]
        mod_scr = refs[7 + 3 * n:7 + 3 * n + len(mod_scratch)]
        s_ici, r_ici, s_d2d, r_d2d, s_loc = refs[7 + 3 * n + len(mod_scratch):7 + 3 * n + len(mod_scratch) + 5]
        w_ada_buf, s_in = refs[7 + 3 * n + len(mod_scratch) + 5:7 + 3 * n + len(mod_scratch) + 7]
        late_bufs = refs[7 + 3 * n + len(mod_scratch) + 7:]
        x, y, _ = _mesh_pos()
        chip = 2 * x + y
        loads = [pltpu.make_async_copy(mod_in[1], w_ada_buf, s_in.at[0])]
        loads += [pltpu.make_async_copy(ins[i], late_bufs[i - n_now], s_in.at[1 + i - n_now]) for i in range(n_now, n)]
        for cp in loads:
            cp.start()

        def place(i, src):
            bufs[i][...] = src[...].astype(BF16)
            cp = pltpu.make_async_copy(bufs[i], outs[i].at[chip], s_loc.at[i])
            cp.start()
            return cp

        start, forward, finish = _stack_gather(outs[:n_now], [s.shape[0] for s in shards[:n_now]],
                                               s_ici, r_ici, s_d2d, r_d2d)
        for cp in [place(i, ins[i]) for i in range(n_now)]:
            cp.wait()
        start()
        later = []
        for i in range(n_now, n):
            loads[1 + i - n_now].wait()
            later.append(place(i, late_bufs[i - n_now]))
        loads[0].wait()
        mod_steps(mod_in[0], w_ada_buf, mod_in[2], mod_in[3], *mod_out, *mod_scr)
        forward()
        finish()
        for cp in later:
            cp.wait()

    any_spec = pl.BlockSpec(memory_space=pl.ANY)
    return pl.pallas_call(
        body, name="setup_exchange",
        out_shape=(jax.ShapeDtypeStruct((SUBLANES, N_CHIPS * na), F32), jax.ShapeDtypeStruct((SUBLANES, d), F32),
                   jax.ShapeDtypeStruct((SUBLANES, N_CHIPS * cq), F32))
        + tuple(jax.ShapeDtypeStruct((N_CHIPS,) + s.shape, BF16) for s in shards),
        in_specs=[VMEM_SPEC, any_spec, VMEM_SPEC, VMEM_SPEC] + [VMEM_SPEC] * n_now + [any_spec] * (n - n_now),
        out_specs=(VMEM_SPEC,) * 3 + (HBM_SPEC,) * n,
        scratch_shapes=[pltpu.VMEM(s.shape, BF16) for s in shards] + mod_scratch
        + [_dma_sems(n_cp), _dma_sems(n_cp), _dma_sems(n_cp), _dma_sems(n_cp), _dma_sems(n)]
        + [pltpu.VMEM(w_ada_s.shape, F32), _dma_sems(1 + n - n_now)]
        + [pltpu.VMEM(s.shape, F32) for s in shards[n_now:]],
        compiler_params=_params(),
    )(c_row, w_ada_s, b_ada, conv_w_s, *shards)


def _sibling_join_plan(shapes):
    def plan(refs, s_sem, r_sem):
        x, y, c = _mesh_pos()

        def half(i, which):
            rh = shapes[i][0] // 2
            return refs[i].at[pl.ds(which * rh, rh), :]

        def start():
            for i in range(len(refs)):
                pltpu.make_async_remote_copy(half(i, c), half(i, c), s_sem.at[i], r_sem.at[i],
                                             device_id=(x, y, 1 - c), device_id_type=MESH).start()

        def finish():
            for i in range(len(refs)):
                pltpu.make_async_remote_copy(half(i, 1 - c), half(i, 1 - c), s_sem.at[i], r_sem.at[i],
                                             device_id=(x, y, c), device_id_type=MESH).wait_recv()
            for i in range(len(refs)):
                pltpu.make_async_remote_copy(half(i, c), half(i, c), s_sem.at[i], r_sem.at[i],
                                             device_id=(x, y, c), device_id_type=MESH).wait_send()

        return start, finish

    return plan


def _chunk_exchange_plan(n):
    return lambda refs, s_sem, r_sem: _chunk_exchange(refs[:n], refs[n:], s_sem, r_sem)


def _sibling_join(gs, tag):
    n = len(gs)
    plan = _sibling_join_plan([g.shape for g in gs])

    def body(*refs):
        start, finish = plan(refs[n:2 * n], refs[2 * n], refs[2 * n + 1])
        start()
        finish()

    return pl.pallas_call(
        body, name="rs_sibling_join_" + tag,
        out_shape=tuple(jax.ShapeDtypeStruct(g.shape, g.dtype) for g in gs),
        in_specs=[HBM_SPEC] * n, out_specs=(HBM_SPEC,) * n,
        input_output_aliases={i: i for i in range(n)},
        scratch_shapes=[_dma_sems(n), _dma_sems(n)],
        compiler_params=_params(),
    )(*gs)


SEM_SPEC = pl.BlockSpec(memory_space=pltpu.SEMAPHORE)
SIDE_EFFECT = pltpu.SideEffectType.DATAFLOW_SIDE_EFFECTING


def _split_start(arrays, plan, n_cp, tag):
    n = len(arrays)

    def body(*refs):
        start, _ = plan(refs[n + 2:2 * n + 2], refs[n], refs[n + 1])
        start()
        token = refs[2 * n + 2]
        token[...] = jnp.zeros_like(token)

    through = [pltpu.with_memory_space_constraint(a, pltpu.HBM) for a in arrays]
    return pl.pallas_call(
        body, name=tag + "_start",
        out_shape=(pltpu.SemaphoreType.DMA((n_cp,)), pltpu.SemaphoreType.DMA((n_cp,)))
        + tuple(pltpu.HBM(a.shape, a.dtype) for a in through) + (jax.ShapeDtypeStruct((SUBLANES, LANES), F32),),
        in_specs=[HBM_SPEC] * n, out_specs=(SEM_SPEC, SEM_SPEC) + (HBM_SPEC,) * n + (VMEM_SPEC,),
        input_output_aliases={k: 2 + k for k in range(n)},
        compiler_params=pltpu.CompilerParams(has_side_effects=SIDE_EFFECT),
    )(*through)


def _split_wait(started, plan, after, tag):
    send_sems, recv_sems, arrays = started[0], started[1], started[2:-1]
    n = len(arrays)

    def body(*refs):
        _, finish = plan(refs[:n], refs[n], refs[n + 1])
        finish()

    return pl.pallas_call(
        body, name=tag + "_wait",
        out_shape=tuple(pltpu.HBM(a.shape, a.dtype) for a in arrays),
        in_specs=[HBM_SPEC] * n + [SEM_SPEC, SEM_SPEC, pl.BlockSpec(memory_space=pl.ANY)],
        out_specs=(HBM_SPEC,) * n,
        input_output_aliases={k: k for k in range(n)},
        compiler_params=pltpu.CompilerParams(has_side_effects=SIDE_EFFECT),
    )(*arrays, send_sems, recv_sems, after)


def _small_allreduce_steps(groups, s_sem, r_sem):
    x, y, c = _mesh_pos()
    me = 4 * x + 2 * y + c
    pairs = [(k, m) for k in range(len(groups)) for m in range(1, N_DEV)]

    def send(k, m, to):
        p_ref, all_ref = groups[k][0], groups[k][3]
        i = k * (N_DEV - 1) + m - 1
        return pltpu.make_async_remote_copy(p_ref, all_ref.at[me], s_sem.at[i], r_sem.at[i],
                                            device_id=to, device_id_type=MESH)

    def start():
        for p_ref, _, _, all_ref in groups:
            all_ref[me] = p_ref[...]
        for k, m in pairs:
            send(k, m, (_flip(x, m & 4), _flip(y, m & 2), _flip(c, m & 1))).start()

    def finish():
        for k, m in pairs:
            p_ref, all_ref = groups[k][0], groups[k][3]
            i = k * (N_DEV - 1) + m - 1
            src = 4 * _flip(x, m & 4) + 2 * _flip(y, m & 2) + _flip(c, m & 1)
            pltpu.make_async_remote_copy(p_ref, all_ref.at[src], s_sem.at[i], r_sem.at[i],
                                         device_id=(x, y, c), device_id_type=MESH).wait_recv()
        for k, m in pairs:
            send(k, m, (x, y, c)).wait_send()
        for _, sum_ref, all_out_ref, all_ref in groups:
            sum_ref[...] = (((all_ref[0] + all_ref[1]) + (all_ref[2] + all_ref[3]))
                            + ((all_ref[4] + all_ref[5]) + (all_ref[6] + all_ref[7])))
            if all_out_ref is not None:
                all_out_ref[...] = all_ref[...]

    return start, finish


def _row_block(rows, cap=256):
    rb = min(rows, cap)
    assert rows % rb == 0
    return rb


def _sum_chips(got, pos, after, name):
    _, rh, cc = got.shape
    rb = _row_block(rh)
    nb = rh // rb

    def body(pos_ref, a_ref, after_ref, o_ref):
        a = a_ref[...].astype(F32)
        o_ref[...] = (a[0] + a[1]) + (a[2] + a[3])

    return pl.pallas_call(
        body, name=name,
        out_shape=jax.ShapeDtypeStruct((2 * rh, cc), F32),
        grid_spec=pltpu.PrefetchScalarGridSpec(
            num_scalar_prefetch=1, grid=(nb,),
            in_specs=[pl.BlockSpec((N_CHIPS, rb, cc), lambda i, p: (0, i, 0)),
                      pl.BlockSpec(memory_space=pl.ANY)],
            out_specs=pl.BlockSpec((rb, cc), lambda i, p: (p[0] * nb + i, 0))),
        compiler_params=_params(dimension_semantics=("arbitrary",)),
    )(pos, got, after)


def _adamw_math(w, g, m, v):
    m = ADAM_B1 * m + (1.0 - ADAM_B1) * g
    v = ADAM_B2 * v + (1.0 - ADAM_B2) * jnp.square(g)
    m_hat = m / (1.0 - ADAM_B1 ** ADAM_STEP)
    v_hat = v / (1.0 - ADAM_B2 ** ADAM_STEP)
    delta = -ADAM_LR * (m_hat / (jnp.sqrt(v_hat) + ADAM_EPS) + ADAM_WD * w)
    return delta, m, v


def _adamw_big(g, w, m, v, name):
    r, cc = w.shape
    rb = _row_block(r)

    def body(g_ref, w_ref, m_ref, v_ref, d_ref, mo_ref, vo_ref):
        d, mn, vn = _adamw_math(w_ref[...], g_ref[...], m_ref[...], v_ref[...])
        d_ref[...] = d
        mo_ref[...] = mn
        vo_ref[...] = vn

    spec = pl.BlockSpec((rb, cc), lambda i: (i, 0))
    return pl.pallas_call(
        body, name=name,
        out_shape=(jax.ShapeDtypeStruct((r, cc), F32),) * 3,
        grid=(r // rb,), in_specs=[spec] * 4, out_specs=(spec,) * 3,
        compiler_params=_params(dimension_semantics=("arbitrary",)),
    )(g, w, m, v)


def _adamw_ada(cond, dmod, w, m, v, after):
    d, na = w.shape
    rb = _row_block(d, 128)

    def body(c_ref, dm_ref, w_ref, m_ref, v_ref, after_ref, g_ref, d_ref, mo_ref, vo_ref):
        g = lax.dot_general(c_ref[...], dm_ref[...], (((0,), (0,)), ((), ())), preferred_element_type=F32,
                            precision=lax.Precision.HIGHEST)
        dl, mn, vn = _adamw_math(w_ref[...], g, m_ref[...], v_ref[...])
        g_ref[...] = g
        d_ref[...] = dl
        mo_ref[...] = mn
        vo_ref[...] = vn

    spec = pl.BlockSpec((rb, na), lambda i: (i, 0))
    return pl.pallas_call(
        body, name="adamw_w_ada",
        out_shape=(jax.ShapeDtypeStruct((d, na), F32),) * 4,
        grid=(d // rb,),
        in_specs=[pl.BlockSpec((N_DEV, rb), lambda i: (0, i)), pl.BlockSpec((N_DEV, na), lambda i: (0, 0)),
                  spec, spec, spec, pl.BlockSpec(memory_space=pl.ANY)],
        out_specs=(spec,) * 4,
        compiler_params=_params(dimension_semantics=("arbitrary",)),
    )(cond, dmod, w, m, v, after)


def _adamw_small(gs, ws, ms, vs):
    n = len(gs)

    def body(*refs):
        ins, outs = refs[:4 * n], refs[4 * n:]
        for i in range(n):
            d, mn, vn = _adamw_math(ins[n + i][...], ins[i][...], ins[2 * n + i][...], ins[3 * n + i][...])
            outs[i][...] = d
            outs[n + i][...] = mn
            outs[2 * n + i][...] = vn

    shapes = tuple(jax.ShapeDtypeStruct(w.shape, F32) for w in ws)
    return pl.pallas_call(
        body, name="adamw_small",
        out_shape=shapes * 3,
        in_specs=[VMEM_SPEC] * (4 * n), out_specs=(VMEM_SPEC,) * (3 * n),
        compiler_params=_params(),
    )(*gs, *ws, *ms, *vs)


def _layer_norm_fwd(r):
    mu = jnp.mean(r, axis=-1, keepdims=True)
    xc = r - mu
    var = jnp.mean(jnp.square(xc), axis=-1, keepdims=True)
    rstd = lax.rsqrt(var + LN_EPS)
    return xc * rstd, rstd


def _layer_norm_bwd(dy, xhat, rstd, gain):
    dxh = dy * gain
    m1 = jnp.mean(dxh, axis=-1, keepdims=True)
    m2 = jnp.mean(dxh * xhat, axis=-1, keepdims=True)
    return rstd * (dxh - m1 - xhat * m2)


def _inv_count(tile, tm, win):
    t = (tile * tm + lax.broadcasted_iota(jnp.int32, (tm, 1), 0) + 1).astype(F32)
    return 1.0 / jnp.minimum(t, float(win))


def _fwd_mix(x, mod, conv_w, w_in4, w_pool, pool_scale, w_out, ln_g, ln_b, late_stacks, tm):
    t_len, d = x.shape
    cw = w_in4.shape[2]
    dg = cw // N_GROUPS
    nt = t_len // tm
    n_late = len(late_stacks)
    n_cp = n_late * (N_CHIPS - 1)
    pass_steps = [min(((k + 1) * nt) // n_late + (1 if k + 1 < n_late else -2), nt - 1) if nt > 2 else nt - 1
                  for k in range(n_late)]
    pass_steps = [max(p, 0) for p in pass_steps]

    def body(x_ref, mod_ref, cw_ref, win_ref, wp_ref, ps_ref, wout_ref, g_ref, b_ref, *rest):
        late = rest[n_late:2 * n_late]
        x1_ref, zs_ref, mix_ref, h1_ref, cat_ref = rest[2 * n_late:2 * n_late + 5]
        ubuf, vbuf, s_ici, r_ici, s_d2d, r_d2d = rest[2 * n_late + 5:]
        i = pl.program_id(0)
        start, forward, finish = _stack_gather(late, [s.shape[1] for s in late_stacks], s_ici, r_ici, s_d2d, r_d2d)

        @pl.when(i == 0)
        def _():
            start()
            ubuf[0:HALO, :] = jnp.zeros((HALO, cw), F32)
            vbuf[0:HALO, :] = jnp.zeros((HALO, cw), F32)

        xv = x_ref[...]
        sh1, sc1, g1 = mod_ref[0:1, 0:d], mod_ref[0:1, d:2 * d], mod_ref[0:1, 2 * d:3 * d]
        h1 = (xv * (1.0 + sc1) + sh1).astype(BF16)
        h1_ref[...] = h1
        zb = _dot(h1, win_ref[0])
        zc = _dot(h1, win_ref[1])
        zv = _dot(h1, win_ref[2])
        zp = _dot(h1, win_ref[3])
        u = zc * zv
        ubuf[HALO:HALO + tm, :] = u
        u1 = ubuf[pl.ds(HALO - 1, tm), :]
        u2 = ubuf[pl.ds(HALO - 2, tm), :]
        conv = cw_ref[0:1, :] * u2 + cw_ref[1:2, :] * u1 + cw_ref[2:3, :] * u
        ubuf[0:HALO, :] = ubuf[tm:tm + HALO, :]
        yc = zb * conv
        vbuf[HALO:HALO + tm, :] = zp
        ps, pms = [], []
        for gi, win in enumerate(POOL_WINDOWS):
            sl = slice(gi * dg, (gi + 1) * dg)
            acc = zp[:, sl]
            for s in range(1, win):
                acc = acc + vbuf[pl.ds(HALO - s, tm), sl]
            p_g = acc * _inv_count(i, tm, win) - zp[:, sl]
            ps.append(p_g)
            pms.append(_dot(p_g.astype(BF16), wp_ref[gi]))
        vbuf[0:HALO, :] = vbuf[tm:tm + HALO, :]
        pooled = jnp.concatenate(ps, axis=1)
        yp = jnp.concatenate(pms, axis=1) * ps_ref[...]
        cat = jnp.concatenate([yc, yp], axis=1).astype(BF16)
        cat_ref[...] = cat
        mix = _dot(cat, wout_ref[...])
        mix_ref[...] = mix
        xhat, _ = _layer_norm_fwd(DEEPNORM_ALPHA * xv + (1.0 + g1) * mix)
        x1_ref[...] = xhat * g_ref[...] + b_ref[...]
        zs_ref[:, 0 * cw:1 * cw] = zb.astype(BF16)
        zs_ref[:, 1 * cw:2 * cw] = zc.astype(BF16)
        zs_ref[:, 2 * cw:3 * cw] = zv.astype(BF16)
        zs_ref[:, 3 * cw:4 * cw] = conv.astype(BF16)
        zs_ref[:, 4 * cw:5 * cw] = pooled.astype(BF16)

        for k in range(n_late):
            @pl.when(i == pass_steps[k])
            def _():
                forward(k)

        @pl.when(i == nt - 1)
        def _():
            finish()

    tile = lambda w: pl.BlockSpec((tm, w), lambda i: (i, 0))
    n_in = 9
    return pl.pallas_call(
        body, name="fwd_mix",
        out_shape=tuple(jax.ShapeDtypeStruct(s.shape, s.dtype) for s in late_stacks)
        + (jax.ShapeDtypeStruct((t_len, d), F32), jax.ShapeDtypeStruct((t_len, 5 * cw), BF16),
           jax.ShapeDtypeStruct((t_len, d), F32), jax.ShapeDtypeStruct((t_len, d), BF16),
           jax.ShapeDtypeStruct((t_len, 2 * cw), BF16)),
        grid=(nt,),
        in_specs=[tile(d)] + [VMEM_SPEC] * 8 + [HBM_SPEC] * n_late,
        out_specs=(HBM_SPEC,) * n_late + (tile(d), tile(5 * cw), tile(d), tile(d), tile(2 * cw)),
        input_output_aliases={n_in + k: k for k in range(n_late)},
        scratch_shapes=[pltpu.VMEM((tm + HALO, cw), F32), pltpu.VMEM((tm + HALO, cw), F32),
                        _dma_sems(n_cp), _dma_sems(n_cp), _dma_sems(n_cp), _dma_sems(n_cp)],
        compiler_params=_params(dimension_semantics=("arbitrary",)),
    )(x, mod, conv_w, w_in4, w_pool, pool_scale, w_out, ln_g, ln_b, *late_stacks)


def _mlp_fwd_bwd(x1, tgt, mod, w_mi4, w_mo4, ln_g, ln_b, tm):
    t_len, d = x1.shape
    fq = w_mi4.shape[2]
    nt = t_len // tm

    def body(x1_ref, tgt_ref, mod_ref, wi_ref, wo_ref, g_ref, b_ref,
             dx1_ref, da_ref, s_ref, df_ref, h2_ref, st_ref, relu_buf, acc):
        i = pl.program_id(0)

        @pl.when(i == 0)
        def _():
            acc[...] = jnp.zeros(acc.shape, F32)

        x1v = x1_ref[...]
        sh2, sc2, g2 = mod_ref[0:1, 3 * d:4 * d], mod_ref[0:1, 4 * d:5 * d], mod_ref[0:1, 5 * d:6 * d]
        h2 = (x1v * (1.0 + sc2) + sh2).astype(BF16)
        h2_ref[...] = h2
        f = jnp.zeros((tm, d), F32)
        for j in range(N_CHIPS):
            a = jnp.maximum(_dot(h2, wi_ref[j]), 0.0)
            relu_buf[j] = a
            s = (a * a).astype(BF16)
            s_ref[j] = s
            f = f + _dot(s, wo_ref[j])
        xhat, rstd = _layer_norm_fwd(DEEPNORM_ALPHA * x1v + (1.0 + g2) * f)
        gain = g_ref[...]
        err = xhat * gain + b_ref[...] - tgt_ref[...]
        dy = err * (1.0 / d)
        dr2 = _layer_norm_bwd(dy, xhat, rstd, gain)
        df = ((1.0 + g2) * dr2).astype(BF16)
        df_ref[...] = df
        dh2 = jnp.zeros((tm, d), F32)
        for j in range(N_CHIPS):
            ds = _dot_nt(df, wo_ref[j])
            da = (ds * (2.0 * relu_buf[j])).astype(BF16)
            da_ref[j] = da
            dh2 = dh2 + _dot_nt(da, wi_ref[j])
        dx1_ref[...] = DEEPNORM_ALPHA * dr2 + dh2 * (1.0 + sc2)
        acc[0] += _fold8(dy * xhat)
        acc[1] += _fold8(dy)
        acc[2] += _fold8(dr2 * f)
        acc[3] += _fold8(dh2 * x1v)
        acc[4] += _fold8(dh2)
        acc[5] += _fold8(err * err)

        @pl.when(i == nt - 1)
        def _():
            for k in range(5):
                st_ref[k:k + 1, :] = jnp.sum(acc[k], axis=0, keepdims=True)
            loss = jnp.sum(acc[5]) * (0.5 / d)
            st_ref[5:6, :] = jnp.broadcast_to(loss, (1, d))
            st_ref[6:8, :] = jnp.zeros((2, d), F32)

    tile = lambda w: pl.BlockSpec((tm, w), lambda i: (i, 0))
    tile4 = pl.BlockSpec((N_CHIPS, tm, fq), lambda i: (0, i, 0))
    return pl.pallas_call(
        body, name="mlp_fwd_bwd",
        out_shape=(jax.ShapeDtypeStruct((t_len, d), F32),
                   jax.ShapeDtypeStruct((N_CHIPS, t_len, fq), BF16), jax.ShapeDtypeStruct((N_CHIPS, t_len, fq), BF16),
                   jax.ShapeDtypeStruct((t_len, d), BF16), jax.ShapeDtypeStruct((t_len, d), BF16),
                   jax.ShapeDtypeStruct((SUBLANES, d), F32)),
        grid=(nt,),
        in_specs=[tile(d), tile(d)] + [VMEM_SPEC] * 5,
        out_specs=(tile(d), tile4, tile4, tile(d), tile(d), pl.BlockSpec((SUBLANES, d), lambda i: (0, 0))),
        scratch_shapes=[pltpu.VMEM((N_CHIPS, tm, fq), F32), pltpu.VMEM((6, SUBLANES, d), F32)],
        compiler_params=_params(dimension_semantics=("arbitrary",)),
    )(x1, tgt, mod, w_mi4, w_mo4, ln_g, ln_b)


def _mlp_wgrad(h2, da4, s4, df, pos, tt):
    t_len, d = h2.shape
    fq = da4.shape[2]
    nt = t_len // tt
    assert nt >= 3
    rh = (d // 2, fq // 2)
    last = N_CHIPS - 1

    def body(pos_ref, h2_ref, da_ref, s_ref, df_ref, qi_ref, qo_ref, gi_ref, go_ref,
             acc_i, acc_o, rcv_i, rcv_o, s_sem, r_sem):
        j, t = pl.program_id(0), pl.program_id(1)
        slot = lax.rem(j, 2)
        x, y, c = _mesh_pos()
        accs, rcvs, q_refs, g_refs = (acc_i, acc_o), (rcv_i, rcv_o), (qi_ref, qo_ref), (gi_ref, go_ref)

        def to_sibling(a, sl):
            theirs = pl.multiple_of((1 - c) * rh[a], rh[a])
            return pltpu.make_async_remote_copy(accs[a].at[sl].at[pl.ds(theirs, rh[a]), :], rcvs[a].at[sl],
                                                s_sem.at[2 * sl + a], r_sem.at[2 * sl + a],
                                                device_id=(x, y, 1 - c), device_id_type=MESH)

        def emit(chunk, sl):
            for a in range(2):
                to_sibling(a, sl).wait()
                mine = pl.multiple_of(c * rh[a], rh[a])
                q = (accs[a][sl, pl.ds(mine, rh[a]), :] + rcvs[a][sl]).astype(BF16)
                q_refs[a][...] = q

                @pl.when(chunk == pos_ref[1])
                def _():
                    g_refs[a][...] = q

        @pl.when(t == 0)
        def _():
            acc_i[slot] = jnp.zeros((d, fq), F32)
            acc_o[slot] = jnp.zeros((fq, d), F32)

        acc_i[slot] += _dot_tn(h2_ref[...], da_ref[...])
        acc_o[slot] += _dot_tn(s_ref[...], df_ref[...])

        @pl.when((t == 1) & (j >= 1))
        def _():
            emit(j - 1, 1 - slot)

        @pl.when(t == nt - 1)
        def _():
            for a in range(2):
                to_sibling(a, slot).start()

        @pl.when((t == nt - 1) & (j == last))
        def _():
            emit(j, slot)

    q_index = lambda j, t, p: (jnp.where(t == nt - 1, j, jnp.maximum(j - 1, 0)), 0, 0)
    got_index = lambda j, t, p: (p[1], 0, 0)
    blocks = ((None, rh[0], fq), (None, rh[1], d))
    return pl.pallas_call(
        body, name="mlp_wgrad",
        out_shape=tuple(jax.ShapeDtypeStruct((N_CHIPS,) + b[1:], BF16) for b in blocks) * 2,
        grid_spec=pltpu.PrefetchScalarGridSpec(
            num_scalar_prefetch=1, grid=(N_CHIPS, nt),
            in_specs=[pl.BlockSpec((tt, d), lambda j, t, p: (t, 0)),
                      pl.BlockSpec((None, tt, fq), lambda j, t, p: (j, t, 0)),
                      pl.BlockSpec((None, tt, fq), lambda j, t, p: (j, t, 0)),
                      pl.BlockSpec((tt, d), lambda j, t, p: (t, 0))],
            out_specs=tuple(pl.BlockSpec(b, q_index) for b in blocks)
            + tuple(pl.BlockSpec(b, got_index) for b in blocks),
            scratch_shapes=[pltpu.VMEM((2, d, fq), F32), pltpu.VMEM((2, fq, d), F32),
                            pltpu.VMEM((2, rh[0], fq), F32), pltpu.VMEM((2, rh[1], d), F32),
                            _dma_sems(4), _dma_sems(4)]),
        compiler_params=_params(dimension_semantics=("arbitrary", "arbitrary")),
    )(pos, h2, da4, s4, df)


def _bwd_mix(x, mix, dx1, zs, mod, conv_w, w_in4, w_pool, pool_scale, w_out, ln_g, after, tm):
    t_len, d = x.shape
    cw = w_in4.shape[2]
    dg = cw // N_GROUPS
    nt = t_len // tm

    def body(x_ref, mix_ref, dx1_ref, zs_ref, mod_ref, cw_ref, win_ref, wp_ref, ps_ref, wout_ref, g_ref, after_ref,
             gx_ref, dz_ref, dmix_ref, sd_ref, sc_ref, dwp_ref, dbuf, qbuf, acc_d, acc_c, acc_p):
        i = pl.program_id(0)
        tile_idx = nt - 1 - i

        @pl.when(i == 0)
        def _():
            dbuf[tm:tm + HALO, :] = jnp.zeros((HALO, cw), F32)
            qbuf[tm:tm + HALO, :] = jnp.zeros((HALO, cw), F32)
            acc_d[...] = jnp.zeros(acc_d.shape, F32)
            acc_c[...] = jnp.zeros(acc_c.shape, F32)
            acc_p[...] = jnp.zeros(acc_p.shape, F32)

        xv, mixv, dx1v = x_ref[...], mix_ref[...], dx1_ref[...]
        sc1, g1 = mod_ref[0:1, d:2 * d], mod_ref[0:1, 2 * d:3 * d]
        xhat, rstd = _layer_norm_fwd(DEEPNORM_ALPHA * xv + (1.0 + g1) * mixv)
        dr1 = _layer_norm_bwd(dx1v, xhat, rstd, g_ref[...])
        dmix = ((1.0 + g1) * dr1).astype(BF16)
        dmix_ref[...] = dmix
        dcat = _dot_nt(dmix, wout_ref[...])
        dyc, dyp = dcat[:, 0:cw], dcat[:, cw:2 * cw]
        zb, zc, zv = (zs_ref[:, k * cw:(k + 1) * cw].astype(F32) for k in range(3))
        conv, pooled = zs_ref[:, 3 * cw:4 * cw].astype(F32), zs_ref[:, 4 * cw:5 * cw]
        dzb = dyc * conv
        dcv = dyc * zb
        dbuf[0:tm, :] = dcv
        d1 = dbuf[pl.ds(1, tm), :]
        d2 = dbuf[pl.ds(2, tm), :]
        du = cw_ref[2:3, :] * dcv + cw_ref[1:2, :] * d1 + cw_ref[0:1, :] * d2
        dbuf[tm:tm + HALO, :] = dbuf[0:HALO, :]
        u = zc * zv
        acc_c[0] += _fold8(u * d2)
        acc_c[1] += _fold8(u * d1)
        acc_c[2] += _fold8(u * dcv)
        dzc = du * zv
        dzv = du * zc
        dpw = dyp * ps_ref[...]
        dps, pms = [], []
        for gi, win in enumerate(POOL_WINDOWS):
            sl = slice(gi * dg, (gi + 1) * dg)
            p_g = pooled[:, sl].astype(BF16)
            dpw_g = dpw[:, sl].astype(BF16)
            pms.append(_dot(p_g, wp_ref[gi]))
            acc_p[gi] += _dot_tn(p_g, dpw_g)
            dp_g = _dot_nt(dpw_g, wp_ref[gi])
            dps.append(dp_g)
            qbuf[0:tm, sl] = dp_g * _inv_count(tile_idx, tm, win)
        acc_c[3] += _fold8(dyp * jnp.concatenate(pms, axis=1))
        dzps = []
        for gi, win in enumerate(POOL_WINDOWS):
            sl = slice(gi * dg, (gi + 1) * dg)
            acc = qbuf[0:tm, sl]
            for s in range(1, win):
                acc = acc + qbuf[pl.ds(s, tm), sl]
            dzps.append(acc - dps[gi])
        qbuf[tm:tm + HALO, :] = qbuf[0:HALO, :]
        dz = [dzb.astype(BF16), dzc.astype(BF16), dzv.astype(BF16), jnp.concatenate(dzps, axis=1).astype(BF16)]
        dh1 = jnp.zeros((tm, d), F32)
        for j in range(N_CHIPS):
            dz_ref[j] = dz[j]
            dh1 = dh1 + _dot_nt(dz[j], win_ref[j])
        gx_ref[...] = DEEPNORM_ALPHA * dr1 + dh1 * (1.0 + sc1)
        acc_d[0] += _fold8(dx1v * xhat)
        acc_d[1] += _fold8(dx1v)
        acc_d[2] += _fold8(dr1 * mixv)
        acc_d[3] += _fold8(dh1 * xv)
        acc_d[4] += _fold8(dh1)

        @pl.when(i == nt - 1)
        def _():
            for k in range(5):
                sd_ref[k:k + 1, :] = jnp.sum(acc_d[k], axis=0, keepdims=True)
            sd_ref[5:8, :] = jnp.zeros((3, d), F32)
            for k in range(4):
                sc_ref[k:k + 1, :] = jnp.sum(acc_c[k], axis=0, keepdims=True)
            sc_ref[4:8, :] = jnp.zeros((4, cw), F32)
            dwp_ref[...] = acc_p[...]

    rtile = lambda w: pl.BlockSpec((tm, w), lambda i: (nt - 1 - i, 0))
    whole = lambda shape: pl.BlockSpec(shape, lambda i: tuple(0 for _ in shape))
    return pl.pallas_call(
        body, name="bwd_mix",
        out_shape=(jax.ShapeDtypeStruct((t_len, d), F32), jax.ShapeDtypeStruct((N_CHIPS, t_len, cw), BF16),
                   jax.ShapeDtypeStruct((t_len, d), BF16), jax.ShapeDtypeStruct((SUBLANES, d), F32),
                   jax.ShapeDtypeStruct((SUBLANES, cw), F32), jax.ShapeDtypeStruct((N_GROUPS, dg, dg), F32)),
        grid=(nt,),
        in_specs=[rtile(d), rtile(d), rtile(d), rtile(5 * cw)] + [VMEM_SPEC] * 7
        + [pl.BlockSpec(memory_space=pl.ANY)],
        out_specs=(rtile(d), pl.BlockSpec((N_CHIPS, tm, cw), lambda i: (0, nt - 1 - i, 0)), rtile(d),
                   whole((SUBLANES, d)), whole((SUBLANES, cw)), whole((N_GROUPS, dg, dg))),
        scratch_shapes=[pltpu.VMEM((tm + HALO, cw), F32), pltpu.VMEM((tm + HALO, cw), F32),
                        pltpu.VMEM((5, SUBLANES, d), F32), pltpu.VMEM((4, SUBLANES, cw), F32),
                        pltpu.VMEM((N_GROUPS, dg, dg), F32)],
        compiler_params=_params(dimension_semantics=("arbitrary",)),
    )(x, mix, dx1, zs, mod, conv_w, w_in4, w_pool, pool_scale, w_out, ln_g, after)


def _mix_wgrad(h1, dz4, cat, dmix, small, dmod_rows, tt):
    t_len, d = h1.shape
    cw = dz4.shape[2]
    nt = t_len // tt
    ro = (2 * cw) // N_CHIPS
    rh = (d // 2, ro // 2)
    n_small = len(small)

    def body(h1_ref, dz_ref, cat_ref, dmix_ref, *rest):
        p_refs = rest[:n_small]
        qi_ref, qo_ref, gi_ref, go_ref = rest[n_small:n_small + 4]
        sum_refs = rest[n_small + 4:2 * n_small + 4]
        dmod_ref, dmod_sum_ref = rest[2 * n_small + 4:2 * n_small + 6]
        scratch = rest[2 * n_small + 6:]
        acc_i, acc_o, rcv_i, rcv_o = scratch[:4]
        all_bufs = scratch[4:4 + n_small]
        s_sem, r_sem, s_small, r_small_sem = scratch[4 + n_small:]
        t = pl.program_id(0)
        x, y, c = _mesh_pos()
        chip = 2 * x + y
        groups = [(p_refs[k], sum_refs[k], None, all_bufs[k]) for k in range(n_small)]
        small_start, small_finish = _small_allreduce_steps(groups, s_small, r_small_sem)
        accs, rcvs, q_refs, g_refs = (acc_i, acc_o), (rcv_i, rcv_o), (qi_ref, qo_ref), (gi_ref, go_ref)

        @pl.when(t == 0)
        def _():
            small_start()
            acc_i[...] = jnp.zeros(acc_i.shape, F32)
            acc_o[...] = jnp.zeros(acc_o.shape, F32)

        def accumulate(j):
            acc_i[j] += _dot_tn(h1_ref[...], dz_ref[j])
            acc_o[j] += _dot_tn(cat_ref[:, j * ro:(j + 1) * ro], dmix_ref[...])

        def to_sibling(a, j):
            theirs = pl.multiple_of((1 - c) * rh[a], rh[a])
            return pltpu.make_async_remote_copy(accs[a].at[j].at[pl.ds(theirs, rh[a]), :], rcvs[a].at[j],
                                                s_sem.at[a * N_CHIPS + j], r_sem.at[a * N_CHIPS + j],
                                                device_id=(x, y, 1 - c), device_id_type=MESH)

        @pl.when(t < nt - 1)
        def _():
            for j in range(N_CHIPS):
                accumulate(j)

        @pl.when(t == nt - 1)
        def _():
            for j in range(N_CHIPS):
                accumulate(j)
                for a in range(2):
                    to_sibling(a, j).start()
            for a in range(2):
                mine = pl.multiple_of(c * rh[a], rh[a])
                for j in range(N_CHIPS):
                    to_sibling(a, j).wait()
                    q_refs[a][j] = (accs[a][j, pl.ds(mine, rh[a]), :] + rcvs[a][j]).astype(BF16)
                g_refs[a][chip] = q_refs[a][chip]
            small_finish()
            for k, (arr, row) in enumerate(dmod_rows):
                cols = slice(k * d, (k + 1) * d)
                dmod_sum_ref[:, cols] = sum_refs[arr][row:row + 1, :]
                for b in range(N_DEV):
                    dmod_ref[b:b + 1, cols] = all_bufs[arr][b, row:row + 1, :]

    stacks = ((N_CHIPS, rh[0], cw), (N_CHIPS, rh[1], d))
    n_cp = n_small * (N_DEV - 1)
    n_mod = len(dmod_rows) * d
    return pl.pallas_call(
        body, name="mix_wgrad",
        out_shape=tuple(jax.ShapeDtypeStruct(s, BF16) for s in stacks) * 2
        + tuple(jax.ShapeDtypeStruct(a.shape, F32) for a in small)
        + (jax.ShapeDtypeStruct((N_DEV, n_mod), F32), jax.ShapeDtypeStruct((1, n_mod), F32)),
        grid=(nt,),
        in_specs=[pl.BlockSpec((tt, d), lambda t: (t, 0)), pl.BlockSpec((N_CHIPS, tt, cw), lambda t: (0, t, 0)),
                  pl.BlockSpec((tt, 2 * cw), lambda t: (t, 0)), pl.BlockSpec((tt, d), lambda t: (t, 0))]
        + [VMEM_SPEC] * n_small,
        out_specs=(VMEM_SPEC,) * (6 + n_small),
        scratch_shapes=[pltpu.VMEM((N_CHIPS, d, cw), F32), pltpu.VMEM((N_CHIPS, ro, d), F32),
                        pltpu.VMEM(stacks[0], F32), pltpu.VMEM(stacks[1], F32)]
        + [pltpu.VMEM((N_DEV,) + a.shape, F32) for a in small]
        + [_dma_sems(2 * N_CHIPS), _dma_sems(2 * N_CHIPS), _dma_sems(n_cp), _dma_sems(n_cp)],
        compiler_params=_params(dimension_semantics=("arbitrary",)),
    )(h1, dz4, cat, dmix, *small)


def kernel(x, c, w_ada, b_ada, w_in, conv_w, w_pool, pool_scale, w_out, ln1_g, ln1_b, w_mlp_in, w_mlp_out, ln2_g, ln2_b, loss_target, m_w_ada, m_b_ada, m_w_in, m_conv_w, m_w_pool, m_pool_scale, m_w_out, m_ln1_g, m_ln1_b, m_w_mlp_in, m_w_mlp_out, m_ln2_g, m_ln2_b, v_w_ada, v_b_ada, v_w_in, v_conv_w, v_w_pool, v_pool_scale, v_w_out, v_ln1_g, v_ln1_b, v_w_mlp_in, v_w_mlp_out, v_ln2_g, v_ln2_b):
    t_len, d = x.shape[1], x.shape[2]
    cw = w_in.shape[2]
    cq = conv_w.shape[2]
    dg = w_pool.shape[2]
    assert cw == N_GROUPS * dg and cq * N_CHIPS == cw and LANES % cq == 0
    tm_mix = min(MIX_TOKEN_TILE, t_len)
    tm_mlp = min(MLP_TOKEN_TILE, t_len)
    tt = min(WGRAD_TILE, t_len // 4)
    chip = 2 * lax.axis_index("x") + lax.axis_index("y")
    pos = jnp.stack([lax.axis_index("c"), chip]).astype(jnp.int32)

    x2, tgt = x[0], loss_target[0]
    big_w = [w_in[0], w_out[0], w_mlp_in[0], w_mlp_out[0]]
    big_m = [m_w_in[0], m_w_out[0], m_w_mlp_in[0], m_w_mlp_out[0]]
    big_v = [v_w_in[0], v_w_out[0], v_w_mlp_in[0], v_w_mlp_out[0]]
    names = ["w_in", "w_out", "w_mlp_in", "w_mlp_out"]

    mod, cond_all, conv_full, w_in4, w_out4, w_mi_own, w_mo_own = _setup_exchange(
        c, w_ada[0], b_ada, conv_w[0], big_w, 2)
    w_out_full = w_out4.reshape(2 * cw, d)
    w_pool_bf = w_pool[0].astype(BF16)

    w_mi4, w_mo4, x1, zs, mix, h1, cat = _fwd_mix(x2, mod, conv_full, w_in4, w_pool_bf, pool_scale, w_out_full,
                                                  ln1_g, ln1_b, [w_mi_own, w_mo_own], tm_mix)
    dx1, da4, s4, df, h2, st2 = _mlp_fwd_bwd(x1, tgt, mod, w_mi4, w_mo4, ln2_g, ln2_b, tm_mlp)
    mlp_qi, mlp_qo, mlp_gi, mlp_go = _mlp_wgrad(h2, da4, s4, df, pos, tt)
    mlp_started = _split_start([mlp_qi, mlp_qo, mlp_gi, mlp_go], _chunk_exchange_plan(2), 2 * (N_CHIPS - 1),
                               "exchange_mlp")
    grad_x, dz4, dmix, st1, stc, dw_pool = _bwd_mix(x2, mix, dx1, zs, mod, conv_full, w_in4, w_pool_bf, pool_scale,
                                                    w_out_full, ln1_g, mlp_started[-1], tm_mix)

    dmod_rows = [(0, 4), (0, 3), (0, 2), (1, 4), (1, 3), (1, 2)]
    mix_qi, mix_qo, mix_gi, mix_go, t1, t2, tc, g_w_pool, dmod_all, g_b_ada = _mix_wgrad(
        h1, dz4, cat, dmix, [st1, st2, stc, dw_pool], dmod_rows, tt)
    loss = t2[5, 0]
    g_ln1_g, g_ln1_b, g_ln2_g, g_ln2_b = t1[0:1], t1[1:2], t2[0:1], t2[1:2]
    g_pool_scale = tc[3:4]
    g_conv = lax.dynamic_slice_in_dim(tc[0:3], chip * cq, cq, axis=1)
    na = w_ada.shape[2]
    dmod_mine = lax.dynamic_slice_in_dim(dmod_all, chip * na, na, axis=1)

    exchange, n_cp = _chunk_exchange_plan(2), 2 * (N_CHIPS - 1)
    mlp_got = _split_wait(mlp_started, exchange, t1, "exchange_mlp")[2:]
    mix_started = _split_start([mix_qi, mix_qo, mix_gi, mix_go], exchange, n_cp, "exchange_mix")
    mlp_halves = [_sum_chips(g, pos, mix_started[-1], "rs_total_" + n) for g, n in zip(mlp_got, names[2:])]
    join = _sibling_join_plan([h.shape for h in mlp_halves])
    join_started = _split_start(mlp_halves, join, 2, "join_mlp")
    g_ada, d_ada, nm_ada, nv_ada = _adamw_ada(cond_all, dmod_mine, w_ada[0], m_w_ada[0], v_w_ada[0],
                                              join_started[-1])
    mlp_g = _split_wait(join_started, join, d_ada, "join_mlp")
    mlp_out = [_adamw_big(g, w, m, v, "adamw_" + n)
               for g, w, m, v, n in zip(mlp_g, big_w[2:], big_m[2:], big_v[2:], names[2:])]
    mix_got = _split_wait(mix_started, exchange, mlp_out[1][0], "exchange_mix")[2:]
    mix_halves = [_sum_chips(g, pos, d_ada, "rs_total_" + n) for g, n in zip(mix_got, names[:2])]
    mix_g = _sibling_join(mix_halves, "mix")
    mix_out = [_adamw_big(g, w, m, v, "adamw_" + n)
               for g, w, m, v, n in zip(mix_g, big_w[:2], big_m[:2], big_v[:2], names[:2])]
    big_g = list(mix_g) + list(mlp_g)
    big_out = mix_out + mlp_out

    small_g = [g_b_ada, g_conv, g_w_pool.reshape(-1, dg), g_pool_scale, g_ln1_g, g_ln1_b, g_ln2_g, g_ln2_b]
    small_w = [b_ada, conv_w[0], w_pool.reshape(-1, dg), pool_scale, ln1_g, ln1_b, ln2_g, ln2_b]
    small_m = [m_b_ada, m_conv_w[0], m_w_pool.reshape(-1, dg), m_pool_scale, m_ln1_g, m_ln1_b, m_ln2_g, m_ln2_b]
    small_v = [v_b_ada, v_conv_w[0], v_w_pool.reshape(-1, dg), v_pool_scale, v_ln1_g, v_ln1_b, v_ln2_g, v_ln2_b]
    sm = _adamw_small(small_g, small_w, small_m, small_v)
    ns = len(small_g)
    s_delta, s_m, s_v = sm[:ns], sm[ns:2 * ns], sm[2 * ns:]

    def assemble(ada, small, big):
        return [ada[None], small[0], big[0][None], small[1][None], small[2].reshape(w_pool.shape), small[3],
                big[1][None], small[4], small[5], big[2][None], big[3][None], small[6], small[7]]

    grads = assemble(g_ada, small_g, big_g)
    deltas = assemble(d_ada, s_delta, [o[0] for o in big_out])
    new_m = assemble(nm_ada, s_m, [o[1] for o in big_out])
    new_v = assemble(nv_ada, s_v, [o[2] for o in big_out])
    return (loss, grad_x[None], *grads, *deltas, *new_m, *new_v)
```

```python
import jax
import jax.numpy as jnp
from jax import lax
from jax.experimental import pallas as pl
from jax.experimental.pallas import tpu as pltpu

F32 = jnp.float32
BF16 = jnp.bfloat16
MESH = pl.DeviceIdType.MESH

LN_EPS = 1e-5
DEEPNORM_ALPHA = 2.0 ** 0.25
POOL_WINDOWS = (2, 4, 8, 16)
N_GROUPS = len(POOL_WINDOWS)
HALO = 16
N_CHIPS = 4
N_DEV = 8
LANES = 128
SUBLANES = 8
VMEM_LIMIT = 56 * 1024 * 1024
MIX_TOKEN_TILE = 512
MLP_TOKEN_TILE = 256
WGRAD_TILE = 1024

ADAM_LR = 0.001
ADAM_B1 = 0.9
ADAM_B2 = 0.999
ADAM_EPS = 1e-08
ADAM_WD = 0.01
ADAM_STEP = 10

VMEM_SPEC = pl.BlockSpec(memory_space=pltpu.VMEM)
HBM_SPEC = pl.BlockSpec(memory_space=pltpu.HBM)


def _dot(a, b):
    return jnp.dot(a, b, preferred_element_type=F32)


def _dot_nt(a, b):
    return lax.dot_general(a, b, (((1,), (1,)), ((), ())), preferred_element_type=F32)


def _dot_tn(a, b):
    return lax.dot_general(a, b, (((0,), (0,)), ((), ())), preferred_element_type=F32)


def _fold8(v):
    r, n = v.shape
    return jnp.sum(v.reshape(r // SUBLANES, SUBLANES, n), axis=0)


def _mesh_pos():
    return lax.axis_index("x"), lax.axis_index("y"), lax.axis_index("c")


def _flip(v, bit):
    return 1 - v if bit else v


def _params(**kw):
    return pltpu.CompilerParams(vmem_limit_bytes=VMEM_LIMIT, **kw)


def _dma_sems(n):
    return pltpu.SemaphoreType.DMA((n,))


def _stack_gather(stacks, rows, s_ici, r_ici, s_d2d, r_d2d):
    x, y, c = _mesh_pos()
    chip = 2 * x + y
    pairs = [(i, m) for i in range(len(stacks)) for m in range(1, N_CHIPS)]

    def blk(i, slot, which):
        rh = rows[i] // 2
        return stacks[i].at[slot].at[pl.ds(which * rh, rh), :]

    def other(m):
        return 2 * _flip(x, m & 2) + _flip(y, m & 1)

    def ici(i, m, slot, to):
        k = i * (N_CHIPS - 1) + m - 1
        return pltpu.make_async_remote_copy(blk(i, slot, c), blk(i, slot, c), s_ici.at[k], r_ici.at[k],
                                            device_id=to, device_id_type=MESH)

    def d2d(i, m, which, to):
        k = i * (N_CHIPS - 1) + m - 1
        return pltpu.make_async_remote_copy(blk(i, other(m), which), blk(i, other(m), which),
                                            s_d2d.at[k], r_d2d.at[k], device_id=to, device_id_type=MESH)

    def start():
        for i, m in pairs:
            ici(i, m, chip, (_flip(x, m & 2), _flip(y, m & 1), c)).start()

    def forward(which=None):
        for i, m in pairs:
            if which is None or i == which:
                ici(i, m, other(m), (x, y, c)).wait_recv()
                d2d(i, m, c, (x, y, 1 - c)).start()

    def finish():
        for i, m in pairs:
            d2d(i, m, 1 - c, (x, y, c)).wait_recv()
        for i, m in pairs:
            ici(i, m, chip, (x, y, c)).wait_send()
            d2d(i, m, c, (x, y, c)).wait_send()

    return start, forward, finish


def _chunk_exchange(qs, gots, s_sem, r_sem):
    x, y, c = _mesh_pos()
    chip = 2 * x + y
    pairs = [(i, m) for i in range(len(qs)) for m in range(1, N_CHIPS)]

    def other(m):
        return 2 * _flip(x, m & 2) + _flip(y, m & 1)

    def send(i, m, to):
        k = i * (N_CHIPS - 1) + m - 1
        return pltpu.make_async_remote_copy(qs[i].at[other(m)], gots[i].at[chip], s_sem.at[k], r_sem.at[k],
                                            device_id=to, device_id_type=MESH)

    def arrival(i, m):
        k = i * (N_CHIPS - 1) + m - 1
        return pltpu.make_async_remote_copy(qs[i].at[other(m)], gots[i].at[other(m)], s_sem.at[k], r_sem.at[k],
                                            device_id=(x, y, c), device_id_type=MESH)

    def start():
        for i, m in pairs:
            send(i, m, (_flip(x, m & 2), _flip(y, m & 1), c)).start()

    def finish():
        for i, m in pairs:
            arrival(i, m).wait_recv()
        for i, m in pairs:
            send(i, m, (x, y, c)).wait_send()

    return start, finish


def _mod_scratch(d, na, cq):
    return [pltpu.VMEM((SUBLANES, d), F32), pltpu.VMEM((N_DEV, SUBLANES, d), F32),
            pltpu.VMEM((N_CHIPS, SUBLANES, na), F32),
            pltpu.VMEM((SUBLANES, cq), F32), pltpu.VMEM((N_CHIPS, SUBLANES, cq), F32),
            _dma_sems(N_DEV - 1), _dma_sems(N_DEV - 1),
            _dma_sems(N_CHIPS - 1), _dma_sems(N_CHIPS - 1), _dma_sems(N_CHIPS - 1), _dma_sems(N_CHIPS - 1)]


def _mod_steps(d, na, cq):
    def steps(c_ref, w_ref, b_ref, cw_ref, mod_ref, cond_ref, conv_ref,
              csend, cbuf, mbuf, cvsend, cvbuf, s1, r1, s2, r2, s3, r3):
        x, y, c = _mesh_pos()
        me = 4 * x + 2 * y + c
        chip = 2 * x + y
        csend[...] = jnp.broadcast_to(c_ref[...], (SUBLANES, d))
        cbuf[me] = csend[...]
        first = []
        for m in range(1, N_DEV):
            peer = (_flip(x, m & 4), _flip(y, m & 2), _flip(c, m & 1))
            cp = pltpu.make_async_remote_copy(csend, cbuf.at[me], s1.at[m - 1], r1.at[m - 1],
                                              device_id=peer, device_id_type=MESH)
            cp.start()
            first.append(cp)
        cvsend[...] = jnp.zeros((SUBLANES, cq), F32)
        cvsend[0:3, :] = cw_ref[...]
        cvbuf[chip] = cvsend[...]
        for m in range(1, N_DEV):
            src = 4 * _flip(x, m & 4) + 2 * _flip(y, m & 2) + _flip(c, m & 1)
            pltpu.make_async_remote_copy(csend, cbuf.at[src], s1.at[m - 1], r1.at[m - 1],
                                         device_id=(x, y, c), device_id_type=MESH).wait_recv()
        rows = lax.broadcasted_iota(jnp.int32, (SUBLANES, d), 0)
        call = jnp.zeros((SUBLANES, d), F32)
        for b in range(N_DEV):
            call = jnp.where(rows == b, cbuf[b], call)
        cond = call * jax.nn.sigmoid(call)
        cond_ref[...] = cond
        part = jnp.dot(cond, w_ref[...], preferred_element_type=F32, precision=lax.Precision.HIGHEST)
        mbuf[chip] = part
        second = []
        for m in range(1, N_CHIPS):
            peer = (_flip(x, m & 2), _flip(y, m & 1), c)
            cp = pltpu.make_async_remote_copy(mbuf.at[chip], mbuf.at[chip], s2.at[m - 1], r2.at[m - 1],
                                              device_id=peer, device_id_type=MESH)
            cp.start()
            second.append(cp)
            cp = pltpu.make_async_remote_copy(cvsend, cvbuf.at[chip], s3.at[m - 1], r3.at[m - 1],
                                              device_id=peer, device_id_type=MESH)
            cp.start()
            second.append(cp)
        for m in range(1, N_CHIPS):
            src = 2 * _flip(x, m & 2) + _flip(y, m & 1)
            pltpu.make_async_remote_copy(mbuf.at[src], mbuf.at[src], s2.at[m - 1], r2.at[m - 1],
                                         device_id=(x, y, c), device_id_type=MESH).wait_recv()
            pltpu.make_async_remote_copy(cvsend, cvbuf.at[src], s3.at[m - 1], r3.at[m - 1],
                                         device_id=(x, y, c), device_id_type=MESH).wait_recv()
        for cp in first + second:
            cp.wait_send()
        rows_n = lax.broadcasted_iota(jnp.int32, (SUBLANES, na), 0)
        for k in range(N_CHIPS):
            mine = jnp.sum(jnp.where(rows_n == me, mbuf[k], 0.0), axis=0, keepdims=True)
            mod_ref[:, k * na:(k + 1) * na] = jnp.broadcast_to(mine + b_ref[:, k * na:(k + 1) * na], (SUBLANES, na))
            conv_ref[:, k * cq:(k + 1) * cq] = cvbuf[k]

    return steps


def _setup_exchange(c_row, w_ada_s, b_ada, conv_w_s, shards, n_now):
    n = len(shards)
    n_cp = n_now * (N_CHIPS - 1)
    d, na, cq = c_row.shape[1], w_ada_s.shape[1], conv_w_s.shape[1]
    mod_scratch = _mod_scratch(d, na, cq)
    mod_steps = _mod_steps(d, na, cq)

    def body(*refs):
        mod_in, ins = refs[:4], refs[4:4 + n]
        mod_out, outs = refs[4 + n:7 + n], refs[7 + n:7 + 2 * n]
        bufs = refs[7 + 2 * n:7 + 3 * n]
        mod_scr = refs[7 + 3 * n:7 + 3 * n + len(mod_scratch)]
        s_ici, r_ici, s_d2d, r_d2d, s_loc = refs[7 + 3 * n + len(mod_scratch):7 + 3 * n + len(mod_scratch) + 5]
        w_ada_buf, s_in = refs[7 + 3 * n + len(mod_scratch) + 5:7 + 3 * n + len(mod_scratch) + 7]
        late_bufs = refs[7 + 3 * n + len(mod_scratch) + 7:]
        x, y, _ = _mesh_pos()
        chip = 2 * x + y
        loads = [pltpu.make_async_copy(mod_in[1], w_ada_buf, s_in.at[0])]
        loads += [pltpu.make_async_copy(ins[i], late_bufs[i - n_now], s_in.at[1 + i - n_now]) for i in range(n_now, n)]
        for cp in loads:
            cp.start()

        def place(i, src):
            bufs[i][...] = src[...].astype(BF16)
            cp = pltpu.make_async_copy(bufs[i], outs[i].at[chip], s_loc.at[i])
            cp.start()
            return cp

        start, forward, finish = _stack_gather(outs[:n_now], [s.shape[0] for s in shards[:n_now]],
                                               s_ici, r_ici, s_d2d, r_d2d)
        for cp in [place(i, ins[i]) for i in range(n_now)]:
            cp.wait()
        start()
        later = []
        for i in range(n_now, n):
            loads[1 + i - n_now].wait()
            later.append(place(i, late_bufs[i - n_now]))
        loads[0].wait()
        mod_steps(mod_in[0], w_ada_buf, mod_in[2], mod_in[3], *mod_out, *mod_scr)
        forward()
        finish()
        for cp in later:
            cp.wait()

    any_spec = pl.BlockSpec(memory_space=pl.ANY)
    return pl.pallas_call(
        body, name="setup_exchange",
        out_shape=(jax.ShapeDtypeStruct((SUBLANES, N_CHIPS * na), F32), jax.ShapeDtypeStruct((SUBLANES, d), F32),
                   jax.ShapeDtypeStruct((SUBLANES, N_CHIPS * cq), F32))
        + tuple(jax.ShapeDtypeStruct((N_CHIPS,) + s.shape, BF16) for s in shards),
        in_specs=[VMEM_SPEC, any_spec, VMEM_SPEC, VMEM_SPEC] + [VMEM_SPEC] * n_now + [any_spec] * (n - n_now),
        out_specs=(VMEM_SPEC,) * 3 + (HBM_SPEC,) * n,
        scratch_shapes=[pltpu.VMEM(s.shape, BF16) for s in shards] + mod_scratch
        + [_dma_sems(n_cp), _dma_sems(n_cp), _dma_sems(n_cp), _dma_sems(n_cp), _dma_sems(n)]
        + [pltpu.VMEM(w_ada_s.shape, F32), _dma_sems(1 + n - n_now)]
        + [pltpu.VMEM(s.shape, F32) for s in shards[n_now:]],
        compiler_params=_params(),
    )(c_row, w_ada_s, b_ada, conv_w_s, *shards)


def _sibling_join_plan(shapes):
    def plan(refs, s_sem, r_sem):
        x, y, c = _mesh_pos()

        def half(i, which):
            rh = shapes[i][0] // 2
            return refs[i].at[pl.ds(which * rh, rh), :]

        def start():
            for i in range(len(refs)):
                pltpu.make_async_remote_copy(half(i, c), half(i, c), s_sem.at[i], r_sem.at[i],
                                             device_id=(x, y, 1 - c), device_id_type=MESH).start()

        def finish():
            for i in range(len(refs)):
                pltpu.make_async_remote_copy(half(i, 1 - c), half(i, 1 - c), s_sem.at[i], r_sem.at[i],
                                             device_id=(x, y, c), device_id_type=MESH).wait_recv()
            for i in range(len(refs)):
                pltpu.make_async_remote_copy(half(i, c), half(i, c), s_sem.at[i], r_sem.at[i],
                                             device_id=(x, y, c), device_id_type=MESH).wait_send()

        return start, finish

    return plan


def _chunk_exchange_plan(n):
    return lambda refs, s_sem, r_sem: _chunk_exchange(refs[:n], refs[n:], s_sem, r_sem)


def _sibling_join(gs, tag):
    n = len(gs)
    plan = _sibling_join_plan([g.shape for g in gs])

    def body(*refs):
        start, finish = plan(refs[n:2 * n], refs[2 * n], refs[2 * n + 1])
        start()
        finish()

    return pl.pallas_call(
        body, name="rs_sibling_join_" + tag,
        out_shape=tuple(jax.ShapeDtypeStruct(g.shape, g.dtype) for g in gs),
        in_specs=[HBM_SPEC] * n, out_specs=(HBM_SPEC,) * n,
        input_output_aliases={i: i for i in range(n)},
        scratch_shapes=[_dma_sems(n), _dma_sems(n)],
        compiler_params=_params(),
    )(*gs)


SEM_SPEC = pl.BlockSpec(memory_space=pltpu.SEMAPHORE)
SIDE_EFFECT = pltpu.SideEffectType.DATAFLOW_SIDE_EFFECTING


def _split_start(arrays, plan, n_cp, tag):
    n = len(arrays)

    def body(*refs):
        start, _ = plan(refs[n + 2:2 * n + 2], refs[n], refs[n + 1])
        start()
        token = refs[2 * n + 2]
        token[...] = jnp.zeros_like(token)

    through = [pltpu.with_memory_space_constraint(a, pltpu.HBM) for a in arrays]
    return pl.pallas_call(
        body, name=tag + "_start",
        out_shape=(pltpu.SemaphoreType.DMA((n_cp,)), pltpu.SemaphoreType.DMA((n_cp,)))
        + tuple(pltpu.HBM(a.shape, a.dtype) for a in through) + (jax.ShapeDtypeStruct((SUBLANES, LANES), F32),),
        in_specs=[HBM_SPEC] * n, out_specs=(SEM_SPEC, SEM_SPEC) + (HBM_SPEC,) * n + (VMEM_SPEC,),
        input_output_aliases={k: 2 + k for k in range(n)},
        compiler_params=pltpu.CompilerParams(has_side_effects=SIDE_EFFECT),
    )(*through)


def _split_wait(started, plan, after, tag):
    send_sems, recv_sems, arrays = started[0], started[1], started[2:-1]
    n = len(arrays)

    def body(*refs):
        _, finish = plan(refs[:n], refs[n], refs[n + 1])
        finish()

    return pl.pallas_call(
        body, name=tag + "_wait",
        out_shape=tuple(pltpu.HBM(a.shape, a.dtype) for a in arrays),
        in_specs=[HBM_SPEC] * n + [SEM_SPEC, SEM_SPEC, pl.BlockSpec(memory_space=pl.ANY)],
        out_specs=(HBM_SPEC,) * n,
        input_output_aliases={k: k for k in range(n)},
        compiler_params=pltpu.CompilerParams(has_side_effects=SIDE_EFFECT),
    )(*arrays, send_sems, recv_sems, after)


def _small_allreduce_steps(groups, s_sem, r_sem):
    x, y, c = _mesh_pos()
    me = 4 * x + 2 * y + c
    pairs = [(k, m) for k in range(len(groups)) for m in range(1, N_DEV)]

    def send(k, m, to):
        p_ref, all_ref = groups[k][0], groups[k][3]
        i = k * (N_DEV - 1) + m - 1
        return pltpu.make_async_remote_copy(p_ref, all_ref.at[me], s_sem.at[i], r_sem.at[i],
                                            device_id=to, device_id_type=MESH)

    def start():
        for p_ref, _, _, all_ref in groups:
            all_ref[me] = p_ref[...]
        for k, m in pairs:
            send(k, m, (_flip(x, m & 4), _flip(y, m & 2), _flip(c, m & 1))).start()

    def finish():
        for k, m in pairs:
            p_ref, all_ref = groups[k][0], groups[k][3]
            i = k * (N_DEV - 1) + m - 1
            src = 4 * _flip(x, m & 4) + 2 * _flip(y, m & 2) + _flip(c, m & 1)
            pltpu.make_async_remote_copy(p_ref, all_ref.at[src], s_sem.at[i], r_sem.at[i],
                                         device_id=(x, y, c), device_id_type=MESH).wait_recv()
        for k, m in pairs:
            send(k, m, (x, y, c)).wait_send()
        for _, sum_ref, all_out_ref, all_ref in groups:
            sum_ref[...] = (((all_ref[0] + all_ref[1]) + (all_ref[2] + all_ref[3]))
                            + ((all_ref[4] + all_ref[5]) + (all_ref[6] + all_ref[7])))
            if all_out_ref is not None:
                all_out_ref[...] = all_ref[...]

    return start, finish


def _row_block(rows, cap=256):
    rb = min(rows, cap)
    assert rows % rb == 0
    return rb


def _sum_chips(got, pos, after, name):
    _, rh, cc = got.shape
    rb = _row_block(rh)
    nb = rh // rb

    def body(pos_ref, a_ref, after_ref, o_ref):
        a = a_ref[...].astype(F32)
        o_ref[...] = (a[0] + a[1]) + (a[2] + a[3])

    return pl.pallas_call(
        body, name=name,
        out_shape=jax.ShapeDtypeStruct((2 * rh, cc), F32),
        grid_spec=pltpu.PrefetchScalarGridSpec(
            num_scalar_prefetch=1, grid=(nb,),
            in_specs=[pl.BlockSpec((N_CHIPS, rb, cc), lambda i, p: (0, i, 0)),
                      pl.BlockSpec(memory_space=pl.ANY)],
            out_specs=pl.BlockSpec((rb, cc), lambda i, p: (p[0] * nb + i, 0))),
        compiler_params=_params(dimension_semantics=("arbitrary",)),
    )(pos, got, after)


def _adamw_math(w, g, m, v):
    m = ADAM_B1 * m + (1.0 - ADAM_B1) * g
    v = ADAM_B2 * v + (1.0 - ADAM_B2) * jnp.square(g)
    m_hat = m / (1.0 - ADAM_B1 ** ADAM_STEP)
    v_hat = v / (1.0 - ADAM_B2 ** ADAM_STEP)
    delta = -ADAM_LR * (m_hat / (jnp.sqrt(v_hat) + ADAM_EPS) + ADAM_WD * w)
    return delta, m, v


def _adamw_big(g, w, m, v, name):
    r, cc = w.shape
    rb = _row_block(r)

    def body(g_ref, w_ref, m_ref, v_ref, d_ref, mo_ref, vo_ref):
        d, mn, vn = _adamw_math(w_ref[...], g_ref[...], m_ref[...], v_ref[...])
        d_ref[...] = d
        mo_ref[...] = mn
        vo_ref[...] = vn

    spec = pl.BlockSpec((rb, cc), lambda i: (i, 0))
    return pl.pallas_call(
        body, name=name,
        out_shape=(jax.ShapeDtypeStruct((r, cc), F32),) * 3,
        grid=(r // rb,), in_specs=[spec] * 4, out_specs=(spec,) * 3,
        compiler_params=_params(dimension_semantics=("arbitrary",)),
    )(g, w, m, v)


def _adamw_ada(cond, dmod, w, m, v, after):
    d, na = w.shape
    rb = _row_block(d, 128)

    def body(c_ref, dm_ref, w_ref, m_ref, v_ref, after_ref, g_ref, d_ref, mo_ref, vo_ref):
        g = lax.dot_general(c_ref[...], dm_ref[...], (((0,), (0,)), ((), ())), preferred_element_type=F32,
                            precision=lax.Precision.HIGHEST)
        dl, mn, vn = _adamw_math(w_ref[...], g, m_ref[...], v_ref[...])
        g_ref[...] = g
        d_ref[...] = dl
        mo_ref[...] = mn
        vo_ref[...] = vn

    spec = pl.BlockSpec((rb, na), lambda i: (i, 0))
    return pl.pallas_call(
        body, name="adamw_w_ada",
        out_shape=(jax.ShapeDtypeStruct((d, na), F32),) * 4,
        grid=(d // rb,),
        in_specs=[pl.BlockSpec((N_DEV, rb), lambda i: (0, i)), pl.BlockSpec((N_DEV, na), lambda i: (0, 0)),
                  spec, spec, spec, pl.BlockSpec(memory_space=pl.ANY)],
        out_specs=(spec,) * 4,
        compiler_params=_params(dimension_semantics=("arbitrary",)),
    )(cond, dmod, w, m, v, after)


def _adamw_small(pos, stats, sources, ws, ms, vs):
    n = len(ws)
    direct = [s for s in sources if not isinstance(s, tuple)]
    n_cut = n - len(direct)

    def body(pos_ref, *refs):
        stat_refs, direct_refs = refs[:len(stats)], refs[len(stats):len(stats) + len(direct)]
        w_refs, m_refs, v_refs = (refs[len(stats) + len(direct) + k * n:len(stats) + len(direct) + (k + 1) * n]
                                  for k in range(3))
        outs = refs[len(stats) + len(direct) + 3 * n:]
        cut_refs, res = outs[:n_cut], outs[n_cut:]
        chip = pos_ref[1]
        i_direct = i_cut = 0
        for i, src in enumerate(sources):
            if not isinstance(src, tuple):
                g = direct_refs[i_direct][...]
                i_direct += 1
            else:
                if src[0] == "rows":
                    g = stat_refs[src[1]][src[2]:src[3], :]
                else:
                    _, k, lo, hi, width = src
                    g = jnp.zeros((hi - lo, width), F32)
                    for j in range(N_CHIPS):
                        g = jnp.where(chip == j, stat_refs[k][lo:hi, j * width:(j + 1) * width], g)
                cut_refs[i_cut][...] = g
                i_cut += 1
            d, mn, vn = _adamw_math(w_refs[i][...], g, m_refs[i][...], v_refs[i][...])
            res[i][...] = d
            res[n + i][...] = mn
            res[2 * n + i][...] = vn

    shapes = tuple(jax.ShapeDtypeStruct(w.shape, F32) for w in ws)
    cut_shapes = tuple(jax.ShapeDtypeStruct(w.shape, F32) for w, s in zip(ws, sources) if isinstance(s, tuple))
    out = pl.pallas_call(
        body, name="adamw_small",
        out_shape=cut_shapes + shapes * 3,
        in_specs=[pl.BlockSpec(memory_space=pltpu.SMEM)] + [VMEM_SPEC] * (len(stats) + len(direct) + 3 * n),
        out_specs=(VMEM_SPEC,) * (n_cut + 3 * n),
        compiler_params=_params(),
    )(pos, *stats, *direct, *ws, *ms, *vs)
    return out[:n_cut], out[n_cut:]


def _layer_norm_fwd(r):
    mu = jnp.mean(r, axis=-1, keepdims=True)
    xc = r - mu
    var = jnp.mean(jnp.square(xc), axis=-1, keepdims=True)
    rstd = lax.rsqrt(var + LN_EPS)
    return xc * rstd, rstd


def _layer_norm_bwd(dy, xhat, rstd, gain):
    dxh = dy * gain
    m1 = jnp.mean(dxh, axis=-1, keepdims=True)
    m2 = jnp.mean(dxh * xhat, axis=-1, keepdims=True)
    return rstd * (dxh - m1 - xhat * m2)


def _inv_count(tile, tm, win):
    t = (tile * tm + lax.broadcasted_iota(jnp.int32, (tm, 1), 0) + 1).astype(F32)
    return 1.0 / jnp.minimum(t, float(win))


def _fwd_mix(x, mod, conv_w, w_in4, w_pool, pool_scale, w_out, ln_g, ln_b, late_stacks, tm):
    t_len, d = x.shape
    cw = w_in4.shape[2]
    dg = cw // N_GROUPS
    nt = t_len // tm
    n_late = len(late_stacks)
    n_cp = n_late * (N_CHIPS - 1)
    pass_steps = [min(((k + 1) * nt) // n_late + (1 if k + 1 < n_late else -2), nt - 1) if nt > 2 else nt - 1
                  for k in range(n_late)]
    pass_steps = [max(p, 0) for p in pass_steps]

    def body(x_ref, mod_ref, cw_ref, win_ref, wp_ref, ps_ref, wout_ref, g_ref, b_ref, *rest):
        late = rest[n_late:2 * n_late]
        x1_ref, zs_ref, mix_ref, h1_ref, cat_ref = rest[2 * n_late:2 * n_late + 5]
        ubuf, vbuf, s_ici, r_ici, s_d2d, r_d2d = rest[2 * n_late + 5:]
        i = pl.program_id(0)
        start, forward, finish = _stack_gather(late, [s.shape[1] for s in late_stacks], s_ici, r_ici, s_d2d, r_d2d)

        @pl.when(i == 0)
        def _():
            start()
            ubuf[0:HALO, :] = jnp.zeros((HALO, cw), F32)
            vbuf[0:HALO, :] = jnp.zeros((HALO, cw), F32)

        xv = x_ref[...]
        sh1, sc1, g1 = mod_ref[0:1, 0:d], mod_ref[0:1, d:2 * d], mod_ref[0:1, 2 * d:3 * d]
        h1 = (xv * (1.0 + sc1) + sh1).astype(BF16)
        h1_ref[...] = h1
        zb = _dot(h1, win_ref[0])
        zc = _dot(h1, win_ref[1])
        zv = _dot(h1, win_ref[2])
        zp = _dot(h1, win_ref[3])
        u = zc * zv
        ubuf[HALO:HALO + tm, :] = u
        u1 = ubuf[pl.ds(HALO - 1, tm), :]
        u2 = ubuf[pl.ds(HALO - 2, tm), :]
        conv = cw_ref[0:1, :] * u2 + cw_ref[1:2, :] * u1 + cw_ref[2:3, :] * u
        ubuf[0:HALO, :] = ubuf[tm:tm + HALO, :]
        yc = zb * conv
        vbuf[HALO:HALO + tm, :] = zp
        ps, pms = [], []
        for gi, win in enumerate(POOL_WINDOWS):
            sl = slice(gi * dg, (gi + 1) * dg)
            acc = zp[:, sl]
            for s in range(1, win):
                acc = acc + vbuf[pl.ds(HALO - s, tm), sl]
            p_g = acc * _inv_count(i, tm, win) - zp[:, sl]
            ps.append(p_g)
            pms.append(_dot(p_g.astype(BF16), wp_ref[gi].astype(BF16)))
        vbuf[0:HALO, :] = vbuf[tm:tm + HALO, :]
        pooled = jnp.concatenate(ps, axis=1)
        yp = jnp.concatenate(pms, axis=1) * ps_ref[...]
        cat = jnp.concatenate([yc, yp], axis=1).astype(BF16)
        cat_ref[...] = cat
        mix = _dot(cat, wout_ref[...])
        mix_ref[...] = mix
        xhat, _ = _layer_norm_fwd(DEEPNORM_ALPHA * xv + (1.0 + g1) * mix)
        x1_ref[...] = xhat * g_ref[...] + b_ref[...]
        zs_ref[:, 0 * cw:1 * cw] = zb.astype(BF16)
        zs_ref[:, 1 * cw:2 * cw] = zc.astype(BF16)
        zs_ref[:, 2 * cw:3 * cw] = zv.astype(BF16)
        zs_ref[:, 3 * cw:4 * cw] = conv.astype(BF16)
        zs_ref[:, 4 * cw:5 * cw] = pooled.astype(BF16)

        for k in range(n_late):
            @pl.when(i == pass_steps[k])
            def _():
                forward(k)

        @pl.when(i == nt - 1)
        def _():
            finish()

    tile = lambda w: pl.BlockSpec((tm, w), lambda i: (i, 0))
    n_in = 9
    return pl.pallas_call(
        body, name="fwd_mix",
        out_shape=tuple(jax.ShapeDtypeStruct(s.shape, s.dtype) for s in late_stacks)
        + (jax.ShapeDtypeStruct((t_len, d), F32), jax.ShapeDtypeStruct((t_len, 5 * cw), BF16),
           jax.ShapeDtypeStruct((t_len, d), F32), jax.ShapeDtypeStruct((t_len, d), BF16),
           jax.ShapeDtypeStruct((t_len, 2 * cw), BF16)),
        grid=(nt,),
        in_specs=[tile(d)] + [VMEM_SPEC] * 8 + [HBM_SPEC] * n_late,
        out_specs=(HBM_SPEC,) * n_late + (tile(d), tile(5 * cw), tile(d), tile(d), tile(2 * cw)),
        input_output_aliases={n_in + k: k for k in range(n_late)},
        scratch_shapes=[pltpu.VMEM((tm + HALO, cw), F32), pltpu.VMEM((tm + HALO, cw), F32),
                        _dma_sems(n_cp), _dma_sems(n_cp), _dma_sems(n_cp), _dma_sems(n_cp)],
        compiler_params=_params(dimension_semantics=("arbitrary",)),
    )(x, mod, conv_w, w_in4, w_pool, pool_scale, w_out, ln_g, ln_b, *late_stacks)


def _mlp_fwd_bwd(x1, tgt, mod, w_mi4, w_mo4, ln_g, ln_b, tm):
    t_len, d = x1.shape
    fq = w_mi4.shape[2]
    nt = t_len // tm

    def body(x1_ref, tgt_ref, mod_ref, wi_ref, wo_ref, g_ref, b_ref,
             dx1_ref, da_ref, s_ref, df_ref, h2_ref, st_ref, relu_buf, acc):
        i = pl.program_id(0)

        @pl.when(i == 0)
        def _():
            acc[...] = jnp.zeros(acc.shape, F32)

        x1v = x1_ref[...]
        sh2, sc2, g2 = mod_ref[0:1, 3 * d:4 * d], mod_ref[0:1, 4 * d:5 * d], mod_ref[0:1, 5 * d:6 * d]
        h2 = (x1v * (1.0 + sc2) + sh2).astype(BF16)
        h2_ref[...] = h2
        f = jnp.zeros((tm, d), F32)
        for j in range(N_CHIPS):
            a = jnp.maximum(_dot(h2, wi_ref[j]), 0.0)
            relu_buf[j] = a
            s = (a * a).astype(BF16)
            s_ref[j] = s
            f = f + _dot(s, wo_ref[j])
        xhat, rstd = _layer_norm_fwd(DEEPNORM_ALPHA * x1v + (1.0 + g2) * f)
        gain = g_ref[...]
        err = xhat * gain + b_ref[...] - tgt_ref[...]
        dy = err * (1.0 / d)
        dr2 = _layer_norm_bwd(dy, xhat, rstd, gain)
        df = ((1.0 + g2) * dr2).astype(BF16)
        df_ref[...] = df
        dh2 = jnp.zeros((tm, d), F32)
        for j in range(N_CHIPS):
            ds = _dot_nt(df, wo_ref[j])
            da = (ds * (2.0 * relu_buf[j])).astype(BF16)
            da_ref[j] = da
            dh2 = dh2 + _dot_nt(da, wi_ref[j])
        dx1_ref[...] = DEEPNORM_ALPHA * dr2 + dh2 * (1.0 + sc2)
        acc[0] += _fold8(dy * xhat)
        acc[1] += _fold8(dy)
        acc[2] += _fold8(dr2 * f)
        acc[3] += _fold8(dh2 * x1v)
        acc[4] += _fold8(dh2)
        acc[5] += _fold8(err * err)

        @pl.when(i == nt - 1)
        def _():
            for k in range(5):
                st_ref[k:k + 1, :] = jnp.sum(acc[k], axis=0, keepdims=True)
            loss = jnp.sum(acc[5]) * (0.5 / d)
            st_ref[5:6, :] = jnp.broadcast_to(loss, (1, d))
            st_ref[6:8, :] = jnp.zeros((2, d), F32)

    tile = lambda w: pl.BlockSpec((tm, w), lambda i: (i, 0))
    tile4 = pl.BlockSpec((N_CHIPS, tm, fq), lambda i: (0, i, 0))
    return pl.pallas_call(
        body, name="mlp_fwd_bwd",
        out_shape=(jax.ShapeDtypeStruct((t_len, d), F32),
                   jax.ShapeDtypeStruct((N_CHIPS, t_len, fq), BF16), jax.ShapeDtypeStruct((N_CHIPS, t_len, fq), BF16),
                   jax.ShapeDtypeStruct((t_len, d), BF16), jax.ShapeDtypeStruct((t_len, d), BF16),
                   jax.ShapeDtypeStruct((SUBLANES, d), F32)),
        grid=(nt,),
        in_specs=[tile(d), tile(d)] + [VMEM_SPEC] * 5,
        out_specs=(tile(d), tile4, tile4, tile(d), tile(d), pl.BlockSpec((SUBLANES, d), lambda i: (0, 0))),
        scratch_shapes=[pltpu.VMEM((N_CHIPS, tm, fq), F32), pltpu.VMEM((6, SUBLANES, d), F32)],
        compiler_params=_params(dimension_semantics=("arbitrary",)),
    )(x1, tgt, mod, w_mi4, w_mo4, ln_g, ln_b)


def _mlp_wgrad(h2, da4, s4, df, pos, tt):
    t_len, d = h2.shape
    fq = da4.shape[2]
    nt = t_len // tt
    assert nt >= 3
    rh = (d // 2, fq // 2)
    last = N_CHIPS - 1

    def body(pos_ref, h2_ref, da_ref, s_ref, df_ref, qi_ref, qo_ref, gi_ref, go_ref,
             acc_i, acc_o, rcv_i, rcv_o, s_sem, r_sem):
        j, t = pl.program_id(0), pl.program_id(1)
        slot = lax.rem(j, 2)
        x, y, c = _mesh_pos()
        accs, rcvs, q_refs, g_refs = (acc_i, acc_o), (rcv_i, rcv_o), (qi_ref, qo_ref), (gi_ref, go_ref)

        def to_sibling(a, sl):
            theirs = pl.multiple_of((1 - c) * rh[a], rh[a])
            return pltpu.make_async_remote_copy(accs[a].at[sl].at[pl.ds(theirs, rh[a]), :], rcvs[a].at[sl],
                                                s_sem.at[2 * sl + a], r_sem.at[2 * sl + a],
                                                device_id=(x, y, 1 - c), device_id_type=MESH)

        def emit(chunk, sl):
            for a in range(2):
                to_sibling(a, sl).wait()
                mine = pl.multiple_of(c * rh[a], rh[a])
                q = (accs[a][sl, pl.ds(mine, rh[a]), :] + rcvs[a][sl]).astype(BF16)
                q_refs[a][...] = q

                @pl.when(chunk == pos_ref[1])
                def _():
                    g_refs[a][...] = q

        @pl.when(t == 0)
        def _():
            acc_i[slot] = jnp.zeros((d, fq), F32)
            acc_o[slot] = jnp.zeros((fq, d), F32)

        acc_i[slot] += _dot_tn(h2_ref[...], da_ref[...])
        acc_o[slot] += _dot_tn(s_ref[...], df_ref[...])

        @pl.when((t == 1) & (j >= 1))
        def _():
            emit(j - 1, 1 - slot)

        @pl.when(t == nt - 1)
        def _():
            for a in range(2):
                to_sibling(a, slot).start()

        @pl.when((t == nt - 1) & (j == last))
        def _():
            emit(j, slot)

    q_index = lambda j, t, p: (jnp.where(t == nt - 1, j, jnp.maximum(j - 1, 0)), 0, 0)
    got_index = lambda j, t, p: (p[1], 0, 0)
    blocks = ((None, rh[0], fq), (None, rh[1], d))
    return pl.pallas_call(
        body, name="mlp_wgrad",
        out_shape=tuple(jax.ShapeDtypeStruct((N_CHIPS,) + b[1:], BF16) for b in blocks) * 2,
        grid_spec=pltpu.PrefetchScalarGridSpec(
            num_scalar_prefetch=1, grid=(N_CHIPS, nt),
            in_specs=[pl.BlockSpec((tt, d), lambda j, t, p: (t, 0)),
                      pl.BlockSpec((None, tt, fq), lambda j, t, p: (j, t, 0)),
                      pl.BlockSpec((None, tt, fq), lambda j, t, p: (j, t, 0)),
                      pl.BlockSpec((tt, d), lambda j, t, p: (t, 0))],
            out_specs=tuple(pl.BlockSpec(b, q_index) for b in blocks)
            + tuple(pl.BlockSpec(b, got_index) for b in blocks),
            scratch_shapes=[pltpu.VMEM((2, d, fq), F32), pltpu.VMEM((2, fq, d), F32),
                            pltpu.VMEM((2, rh[0], fq), F32), pltpu.VMEM((2, rh[1], d), F32),
                            _dma_sems(4), _dma_sems(4)]),
        compiler_params=_params(dimension_semantics=("arbitrary", "arbitrary")),
    )(pos, h2, da4, s4, df)


def _bwd_mix(x, mix, dx1, zs, mod, conv_w, w_in4, w_pool, pool_scale, w_out, ln_g, after, tm):
    t_len, d = x.shape
    cw = w_in4.shape[2]
    dg = cw // N_GROUPS
    nt = t_len // tm

    def body(x_ref, mix_ref, dx1_ref, zs_ref, mod_ref, cw_ref, win_ref, wp_ref, ps_ref, wout_ref, g_ref, after_ref,
             gx_ref, dz_ref, dmix_ref, sd_ref, sc_ref, dwp_ref, dbuf, qbuf, acc_d, acc_c, acc_p):
        i = pl.program_id(0)
        tile_idx = nt - 1 - i

        @pl.when(i == 0)
        def _():
            dbuf[tm:tm + HALO, :] = jnp.zeros((HALO, cw), F32)
            qbuf[tm:tm + HALO, :] = jnp.zeros((HALO, cw), F32)
            acc_d[...] = jnp.zeros(acc_d.shape, F32)
            acc_c[...] = jnp.zeros(acc_c.shape, F32)
            acc_p[...] = jnp.zeros(acc_p.shape, F32)

        xv, mixv, dx1v = x_ref[...], mix_ref[...], dx1_ref[...]
        sc1, g1 = mod_ref[0:1, d:2 * d], mod_ref[0:1, 2 * d:3 * d]
        xhat, rstd = _layer_norm_fwd(DEEPNORM_ALPHA * xv + (1.0 + g1) * mixv)
        dr1 = _layer_norm_bwd(dx1v, xhat, rstd, g_ref[...])
        dmix = ((1.0 + g1) * dr1).astype(BF16)
        dmix_ref[...] = dmix
        dcat = _dot_nt(dmix, wout_ref[...])
        dyc, dyp = dcat[:, 0:cw], dcat[:, cw:2 * cw]
        zb, zc, zv = (zs_ref[:, k * cw:(k + 1) * cw].astype(F32) for k in range(3))
        conv, pooled = zs_ref[:, 3 * cw:4 * cw].astype(F32), zs_ref[:, 4 * cw:5 * cw]
        dzb = dyc * conv
        dcv = dyc * zb
        dbuf[0:tm, :] = dcv
        d1 = dbuf[pl.ds(1, tm), :]
        d2 = dbuf[pl.ds(2, tm), :]
        du = cw_ref[2:3, :] * dcv + cw_ref[1:2, :] * d1 + cw_ref[0:1, :] * d2
        dbuf[tm:tm + HALO, :] = dbuf[0:HALO, :]
        u = zc * zv
        acc_c[0] += _fold8(u * d2)
        acc_c[1] += _fold8(u * d1)
        acc_c[2] += _fold8(u * dcv)
        dzc = du * zv
        dzv = du * zc
        dpw = dyp * ps_ref[...]
        dps, pms = [], []
        for gi, win in enumerate(POOL_WINDOWS):
            sl = slice(gi * dg, (gi + 1) * dg)
            p_g = pooled[:, sl].astype(BF16)
            dpw_g = dpw[:, sl].astype(BF16)
            wp_g = wp_ref[gi].astype(BF16)
            pms.append(_dot(p_g, wp_g))
            acc_p[gi] += _dot_tn(p_g, dpw_g)
            dp_g = _dot_nt(dpw_g, wp_g)
            dps.append(dp_g)
            qbuf[0:tm, sl] = dp_g * _inv_count(tile_idx, tm, win)
        acc_c[3] += _fold8(dyp * jnp.concatenate(pms, axis=1))
        dzps = []
        for gi, win in enumerate(POOL_WINDOWS):
            sl = slice(gi * dg, (gi + 1) * dg)
            acc = qbuf[0:tm, sl]
            for s in range(1, win):
                acc = acc + qbuf[pl.ds(s, tm), sl]
            dzps.append(acc - dps[gi])
        qbuf[tm:tm + HALO, :] = qbuf[0:HALO, :]
        dz = [dzb.astype(BF16), dzc.astype(BF16), dzv.astype(BF16), jnp.concatenate(dzps, axis=1).astype(BF16)]
        dh1 = jnp.zeros((tm, d), F32)
        for j in range(N_CHIPS):
            dz_ref[j] = dz[j]
            dh1 = dh1 + _dot_nt(dz[j], win_ref[j])
        gx_ref[...] = DEEPNORM_ALPHA * dr1 + dh1 * (1.0 + sc1)
        acc_d[0] += _fold8(dx1v * xhat)
        acc_d[1] += _fold8(dx1v)
        acc_d[2] += _fold8(dr1 * mixv)
        acc_d[3] += _fold8(dh1 * xv)
        acc_d[4] += _fold8(dh1)

        @pl.when(i == nt - 1)
        def _():
            for k in range(5):
                sd_ref[k:k + 1, :] = jnp.sum(acc_d[k], axis=0, keepdims=True)
            sd_ref[5:8, :] = jnp.zeros((3, d), F32)
            for k in range(4):
                sc_ref[k:k + 1, :] = jnp.sum(acc_c[k], axis=0, keepdims=True)
            sc_ref[4:8, :] = jnp.zeros((4, cw), F32)
            dwp_ref[...] = acc_p[...]

    rtile = lambda w: pl.BlockSpec((tm, w), lambda i: (nt - 1 - i, 0))
    whole = lambda shape: pl.BlockSpec(shape, lambda i: tuple(0 for _ in shape))
    return pl.pallas_call(
        body, name="bwd_mix",
        out_shape=(jax.ShapeDtypeStruct((t_len, d), F32), jax.ShapeDtypeStruct((N_CHIPS, t_len, cw), BF16),
                   jax.ShapeDtypeStruct((t_len, d), BF16), jax.ShapeDtypeStruct((SUBLANES, d), F32),
                   jax.ShapeDtypeStruct((SUBLANES, cw), F32), jax.ShapeDtypeStruct((N_GROUPS, dg, dg), F32)),
        grid=(nt,),
        in_specs=[rtile(d), rtile(d), rtile(d), rtile(5 * cw)] + [VMEM_SPEC] * 7
        + [pl.BlockSpec(memory_space=pl.ANY)],
        out_specs=(rtile(d), pl.BlockSpec((N_CHIPS, tm, cw), lambda i: (0, nt - 1 - i, 0)), rtile(d),
                   whole((SUBLANES, d)), whole((SUBLANES, cw)), whole((N_GROUPS, dg, dg))),
        scratch_shapes=[pltpu.VMEM((tm + HALO, cw), F32), pltpu.VMEM((tm + HALO, cw), F32),
                        pltpu.VMEM((5, SUBLANES, d), F32), pltpu.VMEM((4, SUBLANES, cw), F32),
                        pltpu.VMEM((N_GROUPS, dg, dg), F32)],
        compiler_params=_params(dimension_semantics=("arbitrary",)),
    )(x, mix, dx1, zs, mod, conv_w, w_in4, w_pool, pool_scale, w_out, ln_g, after)


def _mix_wgrad(h1, dz4, cat, dmix, small, dmod_rows, tt):
    t_len, d = h1.shape
    cw = dz4.shape[2]
    nt = t_len // tt
    ro = (2 * cw) // N_CHIPS
    rh = (d // 2, ro // 2)
    n_small = len(small)

    def body(h1_ref, dz_ref, cat_ref, dmix_ref, *rest):
        p_refs = rest[:n_small]
        qi_ref, qo_ref, gi_ref, go_ref = rest[n_small:n_small + 4]
        sum_refs = rest[n_small + 4:2 * n_small + 4]
        dmod_ref, dmod_sum_ref = rest[2 * n_small + 4:2 * n_small + 6]
        scratch = rest[2 * n_small + 6:]
        acc_i, acc_o, rcv_i, rcv_o = scratch[:4]
        all_bufs = scratch[4:4 + n_small]
        s_sem, r_sem, s_small, r_small_sem = scratch[4 + n_small:]
        t = pl.program_id(0)
        x, y, c = _mesh_pos()
        chip = 2 * x + y
        groups = [(p_refs[k], sum_refs[k], None, all_bufs[k]) for k in range(n_small)]
        small_start, small_finish = _small_allreduce_steps(groups, s_small, r_small_sem)
        accs, rcvs, q_refs, g_refs = (acc_i, acc_o), (rcv_i, rcv_o), (qi_ref, qo_ref), (gi_ref, go_ref)

        @pl.when(t == 0)
        def _():
            small_start()
            acc_i[...] = jnp.zeros(acc_i.shape, F32)
            acc_o[...] = jnp.zeros(acc_o.shape, F32)

        def accumulate(j):
            acc_i[j] += _dot_tn(h1_ref[...], dz_ref[j])
            acc_o[j] += _dot_tn(cat_ref[:, j * ro:(j + 1) * ro], dmix_ref[...])

        def to_sibling(a, j):
            theirs = pl.multiple_of((1 - c) * rh[a], rh[a])
            return pltpu.make_async_remote_copy(accs[a].at[j].at[pl.ds(theirs, rh[a]), :], rcvs[a].at[j],
                                                s_sem.at[a * N_CHIPS + j], r_sem.at[a * N_CHIPS + j],
                                                device_id=(x, y, 1 - c), device_id_type=MESH)

        @pl.when(t < nt - 1)
        def _():
            for j in range(N_CHIPS):
                accumulate(j)

        @pl.when(t == nt - 1)
        def _():
            for j in range(N_CHIPS):
                accumulate(j)
                for a in range(2):
                    to_sibling(a, j).start()
            for a in range(2):
                mine = pl.multiple_of(c * rh[a], rh[a])
                for j in range(N_CHIPS):
                    to_sibling(a, j).wait()
                    q_refs[a][j] = (accs[a][j, pl.ds(mine, rh[a]), :] + rcvs[a][j]).astype(BF16)
                g_refs[a][chip] = q_refs[a][chip]
            small_finish()
            for k, (arr, row) in enumerate(dmod_rows):
                cols = slice(k * d, (k + 1) * d)
                dmod_sum_ref[:, cols] = sum_refs[arr][row:row + 1, :]
                for b in range(N_DEV):
                    dmod_ref[b:b + 1, cols] = all_bufs[arr][b, row:row + 1, :]

    stacks = ((N_CHIPS, rh[0], cw), (N_CHIPS, rh[1], d))
    n_cp = n_small * (N_DEV - 1)
    n_mod = len(dmod_rows) * d
    return pl.pallas_call(
        body, name="mix_wgrad",
        out_shape=tuple(jax.ShapeDtypeStruct(s, BF16) for s in stacks) * 2
        + tuple(jax.ShapeDtypeStruct(a.shape, F32) for a in small)
        + (jax.ShapeDtypeStruct((N_DEV, n_mod), F32), jax.ShapeDtypeStruct((1, n_mod), F32)),
        grid=(nt,),
        in_specs=[pl.BlockSpec((tt, d), lambda t: (t, 0)), pl.BlockSpec((N_CHIPS, tt, cw), lambda t: (0, t, 0)),
                  pl.BlockSpec((tt, 2 * cw), lambda t: (t, 0)), pl.BlockSpec((tt, d), lambda t: (t, 0))]
        + [VMEM_SPEC] * n_small,
        out_specs=(VMEM_SPEC,) * (6 + n_small),
        scratch_shapes=[pltpu.VMEM((N_CHIPS, d, cw), F32), pltpu.VMEM((N_CHIPS, ro, d), F32),
                        pltpu.VMEM(stacks[0], F32), pltpu.VMEM(stacks[1], F32)]
        + [pltpu.VMEM((N_DEV,) + a.shape, F32) for a in small]
        + [_dma_sems(2 * N_CHIPS), _dma_sems(2 * N_CHIPS), _dma_sems(n_cp), _dma_sems(n_cp)],
        compiler_params=_params(dimension_semantics=("arbitrary",)),
    )(h1, dz4, cat, dmix, *small)


def kernel(x, c, w_ada, b_ada, w_in, conv_w, w_pool, pool_scale, w_out, ln1_g, ln1_b, w_mlp_in, w_mlp_out, ln2_g, ln2_b, loss_target, m_w_ada, m_b_ada, m_w_in, m_conv_w, m_w_pool, m_pool_scale, m_w_out, m_ln1_g, m_ln1_b, m_w_mlp_in, m_w_mlp_out, m_ln2_g, m_ln2_b, v_w_ada, v_b_ada, v_w_in, v_conv_w, v_w_pool, v_pool_scale, v_w_out, v_ln1_g, v_ln1_b, v_w_mlp_in, v_w_mlp_out, v_ln2_g, v_ln2_b):
    t_len, d = x.shape[1], x.shape[2]
    cw = w_in.shape[2]
    cq = conv_w.shape[2]
    dg = w_pool.shape[2]
    assert cw == N_GROUPS * dg and cq * N_CHIPS == cw and LANES % cq == 0
    tm_mix = min(MIX_TOKEN_TILE, t_len)
    tm_mlp = min(MLP_TOKEN_TILE, t_len)
    tt = min(WGRAD_TILE, t_len // 4)
    chip = 2 * lax.axis_index("x") + lax.axis_index("y")
    pos = jnp.stack([lax.axis_index("c"), chip]).astype(jnp.int32)

    x2, tgt = x[0], loss_target[0]
    big_w = [w_in[0], w_out[0], w_mlp_in[0], w_mlp_out[0]]
    big_m = [m_w_in[0], m_w_out[0], m_w_mlp_in[0], m_w_mlp_out[0]]
    big_v = [v_w_in[0], v_w_out[0], v_w_mlp_in[0], v_w_mlp_out[0]]
    names = ["w_in", "w_out", "w_mlp_in", "w_mlp_out"]

    mod, cond_all, conv_full, w_in4, w_out4, w_mi_own, w_mo_own = _setup_exchange(
        c, w_ada[0], b_ada, conv_w[0], big_w, 2)
    w_out_full = w_out4.reshape(2 * cw, d)
    w_pool_g = w_pool[0]

    w_mi4, w_mo4, x1, zs, mix, h1, cat = _fwd_mix(x2, mod, conv_full, w_in4, w_pool_g, pool_scale, w_out_full,
                                                  ln1_g, ln1_b, [w_mi_own, w_mo_own], tm_mix)
    dx1, da4, s4, df, h2, st2 = _mlp_fwd_bwd(x1, tgt, mod, w_mi4, w_mo4, ln2_g, ln2_b, tm_mlp)
    mlp_qi, mlp_qo, mlp_gi, mlp_go = _mlp_wgrad(h2, da4, s4, df, pos, tt)
    mlp_started = _split_start([mlp_qi, mlp_qo, mlp_gi, mlp_go], _chunk_exchange_plan(2), 2 * (N_CHIPS - 1),
                               "exchange_mlp")
    grad_x, dz4, dmix, st1, stc, dw_pool = _bwd_mix(x2, mix, dx1, zs, mod, conv_full, w_in4, w_pool_g, pool_scale,
                                                    w_out_full, ln1_g, mlp_started[-1], tm_mix)

    dmod_rows = [(0, 4), (0, 3), (0, 2), (1, 4), (1, 3), (1, 2)]
    mix_qi, mix_qo, mix_gi, mix_go, t1, t2, tc, g_w_pool, dmod_all, g_b_ada = _mix_wgrad(
        h1, dz4, cat, dmix, [st1, st2, stc, dw_pool], dmod_rows, tt)
    loss = t2[5, 0]
    na = w_ada.shape[2]
    dmod_mine = lax.dynamic_slice_in_dim(dmod_all, chip * na, na, axis=1)

    exchange, n_cp = _chunk_exchange_plan(2), 2 * (N_CHIPS - 1)
    mlp_got = _split_wait(mlp_started, exchange, t1, "exchange_mlp")[2:]
    mix_started = _split_start([mix_qi, mix_qo, mix_gi, mix_go], exchange, n_cp, "exchange_mix")
    mlp_halves = [_sum_chips(g, pos, mix_started[-1], "rs_total_" + n) for g, n in zip(mlp_got, names[2:])]
    join = _sibling_join_plan([h.shape for h in mlp_halves])
    join_started = _split_start(mlp_halves, join, 2, "join_mlp")
    g_ada, d_ada, nm_ada, nv_ada = _adamw_ada(cond_all, dmod_mine, w_ada[0], m_w_ada[0], v_w_ada[0],
                                              join_started[-1])
    mlp_g = _split_wait(join_started, join, d_ada, "join_mlp")
    mlp_out = [_adamw_big(g, w, m, v, "adamw_" + n)
               for g, w, m, v, n in zip(mlp_g, big_w[2:], big_m[2:], big_v[2:], names[2:])]
    mix_got = _split_wait(mix_started, exchange, mlp_out[1][0], "exchange_mix")[2:]
    mix_halves = [_sum_chips(g, pos, d_ada, "rs_total_" + n) for g, n in zip(mix_got, names[:2])]
    mix_g = _sibling_join(mix_halves, "mix")
    mix_out = [_adamw_big(g, w, m, v, "adamw_" + n)
               for g, w, m, v, n in zip(mix_g, big_w[:2], big_m[:2], big_v[:2], names[:2])]
    big_g = list(mix_g) + list(mlp_g)
    big_out = mix_out + mlp_out

    sources = [g_b_ada, ("chip_cols", 2, 0, 3, cq), g_w_pool.reshape(-1, dg), ("rows", 2, 3, 4),
               ("rows", 0, 0, 1), ("rows", 0, 1, 2), ("rows", 1, 0, 1), ("rows", 1, 1, 2)]
    small_w = [b_ada, conv_w[0], w_pool.reshape(-1, dg), pool_scale, ln1_g, ln1_b, ln2_g, ln2_b]
    small_m = [m_b_ada, m_conv_w[0], m_w_pool.reshape(-1, dg), m_pool_scale, m_ln1_g, m_ln1_b, m_ln2_g, m_ln2_b]
    small_v = [v_b_ada, v_conv_w[0], v_w_pool.reshape(-1, dg), v_pool_scale, v_ln1_g, v_ln1_b, v_ln2_g, v_ln2_b]
    cut, sm = _adamw_small(pos, [t1, t2, tc], sources, small_w, small_m, small_v)
    small_g = [sources[0], cut[0], sources[2]] + list(cut[1:])
    ns = len(small_w)
    s_delta, s_m, s_v = sm[:ns], sm[ns:2 * ns], sm[2 * ns:]

    def assemble(ada, small, big):
        return [ada[None], small[0], big[0][None], small[1][None], small[2].reshape(w_pool.shape), small[3],
                big[1][None], small[4], small[5], big[2][None], big[3][None], small[6], small[7]]

    grads = assemble(g_ada, small_g, big_g)
    deltas = assemble(d_ada, s_delta, [o[0] for o in big_out])
    new_m = assemble(nm_ada, s_m, [o[1] for o in big_out])
    new_v = assemble(nv_ada, s_v, [o[2] for o in big_out])
    return (loss, grad_x[None], *grads, *deltas, *new_m, *new_v)
```

```python
import jax
import jax.numpy as jnp
from jax import lax
from jax.experimental import pallas as pl
from jax.experimental.pallas import tpu as pltpu

F32 = jnp.float32
BF16 = jnp.bfloat16
MESH = pl.DeviceIdType.MESH

LN_EPS = 1e-5
DEEPNORM_ALPHA = 2.0 ** 0.25
POOL_WINDOWS = (2, 4, 8, 16)
N_GROUPS = len(POOL_WINDOWS)
HALO = 16
N_CHIPS = 4
N_DEV = 8
LANES = 128
SUBLANES = 8
VMEM_LIMIT = 56 * 1024 * 1024
MIX_TOKEN_TILE = 512
MLP_TOKEN_TILE = 256
WGRAD_TILE = 1024

ADAM_LR = 0.001
ADAM_B1 = 0.9
ADAM_B2 = 0.999
ADAM_EPS = 1e-08
ADAM_WD = 0.01
ADAM_STEP = 10

VMEM_SPEC = pl.BlockSpec(memory_space=pltpu.VMEM)
HBM_SPEC = pl.BlockSpec(memory_space=pltpu.HBM)


def _dot(a, b):
    return jnp.dot(a, b, preferred_element_type=F32)


def _dot_nt(a, b):
    return lax.dot_general(a, b, (((1,), (1,)), ((), ())), preferred_element_type=F32)


def _dot_tn(a, b):
    return lax.dot_general(a, b, (((0,), (0,)), ((), ())), preferred_element_type=F32)


def _fold8(v):
    r, n = v.shape
    return jnp.sum(v.reshape(r // SUBLANES, SUBLANES, n), axis=0)


def _mesh_pos():
    return lax.axis_index("x"), lax.axis_index("y"), lax.axis_index("c")


def _flip(v, bit):
    return 1 - v if bit else v


def _params(**kw):
    return pltpu.CompilerParams(vmem_limit_bytes=VMEM_LIMIT, **kw)


def _dma_sems(n):
    return pltpu.SemaphoreType.DMA((n,))


def _stack_gather(stacks, rows, s_ici, r_ici, s_d2d, r_d2d):
    x, y, c = _mesh_pos()
    chip = 2 * x + y
    pairs = [(i, m) for i in range(len(stacks)) for m in range(1, N_CHIPS)]

    def blk(i, slot, which):
        rh = rows[i] // 2
        return stacks[i].at[slot].at[pl.ds(which * rh, rh), :]

    def other(m):
        return 2 * _flip(x, m & 2) + _flip(y, m & 1)

    def ici(i, m, slot, to):
        k = i * (N_CHIPS - 1) + m - 1
        return pltpu.make_async_remote_copy(blk(i, slot, c), blk(i, slot, c), s_ici.at[k], r_ici.at[k],
                                            device_id=to, device_id_type=MESH)

    def d2d(i, m, which, to):
        k = i * (N_CHIPS - 1) + m - 1
        return pltpu.make_async_remote_copy(blk(i, other(m), which), blk(i, other(m), which),
                                            s_d2d.at[k], r_d2d.at[k], device_id=to, device_id_type=MESH)

    def start():
        for i, m in pairs:
            ici(i, m, chip, (_flip(x, m & 2), _flip(y, m & 1), c)).start()

    def forward(which=None):
        for i, m in pairs:
            if which is None or i == which:
                ici(i, m, other(m), (x, y, c)).wait_recv()
                d2d(i, m, c, (x, y, 1 - c)).start()

    def finish():
        for i, m in pairs:
            d2d(i, m, 1 - c, (x, y, c)).wait_recv()
        for i, m in pairs:
            ici(i, m, chip, (x, y, c)).wait_send()
            d2d(i, m, c, (x, y, c)).wait_send()

    return start, forward, finish


def _chunk_exchange(qs, gots, s_sem, r_sem):
    x, y, c = _mesh_pos()
    chip = 2 * x + y
    pairs = [(i, m) for i in range(len(qs)) for m in range(1, N_CHIPS)]

    def other(m):
        return 2 * _flip(x, m & 2) + _flip(y, m & 1)

    def send(i, m, to):
        k = i * (N_CHIPS - 1) + m - 1
        return pltpu.make_async_remote_copy(qs[i].at[other(m)], gots[i].at[chip], s_sem.at[k], r_sem.at[k],
                                            device_id=to, device_id_type=MESH)

    def arrival(i, m):
        k = i * (N_CHIPS - 1) + m - 1
        return pltpu.make_async_remote_copy(qs[i].at[other(m)], gots[i].at[other(m)], s_sem.at[k], r_sem.at[k],
                                            device_id=(x, y, c), device_id_type=MESH)

    def start():
        for i, m in pairs:
            send(i, m, (_flip(x, m & 2), _flip(y, m & 1), c)).start()

    def finish():
        for i, m in pairs:
            arrival(i, m).wait_recv()
        for i, m in pairs:
            send(i, m, (x, y, c)).wait_send()

    return start, finish


def _mod_scratch(d, na, cq):
    return [pltpu.VMEM((SUBLANES, d), F32), pltpu.VMEM((N_DEV, SUBLANES, d), F32),
            pltpu.VMEM((N_CHIPS, SUBLANES, na), F32),
            pltpu.VMEM((SUBLANES, cq), F32), pltpu.VMEM((N_CHIPS, SUBLANES, cq), F32),
            _dma_sems(N_DEV - 1), _dma_sems(N_DEV - 1),
            _dma_sems(N_CHIPS - 1), _dma_sems(N_CHIPS - 1), _dma_sems(N_CHIPS - 1), _dma_sems(N_CHIPS - 1)]


def _mod_steps(d, na, cq):
    def steps(c_ref, w_ref, b_ref, cw_ref, mod_ref, cond_ref, conv_ref,
              csend, cbuf, mbuf, cvsend, cvbuf, s1, r1, s2, r2, s3, r3):
        x, y, c = _mesh_pos()
        me = 4 * x + 2 * y + c
        chip = 2 * x + y
        csend[...] = jnp.broadcast_to(c_ref[...], (SUBLANES, d))
        cbuf[me] = csend[...]
        first = []
        for m in range(1, N_DEV):
            peer = (_flip(x, m & 4), _flip(y, m & 2), _flip(c, m & 1))
            cp = pltpu.make_async_remote_copy(csend, cbuf.at[me], s1.at[m - 1], r1.at[m - 1],
                                              device_id=peer, device_id_type=MESH)
            cp.start()
            first.append(cp)
        cvsend[...] = jnp.zeros((SUBLANES, cq), F32)
        cvsend[0:3, :] = cw_ref[...]
        cvbuf[chip] = cvsend[...]
        for m in range(1, N_DEV):
            src = 4 * _flip(x, m & 4) + 2 * _flip(y, m & 2) + _flip(c, m & 1)
            pltpu.make_async_remote_copy(csend, cbuf.at[src], s1.at[m - 1], r1.at[m - 1],
                                         device_id=(x, y, c), device_id_type=MESH).wait_recv()
        rows = lax.broadcasted_iota(jnp.int32, (SUBLANES, d), 0)
        call = jnp.zeros((SUBLANES, d), F32)
        for b in range(N_DEV):
            call = jnp.where(rows == b, cbuf[b], call)
        cond = call * jax.nn.sigmoid(call)
        cond_ref[...] = cond
        part = jnp.dot(cond, w_ref[...], preferred_element_type=F32, precision=lax.Precision.HIGHEST)
        mbuf[chip] = part
        second = []
        for m in range(1, N_CHIPS):
            peer = (_flip(x, m & 2), _flip(y, m & 1), c)
            cp = pltpu.make_async_remote_copy(mbuf.at[chip], mbuf.at[chip], s2.at[m - 1], r2.at[m - 1],
                                              device_id=peer, device_id_type=MESH)
            cp.start()
            second.append(cp)
            cp = pltpu.make_async_remote_copy(cvsend, cvbuf.at[chip], s3.at[m - 1], r3.at[m - 1],
                                              device_id=peer, device_id_type=MESH)
            cp.start()
            second.append(cp)
        for m in range(1, N_CHIPS):
            src = 2 * _flip(x, m & 2) + _flip(y, m & 1)
            pltpu.make_async_remote_copy(mbuf.at[src], mbuf.at[src], s2.at[m - 1], r2.at[m - 1],
                                         device_id=(x, y, c), device_id_type=MESH).wait_recv()
            pltpu.make_async_remote_copy(cvsend, cvbuf.at[src], s3.at[m - 1], r3.at[m - 1],
                                         device_id=(x, y, c), device_id_type=MESH).wait_recv()
        for cp in first + second:
            cp.wait_send()
        rows_n = lax.broadcasted_iota(jnp.int32, (SUBLANES, na), 0)
        for k in range(N_CHIPS):
            mine = jnp.sum(jnp.where(rows_n == me, mbuf[k], 0.0), axis=0, keepdims=True)
            mod_ref[:, k * na:(k + 1) * na] = jnp.broadcast_to(mine + b_ref[:, k * na:(k + 1) * na], (SUBLANES, na))
            conv_ref[:, k * cq:(k + 1) * cq] = cvbuf[k]

    return steps


def _setup_exchange(c_row, w_ada_s, b_ada, conv_w_s, shards, n_now):
    n = len(shards)
    n_cp = n_now * (N_CHIPS - 1)
    d, na, cq = c_row.shape[1], w_ada_s.shape[1], conv_w_s.shape[1]
    mod_scratch = _mod_scratch(d, na, cq)
    mod_steps = _mod_steps(d, na, cq)

    def body(*refs):
        mod_in, ins = refs[:4], refs[4:4 + n]
        mod_out, outs = refs[4 + n:7 + n], refs[7 + n:7 + 2 * n]
        bufs = refs[7 + 2 * n:7 + 3 * n]
        mod_scr = refs[7 + 3 * n:7 + 3 * n + len(mod_scratch)]
        s_ici, r_ici, s_d2d, r_d2d, s_loc = refs[7 + 3 * n + len(mod_scratch):7 + 3 * n + len(mod_scratch) + 5]
        w_ada_buf, s_in = refs[7 + 3 * n + len(mod_scratch) + 5:7 + 3 * n + len(mod_scratch) + 7]
        late_bufs = refs[7 + 3 * n + len(mod_scratch) + 7:]
        x, y, _ = _mesh_pos()
        chip = 2 * x + y
        loads = [pltpu.make_async_copy(mod_in[1], w_ada_buf, s_in.at[0])]
        loads += [pltpu.make_async_copy(ins[i], late_bufs[i - n_now], s_in.at[1 + i - n_now]) for i in range(n_now, n)]
        for cp in loads:
            cp.start()

        def place(i, src):
            bufs[i][...] = src[...].astype(BF16)
            cp = pltpu.make_async_copy(bufs[i], outs[i].at[chip], s_loc.at[i])
            cp.start()
            return cp

        start, forward, finish = _stack_gather(outs[:n_now], [s.shape[0] for s in shards[:n_now]],
                                               s_ici, r_ici, s_d2d, r_d2d)
        for cp in [place(i, ins[i]) for i in range(n_now)]:
            cp.wait()
        start()
        later = []
        for i in range(n_now, n):
            loads[1 + i - n_now].wait()
            later.append(place(i, late_bufs[i - n_now]))
        loads[0].wait()
        mod_steps(mod_in[0], w_ada_buf, mod_in[2], mod_in[3], *mod_out, *mod_scr)
        forward()
        finish()
        for cp in later:
            cp.wait()

    any_spec = pl.BlockSpec(memory_space=pl.ANY)
    return pl.pallas_call(
        body, name="setup_exchange",
        out_shape=(jax.ShapeDtypeStruct((SUBLANES, N_CHIPS * na), F32), jax.ShapeDtypeStruct((SUBLANES, d), F32),
                   jax.ShapeDtypeStruct((SUBLANES, N_CHIPS * cq), F32))
        + tuple(jax.ShapeDtypeStruct((N_CHIPS,) + s.shape, BF16) for s in shards),
        in_specs=[VMEM_SPEC, any_spec, VMEM_SPEC, VMEM_SPEC] + [VMEM_SPEC] * n_now + [any_spec] * (n - n_now),
        out_specs=(VMEM_SPEC,) * 3 + (HBM_SPEC,) * n,
        scratch_shapes=[pltpu.VMEM(s.shape, BF16) for s in shards] + mod_scratch
        + [_dma_sems(n_cp), _dma_sems(n_cp), _dma_sems(n_cp), _dma_sems(n_cp), _dma_sems(n)]
        + [pltpu.VMEM(w_ada_s.shape, F32), _dma_sems(1 + n - n_now)]
        + [pltpu.VMEM(s.shape, F32) for s in shards[n_now:]],
        compiler_params=_params(),
    )(c_row, w_ada_s, b_ada, conv_w_s, *shards)


def _sibling_join_plan(shapes):
    def plan(refs, s_sem, r_sem):
        x, y, c = _mesh_pos()

        def half(i, which):
            rh = shapes[i][0] // 2
            return refs[i].at[pl.ds(which * rh, rh), :]

        def start():
            for i in range(len(refs)):
                pltpu.make_async_remote_copy(half(i, c), half(i, c), s_sem.at[i], r_sem.at[i],
                                             device_id=(x, y, 1 - c), device_id_type=MESH).start()

        def finish():
            for i in range(len(refs)):
                pltpu.make_async_remote_copy(half(i, 1 - c), half(i, 1 - c), s_sem.at[i], r_sem.at[i],
                                             device_id=(x, y, c), device_id_type=MESH).wait_recv()
            for i in range(len(refs)):
                pltpu.make_async_remote_copy(half(i, c), half(i, c), s_sem.at[i], r_sem.at[i],
                                             device_id=(x, y, c), device_id_type=MESH).wait_send()

        return start, finish

    return plan


def _chunk_exchange_plan(n):
    return lambda refs, s_sem, r_sem: _chunk_exchange(refs[:n], refs[n:], s_sem, r_sem)


def _sibling_join(gs, tag):
    n = len(gs)
    plan = _sibling_join_plan([g.shape for g in gs])

    def body(*refs):
        start, finish = plan(refs[n:2 * n], refs[2 * n], refs[2 * n + 1])
        start()
        finish()

    return pl.pallas_call(
        body, name="rs_sibling_join_" + tag,
        out_shape=tuple(jax.ShapeDtypeStruct(g.shape, g.dtype) for g in gs),
        in_specs=[HBM_SPEC] * n, out_specs=(HBM_SPEC,) * n,
        input_output_aliases={i: i for i in range(n)},
        scratch_shapes=[_dma_sems(n), _dma_sems(n)],
        compiler_params=_params(),
    )(*gs)


SEM_SPEC = pl.BlockSpec(memory_space=pltpu.SEMAPHORE)
SIDE_EFFECT = pltpu.SideEffectType.DATAFLOW_SIDE_EFFECTING


def _split_start(arrays, plan, n_cp, tag):
    n = len(arrays)

    def body(*refs):
        start, _ = plan(refs[n + 2:2 * n + 2], refs[n], refs[n + 1])
        start()
        token = refs[2 * n + 2]
        token[...] = jnp.zeros_like(token)

    through = [pltpu.with_memory_space_constraint(a, pltpu.HBM) for a in arrays]
    return pl.pallas_call(
        body, name=tag + "_start",
        out_shape=(pltpu.SemaphoreType.DMA((n_cp,)), pltpu.SemaphoreType.DMA((n_cp,)))
        + tuple(pltpu.HBM(a.shape, a.dtype) for a in through) + (jax.ShapeDtypeStruct((SUBLANES, LANES), F32),),
        in_specs=[HBM_SPEC] * n, out_specs=(SEM_SPEC, SEM_SPEC) + (HBM_SPEC,) * n + (VMEM_SPEC,),
        input_output_aliases={k: 2 + k for k in range(n)},
        compiler_params=pltpu.CompilerParams(has_side_effects=SIDE_EFFECT),
    )(*through)


def _split_wait(started, plan, after, tag):
    send_sems, recv_sems, arrays = started[0], started[1], started[2:-1]
    n = len(arrays)

    def body(*refs):
        _, finish = plan(refs[:n], refs[n], refs[n + 1])
        finish()

    return pl.pallas_call(
        body, name=tag + "_wait",
        out_shape=tuple(pltpu.HBM(a.shape, a.dtype) for a in arrays),
        in_specs=[HBM_SPEC] * n + [SEM_SPEC, SEM_SPEC, pl.BlockSpec(memory_space=pl.ANY)],
        out_specs=(HBM_SPEC,) * n,
        input_output_aliases={k: k for k in range(n)},
        compiler_params=pltpu.CompilerParams(has_side_effects=SIDE_EFFECT),
    )(*arrays, send_sems, recv_sems, after)


def _small_allreduce_steps(groups, s_sem, r_sem):
    x, y, c = _mesh_pos()
    me = 4 * x + 2 * y + c
    pairs = [(k, m) for k in range(len(groups)) for m in range(1, N_DEV)]

    def send(k, m, to):
        p_ref, all_ref = groups[k][0], groups[k][3]
        i = k * (N_DEV - 1) + m - 1
        return pltpu.make_async_remote_copy(p_ref, all_ref.at[me], s_sem.at[i], r_sem.at[i],
                                            device_id=to, device_id_type=MESH)

    def start():
        for p_ref, _, _, all_ref in groups:
            all_ref[me] = p_ref[...]
        for k, m in pairs:
            send(k, m, (_flip(x, m & 4), _flip(y, m & 2), _flip(c, m & 1))).start()

    def finish():
        for k, m in pairs:
            p_ref, all_ref = groups[k][0], groups[k][3]
            i = k * (N_DEV - 1) + m - 1
            src = 4 * _flip(x, m & 4) + 2 * _flip(y, m & 2) + _flip(c, m & 1)
            pltpu.make_async_remote_copy(p_ref, all_ref.at[src], s_sem.at[i], r_sem.at[i],
                                         device_id=(x, y, c), device_id_type=MESH).wait_recv()
        for k, m in pairs:
            send(k, m, (x, y, c)).wait_send()
        for _, sum_ref, all_out_ref, all_ref in groups:
            sum_ref[...] = (((all_ref[0] + all_ref[1]) + (all_ref[2] + all_ref[3]))
                            + ((all_ref[4] + all_ref[5]) + (all_ref[6] + all_ref[7])))
            if all_out_ref is not None:
                all_out_ref[...] = all_ref[...]

    return start, finish


def _row_block(rows, cap=256):
    rb = min(rows, cap)
    assert rows % rb == 0
    return rb


def _sum_chips(gots, pos, after, name, nb=2):
    n = len(gots)
    blocks = [(g.shape[1] // nb, g.shape[2]) for g in gots]

    def body(pos_ref, *refs):
        for a_ref, o_ref in zip(refs[:n], refs[n + 1:]):
            a = a_ref[...].astype(F32)
            o_ref[...] = (a[0] + a[1]) + (a[2] + a[3])

    return pl.pallas_call(
        body, name=name,
        out_shape=tuple(jax.ShapeDtypeStruct((2 * g.shape[1], g.shape[2]), F32) for g in gots),
        grid_spec=pltpu.PrefetchScalarGridSpec(
            num_scalar_prefetch=1, grid=(nb,),
            in_specs=[pl.BlockSpec((N_CHIPS,) + b, lambda i, p: (0, i, 0)) for b in blocks]
            + [pl.BlockSpec(memory_space=pl.ANY)],
            out_specs=tuple(pl.BlockSpec(b, lambda i, p: (p[0] * nb + i, 0)) for b in blocks)),
        compiler_params=_params(dimension_semantics=("arbitrary",)),
    )(pos, *gots, after)


def _adamw_math(w, g, m, v):
    m = ADAM_B1 * m + (1.0 - ADAM_B1) * g
    v = ADAM_B2 * v + (1.0 - ADAM_B2) * jnp.square(g)
    m_hat = m / (1.0 - ADAM_B1 ** ADAM_STEP)
    v_hat = v / (1.0 - ADAM_B2 ** ADAM_STEP)
    delta = -ADAM_LR * (m_hat / (jnp.sqrt(v_hat) + ADAM_EPS) + ADAM_WD * w)
    return delta, m, v


def _adamw_big(gs, ws, ms, vs, name, nb=4):
    n = len(ws)
    specs = [pl.BlockSpec((w.shape[0] // nb, w.shape[1]), lambda i: (i, 0)) for w in ws]

    def body(*refs):
        g_refs, w_refs, m_refs, v_refs = (refs[k * n:(k + 1) * n] for k in range(4))
        outs = refs[4 * n:]
        for i in range(n):
            d, mn, vn = _adamw_math(w_refs[i][...], g_refs[i][...], m_refs[i][...], v_refs[i][...])
            outs[3 * i][...] = d
            outs[3 * i + 1][...] = mn
            outs[3 * i + 2][...] = vn

    out = pl.pallas_call(
        body, name=name,
        out_shape=tuple(jax.ShapeDtypeStruct(w.shape, F32) for w in ws for _ in range(3)),
        grid=(nb,), in_specs=specs * 4, out_specs=tuple(s for s in specs for _ in range(3)),
        compiler_params=_params(dimension_semantics=("arbitrary",)),
    )(*gs, *ws, *ms, *vs)
    return [out[3 * i:3 * i + 3] for i in range(n)]


def _adamw_ada(cond, dmod, w, m, v, after):
    d, na = w.shape
    rb = _row_block(d, 128)

    def body(c_ref, dm_ref, w_ref, m_ref, v_ref, after_ref, g_ref, d_ref, mo_ref, vo_ref):
        g = lax.dot_general(c_ref[...], dm_ref[...], (((0,), (0,)), ((), ())), preferred_element_type=F32,
                            precision=lax.Precision.HIGHEST)
        dl, mn, vn = _adamw_math(w_ref[...], g, m_ref[...], v_ref[...])
        g_ref[...] = g
        d_ref[...] = dl
        mo_ref[...] = mn
        vo_ref[...] = vn

    spec = pl.BlockSpec((rb, na), lambda i: (i, 0))
    return pl.pallas_call(
        body, name="adamw_w_ada",
        out_shape=(jax.ShapeDtypeStruct((d, na), F32),) * 4,
        grid=(d // rb,),
        in_specs=[pl.BlockSpec((N_DEV, rb), lambda i: (0, i)), pl.BlockSpec((N_DEV, na), lambda i: (0, 0)),
                  spec, spec, spec, pl.BlockSpec(memory_space=pl.ANY)],
        out_specs=(spec,) * 4,
        compiler_params=_params(dimension_semantics=("arbitrary",)),
    )(cond, dmod, w, m, v, after)


def _adamw_small(pos, stats, sources, ws, ms, vs):
    n = len(ws)
    direct = [s for s in sources if not isinstance(s, tuple)]
    n_cut = n - len(direct)

    def body(pos_ref, *refs):
        stat_refs, direct_refs = refs[:len(stats)], refs[len(stats):len(stats) + len(direct)]
        w_refs, m_refs, v_refs = (refs[len(stats) + len(direct) + k * n:len(stats) + len(direct) + (k + 1) * n]
                                  for k in range(3))
        outs = refs[len(stats) + len(direct) + 3 * n:]
        cut_refs, res = outs[:n_cut], outs[n_cut:]
        chip = pos_ref[1]
        i_direct = i_cut = 0
        for i, src in enumerate(sources):
            if not isinstance(src, tuple):
                g = direct_refs[i_direct][...]
                i_direct += 1
            else:
                if src[0] == "rows":
                    g = stat_refs[src[1]][src[2]:src[3], :]
                else:
                    _, k, lo, hi, width = src
                    g = jnp.zeros((hi - lo, width), F32)
                    for j in range(N_CHIPS):
                        g = jnp.where(chip == j, stat_refs[k][lo:hi, j * width:(j + 1) * width], g)
                cut_refs[i_cut][...] = g
                i_cut += 1
            d, mn, vn = _adamw_math(w_refs[i][...], g, m_refs[i][...], v_refs[i][...])
            res[i][...] = d
            res[n + i][...] = mn
            res[2 * n + i][...] = vn

    shapes = tuple(jax.ShapeDtypeStruct(w.shape, F32) for w in ws)
    cut_shapes = tuple(jax.ShapeDtypeStruct(w.shape, F32) for w, s in zip(ws, sources) if isinstance(s, tuple))
    out = pl.pallas_call(
        body, name="adamw_small",
        out_shape=cut_shapes + shapes * 3,
        in_specs=[pl.BlockSpec(memory_space=pltpu.SMEM)] + [VMEM_SPEC] * (len(stats) + len(direct) + 3 * n),
        out_specs=(VMEM_SPEC,) * (n_cut + 3 * n),
        compiler_params=_params(),
    )(pos, *stats, *direct, *ws, *ms, *vs)
    return out[:n_cut], out[n_cut:]


def _layer_norm_fwd(r):
    mu = jnp.mean(r, axis=-1, keepdims=True)
    xc = r - mu
    var = jnp.mean(jnp.square(xc), axis=-1, keepdims=True)
    rstd = lax.rsqrt(var + LN_EPS)
    return xc * rstd, rstd


def _layer_norm_bwd(dy, xhat, rstd, gain):
    dxh = dy * gain
    m1 = jnp.mean(dxh, axis=-1, keepdims=True)
    m2 = jnp.mean(dxh * xhat, axis=-1, keepdims=True)
    return rstd * (dxh - m1 - xhat * m2)


def _inv_count(tile, tm, win):
    t = (tile * tm + lax.broadcasted_iota(jnp.int32, (tm, 1), 0) + 1).astype(F32)
    return 1.0 / jnp.minimum(t, float(win))


def _fwd_mix(x, mod, conv_w, w_in4, w_pool, pool_scale, w_out, ln_g, ln_b, late_stacks, tm):
    t_len, d = x.shape
    cw = w_in4.shape[2]
    dg = cw // N_GROUPS
    nt = t_len // tm
    n_late = len(late_stacks)
    n_cp = n_late * (N_CHIPS - 1)
    pass_steps = [min(((k + 1) * nt) // n_late + (1 if k + 1 < n_late else -2), nt - 1) if nt > 2 else nt - 1
                  for k in range(n_late)]
    pass_steps = [max(p, 0) for p in pass_steps]

    def body(x_ref, mod_ref, cw_ref, win_ref, wp_ref, ps_ref, wout_ref, g_ref, b_ref, *rest):
        late = rest[n_late:2 * n_late]
        x1_ref, zs_ref, mix_ref, h1_ref, cat_ref = rest[2 * n_late:2 * n_late + 5]
        ubuf, vbuf, s_ici, r_ici, s_d2d, r_d2d = rest[2 * n_late + 5:]
        i = pl.program_id(0)
        start, forward, finish = _stack_gather(late, [s.shape[1] for s in late_stacks], s_ici, r_ici, s_d2d, r_d2d)

        @pl.when(i == 0)
        def _():
            start()
            ubuf[0:HALO, :] = jnp.zeros((HALO, cw), F32)
            vbuf[0:HALO, :] = jnp.zeros((HALO, cw), F32)

        xv = x_ref[...]
        sh1, sc1, g1 = mod_ref[0:1, 0:d], mod_ref[0:1, d:2 * d], mod_ref[0:1, 2 * d:3 * d]
        h1 = (xv * (1.0 + sc1) + sh1).astype(BF16)
        h1_ref[...] = h1
        zb = _dot(h1, win_ref[0])
        zc = _dot(h1, win_ref[1])
        zv = _dot(h1, win_ref[2])
        zp = _dot(h1, win_ref[3])
        u = zc * zv
        ubuf[HALO:HALO + tm, :] = u
        u1 = ubuf[pl.ds(HALO - 1, tm), :]
        u2 = ubuf[pl.ds(HALO - 2, tm), :]
        conv = cw_ref[0:1, :] * u2 + cw_ref[1:2, :] * u1 + cw_ref[2:3, :] * u
        ubuf[0:HALO, :] = ubuf[tm:tm + HALO, :]
        yc = zb * conv
        vbuf[HALO:HALO + tm, :] = zp
        ps, pms = [], []
        for gi, win in enumerate(POOL_WINDOWS):
            sl = slice(gi * dg, (gi + 1) * dg)
            acc = zp[:, sl]
            for s in range(1, win):
                acc = acc + vbuf[pl.ds(HALO - s, tm), sl]
            p_g = acc * _inv_count(i, tm, win) - zp[:, sl]
            ps.append(p_g)
            pms.append(_dot(p_g.astype(BF16), wp_ref[gi].astype(BF16)))
        vbuf[0:HALO, :] = vbuf[tm:tm + HALO, :]
        pooled = jnp.concatenate(ps, axis=1)
        yp = jnp.concatenate(pms, axis=1) * ps_ref[...]
        cat = jnp.concatenate([yc, yp], axis=1).astype(BF16)
        cat_ref[...] = cat
        mix = _dot(cat, wout_ref[...])
        mix_ref[...] = mix
        xhat, _ = _layer_norm_fwd(DEEPNORM_ALPHA * xv + (1.0 + g1) * mix)
        x1_ref[...] = xhat * g_ref[...] + b_ref[...]
        zs_ref[:, 0 * cw:1 * cw] = zb.astype(BF16)
        zs_ref[:, 1 * cw:2 * cw] = zc.astype(BF16)
        zs_ref[:, 2 * cw:3 * cw] = zv.astype(BF16)
        zs_ref[:, 3 * cw:4 * cw] = conv.astype(BF16)
        zs_ref[:, 4 * cw:5 * cw] = pooled.astype(BF16)

        for k in range(n_late):
            @pl.when(i == pass_steps[k])
            def _():
                forward(k)

        @pl.when(i == nt - 1)
        def _():
            finish()

    tile = lambda w: pl.BlockSpec((tm, w), lambda i: (i, 0))
    n_in = 9
    return pl.pallas_call(
        body, name="fwd_mix",
        out_shape=tuple(jax.ShapeDtypeStruct(s.shape, s.dtype) for s in late_stacks)
        + (jax.ShapeDtypeStruct((t_len, d), F32), jax.ShapeDtypeStruct((t_len, 5 * cw), BF16),
           jax.ShapeDtypeStruct((t_len, d), F32), jax.ShapeDtypeStruct((t_len, d), BF16),
           jax.ShapeDtypeStruct((t_len, 2 * cw), BF16)),
        grid=(nt,),
        in_specs=[tile(d)] + [VMEM_SPEC] * 8 + [HBM_SPEC] * n_late,
        out_specs=(HBM_SPEC,) * n_late + (tile(d), tile(5 * cw), tile(d), tile(d), tile(2 * cw)),
        input_output_aliases={n_in + k: k for k in range(n_late)},
        scratch_shapes=[pltpu.VMEM((tm + HALO, cw), F32), pltpu.VMEM((tm + HALO, cw), F32),
                        _dma_sems(n_cp), _dma_sems(n_cp), _dma_sems(n_cp), _dma_sems(n_cp)],
        compiler_params=_params(dimension_semantics=("arbitrary",)),
    )(x, mod, conv_w, w_in4, w_pool, pool_scale, w_out, ln_g, ln_b, *late_stacks)


def _mlp_fwd_bwd(x1, tgt, mod, w_mi4, w_mo4, ln_g, ln_b, tm):
    t_len, d = x1.shape
    fq = w_mi4.shape[2]
    nt = t_len // tm

    def body(x1_ref, tgt_ref, mod_ref, wi_ref, wo_ref, g_ref, b_ref,
             dx1_ref, da_ref, s_ref, df_ref, h2_ref, st_ref, relu_buf, acc):
        i = pl.program_id(0)

        @pl.when(i == 0)
        def _():
            acc[...] = jnp.zeros(acc.shape, F32)

        x1v = x1_ref[...]
        sh2, sc2, g2 = mod_ref[0:1, 3 * d:4 * d], mod_ref[0:1, 4 * d:5 * d], mod_ref[0:1, 5 * d:6 * d]
        h2 = (x1v * (1.0 + sc2) + sh2).astype(BF16)
        h2_ref[...] = h2
        f = jnp.zeros((tm, d), F32)
        for j in range(N_CHIPS):
            a = jnp.maximum(_dot(h2, wi_ref[j]), 0.0)
            relu_buf[j] = a
            s = (a * a).astype(BF16)
            s_ref[j] = s
            f = f + _dot(s, wo_ref[j])
        xhat, rstd = _layer_norm_fwd(DEEPNORM_ALPHA * x1v + (1.0 + g2) * f)
        gain = g_ref[...]
        err = xhat * gain + b_ref[...] - tgt_ref[...]
        dy = err * (1.0 / d)
        dr2 = _layer_norm_bwd(dy, xhat, rstd, gain)
        df = ((1.0 + g2) * dr2).astype(BF16)
        df_ref[...] = df
        dh2 = jnp.zeros((tm, d), F32)
        for j in range(N_CHIPS):
            ds = _dot_nt(df, wo_ref[j])
            da = (ds * (2.0 * relu_buf[j])).astype(BF16)
            da_ref[j] = da
            dh2 = dh2 + _dot_nt(da, wi_ref[j])
        dx1_ref[...] = DEEPNORM_ALPHA * dr2 + dh2 * (1.0 + sc2)
        acc[0] += _fold8(dy * xhat)
        acc[1] += _fold8(dy)
        acc[2] += _fold8(dr2 * f)
        acc[3] += _fold8(dh2 * x1v)
        acc[4] += _fold8(dh2)
        acc[5] += _fold8(err * err)

        @pl.when(i == nt - 1)
        def _():
            for k in range(5):
                st_ref[k:k + 1, :] = jnp.sum(acc[k], axis=0, keepdims=True)
            loss = jnp.sum(acc[5]) * (0.5 / d)
            st_ref[5:6, :] = jnp.broadcast_to(loss, (1, d))
            st_ref[6:8, :] = jnp.zeros((2, d), F32)

    tile = lambda w: pl.BlockSpec((tm, w), lambda i: (i, 0))
    tile4 = pl.BlockSpec((N_CHIPS, tm, fq), lambda i: (0, i, 0))
    return pl.pallas_call(
        body, name="mlp_fwd_bwd",
        out_shape=(jax.ShapeDtypeStruct((t_len, d), F32),
                   jax.ShapeDtypeStruct((N_CHIPS, t_len, fq), BF16), jax.ShapeDtypeStruct((N_CHIPS, t_len, fq), BF16),
                   jax.ShapeDtypeStruct((t_len, d), BF16), jax.ShapeDtypeStruct((t_len, d), BF16),
                   jax.ShapeDtypeStruct((SUBLANES, d), F32)),
        grid=(nt,),
        in_specs=[tile(d), tile(d)] + [VMEM_SPEC] * 5,
        out_specs=(tile(d), tile4, tile4, tile(d), tile(d), pl.BlockSpec((SUBLANES, d), lambda i: (0, 0))),
        scratch_shapes=[pltpu.VMEM((N_CHIPS, tm, fq), F32), pltpu.VMEM((6, SUBLANES, d), F32)],
        compiler_params=_params(dimension_semantics=("arbitrary",)),
    )(x1, tgt, mod, w_mi4, w_mo4, ln_g, ln_b)


def _mlp_wgrad(h2, da4, s4, df, pos, tt):
    t_len, d = h2.shape
    fq = da4.shape[2]
    nt = t_len // tt
    assert nt >= 3
    rh = (d // 2, fq // 2)
    last = N_CHIPS - 1

    def body(pos_ref, h2_ref, da_ref, s_ref, df_ref, qi_ref, qo_ref, gi_ref, go_ref,
             acc_i, acc_o, rcv_i, rcv_o, s_sem, r_sem):
        j, t = pl.program_id(0), pl.program_id(1)
        slot = lax.rem(j, 2)
        x, y, c = _mesh_pos()
        accs, rcvs, q_refs, g_refs = (acc_i, acc_o), (rcv_i, rcv_o), (qi_ref, qo_ref), (gi_ref, go_ref)

        def to_sibling(a, sl):
            theirs = pl.multiple_of((1 - c) * rh[a], rh[a])
            return pltpu.make_async_remote_copy(accs[a].at[sl].at[pl.ds(theirs, rh[a]), :], rcvs[a].at[sl],
                                                s_sem.at[2 * sl + a], r_sem.at[2 * sl + a],
                                                device_id=(x, y, 1 - c), device_id_type=MESH)

        def emit(chunk, sl):
            for a in range(2):
                to_sibling(a, sl).wait()
                mine = pl.multiple_of(c * rh[a], rh[a])
                q = (accs[a][sl, pl.ds(mine, rh[a]), :] + rcvs[a][sl]).astype(BF16)
                q_refs[a][...] = q

                @pl.when(chunk == pos_ref[1])
                def _():
                    g_refs[a][...] = q

        @pl.when(t == 0)
        def _():
            acc_i[slot] = jnp.zeros((d, fq), F32)
            acc_o[slot] = jnp.zeros((fq, d), F32)

        acc_i[slot] += _dot_tn(h2_ref[...], da_ref[...])
        acc_o[slot] += _dot_tn(s_ref[...], df_ref[...])

        @pl.when((t == 1) & (j >= 1))
        def _():
            emit(j - 1, 1 - slot)

        @pl.when(t == nt - 1)
        def _():
            for a in range(2):
                to_sibling(a, slot).start()

        @pl.when((t == nt - 1) & (j == last))
        def _():
            emit(j, slot)

    q_index = lambda j, t, p: (jnp.where(t == nt - 1, j, jnp.maximum(j - 1, 0)), 0, 0)
    got_index = lambda j, t, p: (p[1], 0, 0)
    blocks = ((None, rh[0], fq), (None, rh[1], d))
    return pl.pallas_call(
        body, name="mlp_wgrad",
        out_shape=tuple(jax.ShapeDtypeStruct((N_CHIPS,) + b[1:], BF16) for b in blocks) * 2,
        grid_spec=pltpu.PrefetchScalarGridSpec(
            num_scalar_prefetch=1, grid=(N_CHIPS, nt),
            in_specs=[pl.BlockSpec((tt, d), lambda j, t, p: (t, 0)),
                      pl.BlockSpec((None, tt, fq), lambda j, t, p: (j, t, 0)),
                      pl.BlockSpec((None, tt, fq), lambda j, t, p: (j, t, 0)),
                      pl.BlockSpec((tt, d), lambda j, t, p: (t, 0))],
            out_specs=tuple(pl.BlockSpec(b, q_index) for b in blocks)
            + tuple(pl.BlockSpec(b, got_index) for b in blocks),
            scratch_shapes=[pltpu.VMEM((2, d, fq), F32), pltpu.VMEM((2, fq, d), F32),
                            pltpu.VMEM((2, rh[0], fq), F32), pltpu.VMEM((2, rh[1], d), F32),
                            _dma_sems(4), _dma_sems(4)]),
        compiler_params=_params(dimension_semantics=("arbitrary", "arbitrary")),
    )(pos, h2, da4, s4, df)


def _bwd_mix(x, mix, dx1, zs, mod, conv_w, w_in4, w_pool, pool_scale, w_out, ln_g, after, tm):
    t_len, d = x.shape
    cw = w_in4.shape[2]
    dg = cw // N_GROUPS
    nt = t_len // tm

    def body(x_ref, mix_ref, dx1_ref, zs_ref, mod_ref, cw_ref, win_ref, wp_ref, ps_ref, wout_ref, g_ref, after_ref,
             gx_ref, dz_ref, dmix_ref, sd_ref, sc_ref, dwp_ref, dbuf, qbuf, acc_d, acc_c, acc_p):
        i = pl.program_id(0)
        tile_idx = nt - 1 - i

        @pl.when(i == 0)
        def _():
            dbuf[tm:tm + HALO, :] = jnp.zeros((HALO, cw), F32)
            qbuf[tm:tm + HALO, :] = jnp.zeros((HALO, cw), F32)
            acc_d[...] = jnp.zeros(acc_d.shape, F32)
            acc_c[...] = jnp.zeros(acc_c.shape, F32)
            acc_p[...] = jnp.zeros(acc_p.shape, F32)

        xv, mixv, dx1v = x_ref[...], mix_ref[...], dx1_ref[...]
        sc1, g1 = mod_ref[0:1, d:2 * d], mod_ref[0:1, 2 * d:3 * d]
        xhat, rstd = _layer_norm_fwd(DEEPNORM_ALPHA * xv + (1.0 + g1) * mixv)
        dr1 = _layer_norm_bwd(dx1v, xhat, rstd, g_ref[...])
        dmix = ((1.0 + g1) * dr1).astype(BF16)
        dmix_ref[...] = dmix
        dcat = _dot_nt(dmix, wout_ref[...])
        dyc, dyp = dcat[:, 0:cw], dcat[:, cw:2 * cw]
        zb, zc, zv = (zs_ref[:, k * cw:(k + 1) * cw].astype(F32) for k in range(3))
        conv, pooled = zs_ref[:, 3 * cw:4 * cw].astype(F32), zs_ref[:, 4 * cw:5 * cw]
        dzb = dyc * conv
        dcv = dyc * zb
        dbuf[0:tm, :] = dcv
        d1 = dbuf[pl.ds(1, tm), :]
        d2 = dbuf[pl.ds(2, tm), :]
        du = cw_ref[2:3, :] * dcv + cw_ref[1:2, :] * d1 + cw_ref[0:1, :] * d2
        dbuf[tm:tm + HALO, :] = dbuf[0:HALO, :]
        u = zc * zv
        acc_c[0] += _fold8(u * d2)
        acc_c[1] += _fold8(u * d1)
        acc_c[2] += _fold8(u * dcv)
        dzc = du * zv
        dzv = du * zc
        dpw = dyp * ps_ref[...]
        dps, pms = [], []
        for gi, win in enumerate(POOL_WINDOWS):
            sl = slice(gi * dg, (gi + 1) * dg)
            p_g = pooled[:, sl].astype(BF16)
            dpw_g = dpw[:, sl].astype(BF16)
            wp_g = wp_ref[gi].astype(BF16)
            pms.append(_dot(p_g, wp_g))
            acc_p[gi] += _dot_tn(p_g, dpw_g)
            dp_g = _dot_nt(dpw_g, wp_g)
            dps.append(dp_g)
            qbuf[0:tm, sl] = dp_g * _inv_count(tile_idx, tm, win)
        acc_c[3] += _fold8(dyp * jnp.concatenate(pms, axis=1))
        dzps = []
        for gi, win in enumerate(POOL_WINDOWS):
            sl = slice(gi * dg, (gi + 1) * dg)
            acc = qbuf[0:tm, sl]
            for s in range(1, win):
                acc = acc + qbuf[pl.ds(s, tm), sl]
            dzps.append(acc - dps[gi])
        qbuf[tm:tm + HALO, :] = qbuf[0:HALO, :]
        dz = [dzb.astype(BF16), dzc.astype(BF16), dzv.astype(BF16), jnp.concatenate(dzps, axis=1).astype(BF16)]
        dh1 = jnp.zeros((tm, d), F32)
        for j in range(N_CHIPS):
            dz_ref[j] = dz[j]
            dh1 = dh1 + _dot_nt(dz[j], win_ref[j])
        gx_ref[...] = DEEPNORM_ALPHA * dr1 + dh1 * (1.0 + sc1)
        acc_d[0] += _fold8(dx1v * xhat)
        acc_d[1] += _fold8(dx1v)
        acc_d[2] += _fold8(dr1 * mixv)
        acc_d[3] += _fold8(dh1 * xv)
        acc_d[4] += _fold8(dh1)

        @pl.when(i == nt - 1)
        def _():
            for k in range(5):
                sd_ref[k:k + 1, :] = jnp.sum(acc_d[k], axis=0, keepdims=True)
            sd_ref[5:8, :] = jnp.zeros((3, d), F32)
            for k in range(4):
                sc_ref[k:k + 1, :] = jnp.sum(acc_c[k], axis=0, keepdims=True)
            sc_ref[4:8, :] = jnp.zeros((4, cw), F32)
            dwp_ref[...] = acc_p[...]

    rtile = lambda w: pl.BlockSpec((tm, w), lambda i: (nt - 1 - i, 0))
    whole = lambda shape: pl.BlockSpec(shape, lambda i: tuple(0 for _ in shape))
    return pl.pallas_call(
        body, name="bwd_mix",
        out_shape=(jax.ShapeDtypeStruct((t_len, d), F32), jax.ShapeDtypeStruct((N_CHIPS, t_len, cw), BF16),
                   jax.ShapeDtypeStruct((t_len, d), BF16), jax.ShapeDtypeStruct((SUBLANES, d), F32),
                   jax.ShapeDtypeStruct((SUBLANES, cw), F32), jax.ShapeDtypeStruct((N_GROUPS, dg, dg), F32)),
        grid=(nt,),
        in_specs=[rtile(d), rtile(d), rtile(d), rtile(5 * cw)] + [VMEM_SPEC] * 7
        + [pl.BlockSpec(memory_space=pl.ANY)],
        out_specs=(rtile(d), pl.BlockSpec((N_CHIPS, tm, cw), lambda i: (0, nt - 1 - i, 0)), rtile(d),
                   whole((SUBLANES, d)), whole((SUBLANES, cw)), whole((N_GROUPS, dg, dg))),
        scratch_shapes=[pltpu.VMEM((tm + HALO, cw), F32), pltpu.VMEM((tm + HALO, cw), F32),
                        pltpu.VMEM((5, SUBLANES, d), F32), pltpu.VMEM((4, SUBLANES, cw), F32),
                        pltpu.VMEM((N_GROUPS, dg, dg), F32)],
        compiler_params=_params(dimension_semantics=("arbitrary",)),
    )(x, mix, dx1, zs, mod, conv_w, w_in4, w_pool, pool_scale, w_out, ln_g, after)


def _mix_wgrad(h1, dz4, cat, dmix, small, dmod_rows, tt):
    t_len, d = h1.shape
    cw = dz4.shape[2]
    nt = t_len // tt
    ro = (2 * cw) // N_CHIPS
    rh = (d // 2, ro // 2)
    n_small = len(small)

    def body(h1_ref, dz_ref, cat_ref, dmix_ref, *rest):
        p_refs = rest[:n_small]
        qi_ref, qo_ref, gi_ref, go_ref = rest[n_small:n_small + 4]
        sum_refs = rest[n_small + 4:2 * n_small + 4]
        dmod_ref, dmod_sum_ref = rest[2 * n_small + 4:2 * n_small + 6]
        scratch = rest[2 * n_small + 6:]
        acc_i, acc_o, rcv_i, rcv_o = scratch[:4]
        all_bufs = scratch[4:4 + n_small]
        s_sem, r_sem, s_small, r_small_sem = scratch[4 + n_small:]
        t = pl.program_id(0)
        x, y, c = _mesh_pos()
        chip = 2 * x + y
        groups = [(p_refs[k], sum_refs[k], None, all_bufs[k]) for k in range(n_small)]
        small_start, small_finish = _small_allreduce_steps(groups, s_small, r_small_sem)
        accs, rcvs, q_refs, g_refs = (acc_i, acc_o), (rcv_i, rcv_o), (qi_ref, qo_ref), (gi_ref, go_ref)

        @pl.when(t == 0)
        def _():
            small_start()
            acc_i[...] = jnp.zeros(acc_i.shape, F32)
            acc_o[...] = jnp.zeros(acc_o.shape, F32)

        def accumulate(j):
            acc_i[j] += _dot_tn(h1_ref[...], dz_ref[j])
            acc_o[j] += _dot_tn(cat_ref[:, j * ro:(j + 1) * ro], dmix_ref[...])

        def to_sibling(a, j):
            theirs = pl.multiple_of((1 - c) * rh[a], rh[a])
            return pltpu.make_async_remote_copy(accs[a].at[j].at[pl.ds(theirs, rh[a]), :], rcvs[a].at[j],
                                                s_sem.at[a * N_CHIPS + j], r_sem.at[a * N_CHIPS + j],
                                                device_id=(x, y, 1 - c), device_id_type=MESH)

        @pl.when(t < nt - 1)
        def _():
            for j in range(N_CHIPS):
                accumulate(j)

        @pl.when(t == nt - 1)
        def _():
            for j in range(N_CHIPS):
                accumulate(j)
                for a in range(2):
                    to_sibling(a, j).start()
            for a in range(2):
                mine = pl.multiple_of(c * rh[a], rh[a])
                for j in range(N_CHIPS):
                    to_sibling(a, j).wait()
                    q_refs[a][j] = (accs[a][j, pl.ds(mine, rh[a]), :] + rcvs[a][j]).astype(BF16)
                g_refs[a][chip] = q_refs[a][chip]
            small_finish()
            for k, (arr, row) in enumerate(dmod_rows):
                cols = slice(k * d, (k + 1) * d)
                dmod_sum_ref[:, cols] = sum_refs[arr][row:row + 1, :]
                for b in range(N_DEV):
                    dmod_ref[b:b + 1, cols] = all_bufs[arr][b, row:row + 1, :]

    stacks = ((N_CHIPS, rh[0], cw), (N_CHIPS, rh[1], d))
    n_cp = n_small * (N_DEV - 1)
    n_mod = len(dmod_rows) * d
    return pl.pallas_call(
        body, name="mix_wgrad",
        out_shape=tuple(jax.ShapeDtypeStruct(s, BF16) for s in stacks) * 2
        + tuple(jax.ShapeDtypeStruct(a.shape, F32) for a in small)
        + (jax.ShapeDtypeStruct((N_DEV, n_mod), F32), jax.ShapeDtypeStruct((1, n_mod), F32)),
        grid=(nt,),
        in_specs=[pl.BlockSpec((tt, d), lambda t: (t, 0)), pl.BlockSpec((N_CHIPS, tt, cw), lambda t: (0, t, 0)),
                  pl.BlockSpec((tt, 2 * cw), lambda t: (t, 0)), pl.BlockSpec((tt, d), lambda t: (t, 0))]
        + [VMEM_SPEC] * n_small,
        out_specs=(VMEM_SPEC,) * (6 + n_small),
        scratch_shapes=[pltpu.VMEM((N_CHIPS, d, cw), F32), pltpu.VMEM((N_CHIPS, ro, d), F32),
                        pltpu.VMEM(stacks[0], F32), pltpu.VMEM(stacks[1], F32)]
        + [pltpu.VMEM((N_DEV,) + a.shape, F32) for a in small]
        + [_dma_sems(2 * N_CHIPS), _dma_sems(2 * N_CHIPS), _dma_sems(n_cp), _dma_sems(n_cp)],
        compiler_params=_params(dimension_semantics=("arbitrary",)),
    )(h1, dz4, cat, dmix, *small)


def kernel(x, c, w_ada, b_ada, w_in, conv_w, w_pool, pool_scale, w_out, ln1_g, ln1_b, w_mlp_in, w_mlp_out, ln2_g, ln2_b, loss_target, m_w_ada, m_b_ada, m_w_in, m_conv_w, m_w_pool, m_pool_scale, m_w_out, m_ln1_g, m_ln1_b, m_w_mlp_in, m_w_mlp_out, m_ln2_g, m_ln2_b, v_w_ada, v_b_ada, v_w_in, v_conv_w, v_w_pool, v_pool_scale, v_w_out, v_ln1_g, v_ln1_b, v_w_mlp_in, v_w_mlp_out, v_ln2_g, v_ln2_b):
    t_len, d = x.shape[1], x.shape[2]
    cw = w_in.shape[2]
    cq = conv_w.shape[2]
    dg = w_pool.shape[2]
    assert cw == N_GROUPS * dg and cq * N_CHIPS == cw and LANES % cq == 0
    tm_mix = min(MIX_TOKEN_TILE, t_len)
    tm_mlp = min(MLP_TOKEN_TILE, t_len)
    tt = min(WGRAD_TILE, t_len // 4)
    chip = 2 * lax.axis_index("x") + lax.axis_index("y")
    pos = jnp.stack([lax.axis_index("c"), chip]).astype(jnp.int32)

    x2, tgt = x[0], loss_target[0]
    big_w = [w_in[0], w_out[0], w_mlp_in[0], w_mlp_out[0]]
    big_m = [m_w_in[0], m_w_out[0], m_w_mlp_in[0], m_w_mlp_out[0]]
    big_v = [v_w_in[0], v_w_out[0], v_w_mlp_in[0], v_w_mlp_out[0]]
    names = ["w_in", "w_out", "w_mlp_in", "w_mlp_out"]

    mod, cond_all, conv_full, w_in4, w_out4, w_mi_own, w_mo_own = _setup_exchange(
        c, w_ada[0], b_ada, conv_w[0], big_w, 2)
    w_out_full = w_out4.reshape(2 * cw, d)
    w_pool_g = w_pool[0]

    w_mi4, w_mo4, x1, zs, mix, h1, cat = _fwd_mix(x2, mod, conv_full, w_in4, w_pool_g, pool_scale, w_out_full,
                                                  ln1_g, ln1_b, [w_mi_own, w_mo_own], tm_mix)
    dx1, da4, s4, df, h2, st2 = _mlp_fwd_bwd(x1, tgt, mod, w_mi4, w_mo4, ln2_g, ln2_b, tm_mlp)
    mlp_qi, mlp_qo, mlp_gi, mlp_go = _mlp_wgrad(h2, da4, s4, df, pos, tt)
    mlp_started = _split_start([mlp_qi, mlp_qo, mlp_gi, mlp_go], _chunk_exchange_plan(2), 2 * (N_CHIPS - 1),
                               "exchange_mlp")
    grad_x, dz4, dmix, st1, stc, dw_pool = _bwd_mix(x2, mix, dx1, zs, mod, conv_full, w_in4, w_pool_g, pool_scale,
                                                    w_out_full, ln1_g, mlp_started[-1], tm_mix)

    dmod_rows = [(0, 4), (0, 3), (0, 2), (1, 4), (1, 3), (1, 2)]
    mix_qi, mix_qo, mix_gi, mix_go, t1, t2, tc, g_w_pool, dmod_all, g_b_ada = _mix_wgrad(
        h1, dz4, cat, dmix, [st1, st2, stc, dw_pool], dmod_rows, tt)
    loss = t2[5, 0]
    na = w_ada.shape[2]
    dmod_mine = lax.dynamic_slice_in_dim(dmod_all, chip * na, na, axis=1)

    exchange, n_cp = _chunk_exchange_plan(2), 2 * (N_CHIPS - 1)
    mlp_got = _split_wait(mlp_started, exchange, t1, "exchange_mlp")[2:]
    mix_started = _split_start([mix_qi, mix_qo, mix_gi, mix_go], exchange, n_cp, "exchange_mix")
    mlp_halves = list(_sum_chips(mlp_got, pos, mix_started[-1], "rs_total_mlp"))
    join = _sibling_join_plan([h.shape for h in mlp_halves])
    join_started = _split_start(mlp_halves, join, 2, "join_mlp")
    g_ada, d_ada, nm_ada, nv_ada = _adamw_ada(cond_all, dmod_mine, w_ada[0], m_w_ada[0], v_w_ada[0],
                                              join_started[-1])
    mlp_g = _split_wait(join_started, join, d_ada, "join_mlp")
    mlp_out = _adamw_big(mlp_g, big_w[2:], big_m[2:], big_v[2:], "adamw_mlp")
    mix_got = _split_wait(mix_started, exchange, mlp_out[1][0], "exchange_mix")[2:]
    mix_halves = list(_sum_chips(mix_got, pos, d_ada, "rs_total_mix"))
    mix_g = _sibling_join(mix_halves, "mix")
    mix_out = _adamw_big(mix_g, big_w[:2], big_m[:2], big_v[:2], "adamw_mix")
    big_g = list(mix_g) + list(mlp_g)
    big_out = mix_out + mlp_out

    sources = [g_b_ada, ("chip_cols", 2, 0, 3, cq), g_w_pool.reshape(-1, dg), ("rows", 2, 3, 4),
               ("rows", 0, 0, 1), ("rows", 0, 1, 2), ("rows", 1, 0, 1), ("rows", 1, 1, 2)]
    small_w = [b_ada, conv_w[0], w_pool.reshape(-1, dg), pool_scale, ln1_g, ln1_b, ln2_g, ln2_b]
    small_m = [m_b_ada, m_conv_w[0], m_w_pool.reshape(-1, dg), m_pool_scale, m_ln1_g, m_ln1_b, m_ln2_g, m_ln2_b]
    small_v = [v_b_ada, v_conv_w[0], v_w_pool.reshape(-1, dg), v_pool_scale, v_ln1_g, v_ln1_b, v_ln2_g, v_ln2_b]
    cut, sm = _adamw_small(pos, [t1, t2, tc], sources, small_w, small_m, small_v)
    small_g = [sources[0], cut[0], sources[2]] + list(cut[1:])
    ns = len(small_w)
    s_delta, s_m, s_v = sm[:ns], sm[ns:2 * ns], sm[2 * ns:]

    def assemble(ada, small, big):
        return [ada[None], small[0], big[0][None], small[1][None], small[2].reshape(w_pool.shape), small[3],
                big[1][None], small[4], small[5], big[2][None], big[3][None], small[6], small[7]]

    grads = assemble(g_ada, small_g, big_g)
    deltas = assemble(d_ada, s_delta, [o[0] for o in big_out])
    new_m = assemble(nm_ada, s_m, [o[1] for o in big_out])
    new_v = assemble(nv_ada, s_v, [o[2] for o in big_out])
    return (loss, grad_x[None], *grads, *deltas, *new_m, *new_v)
```

```python
import jax
import jax.numpy as jnp
from jax import lax
from jax.experimental import pallas as pl
from jax.experimental.pallas import tpu as pltpu

F32 = jnp.float32
BF16 = jnp.bfloat16
MESH = pl.DeviceIdType.MESH

LN_EPS = 1e-5
DEEPNORM_ALPHA = 2.0 ** 0.25
POOL_WINDOWS = (2, 4, 8, 16)
N_GROUPS = len(POOL_WINDOWS)
HALO = 16
N_CHIPS = 4
N_DEV = 8
LANES = 128
SUBLANES = 8
VMEM_LIMIT = 56 * 1024 * 1024
MIX_TOKEN_TILE = 512
MLP_TOKEN_TILE = 256
WGRAD_TILE = 1024

ADAM_LR = 0.001
ADAM_B1 = 0.9
ADAM_B2 = 0.999
ADAM_EPS = 1e-08
ADAM_WD = 0.01
ADAM_STEP = 10

VMEM_SPEC = pl.BlockSpec(memory_space=pltpu.VMEM)
HBM_SPEC = pl.BlockSpec(memory_space=pltpu.HBM)


def _dot(a, b):
    return jnp.dot(a, b, preferred_element_type=F32)


def _dot_nt(a, b):
    return lax.dot_general(a, b, (((1,), (1,)), ((), ())), preferred_element_type=F32)


def _dot_tn(a, b):
    return lax.dot_general(a, b, (((0,), (0,)), ((), ())), preferred_element_type=F32)


def _fold8(v):
    r, n = v.shape
    return jnp.sum(v.reshape(r // SUBLANES, SUBLANES, n), axis=0)


def _mesh_pos():
    return lax.axis_index("x"), lax.axis_index("y"), lax.axis_index("c")


def _flip(v, bit):
    return 1 - v if bit else v


def _params(**kw):
    return pltpu.CompilerParams(vmem_limit_bytes=VMEM_LIMIT, **kw)


def _dma_sems(n):
    return pltpu.SemaphoreType.DMA((n,))


def _stack_gather(stacks, rows, s_ici, r_ici, s_d2d, r_d2d):
    x, y, c = _mesh_pos()
    chip = 2 * x + y
    pairs = [(i, m) for i in range(len(stacks)) for m in range(1, N_CHIPS)]

    def blk(i, slot, which):
        rh = rows[i] // 2
        return stacks[i].at[slot].at[pl.ds(which * rh, rh), :]

    def other(m):
        return 2 * _flip(x, m & 2) + _flip(y, m & 1)

    def ici(i, m, slot, to):
        k = i * (N_CHIPS - 1) + m - 1
        return pltpu.make_async_remote_copy(blk(i, slot, c), blk(i, slot, c), s_ici.at[k], r_ici.at[k],
                                            device_id=to, device_id_type=MESH)

    def d2d(i, m, which, to):
        k = i * (N_CHIPS - 1) + m - 1
        return pltpu.make_async_remote_copy(blk(i, other(m), which), blk(i, other(m), which),
                                            s_d2d.at[k], r_d2d.at[k], device_id=to, device_id_type=MESH)

    def start():
        for i, m in pairs:
            ici(i, m, chip, (_flip(x, m & 2), _flip(y, m & 1), c)).start()

    def forward(which=None):
        for i, m in pairs:
            if which is None or i == which:
                ici(i, m, other(m), (x, y, c)).wait_recv()
                d2d(i, m, c, (x, y, 1 - c)).start()

    def finish():
        for i, m in pairs:
            d2d(i, m, 1 - c, (x, y, c)).wait_recv()
        for i, m in pairs:
            ici(i, m, chip, (x, y, c)).wait_send()
            d2d(i, m, c, (x, y, c)).wait_send()

    return start, forward, finish


def _chunk_exchange(qs, gots, s_sem, r_sem):
    x, y, c = _mesh_pos()
    chip = 2 * x + y
    pairs = [(i, m) for i in range(len(qs)) for m in range(1, N_CHIPS)]

    def other(m):
        return 2 * _flip(x, m & 2) + _flip(y, m & 1)

    def send(i, m, to):
        k = i * (N_CHIPS - 1) + m - 1
        return pltpu.make_async_remote_copy(qs[i].at[other(m)], gots[i].at[chip], s_sem.at[k], r_sem.at[k],
                                            device_id=to, device_id_type=MESH)

    def arrival(i, m):
        k = i * (N_CHIPS - 1) + m - 1
        return pltpu.make_async_remote_copy(qs[i].at[other(m)], gots[i].at[other(m)], s_sem.at[k], r_sem.at[k],
                                            device_id=(x, y, c), device_id_type=MESH)

    def start():
        for i, m in pairs:
            send(i, m, (_flip(x, m & 2), _flip(y, m & 1), c)).start()

    def finish():
        for i, m in pairs:
            arrival(i, m).wait_recv()
        for i, m in pairs:
            send(i, m, (x, y, c)).wait_send()

    return start, finish


def _mod_scratch(d, na, cq):
    return [pltpu.VMEM((SUBLANES, d), F32), pltpu.VMEM((N_DEV, SUBLANES, d), F32),
            pltpu.VMEM((N_CHIPS, SUBLANES, na), F32),
            pltpu.VMEM((SUBLANES, cq), F32), pltpu.VMEM((N_CHIPS, SUBLANES, cq), F32),
            _dma_sems(N_DEV - 1), _dma_sems(N_DEV - 1),
            _dma_sems(N_CHIPS - 1), _dma_sems(N_CHIPS - 1), _dma_sems(N_CHIPS - 1), _dma_sems(N_CHIPS - 1)]


def _mod_steps(d, na, cq):
    def steps(c_ref, w_ref, b_ref, cw_ref, mod_ref, cond_ref, conv_ref,
              csend, cbuf, mbuf, cvsend, cvbuf, s1, r1, s2, r2, s3, r3):
        x, y, c = _mesh_pos()
        me = 4 * x + 2 * y + c
        chip = 2 * x + y
        csend[...] = jnp.broadcast_to(c_ref[...], (SUBLANES, d))
        cbuf[me] = csend[...]
        first = []
        for m in range(1, N_DEV):
            peer = (_flip(x, m & 4), _flip(y, m & 2), _flip(c, m & 1))
            cp = pltpu.make_async_remote_copy(csend, cbuf.at[me], s1.at[m - 1], r1.at[m - 1],
                                              device_id=peer, device_id_type=MESH)
            cp.start()
            first.append(cp)
        cvsend[...] = jnp.zeros((SUBLANES, cq), F32)
        cvsend[0:3, :] = cw_ref[...]
        cvbuf[chip] = cvsend[...]
        for m in range(1, N_DEV):
            src = 4 * _flip(x, m & 4) + 2 * _flip(y, m & 2) + _flip(c, m & 1)
            pltpu.make_async_remote_copy(csend, cbuf.at[src], s1.at[m - 1], r1.at[m - 1],
                                         device_id=(x, y, c), device_id_type=MESH).wait_recv()
        rows = lax.broadcasted_iota(jnp.int32, (SUBLANES, d), 0)
        call = jnp.zeros((SUBLANES, d), F32)
        for b in range(N_DEV):
            call = jnp.where(rows == b, cbuf[b], call)
        cond = call * jax.nn.sigmoid(call)
        cond_ref[...] = cond
        part = jnp.dot(cond, w_ref[...], preferred_element_type=F32, precision=lax.Precision.HIGHEST)
        mbuf[chip] = part
        second = []
        for m in range(1, N_CHIPS):
            peer = (_flip(x, m & 2), _flip(y, m & 1), c)
            cp = pltpu.make_async_remote_copy(mbuf.at[chip], mbuf.at[chip], s2.at[m - 1], r2.at[m - 1],
                                              device_id=peer, device_id_type=MESH)
            cp.start()
            second.append(cp)
            cp = pltpu.make_async_remote_copy(cvsend, cvbuf.at[chip], s3.at[m - 1], r3.at[m - 1],
                                              device_id=peer, device_id_type=MESH)
            cp.start()
            second.append(cp)
        for m in range(1, N_CHIPS):
            src = 2 * _flip(x, m & 2) + _flip(y, m & 1)
            pltpu.make_async_remote_copy(mbuf.at[src], mbuf.at[src], s2.at[m - 1], r2.at[m - 1],
                                         device_id=(x, y, c), device_id_type=MESH).wait_recv()
            pltpu.make_async_remote_copy(cvsend, cvbuf.at[src], s3.at[m - 1], r3.at[m - 1],
                                         device_id=(x, y, c), device_id_type=MESH).wait_recv()
        for cp in first + second:
            cp.wait_send()
        rows_n = lax.broadcasted_iota(jnp.int32, (SUBLANES, na), 0)
        for k in range(N_CHIPS):
            mine = jnp.sum(jnp.where(rows_n == me, mbuf[k], 0.0), axis=0, keepdims=True)
            mod_ref[:, k * na:(k + 1) * na] = jnp.broadcast_to(mine + b_ref[:, k * na:(k + 1) * na], (SUBLANES, na))
            conv_ref[:, k * cq:(k + 1) * cq] = cvbuf[k]

    return steps


def _setup_exchange(c_row, w_ada_s, b_ada, conv_w_s, shards, n_now):
    n = len(shards)
    n_cp = n_now * (N_CHIPS - 1)
    d, na, cq = c_row.shape[1], w_ada_s.shape[1], conv_w_s.shape[1]
    mod_scratch = _mod_scratch(d, na, cq)
    mod_steps = _mod_steps(d, na, cq)

    def body(*refs):
        mod_in, ins = refs[:4], refs[4:4 + n]
        mod_out, outs = refs[4 + n:7 + n], refs[7 + n:7 + 2 * n]
        bufs = refs[7 + 2 * n:7 + 3 * n]
        mod_scr = refs[7 + 3 * n:7 + 3 * n + len(mod_scratch)]
        s_ici, r_ici, s_d2d, r_d2d, s_loc = refs[7 + 3 * n + len(mod_scratch):7 + 3 * n + len(mod_scratch) + 5]
        w_ada_buf, s_in = refs[7 + 3 * n + len(mod_scratch) + 5:7 + 3 * n + len(mod_scratch) + 7]
        late_bufs = refs[7 + 3 * n + len(mod_scratch) + 7:]
        x, y, _ = _mesh_pos()
        chip = 2 * x + y
        loads = [pltpu.make_async_copy(mod_in[1], w_ada_buf, s_in.at[0])]
        loads += [pltpu.make_async_copy(ins[i], late_bufs[i - n_now], s_in.at[1 + i - n_now]) for i in range(n_now, n)]
        for cp in loads:
            cp.start()

        def place(i, src):
            bufs[i][...] = src[...].astype(BF16)
            cp = pltpu.make_async_copy(bufs[i], outs[i].at[chip], s_loc.at[i])
            cp.start()
            return cp

        start, forward, finish = _stack_gather(outs[:n_now], [s.shape[0] for s in shards[:n_now]],
                                               s_ici, r_ici, s_d2d, r_d2d)
        for cp in [place(i, ins[i]) for i in range(n_now)]:
            cp.wait()
        start()
        later = []
        for i in range(n_now, n):
            loads[1 + i - n_now].wait()
            later.append(place(i, late_bufs[i - n_now]))
        loads[0].wait()
        mod_steps(mod_in[0], w_ada_buf, mod_in[2], mod_in[3], *mod_out, *mod_scr)
        forward()
        finish()
        for cp in later:
            cp.wait()

    any_spec = pl.BlockSpec(memory_space=pl.ANY)
    return pl.pallas_call(
        body, name="setup_exchange",
        out_shape=(jax.ShapeDtypeStruct((SUBLANES, N_CHIPS * na), F32), jax.ShapeDtypeStruct((SUBLANES, d), F32),
                   jax.ShapeDtypeStruct((SUBLANES, N_CHIPS * cq), F32))
        + tuple(jax.ShapeDtypeStruct((N_CHIPS,) + s.shape, BF16) for s in shards),
        in_specs=[VMEM_SPEC, any_spec, VMEM_SPEC, VMEM_SPEC] + [VMEM_SPEC] * n_now + [any_spec] * (n - n_now),
        out_specs=(VMEM_SPEC,) * 3 + (HBM_SPEC,) * n,
        scratch_shapes=[pltpu.VMEM(s.shape, BF16) for s in shards] + mod_scratch
        + [_dma_sems(n_cp), _dma_sems(n_cp), _dma_sems(n_cp), _dma_sems(n_cp), _dma_sems(n)]
        + [pltpu.VMEM(w_ada_s.shape, F32), _dma_sems(1 + n - n_now)]
        + [pltpu.VMEM(s.shape, F32) for s in shards[n_now:]],
        compiler_params=_params(),
    )(c_row, w_ada_s, b_ada, conv_w_s, *shards)


def _sibling_join_plan(shapes):
    def plan(refs, s_sem, r_sem):
        x, y, c = _mesh_pos()

        def half(i, which):
            rh = shapes[i][0] // 2
            return refs[i].at[pl.ds(which * rh, rh), :]

        def start():
            for i in range(len(refs)):
                pltpu.make_async_remote_copy(half(i, c), half(i, c), s_sem.at[i], r_sem.at[i],
                                             device_id=(x, y, 1 - c), device_id_type=MESH).start()

        def finish():
            for i in range(len(refs)):
                pltpu.make_async_remote_copy(half(i, 1 - c), half(i, 1 - c), s_sem.at[i], r_sem.at[i],
                                             device_id=(x, y, c), device_id_type=MESH).wait_recv()
            for i in range(len(refs)):
                pltpu.make_async_remote_copy(half(i, c), half(i, c), s_sem.at[i], r_sem.at[i],
                                             device_id=(x, y, c), device_id_type=MESH).wait_send()

        return start, finish

    return plan


def _chunk_exchange_plan(n):
    return lambda refs, s_sem, r_sem: _chunk_exchange(refs[:n], refs[n:], s_sem, r_sem)


def _sibling_join(gs, tag):
    n = len(gs)
    plan = _sibling_join_plan([g.shape for g in gs])

    def body(*refs):
        start, finish = plan(refs[n:2 * n], refs[2 * n], refs[2 * n + 1])
        start()
        finish()

    return pl.pallas_call(
        body, name="rs_sibling_join_" + tag,
        out_shape=tuple(jax.ShapeDtypeStruct(g.shape, g.dtype) for g in gs),
        in_specs=[HBM_SPEC] * n, out_specs=(HBM_SPEC,) * n,
        input_output_aliases={i: i for i in range(n)},
        scratch_shapes=[_dma_sems(n), _dma_sems(n)],
        compiler_params=_params(),
    )(*gs)


SEM_SPEC = pl.BlockSpec(memory_space=pltpu.SEMAPHORE)
SIDE_EFFECT = pltpu.SideEffectType.DATAFLOW_SIDE_EFFECTING


def _split_start(arrays, plan, n_cp, tag):
    n = len(arrays)

    def body(*refs):
        start, _ = plan(refs[n + 2:2 * n + 2], refs[n], refs[n + 1])
        start()
        token = refs[2 * n + 2]
        token[...] = jnp.zeros_like(token)

    through = [pltpu.with_memory_space_constraint(a, pltpu.HBM) for a in arrays]
    return pl.pallas_call(
        body, name=tag + "_start",
        out_shape=(pltpu.SemaphoreType.DMA((n_cp,)), pltpu.SemaphoreType.DMA((n_cp,)))
        + tuple(pltpu.HBM(a.shape, a.dtype) for a in through) + (jax.ShapeDtypeStruct((SUBLANES, LANES), F32),),
        in_specs=[HBM_SPEC] * n, out_specs=(SEM_SPEC, SEM_SPEC) + (HBM_SPEC,) * n + (VMEM_SPEC,),
        input_output_aliases={k: 2 + k for k in range(n)},
        compiler_params=pltpu.CompilerParams(has_side_effects=SIDE_EFFECT),
    )(*through)


def _split_wait(started, plan, after, tag):
    send_sems, recv_sems, arrays = started[0], started[1], started[2:-1]
    n = len(arrays)

    def body(*refs):
        _, finish = plan(refs[:n], refs[n], refs[n + 1])
        finish()

    return pl.pallas_call(
        body, name=tag + "_wait",
        out_shape=tuple(pltpu.HBM(a.shape, a.dtype) for a in arrays),
        in_specs=[HBM_SPEC] * n + [SEM_SPEC, SEM_SPEC, pl.BlockSpec(memory_space=pl.ANY)],
        out_specs=(HBM_SPEC,) * n,
        input_output_aliases={k: k for k in range(n)},
        compiler_params=pltpu.CompilerParams(has_side_effects=SIDE_EFFECT),
    )(*arrays, send_sems, recv_sems, after)


def _small_allreduce_steps(groups, s_sem, r_sem):
    x, y, c = _mesh_pos()
    me = 4 * x + 2 * y + c
    pairs = [(k, m) for k in range(len(groups)) for m in range(1, N_DEV)]

    def send(k, m, to):
        p_ref, all_ref = groups[k][0], groups[k][3]
        i = k * (N_DEV - 1) + m - 1
        return pltpu.make_async_remote_copy(p_ref, all_ref.at[me], s_sem.at[i], r_sem.at[i],
                                            device_id=to, device_id_type=MESH)

    def start():
        for p_ref, _, _, all_ref in groups:
            all_ref[me] = p_ref[...]
        for k, m in pairs:
            send(k, m, (_flip(x, m & 4), _flip(y, m & 2), _flip(c, m & 1))).start()

    def finish():
        for k, m in pairs:
            p_ref, all_ref = groups[k][0], groups[k][3]
            i = k * (N_DEV - 1) + m - 1
            src = 4 * _flip(x, m & 4) + 2 * _flip(y, m & 2) + _flip(c, m & 1)
            pltpu.make_async_remote_copy(p_ref, all_ref.at[src], s_sem.at[i], r_sem.at[i],
                                         device_id=(x, y, c), device_id_type=MESH).wait_recv()
        for k, m in pairs:
            send(k, m, (x, y, c)).wait_send()
        for _, sum_ref, all_out_ref, all_ref in groups:
            sum_ref[...] = (((all_ref[0] + all_ref[1]) + (all_ref[2] + all_ref[3]))
                            + ((all_ref[4] + all_ref[5]) + (all_ref[6] + all_ref[7])))
            if all_out_ref is not None:
                all_out_ref[...] = all_ref[...]

    return start, finish


def _row_block(rows, cap=256):
    rb = min(rows, cap)
    assert rows % rb == 0
    return rb


def _sum_chips(gots, pos, after, name, nb=2):
    n = len(gots)
    blocks = [(g.shape[1] // nb, g.shape[2]) for g in gots]

    def body(pos_ref, *refs):
        for a_ref, o_ref in zip(refs[:n], refs[n + 1:]):
            a = a_ref[...].astype(F32)
            o_ref[...] = (a[0] + a[1]) + (a[2] + a[3])

    return pl.pallas_call(
        body, name=name,
        out_shape=tuple(jax.ShapeDtypeStruct((2 * g.shape[1], g.shape[2]), F32) for g in gots),
        grid_spec=pltpu.PrefetchScalarGridSpec(
            num_scalar_prefetch=1, grid=(nb,),
            in_specs=[pl.BlockSpec((N_CHIPS,) + b, lambda i, p: (0, i, 0)) for b in blocks]
            + [pl.BlockSpec(memory_space=pl.ANY)],
            out_specs=tuple(pl.BlockSpec(b, lambda i, p: (p[0] * nb + i, 0)) for b in blocks)),
        compiler_params=_params(dimension_semantics=("arbitrary",)),
    )(pos, *gots, after)


def _adamw_math(w, g, m, v):
    m = ADAM_B1 * m + (1.0 - ADAM_B1) * g
    v = ADAM_B2 * v + (1.0 - ADAM_B2) * jnp.square(g)
    m_hat = m / (1.0 - ADAM_B1 ** ADAM_STEP)
    v_hat = v / (1.0 - ADAM_B2 ** ADAM_STEP)
    delta = -ADAM_LR * (m_hat / (jnp.sqrt(v_hat) + ADAM_EPS) + ADAM_WD * w)
    return delta, m, v


def _adamw_big(gs, ws, ms, vs, name, nb=4):
    n = len(ws)
    specs = [pl.BlockSpec((w.shape[0] // nb, w.shape[1]), lambda i: (i, 0)) for w in ws]

    def body(*refs):
        g_refs, w_refs, m_refs, v_refs = (refs[k * n:(k + 1) * n] for k in range(4))
        outs = refs[4 * n:]
        for i in range(n):
            d, mn, vn = _adamw_math(w_refs[i][...], g_refs[i][...], m_refs[i][...], v_refs[i][...])
            outs[3 * i][...] = d
            outs[3 * i + 1][...] = mn
            outs[3 * i + 2][...] = vn

    out = pl.pallas_call(
        body, name=name,
        out_shape=tuple(jax.ShapeDtypeStruct(w.shape, F32) for w in ws for _ in range(3)),
        grid=(nb,), in_specs=specs * 4, out_specs=tuple(s for s in specs for _ in range(3)),
        compiler_params=_params(dimension_semantics=("arbitrary",)),
    )(*gs, *ws, *ms, *vs)
    return [out[3 * i:3 * i + 3] for i in range(n)]


def _adamw_ada(cond, dmod, w, m, v, after):
    d, na = w.shape
    rb = _row_block(d, 128)

    def body(c_ref, dm_ref, w_ref, m_ref, v_ref, after_ref, g_ref, d_ref, mo_ref, vo_ref):
        g = _dot_tn(c_ref[...].astype(BF16), dm_ref[...].astype(BF16))
        dl, mn, vn = _adamw_math(w_ref[...], g, m_ref[...], v_ref[...])
        g_ref[...] = g
        d_ref[...] = dl
        mo_ref[...] = mn
        vo_ref[...] = vn

    spec = pl.BlockSpec((rb, na), lambda i: (i, 0))
    return pl.pallas_call(
        body, name="adamw_w_ada",
        out_shape=(jax.ShapeDtypeStruct((d, na), F32),) * 4,
        grid=(d // rb,),
        in_specs=[pl.BlockSpec((N_DEV, rb), lambda i: (0, i)), pl.BlockSpec((N_DEV, na), lambda i: (0, 0)),
                  spec, spec, spec, pl.BlockSpec(memory_space=pl.ANY)],
        out_specs=(spec,) * 4,
        compiler_params=_params(dimension_semantics=("arbitrary",)),
    )(cond, dmod, w, m, v, after)


def _adamw_small(pos, stats, sources, ws, ms, vs):
    n = len(ws)
    direct = [s for s in sources if not isinstance(s, tuple)]
    n_cut = n - len(direct)

    def body(pos_ref, *refs):
        stat_refs, direct_refs = refs[:len(stats)], refs[len(stats):len(stats) + len(direct)]
        w_refs, m_refs, v_refs = (refs[len(stats) + len(direct) + k * n:len(stats) + len(direct) + (k + 1) * n]
                                  for k in range(3))
        outs = refs[len(stats) + len(direct) + 3 * n:]
        cut_refs, res = outs[:n_cut], outs[n_cut:]
        chip = pos_ref[1]
        i_direct = i_cut = 0
        for i, src in enumerate(sources):
            if not isinstance(src, tuple):
                g = direct_refs[i_direct][...]
                i_direct += 1
            else:
                if src[0] == "rows":
                    g = stat_refs[src[1]][src[2]:src[3], :]
                else:
                    _, k, lo, hi, width = src
                    g = jnp.zeros((hi - lo, width), F32)
                    for j in range(N_CHIPS):
                        g = jnp.where(chip == j, stat_refs[k][lo:hi, j * width:(j + 1) * width], g)
                cut_refs[i_cut][...] = g
                i_cut += 1
            d, mn, vn = _adamw_math(w_refs[i][...], g, m_refs[i][...], v_refs[i][...])
            res[i][...] = d
            res[n + i][...] = mn
            res[2 * n + i][...] = vn

    shapes = tuple(jax.ShapeDtypeStruct(w.shape, F32) for w in ws)
    cut_shapes = tuple(jax.ShapeDtypeStruct(w.shape, F32) for w, s in zip(ws, sources) if isinstance(s, tuple))
    out = pl.pallas_call(
        body, name="adamw_small",
        out_shape=cut_shapes + shapes * 3,
        in_specs=[pl.BlockSpec(memory_space=pltpu.SMEM)] + [VMEM_SPEC] * (len(stats) + len(direct) + 3 * n),
        out_specs=(VMEM_SPEC,) * (n_cut + 3 * n),
        compiler_params=_params(),
    )(pos, *stats, *direct, *ws, *ms, *vs)
    return out[:n_cut], out[n_cut:]


def _layer_norm_fwd(r):
    mu = jnp.mean(r, axis=-1, keepdims=True)
    xc = r - mu
    var = jnp.mean(jnp.square(xc), axis=-1, keepdims=True)
    rstd = lax.rsqrt(var + LN_EPS)
    return xc * rstd, rstd


def _layer_norm_bwd(dy, xhat, rstd, gain):
    dxh = dy * gain
    m1 = jnp.mean(dxh, axis=-1, keepdims=True)
    m2 = jnp.mean(dxh * xhat, axis=-1, keepdims=True)
    return rstd * (dxh - m1 - xhat * m2)


def _inv_count(tile, tm, win):
    t = (tile * tm + lax.broadcasted_iota(jnp.int32, (tm, 1), 0) + 1).astype(F32)
    return 1.0 / jnp.minimum(t, float(win))


def _fwd_mix(x, mod, conv_w, w_in4, w_pool, pool_scale, w_out, ln_g, ln_b, late_stacks, tm):
    t_len, d = x.shape
    cw = w_in4.shape[2]
    dg = cw // N_GROUPS
    nt = t_len // tm
    n_late = len(late_stacks)
    n_cp = n_late * (N_CHIPS - 1)
    pass_steps = [min(((k + 1) * nt) // n_late + (1 if k + 1 < n_late else -2), nt - 1) if nt > 2 else nt - 1
                  for k in range(n_late)]
    pass_steps = [max(p, 0) for p in pass_steps]

    def body(x_ref, mod_ref, cw_ref, win_ref, wp_ref, ps_ref, wout_ref, g_ref, b_ref, *rest):
        late = rest[n_late:2 * n_late]
        x1_ref, zs_ref, mix_ref, h1_ref, cat_ref = rest[2 * n_late:2 * n_late + 5]
        ubuf, vbuf, s_ici, r_ici, s_d2d, r_d2d = rest[2 * n_late + 5:]
        i = pl.program_id(0)
        start, forward, finish = _stack_gather(late, [s.shape[1] for s in late_stacks], s_ici, r_ici, s_d2d, r_d2d)

        @pl.when(i == 0)
        def _():
            start()
            ubuf[0:HALO, :] = jnp.zeros((HALO, cw), F32)
            vbuf[0:HALO, :] = jnp.zeros((HALO, cw), F32)

        xv = x_ref[...]
        sh1, sc1, g1 = mod_ref[0:1, 0:d], mod_ref[0:1, d:2 * d], mod_ref[0:1, 2 * d:3 * d]
        h1 = (xv * (1.0 + sc1) + sh1).astype(BF16)
        h1_ref[...] = h1
        zb = _dot(h1, win_ref[0])
        zc = _dot(h1, win_ref[1])
        zv = _dot(h1, win_ref[2])
        zp = _dot(h1, win_ref[3])
        u = zc * zv
        ubuf[HALO:HALO + tm, :] = u
        u1 = ubuf[pl.ds(HALO - 1, tm), :]
        u2 = ubuf[pl.ds(HALO - 2, tm), :]
        conv = cw_ref[0:1, :] * u2 + cw_ref[1:2, :] * u1 + cw_ref[2:3, :] * u
        ubuf[0:HALO, :] = ubuf[tm:tm + HALO, :]
        yc = zb * conv
        vbuf[HALO:HALO + tm, :] = zp
        ps, pms = [], []
        for gi, win in enumerate(POOL_WINDOWS):
            sl = slice(gi * dg, (gi + 1) * dg)
            acc = zp[:, sl]
            for s in range(1, win):
                acc = acc + vbuf[pl.ds(HALO - s, tm), sl]
            p_g = acc * _inv_count(i, tm, win) - zp[:, sl]
            ps.append(p_g)
            pms.append(_dot(p_g.astype(BF16), wp_ref[gi].astype(BF16)))
        vbuf[0:HALO, :] = vbuf[tm:tm + HALO, :]
        pooled = jnp.concatenate(ps, axis=1)
        yp = jnp.concatenate(pms, axis=1) * ps_ref[...]
        cat = jnp.concatenate([yc, yp], axis=1).astype(BF16)
        cat_ref[...] = cat
        mix = _dot(cat, wout_ref[...])
        mix_ref[...] = mix
        xhat, _ = _layer_norm_fwd(DEEPNORM_ALPHA * xv + (1.0 + g1) * mix)
        x1_ref[...] = xhat * g_ref[...] + b_ref[...]
        zs_ref[:, 0 * cw:1 * cw] = zb.astype(BF16)
        zs_ref[:, 1 * cw:2 * cw] = zc.astype(BF16)
        zs_ref[:, 2 * cw:3 * cw] = zv.astype(BF16)
        zs_ref[:, 3 * cw:4 * cw] = conv.astype(BF16)
        zs_ref[:, 4 * cw:5 * cw] = pooled.astype(BF16)

        for k in range(n_late):
            @pl.when(i == pass_steps[k])
            def _():
                forward(k)

        @pl.when(i == nt - 1)
        def _():
            finish()

    tile = lambda w: pl.BlockSpec((tm, w), lambda i: (i, 0))
    n_in = 9
    return pl.pallas_call(
        body, name="fwd_mix",
        out_shape=tuple(jax.ShapeDtypeStruct(s.shape, s.dtype) for s in late_stacks)
        + (jax.ShapeDtypeStruct((t_len, d), F32), jax.ShapeDtypeStruct((t_len, 5 * cw), BF16),
           jax.ShapeDtypeStruct((t_len, d), F32), jax.ShapeDtypeStruct((t_len, d), BF16),
           jax.ShapeDtypeStruct((t_len, 2 * cw), BF16)),
        grid=(nt,),
        in_specs=[tile(d)] + [VMEM_SPEC] * 8 + [HBM_SPEC] * n_late,
        out_specs=(HBM_SPEC,) * n_late + (tile(d), tile(5 * cw), tile(d), tile(d), tile(2 * cw)),
        input_output_aliases={n_in + k: k for k in range(n_late)},
        scratch_shapes=[pltpu.VMEM((tm + HALO, cw), F32), pltpu.VMEM((tm + HALO, cw), F32),
                        _dma_sems(n_cp), _dma_sems(n_cp), _dma_sems(n_cp), _dma_sems(n_cp)],
        compiler_params=_params(dimension_semantics=("arbitrary",)),
    )(x, mod, conv_w, w_in4, w_pool, pool_scale, w_out, ln_g, ln_b, *late_stacks)


def _mlp_fwd_bwd(x1, tgt, mod, w_mi4, w_mo4, ln_g, ln_b, tm):
    t_len, d = x1.shape
    fq = w_mi4.shape[2]
    nt = t_len // tm

    def body(x1_ref, tgt_ref, mod_ref, wi_ref, wo_ref, g_ref, b_ref,
             dx1_ref, da_ref, s_ref, df_ref, h2_ref, st_ref, relu_buf, acc):
        i = pl.program_id(0)

        @pl.when(i == 0)
        def _():
            acc[...] = jnp.zeros(acc.shape, F32)

        x1v = x1_ref[...]
        sh2, sc2, g2 = mod_ref[0:1, 3 * d:4 * d], mod_ref[0:1, 4 * d:5 * d], mod_ref[0:1, 5 * d:6 * d]
        h2 = (x1v * (1.0 + sc2) + sh2).astype(BF16)
        h2_ref[...] = h2
        f = jnp.zeros((tm, d), F32)
        for j in range(N_CHIPS):
            a = jnp.maximum(_dot(h2, wi_ref[j]), 0.0)
            relu_buf[j] = a
            s = (a * a).astype(BF16)
            s_ref[j] = s
            f = f + _dot(s, wo_ref[j])
        xhat, rstd = _layer_norm_fwd(DEEPNORM_ALPHA * x1v + (1.0 + g2) * f)
        gain = g_ref[...]
        err = xhat * gain + b_ref[...] - tgt_ref[...]
        dy = err * (1.0 / d)
        dr2 = _layer_norm_bwd(dy, xhat, rstd, gain)
        df = ((1.0 + g2) * dr2).astype(BF16)
        df_ref[...] = df
        dh2 = jnp.zeros((tm, d), F32)
        for j in range(N_CHIPS):
            ds = _dot_nt(df, wo_ref[j])
            da = (ds * (2.0 * relu_buf[j])).astype(BF16)
            da_ref[j] = da
            dh2 = dh2 + _dot_nt(da, wi_ref[j])
        dx1_ref[...] = DEEPNORM_ALPHA * dr2 + dh2 * (1.0 + sc2)
        acc[0] += _fold8(dy * xhat)
        acc[1] += _fold8(dy)
        acc[2] += _fold8(dr2 * f)
        acc[3] += _fold8(dh2 * x1v)
        acc[4] += _fold8(dh2)
        acc[5] += _fold8(err * err)

        @pl.when(i == nt - 1)
        def _():
            for k in range(5):
                st_ref[k:k + 1, :] = jnp.sum(acc[k], axis=0, keepdims=True)
            loss = jnp.sum(acc[5]) * (0.5 / d)
            st_ref[5:6, :] = jnp.broadcast_to(loss, (1, d))
            st_ref[6:8, :] = jnp.zeros((2, d), F32)

    tile = lambda w: pl.BlockSpec((tm, w), lambda i: (i, 0))
    tile4 = pl.BlockSpec((N_CHIPS, tm, fq), lambda i: (0, i, 0))
    return pl.pallas_call(
        body, name="mlp_fwd_bwd",
        out_shape=(jax.ShapeDtypeStruct((t_len, d), F32),
                   jax.ShapeDtypeStruct((N_CHIPS, t_len, fq), BF16), jax.ShapeDtypeStruct((N_CHIPS, t_len, fq), BF16),
                   jax.ShapeDtypeStruct((t_len, d), BF16), jax.ShapeDtypeStruct((t_len, d), BF16),
                   jax.ShapeDtypeStruct((SUBLANES, d), F32)),
        grid=(nt,),
        in_specs=[tile(d), tile(d)] + [VMEM_SPEC] * 5,
        out_specs=(tile(d), tile4, tile4, tile(d), tile(d), pl.BlockSpec((SUBLANES, d), lambda i: (0, 0))),
        scratch_shapes=[pltpu.VMEM((N_CHIPS, tm, fq), F32), pltpu.VMEM((6, SUBLANES, d), F32)],
        compiler_params=_params(dimension_semantics=("arbitrary",)),
    )(x1, tgt, mod, w_mi4, w_mo4, ln_g, ln_b)


def _mlp_wgrad(h2, da4, s4, df, pos, tt):
    t_len, d = h2.shape
    fq = da4.shape[2]
    nt = t_len // tt
    assert nt >= 3
    rh = (d // 2, fq // 2)
    last = N_CHIPS - 1

    def body(pos_ref, h2_ref, da_ref, s_ref, df_ref, qi_ref, qo_ref, gi_ref, go_ref,
             acc_i, acc_o, rcv_i, rcv_o, s_sem, r_sem):
        j, t = pl.program_id(0), pl.program_id(1)
        slot = lax.rem(j, 2)
        x, y, c = _mesh_pos()
        accs, rcvs, q_refs, g_refs = (acc_i, acc_o), (rcv_i, rcv_o), (qi_ref, qo_ref), (gi_ref, go_ref)

        def to_sibling(a, sl):
            theirs = pl.multiple_of((1 - c) * rh[a], rh[a])
            return pltpu.make_async_remote_copy(accs[a].at[sl].at[pl.ds(theirs, rh[a]), :], rcvs[a].at[sl],
                                                s_sem.at[2 * sl + a], r_sem.at[2 * sl + a],
                                                device_id=(x, y, 1 - c), device_id_type=MESH)

        def emit(chunk, sl):
            for a in range(2):
                to_sibling(a, sl).wait()
                mine = pl.multiple_of(c * rh[a], rh[a])
                q = (accs[a][sl, pl.ds(mine, rh[a]), :] + rcvs[a][sl]).astype(BF16)
                q_refs[a][...] = q

                @pl.when(chunk == pos_ref[1])
                def _():
                    g_refs[a][...] = q

        @pl.when(t == 0)
        def _():
            acc_i[slot] = jnp.zeros((d, fq), F32)
            acc_o[slot] = jnp.zeros((fq, d), F32)

        acc_i[slot] += _dot_tn(h2_ref[...], da_ref[...])
        acc_o[slot] += _dot_tn(s_ref[...], df_ref[...])

        @pl.when((t == 1) & (j >= 1))
        def _():
            emit(j - 1, 1 - slot)

        @pl.when(t == nt - 1)
        def _():
            for a in range(2):
                to_sibling(a, slot).start()

        @pl.when((t == nt - 1) & (j == last))
        def _():
            emit(j, slot)

    q_index = lambda j, t, p: (jnp.where(t == nt - 1, j, jnp.maximum(j - 1, 0)), 0, 0)
    got_index = lambda j, t, p: (p[1], 0, 0)
    blocks = ((None, rh[0], fq), (None, rh[1], d))
    return pl.pallas_call(
        body, name="mlp_wgrad",
        out_shape=tuple(jax.ShapeDtypeStruct((N_CHIPS,) + b[1:], BF16) for b in blocks) * 2,
        grid_spec=pltpu.PrefetchScalarGridSpec(
            num_scalar_prefetch=1, grid=(N_CHIPS, nt),
            in_specs=[pl.BlockSpec((tt, d), lambda j, t, p: (t, 0)),
                      pl.BlockSpec((None, tt, fq), lambda j, t, p: (j, t, 0)),
                      pl.BlockSpec((None, tt, fq), lambda j, t, p: (j, t, 0)),
                      pl.BlockSpec((tt, d), lambda j, t, p: (t, 0))],
            out_specs=tuple(pl.BlockSpec(b, q_index) for b in blocks)
            + tuple(pl.BlockSpec(b, got_index) for b in blocks),
            scratch_shapes=[pltpu.VMEM((2, d, fq), F32), pltpu.VMEM((2, fq, d), F32),
                            pltpu.VMEM((2, rh[0], fq), F32), pltpu.VMEM((2, rh[1], d), F32),
                            _dma_sems(4), _dma_sems(4)]),
        compiler_params=_params(dimension_semantics=("arbitrary", "arbitrary")),
    )(pos, h2, da4, s4, df)


def _bwd_mix(x, mix, dx1, zs, mod, conv_w, w_in4, w_pool, pool_scale, w_out, ln_g, after, tm):
    t_len, d = x.shape
    cw = w_in4.shape[2]
    dg = cw // N_GROUPS
    nt = t_len // tm

    def body(x_ref, mix_ref, dx1_ref, zs_ref, mod_ref, cw_ref, win_ref, wp_ref, ps_ref, wout_ref, g_ref, after_ref,
             gx_ref, dz_ref, dmix_ref, sd_ref, sc_ref, dwp_ref, dbuf, qbuf, acc_d, acc_c, acc_p):
        i = pl.program_id(0)
        tile_idx = nt - 1 - i

        @pl.when(i == 0)
        def _():
            dbuf[tm:tm + HALO, :] = jnp.zeros((HALO, cw), F32)
            qbuf[tm:tm + HALO, :] = jnp.zeros((HALO, cw), F32)
            acc_d[...] = jnp.zeros(acc_d.shape, F32)
            acc_c[...] = jnp.zeros(acc_c.shape, F32)
            acc_p[...] = jnp.zeros(acc_p.shape, F32)

        xv, mixv, dx1v = x_ref[...], mix_ref[...], dx1_ref[...]
        sc1, g1 = mod_ref[0:1, d:2 * d], mod_ref[0:1, 2 * d:3 * d]
        xhat, rstd = _layer_norm_fwd(DEEPNORM_ALPHA * xv + (1.0 + g1) * mixv)
        dr1 = _layer_norm_bwd(dx1v, xhat, rstd, g_ref[...])
        dmix = ((1.0 + g1) * dr1).astype(BF16)
        dmix_ref[...] = dmix
        dcat = _dot_nt(dmix, wout_ref[...])
        dyc, dyp = dcat[:, 0:cw], dcat[:, cw:2 * cw]
        zb, zc, zv = (zs_ref[:, k * cw:(k + 1) * cw].astype(F32) for k in range(3))
        conv, pooled = zs_ref[:, 3 * cw:4 * cw].astype(F32), zs_ref[:, 4 * cw:5 * cw]
        dzb = dyc * conv
        dcv = dyc * zb
        dbuf[0:tm, :] = dcv
        d1 = dbuf[pl.ds(1, tm), :]
        d2 = dbuf[pl.ds(2, tm), :]
        du = cw_ref[2:3, :] * dcv + cw_ref[1:2, :] * d1 + cw_ref[0:1, :] * d2
        dbuf[tm:tm + HALO, :] = dbuf[0:HALO, :]
        u = zc * zv
        acc_c[0] += _fold8(u * d2)
        acc_c[1] += _fold8(u * d1)
        acc_c[2] += _fold8(u * dcv)
        dzc = du * zv
        dzv = du * zc
        dpw = dyp * ps_ref[...]
        dps, pms = [], []
        for gi, win in enumerate(POOL_WINDOWS):
            sl = slice(gi * dg, (gi + 1) * dg)
            p_g = pooled[:, sl].astype(BF16)
            dpw_g = dpw[:, sl].astype(BF16)
            wp_g = wp_ref[gi].astype(BF16)
            pms.append(_dot(p_g, wp_g))
            acc_p[gi] += _dot_tn(p_g, dpw_g)
            dp_g = _dot_nt(dpw_g, wp_g)
            dps.append(dp_g)
            qbuf[0:tm, sl] = dp_g * _inv_count(tile_idx, tm, win)
        acc_c[3] += _fold8(dyp * jnp.concatenate(pms, axis=1))
        dzps = []
        for gi, win in enumerate(POOL_WINDOWS):
            sl = slice(gi * dg, (gi + 1) * dg)
            acc = qbuf[0:tm, sl]
            for s in range(1, win):
                acc = acc + qbuf[pl.ds(s, tm), sl]
            dzps.append(acc - dps[gi])
        qbuf[tm:tm + HALO, :] = qbuf[0:HALO, :]
        dz = [dzb.astype(BF16), dzc.astype(BF16), dzv.astype(BF16), jnp.concatenate(dzps, axis=1).astype(BF16)]
        dh1 = jnp.zeros((tm, d), F32)
        for j in range(N_CHIPS):
            dz_ref[j] = dz[j]
            dh1 = dh1 + _dot_nt(dz[j], win_ref[j])
        gx_ref[...] = DEEPNORM_ALPHA * dr1 + dh1 * (1.0 + sc1)
        acc_d[0] += _fold8(dx1v * xhat)
        acc_d[1] += _fold8(dx1v)
        acc_d[2] += _fold8(dr1 * mixv)
        acc_d[3] += _fold8(dh1 * xv)
        acc_d[4] += _fold8(dh1)

        @pl.when(i == nt - 1)
        def _():
            for k in range(5):
                sd_ref[k:k + 1, :] = jnp.sum(acc_d[k], axis=0, keepdims=True)
            sd_ref[5:8, :] = jnp.zeros((3, d), F32)
            for k in range(4):
                sc_ref[k:k + 1, :] = jnp.sum(acc_c[k], axis=0, keepdims=True)
            sc_ref[4:8, :] = jnp.zeros((4, cw), F32)
            dwp_ref[...] = acc_p[...]

    rtile = lambda w: pl.BlockSpec((tm, w), lambda i: (nt - 1 - i, 0))
    whole = lambda shape: pl.BlockSpec(shape, lambda i: tuple(0 for _ in shape))
    return pl.pallas_call(
        body, name="bwd_mix",
        out_shape=(jax.ShapeDtypeStruct((t_len, d), F32), jax.ShapeDtypeStruct((N_CHIPS, t_len, cw), BF16),
                   jax.ShapeDtypeStruct((t_len, d), BF16), jax.ShapeDtypeStruct((SUBLANES, d), F32),
                   jax.ShapeDtypeStruct((SUBLANES, cw), F32), jax.ShapeDtypeStruct((N_GROUPS, dg, dg), F32)),
        grid=(nt,),
        in_specs=[rtile(d), rtile(d), rtile(d), rtile(5 * cw)] + [VMEM_SPEC] * 7
        + [pl.BlockSpec(memory_space=pl.ANY)],
        out_specs=(rtile(d), pl.BlockSpec((N_CHIPS, tm, cw), lambda i: (0, nt - 1 - i, 0)), rtile(d),
                   whole((SUBLANES, d)), whole((SUBLANES, cw)), whole((N_GROUPS, dg, dg))),
        scratch_shapes=[pltpu.VMEM((tm + HALO, cw), F32), pltpu.VMEM((tm + HALO, cw), F32),
                        pltpu.VMEM((5, SUBLANES, d), F32), pltpu.VMEM((4, SUBLANES, cw), F32),
                        pltpu.VMEM((N_GROUPS, dg, dg), F32)],
        compiler_params=_params(dimension_semantics=("arbitrary",)),
    )(x, mix, dx1, zs, mod, conv_w, w_in4, w_pool, pool_scale, w_out, ln_g, after)


def _mix_wgrad(h1, dz4, cat, dmix, small, dmod_rows, tt):
    t_len, d = h1.shape
    cw = dz4.shape[2]
    nt = t_len // tt
    ro = (2 * cw) // N_CHIPS
    rh = (d // 2, ro // 2)
    n_small = len(small)

    def body(h1_ref, dz_ref, cat_ref, dmix_ref, *rest):
        p_refs = rest[:n_small]
        qi_ref, qo_ref, gi_ref, go_ref = rest[n_small:n_small + 4]
        sum_refs = rest[n_small + 4:2 * n_small + 4]
        dmod_ref, dmod_sum_ref = rest[2 * n_small + 4:2 * n_small + 6]
        scratch = rest[2 * n_small + 6:]
        acc_i, acc_o, rcv_i, rcv_o = scratch[:4]
        all_bufs = scratch[4:4 + n_small]
        s_sem, r_sem, s_small, r_small_sem = scratch[4 + n_small:]
        t = pl.program_id(0)
        x, y, c = _mesh_pos()
        chip = 2 * x + y
        groups = [(p_refs[k], sum_refs[k], None, all_bufs[k]) for k in range(n_small)]
        small_start, small_finish = _small_allreduce_steps(groups, s_small, r_small_sem)
        accs, rcvs, q_refs, g_refs = (acc_i, acc_o), (rcv_i, rcv_o), (qi_ref, qo_ref), (gi_ref, go_ref)

        @pl.when(t == 0)
        def _():
            small_start()
            acc_i[...] = jnp.zeros(acc_i.shape, F32)
            acc_o[...] = jnp.zeros(acc_o.shape, F32)

        def accumulate(j):
            acc_i[j] += _dot_tn(h1_ref[...], dz_ref[j])
            acc_o[j] += _dot_tn(cat_ref[:, j * ro:(j + 1) * ro], dmix_ref[...])

        def to_sibling(a, j):
            theirs = pl.multiple_of((1 - c) * rh[a], rh[a])
            return pltpu.make_async_remote_copy(accs[a].at[j].at[pl.ds(theirs, rh[a]), :], rcvs[a].at[j],
                                                s_sem.at[a * N_CHIPS + j], r_sem.at[a * N_CHIPS + j],
                                                device_id=(x, y, 1 - c), device_id_type=MESH)

        @pl.when(t < nt - 1)
        def _():
            for j in range(N_CHIPS):
                accumulate(j)

        @pl.when(t == nt - 1)
        def _():
            for j in range(N_CHIPS):
                accumulate(j)
                for a in range(2):
                    to_sibling(a, j).start()
            for a in range(2):
                mine = pl.multiple_of(c * rh[a], rh[a])
                for j in range(N_CHIPS):
                    to_sibling(a, j).wait()
                    q_refs[a][j] = (accs[a][j, pl.ds(mine, rh[a]), :] + rcvs[a][j]).astype(BF16)
                g_refs[a][chip] = q_refs[a][chip]
            small_finish()
            for k, (arr, row) in enumerate(dmod_rows):
                cols = slice(k * d, (k + 1) * d)
                dmod_sum_ref[:, cols] = sum_refs[arr][row:row + 1, :]
                for b in range(N_DEV):
                    dmod_ref[b:b + 1, cols] = all_bufs[arr][b, row:row + 1, :]

    stacks = ((N_CHIPS, rh[0], cw), (N_CHIPS, rh[1], d))
    n_cp = n_small * (N_DEV - 1)
    n_mod = len(dmod_rows) * d
    return pl.pallas_call(
        body, name="mix_wgrad",
        out_shape=tuple(jax.ShapeDtypeStruct(s, BF16) for s in stacks) * 2
        + tuple(jax.ShapeDtypeStruct(a.shape, F32) for a in small)
        + (jax.ShapeDtypeStruct((N_DEV, n_mod), F32), jax.ShapeDtypeStruct((1, n_mod), F32)),
        grid=(nt,),
        in_specs=[pl.BlockSpec((tt, d), lambda t: (t, 0)), pl.BlockSpec((N_CHIPS, tt, cw), lambda t: (0, t, 0)),
                  pl.BlockSpec((tt, 2 * cw), lambda t: (t, 0)), pl.BlockSpec((tt, d), lambda t: (t, 0))]
        + [VMEM_SPEC] * n_small,
        out_specs=(VMEM_SPEC,) * (6 + n_small),
        scratch_shapes=[pltpu.VMEM((N_CHIPS, d, cw), F32), pltpu.VMEM((N_CHIPS, ro, d), F32),
                        pltpu.VMEM(stacks[0], F32), pltpu.VMEM(stacks[1], F32)]
        + [pltpu.VMEM((N_DEV,) + a.shape, F32) for a in small]
        + [_dma_sems(2 * N_CHIPS), _dma_sems(2 * N_CHIPS), _dma_sems(n_cp), _dma_sems(n_cp)],
        compiler_params=_params(dimension_semantics=("arbitrary",)),
    )(h1, dz4, cat, dmix, *small)


def kernel(x, c, w_ada, b_ada, w_in, conv_w, w_pool, pool_scale, w_out, ln1_g, ln1_b, w_mlp_in, w_mlp_out, ln2_g, ln2_b, loss_target, m_w_ada, m_b_ada, m_w_in, m_conv_w, m_w_pool, m_pool_scale, m_w_out, m_ln1_g, m_ln1_b, m_w_mlp_in, m_w_mlp_out, m_ln2_g, m_ln2_b, v_w_ada, v_b_ada, v_w_in, v_conv_w, v_w_pool, v_pool_scale, v_w_out, v_ln1_g, v_ln1_b, v_w_mlp_in, v_w_mlp_out, v_ln2_g, v_ln2_b):
    t_len, d = x.shape[1], x.shape[2]
    cw = w_in.shape[2]
    cq = conv_w.shape[2]
    dg = w_pool.shape[2]
    assert cw == N_GROUPS * dg and cq * N_CHIPS == cw and LANES % cq == 0
    tm_mix = min(MIX_TOKEN_TILE, t_len)
    tm_mlp = min(MLP_TOKEN_TILE, t_len)
    tt = min(WGRAD_TILE, t_len // 4)
    chip = 2 * lax.axis_index("x") + lax.axis_index("y")
    pos = jnp.stack([lax.axis_index("c"), chip]).astype(jnp.int32)

    x2, tgt = x[0], loss_target[0]
    big_w = [w_in[0], w_out[0], w_mlp_in[0], w_mlp_out[0]]
    big_m = [m_w_in[0], m_w_out[0], m_w_mlp_in[0], m_w_mlp_out[0]]
    big_v = [v_w_in[0], v_w_out[0], v_w_mlp_in[0], v_w_mlp_out[0]]
    names = ["w_in", "w_out", "w_mlp_in", "w_mlp_out"]

    mod, cond_all, conv_full, w_in4, w_out4, w_mi_own, w_mo_own = _setup_exchange(
        c, w_ada[0], b_ada, conv_w[0], big_w, 2)
    w_out_full = w_out4.reshape(2 * cw, d)
    w_pool_g = w_pool[0]

    w_mi4, w_mo4, x1, zs, mix, h1, cat = _fwd_mix(x2, mod, conv_full, w_in4, w_pool_g, pool_scale, w_out_full,
                                                  ln1_g, ln1_b, [w_mi_own, w_mo_own], tm_mix)
    dx1, da4, s4, df, h2, st2 = _mlp_fwd_bwd(x1, tgt, mod, w_mi4, w_mo4, ln2_g, ln2_b, tm_mlp)
    mlp_qi, mlp_qo, mlp_gi, mlp_go = _mlp_wgrad(h2, da4, s4, df, pos, tt)
    mlp_started = _split_start([mlp_qi, mlp_qo, mlp_gi, mlp_go], _chunk_exchange_plan(2), 2 * (N_CHIPS - 1),
                               "exchange_mlp")
    grad_x, dz4, dmix, st1, stc, dw_pool = _bwd_mix(x2, mix, dx1, zs, mod, conv_full, w_in4, w_pool_g, pool_scale,
                                                    w_out_full, ln1_g, mlp_started[-1], tm_mix)

    dmod_rows = [(0, 4), (0, 3), (0, 2), (1, 4), (1, 3), (1, 2)]
    mix_qi, mix_qo, mix_gi, mix_go, t1, t2, tc, g_w_pool, dmod_all, g_b_ada = _mix_wgrad(
        h1, dz4, cat, dmix, [st1, st2, stc, dw_pool], dmod_rows, tt)
    loss = t2[5, 0]
    na = w_ada.shape[2]
    dmod_mine = lax.dynamic_slice_in_dim(dmod_all, chip * na, na, axis=1)

    exchange, n_cp = _chunk_exchange_plan(2), 2 * (N_CHIPS - 1)
    mlp_got = _split_wait(mlp_started, exchange, t1, "exchange_mlp")[2:]
    mix_started = _split_start([mix_qi, mix_qo, mix_gi, mix_go], exchange, n_cp, "exchange_mix")
    mlp_halves = list(_sum_chips(mlp_got, pos, mix_started[-1], "rs_total_mlp"))
    join = _sibling_join_plan([h.shape for h in mlp_halves])
    join_started = _split_start(mlp_halves, join, 2, "join_mlp")
    g_ada, d_ada, nm_ada, nv_ada = _adamw_ada(cond_all, dmod_mine, w_ada[0], m_w_ada[0], v_w_ada[0],
                                              join_started[-1])
    mlp_g = _split_wait(join_started, join, d_ada, "join_mlp")
    mlp_out = _adamw_big(mlp_g, big_w[2:], big_m[2:], big_v[2:], "adamw_mlp")
    mix_got = _split_wait(mix_started, exchange, mlp_out[1][0], "exchange_mix")[2:]
    mix_halves = list(_sum_chips(mix_got, pos, d_ada, "rs_total_mix"))
    mix_g = _sibling_join(mix_halves, "mix")
    mix_out = _adamw_big(mix_g, big_w[:2], big_m[:2], big_v[:2], "adamw_mix")
    big_g = list(mix_g) + list(mlp_g)
    big_out = mix_out + mlp_out

    sources = [g_b_ada, ("chip_cols", 2, 0, 3, cq), g_w_pool.reshape(-1, dg), ("rows", 2, 3, 4),
               ("rows", 0, 0, 1), ("rows", 0, 1, 2), ("rows", 1, 0, 1), ("rows", 1, 1, 2)]
    small_w = [b_ada, conv_w[0], w_pool.reshape(-1, dg), pool_scale, ln1_g, ln1_b, ln2_g, ln2_b]
    small_m = [m_b_ada, m_conv_w[0], m_w_pool.reshape(-1, dg), m_pool_scale, m_ln1_g, m_ln1_b, m_ln2_g, m_ln2_b]
    small_v = [v_b_ada, v_conv_w[0], v_w_pool.reshape(-1, dg), v_pool_scale, v_ln1_g, v_ln1_b, v_ln2_g, v_ln2_b]
    cut, sm = _adamw_small(pos, [t1, t2, tc], sources, small_w, small_m, small_v)
    small_g = [sources[0], cut[0], sources[2]] + list(cut[1:])
    ns = len(small_w)
    s_delta, s_m, s_v = sm[:ns], sm[ns:2 * ns], sm[2 * ns:]

    def assemble(ada, small, big):
        return [ada[None], small[0], big[0][None], small[1][None], small[2].reshape(w_pool.shape), small[3],
                big[1][None], small[4], small[5], big[2][None], big[3][None], small[6], small[7]]

    grads = assemble(g_ada, small_g, big_g)
    deltas = assemble(d_ada, s_delta, [o[0] for o in big_out])
    new_m = assemble(nm_ada, s_m, [o[1] for o in big_out])
    new_v = assemble(nv_ada, s_v, [o[2] for o in big_out])
    return (loss, grad_x[None], *grads, *deltas, *new_m, *new_v)
```

```python
import jax
import jax.numpy as jnp
from jax import lax
from jax.experimental import pallas as pl
from jax.experimental.pallas import tpu as pltpu

F32 = jnp.float32
BF16 = jnp.bfloat16
MESH = pl.DeviceIdType.MESH

LN_EPS = 1e-5
DEEPNORM_ALPHA = 2.0 ** 0.25
POOL_WINDOWS = (2, 4, 8, 16)
N_GROUPS = len(POOL_WINDOWS)
HALO = 16
N_CHIPS = 4
N_DEV = 8
LANES = 128
SUBLANES = 8
VMEM_LIMIT = 56 * 1024 * 1024
MIX_TOKEN_TILE = 512
MLP_TOKEN_TILE = 256
WGRAD_TILE = 1024

ADAM_LR = 0.001
ADAM_B1 = 0.9
ADAM_B2 = 0.999
ADAM_EPS = 1e-08
ADAM_WD = 0.01
ADAM_STEP = 10

VMEM_SPEC = pl.BlockSpec(memory_space=pltpu.VMEM)
HBM_SPEC = pl.BlockSpec(memory_space=pltpu.HBM)


def _dot(a, b):
    return jnp.dot(a, b, preferred_element_type=F32)


def _dot_nt(a, b):
    return lax.dot_general(a, b, (((1,), (1,)), ((), ())), preferred_element_type=F32)


def _dot_tn(a, b):
    return lax.dot_general(a, b, (((0,), (0,)), ((), ())), preferred_element_type=F32)


def _fold8(v):
    r, n = v.shape
    return jnp.sum(v.reshape(r // SUBLANES, SUBLANES, n), axis=0)


def _mesh_pos():
    return lax.axis_index("x"), lax.axis_index("y"), lax.axis_index("c")


def _flip(v, bit):
    return 1 - v if bit else v


def _params(**kw):
    return pltpu.CompilerParams(vmem_limit_bytes=VMEM_LIMIT, **kw)


def _dma_sems(n):
    return pltpu.SemaphoreType.DMA((n,))


def _stack_gather(stacks, rows, s_ici, r_ici, s_d2d, r_d2d):
    x, y, c = _mesh_pos()
    chip = 2 * x + y
    pairs = [(i, m) for i in range(len(stacks)) for m in range(1, N_CHIPS)]

    def blk(i, slot, which):
        rh = rows[i] // 2
        return stacks[i].at[slot].at[pl.ds(which * rh, rh), :]

    def other(m):
        return 2 * _flip(x, m & 2) + _flip(y, m & 1)

    def ici(i, m, slot, to):
        k = i * (N_CHIPS - 1) + m - 1
        return pltpu.make_async_remote_copy(blk(i, slot, c), blk(i, slot, c), s_ici.at[k], r_ici.at[k],
                                            device_id=to, device_id_type=MESH)

    def d2d(i, m, which, to):
        k = i * (N_CHIPS - 1) + m - 1
        return pltpu.make_async_remote_copy(blk(i, other(m), which), blk(i, other(m), which),
                                            s_d2d.at[k], r_d2d.at[k], device_id=to, device_id_type=MESH)

    def start():
        for i, m in pairs:
            ici(i, m, chip, (_flip(x, m & 2), _flip(y, m & 1), c)).start()

    def forward(which=None):
        for i, m in pairs:
            if which is None or i == which:
                ici(i, m, other(m), (x, y, c)).wait_recv()
                d2d(i, m, c, (x, y, 1 - c)).start()

    def finish():
        for i, m in pairs:
            d2d(i, m, 1 - c, (x, y, c)).wait_recv()
        for i, m in pairs:
            ici(i, m, chip, (x, y, c)).wait_send()
            d2d(i, m, c, (x, y, c)).wait_send()

    return start, forward, finish


def _chunk_exchange(qs, gots, s_sem, r_sem):
    x, y, c = _mesh_pos()
    chip = 2 * x + y
    pairs = [(i, m) for i in range(len(qs)) for m in range(1, N_CHIPS)]

    def other(m):
        return 2 * _flip(x, m & 2) + _flip(y, m & 1)

    def send(i, m, to):
        k = i * (N_CHIPS - 1) + m - 1
        return pltpu.make_async_remote_copy(qs[i].at[other(m)], gots[i].at[chip], s_sem.at[k], r_sem.at[k],
                                            device_id=to, device_id_type=MESH)

    def arrival(i, m):
        k = i * (N_CHIPS - 1) + m - 1
        return pltpu.make_async_remote_copy(qs[i].at[other(m)], gots[i].at[other(m)], s_sem.at[k], r_sem.at[k],
                                            device_id=(x, y, c), device_id_type=MESH)

    def start():
        for i, m in pairs:
            send(i, m, (_flip(x, m & 2), _flip(y, m & 1), c)).start()

    def finish():
        for i, m in pairs:
            arrival(i, m).wait_recv()
        for i, m in pairs:
            send(i, m, (x, y, c)).wait_send()

    return start, finish


def _mod_scratch(d, na, cq):
    return [pltpu.VMEM((SUBLANES, d), F32), pltpu.VMEM((N_DEV, SUBLANES, d), F32),
            pltpu.VMEM((N_CHIPS, SUBLANES, na), F32),
            pltpu.VMEM((SUBLANES, cq), F32), pltpu.VMEM((N_CHIPS, SUBLANES, cq), F32),
            _dma_sems(N_DEV - 1), _dma_sems(N_DEV - 1),
            _dma_sems(N_CHIPS - 1), _dma_sems(N_CHIPS - 1), _dma_sems(N_CHIPS - 1), _dma_sems(N_CHIPS - 1)]


def _mod_steps(d, na, cq):
    def steps(c_ref, w_ref, b_ref, cw_ref, mod_ref, cond_ref, conv_ref,
              csend, cbuf, mbuf, cvsend, cvbuf, s1, r1, s2, r2, s3, r3):
        x, y, c = _mesh_pos()
        me = 4 * x + 2 * y + c
        chip = 2 * x + y
        csend[...] = jnp.broadcast_to(c_ref[...], (SUBLANES, d))
        cbuf[me] = csend[...]
        first = []
        for m in range(1, N_DEV):
            peer = (_flip(x, m & 4), _flip(y, m & 2), _flip(c, m & 1))
            cp = pltpu.make_async_remote_copy(csend, cbuf.at[me], s1.at[m - 1], r1.at[m - 1],
                                              device_id=peer, device_id_type=MESH)
            cp.start()
            first.append(cp)
        cvsend[...] = jnp.zeros((SUBLANES, cq), F32)
        cvsend[0:3, :] = cw_ref[...]
        cvbuf[chip] = cvsend[...]
        for m in range(1, N_DEV):
            src = 4 * _flip(x, m & 4) + 2 * _flip(y, m & 2) + _flip(c, m & 1)
            pltpu.make_async_remote_copy(csend, cbuf.at[src], s1.at[m - 1], r1.at[m - 1],
                                         device_id=(x, y, c), device_id_type=MESH).wait_recv()
        rows = lax.broadcasted_iota(jnp.int32, (SUBLANES, d), 0)
        call = jnp.zeros((SUBLANES, d), F32)
        for b in range(N_DEV):
            call = jnp.where(rows == b, cbuf[b], call)
        cond = call * jax.nn.sigmoid(call)
        cond_ref[...] = cond
        part = jnp.dot(cond, w_ref[...], preferred_element_type=F32, precision=lax.Precision.HIGHEST)
        mbuf[chip] = part
        second = []
        for m in range(1, N_CHIPS):
            peer = (_flip(x, m & 2), _flip(y, m & 1), c)
            cp = pltpu.make_async_remote_copy(mbuf.at[chip], mbuf.at[chip], s2.at[m - 1], r2.at[m - 1],
                                              device_id=peer, device_id_type=MESH)
            cp.start()
            second.append(cp)
            cp = pltpu.make_async_remote_copy(cvsend, cvbuf.at[chip], s3.at[m - 1], r3.at[m - 1],
                                              device_id=peer, device_id_type=MESH)
            cp.start()
            second.append(cp)
        for m in range(1, N_CHIPS):
            src = 2 * _flip(x, m & 2) + _flip(y, m & 1)
            pltpu.make_async_remote_copy(mbuf.at[src], mbuf.at[src], s2.at[m - 1], r2.at[m - 1],
                                         device_id=(x, y, c), device_id_type=MESH).wait_recv()
            pltpu.make_async_remote_copy(cvsend, cvbuf.at[src], s3.at[m - 1], r3.at[m - 1],
                                         device_id=(x, y, c), device_id_type=MESH).wait_recv()
        for cp in first + second:
            cp.wait_send()
        rows_n = lax.broadcasted_iota(jnp.int32, (SUBLANES, na), 0)
        for k in range(N_CHIPS):
            mine = jnp.sum(jnp.where(rows_n == me, mbuf[k], 0.0), axis=0, keepdims=True)
            mod_ref[:, k * na:(k + 1) * na] = jnp.broadcast_to(mine + b_ref[:, k * na:(k + 1) * na], (SUBLANES, na))
            conv_ref[:, k * cq:(k + 1) * cq] = cvbuf[k]

    return steps


def _setup_exchange(c_row, w_ada_s, b_ada, conv_w_s, shards, n_now):
    n = len(shards)
    n_cp = n_now * (N_CHIPS - 1)
    d, na, cq = c_row.shape[1], w_ada_s.shape[1], conv_w_s.shape[1]
    mod_scratch = _mod_scratch(d, na, cq)
    mod_steps = _mod_steps(d, na, cq)

    def body(*refs):
        mod_in, ins = refs[:4], refs[4:4 + n]
        mod_out, outs = refs[4 + n:7 + n], refs[7 + n:7 + 2 * n]
        bufs = refs[7 + 2 * n:7 + 3 * n]
        mod_scr = refs[7 + 3 * n:7 + 3 * n + len(mod_scratch)]
        s_ici, r_ici, s_d2d, r_d2d, s_loc = refs[7 + 3 * n + len(mod_scratch):7 + 3 * n + len(mod_scratch) + 5]
        w_ada_buf, s_in = refs[7 + 3 * n + len(mod_scratch) + 5:7 + 3 * n + len(mod_scratch) + 7]
        late_bufs = refs[7 + 3 * n + len(mod_scratch) + 7:]
        x, y, _ = _mesh_pos()
        chip = 2 * x + y
        loads = [pltpu.make_async_copy(mod_in[1], w_ada_buf, s_in.at[0])]
        loads += [pltpu.make_async_copy(ins[i], late_bufs[i - n_now], s_in.at[1 + i - n_now]) for i in range(n_now, n)]
        for cp in loads:
            cp.start()

        def place(i, src):
            bufs[i][...] = src[...].astype(BF16)
            cp = pltpu.make_async_copy(bufs[i], outs[i].at[chip], s_loc.at[i])
            cp.start()
            return cp

        start, forward, finish = _stack_gather(outs[:n_now], [s.shape[0] for s in shards[:n_now]],
                                               s_ici, r_ici, s_d2d, r_d2d)
        for cp in [place(i, ins[i]) for i in range(n_now)]:
            cp.wait()
        start()
        later = []
        for i in range(n_now, n):
            loads[1 + i - n_now].wait()
            later.append(place(i, late_bufs[i - n_now]))
        loads[0].wait()
        mod_steps(mod_in[0], w_ada_buf, mod_in[2], mod_in[3], *mod_out, *mod_scr)
        forward()
        finish()
        for cp in later:
            cp.wait()

    any_spec = pl.BlockSpec(memory_space=pl.ANY)
    return pl.pallas_call(
        body, name="setup_exchange",
        out_shape=(jax.ShapeDtypeStruct((SUBLANES, N_CHIPS * na), F32), jax.ShapeDtypeStruct((SUBLANES, d), F32),
                   jax.ShapeDtypeStruct((SUBLANES, N_CHIPS * cq), F32))
        + tuple(jax.ShapeDtypeStruct((N_CHIPS,) + s.shape, BF16) for s in shards),
        in_specs=[VMEM_SPEC, any_spec, VMEM_SPEC, VMEM_SPEC] + [VMEM_SPEC] * n_now + [any_spec] * (n - n_now),
        out_specs=(VMEM_SPEC,) * 3 + (HBM_SPEC,) * n,
        scratch_shapes=[pltpu.VMEM(s.shape, BF16) for s in shards] + mod_scratch
        + [_dma_sems(n_cp), _dma_sems(n_cp), _dma_sems(n_cp), _dma_sems(n_cp), _dma_sems(n)]
        + [pltpu.VMEM(w_ada_s.shape, F32), _dma_sems(1 + n - n_now)]
        + [pltpu.VMEM(s.shape, F32) for s in shards[n_now:]],
        compiler_params=_params(),
    )(c_row, w_ada_s, b_ada, conv_w_s, *shards)


def _sibling_join_plan(shapes):
    def plan(refs, s_sem, r_sem):
        x, y, c = _mesh_pos()

        def half(i, which):
            rh = shapes[i][0] // 2
            return refs[i].at[pl.ds(which * rh, rh), :]

        def start():
            for i in range(len(refs)):
                pltpu.make_async_remote_copy(half(i, c), half(i, c), s_sem.at[i], r_sem.at[i],
                                             device_id=(x, y, 1 - c), device_id_type=MESH).start()

        def finish():
            for i in range(len(refs)):
                pltpu.make_async_remote_copy(half(i, 1 - c), half(i, 1 - c), s_sem.at[i], r_sem.at[i],
                                             device_id=(x, y, c), device_id_type=MESH).wait_recv()
            for i in range(len(refs)):
                pltpu.make_async_remote_copy(half(i, c), half(i, c), s_sem.at[i], r_sem.at[i],
                                             device_id=(x, y, c), device_id_type=MESH).wait_send()

        return start, finish

    return plan


def _chunk_exchange_plan(n):
    return lambda refs, s_sem, r_sem: _chunk_exchange(refs[:n], refs[n:], s_sem, r_sem)


def _sibling_join(gs, tag):
    n = len(gs)
    plan = _sibling_join_plan([g.shape for g in gs])

    def body(*refs):
        start, finish = plan(refs[n:2 * n], refs[2 * n], refs[2 * n + 1])
        start()
        finish()

    return pl.pallas_call(
        body, name="rs_sibling_join_" + tag,
        out_shape=tuple(jax.ShapeDtypeStruct(g.shape, g.dtype) for g in gs),
        in_specs=[HBM_SPEC] * n, out_specs=(HBM_SPEC,) * n,
        input_output_aliases={i: i for i in range(n)},
        scratch_shapes=[_dma_sems(n), _dma_sems(n)],
        compiler_params=_params(),
    )(*gs)


SEM_SPEC = pl.BlockSpec(memory_space=pltpu.SEMAPHORE)
SIDE_EFFECT = pltpu.SideEffectType.DATAFLOW_SIDE_EFFECTING


def _split_start(arrays, plan, n_cp, tag):
    n = len(arrays)

    def body(*refs):
        start, _ = plan(refs[n + 2:2 * n + 2], refs[n], refs[n + 1])
        start()
        token = refs[2 * n + 2]
        token[...] = jnp.zeros_like(token)

    through = [pltpu.with_memory_space_constraint(a, pltpu.HBM) for a in arrays]
    return pl.pallas_call(
        body, name=tag + "_start",
        out_shape=(pltpu.SemaphoreType.DMA((n_cp,)), pltpu.SemaphoreType.DMA((n_cp,)))
        + tuple(pltpu.HBM(a.shape, a.dtype) for a in through) + (jax.ShapeDtypeStruct((SUBLANES, LANES), F32),),
        in_specs=[HBM_SPEC] * n, out_specs=(SEM_SPEC, SEM_SPEC) + (HBM_SPEC,) * n + (VMEM_SPEC,),
        input_output_aliases={k: 2 + k for k in range(n)},
        compiler_params=pltpu.CompilerParams(has_side_effects=SIDE_EFFECT),
    )(*through)


def _split_wait(started, plan, after, tag):
    send_sems, recv_sems, arrays = started[0], started[1], started[2:-1]
    n = len(arrays)

    def body(*refs):
        _, finish = plan(refs[:n], refs[n], refs[n + 1])
        finish()

    return pl.pallas_call(
        body, name=tag + "_wait",
        out_shape=tuple(pltpu.HBM(a.shape, a.dtype) for a in arrays),
        in_specs=[HBM_SPEC] * n + [SEM_SPEC, SEM_SPEC, pl.BlockSpec(memory_space=pl.ANY)],
        out_specs=(HBM_SPEC,) * n,
        input_output_aliases={k: k for k in range(n)},
        compiler_params=pltpu.CompilerParams(has_side_effects=SIDE_EFFECT),
    )(*arrays, send_sems, recv_sems, after)


def _small_allreduce_steps(groups, s_sem, r_sem):
    x, y, c = _mesh_pos()
    me = 4 * x + 2 * y + c
    pairs = [(k, m) for k in range(len(groups)) for m in range(1, N_DEV)]

    def send(k, m, to):
        p_ref, all_ref = groups[k][0], groups[k][3]
        i = k * (N_DEV - 1) + m - 1
        return pltpu.make_async_remote_copy(p_ref, all_ref.at[me], s_sem.at[i], r_sem.at[i],
                                            device_id=to, device_id_type=MESH)

    def start():
        for p_ref, _, _, all_ref in groups:
            all_ref[me] = p_ref[...]
        for k, m in pairs:
            send(k, m, (_flip(x, m & 4), _flip(y, m & 2), _flip(c, m & 1))).start()

    def finish():
        for k, m in pairs:
            p_ref, all_ref = groups[k][0], groups[k][3]
            i = k * (N_DEV - 1) + m - 1
            src = 4 * _flip(x, m & 4) + 2 * _flip(y, m & 2) + _flip(c, m & 1)
            pltpu.make_async_remote_copy(p_ref, all_ref.at[src], s_sem.at[i], r_sem.at[i],
                                         device_id=(x, y, c), device_id_type=MESH).wait_recv()
        for k, m in pairs:
            send(k, m, (x, y, c)).wait_send()
        for _, sum_ref, all_out_ref, all_ref in groups:
            sum_ref[...] = (((all_ref[0] + all_ref[1]) + (all_ref[2] + all_ref[3]))
                            + ((all_ref[4] + all_ref[5]) + (all_ref[6] + all_ref[7])))
            if all_out_ref is not None:
                all_out_ref[...] = all_ref[...]

    return start, finish


def _row_block(rows, cap=256):
    rb = min(rows, cap)
    assert rows % rb == 0
    return rb


def _sum_chips(gots, pos, after, name, nb=2):
    n = len(gots)
    blocks = [(g.shape[1] // nb, g.shape[2]) for g in gots]

    def body(pos_ref, *refs):
        for a_ref, o_ref in zip(refs[:n], refs[n + 1:]):
            a = a_ref[...].astype(F32)
            o_ref[...] = (a[0] + a[1]) + (a[2] + a[3])

    return pl.pallas_call(
        body, name=name,
        out_shape=tuple(jax.ShapeDtypeStruct((2 * g.shape[1], g.shape[2]), F32) for g in gots),
        grid_spec=pltpu.PrefetchScalarGridSpec(
            num_scalar_prefetch=1, grid=(nb,),
            in_specs=[pl.BlockSpec((N_CHIPS,) + b, lambda i, p: (0, i, 0)) for b in blocks]
            + [pl.BlockSpec(memory_space=pl.ANY)],
            out_specs=tuple(pl.BlockSpec(b, lambda i, p: (p[0] * nb + i, 0)) for b in blocks)),
        compiler_params=_params(dimension_semantics=("arbitrary",)),
    )(pos, *gots, after)


def _adamw_math(w, g, m, v):
    m = ADAM_B1 * m + (1.0 - ADAM_B1) * g
    v = ADAM_B2 * v + (1.0 - ADAM_B2) * jnp.square(g)
    m_hat = m / (1.0 - ADAM_B1 ** ADAM_STEP)
    v_hat = v / (1.0 - ADAM_B2 ** ADAM_STEP)
    delta = -ADAM_LR * (m_hat / (jnp.sqrt(v_hat) + ADAM_EPS) + ADAM_WD * w)
    return delta, m, v


def _adamw_big(gs, ws, ms, vs, name, nb=4):
    n = len(ws)
    specs = [pl.BlockSpec((w.shape[0] // nb, w.shape[1]), lambda i: (i, 0)) for w in ws]

    def body(*refs):
        g_refs, w_refs, m_refs, v_refs = (refs[k * n:(k + 1) * n] for k in range(4))
        outs = refs[4 * n:]
        for i in range(n):
            d, mn, vn = _adamw_math(w_refs[i][...], g_refs[i][...], m_refs[i][...], v_refs[i][...])
            outs[3 * i][...] = d
            outs[3 * i + 1][...] = mn
            outs[3 * i + 2][...] = vn

    out = pl.pallas_call(
        body, name=name,
        out_shape=tuple(jax.ShapeDtypeStruct(w.shape, F32) for w in ws for _ in range(3)),
        grid=(nb,), in_specs=specs * 4, out_specs=tuple(s for s in specs for _ in range(3)),
        compiler_params=_params(dimension_semantics=("arbitrary",)),
    )(*gs, *ws, *ms, *vs)
    return [out[3 * i:3 * i + 3] for i in range(n)]


def _adamw_ada(cond, dmod, w, m, v, after):
    d, na = w.shape
    rb = _row_block(d, 256)

    def body(c_ref, dm_ref, w_ref, m_ref, v_ref, after_ref, g_ref, d_ref, mo_ref, vo_ref):
        g = lax.dot_general(c_ref[...], dm_ref[...], (((0,), (0,)), ((), ())), preferred_element_type=F32,
                            precision=lax.Precision.HIGHEST)
        dl, mn, vn = _adamw_math(w_ref[...], g, m_ref[...], v_ref[...])
        g_ref[...] = g
        d_ref[...] = dl
        mo_ref[...] = mn
        vo_ref[...] = vn

    spec = pl.BlockSpec((rb, na), lambda i: (i, 0))
    return pl.pallas_call(
        body, name="adamw_w_ada",
        out_shape=(jax.ShapeDtypeStruct((d, na), F32),) * 4,
        grid=(d // rb,),
        in_specs=[pl.BlockSpec((N_DEV, rb), lambda i: (0, i)), pl.BlockSpec((N_DEV, na), lambda i: (0, 0)),
                  spec, spec, spec, pl.BlockSpec(memory_space=pl.ANY)],
        out_specs=(spec,) * 4,
        compiler_params=_params(dimension_semantics=("arbitrary",)),
    )(cond, dmod, w, m, v, after)


def _adamw_small(pos, stats, sources, ws, ms, vs):
    n = len(ws)
    direct = [s for s in sources if not isinstance(s, tuple)]
    n_cut = n - len(direct)

    def body(pos_ref, *refs):
        stat_refs, direct_refs = refs[:len(stats)], refs[len(stats):len(stats) + len(direct)]
        w_refs, m_refs, v_refs = (refs[len(stats) + len(direct) + k * n:len(stats) + len(direct) + (k + 1) * n]
                                  for k in range(3))
        outs = refs[len(stats) + len(direct) + 3 * n:]
        cut_refs, res = outs[:n_cut], outs[n_cut:]
        chip = pos_ref[1]
        i_direct = i_cut = 0
        for i, src in enumerate(sources):
            if not isinstance(src, tuple):
                g = direct_refs[i_direct][...]
                i_direct += 1
            else:
                if src[0] == "rows":
                    g = stat_refs[src[1]][src[2]:src[3], :]
                else:
                    _, k, lo, hi, width = src
                    g = jnp.zeros((hi - lo, width), F32)
                    for j in range(N_CHIPS):
                        g = jnp.where(chip == j, stat_refs[k][lo:hi, j * width:(j + 1) * width], g)
                cut_refs[i_cut][...] = g
                i_cut += 1
            d, mn, vn = _adamw_math(w_refs[i][...], g, m_refs[i][...], v_refs[i][...])
            res[i][...] = d
            res[n + i][...] = mn
            res[2 * n + i][...] = vn

    shapes = tuple(jax.ShapeDtypeStruct(w.shape, F32) for w in ws)
    cut_shapes = tuple(jax.ShapeDtypeStruct(w.shape, F32) for w, s in zip(ws, sources) if isinstance(s, tuple))
    out = pl.pallas_call(
        body, name="adamw_small",
        out_shape=cut_shapes + shapes * 3,
        in_specs=[pl.BlockSpec(memory_space=pltpu.SMEM)] + [VMEM_SPEC] * (len(stats) + len(direct) + 3 * n),
        out_specs=(VMEM_SPEC,) * (n_cut + 3 * n),
        compiler_params=_params(),
    )(pos, *stats, *direct, *ws, *ms, *vs)
    return out[:n_cut], out[n_cut:]


def _layer_norm_fwd(r):
    mu = jnp.mean(r, axis=-1, keepdims=True)
    xc = r - mu
    var = jnp.mean(jnp.square(xc), axis=-1, keepdims=True)
    rstd = lax.rsqrt(var + LN_EPS)
    return xc * rstd, rstd


def _layer_norm_bwd(dy, xhat, rstd, gain):
    dxh = dy * gain
    m1 = jnp.mean(dxh, axis=-1, keepdims=True)
    m2 = jnp.mean(dxh * xhat, axis=-1, keepdims=True)
    return rstd * (dxh - m1 - xhat * m2)


def _inv_count(tile, tm, win):
    t = (tile * tm + lax.broadcasted_iota(jnp.int32, (tm, 1), 0) + 1).astype(F32)
    return 1.0 / jnp.minimum(t, float(win))


def _fwd_mix(x, mod, conv_w, w_in4, w_pool, pool_scale, w_out, ln_g, ln_b, late_stacks, tm):
    t_len, d = x.shape
    cw = w_in4.shape[2]
    dg = cw // N_GROUPS
    nt = t_len // tm
    n_late = len(late_stacks)
    n_cp = n_late * (N_CHIPS - 1)
    pass_steps = [min(((k + 1) * nt) // n_late + (1 if k + 1 < n_late else -2), nt - 1) if nt > 2 else nt - 1
                  for k in range(n_late)]
    pass_steps = [max(p, 0) for p in pass_steps]

    def body(x_ref, mod_ref, cw_ref, win_ref, wp_ref, ps_ref, wout_ref, g_ref, b_ref, *rest):
        late = rest[n_late:2 * n_late]
        x1_ref, zs_ref, mix_ref, h1_ref, cat_ref = rest[2 * n_late:2 * n_late + 5]
        ubuf, vbuf, s_ici, r_ici, s_d2d, r_d2d = rest[2 * n_late + 5:]
        i = pl.program_id(0)
        start, forward, finish = _stack_gather(late, [s.shape[1] for s in late_stacks], s_ici, r_ici, s_d2d, r_d2d)

        @pl.when(i == 0)
        def _():
            start()
            ubuf[0:HALO, :] = jnp.zeros((HALO, cw), F32)
            vbuf[0:HALO, :] = jnp.zeros((HALO, cw), F32)

        xv = x_ref[...]
        sh1, sc1, g1 = mod_ref[0:1, 0:d], mod_ref[0:1, d:2 * d], mod_ref[0:1, 2 * d:3 * d]
        h1 = (xv * (1.0 + sc1) + sh1).astype(BF16)
        h1_ref[...] = h1
        zb = _dot(h1, win_ref[0])
        zc = _dot(h1, win_ref[1])
        zv = _dot(h1, win_ref[2])
        zp = _dot(h1, win_ref[3])
        u = zc * zv
        ubuf[HALO:HALO + tm, :] = u
        u1 = ubuf[pl.ds(HALO - 1, tm), :]
        u2 = ubuf[pl.ds(HALO - 2, tm), :]
        conv = cw_ref[0:1, :] * u2 + cw_ref[1:2, :] * u1 + cw_ref[2:3, :] * u
        ubuf[0:HALO, :] = ubuf[tm:tm + HALO, :]
        yc = zb * conv
        vbuf[HALO:HALO + tm, :] = zp
        ps, pms = [], []
        for gi, win in enumerate(POOL_WINDOWS):
            sl = slice(gi * dg, (gi + 1) * dg)
            acc = zp[:, sl]
            for s in range(1, win):
                acc = acc + vbuf[pl.ds(HALO - s, tm), sl]
            p_g = acc * _inv_count(i, tm, win) - zp[:, sl]
            ps.append(p_g)
            pms.append(_dot(p_g.astype(BF16), wp_ref[gi].astype(BF16)))
        vbuf[0:HALO, :] = vbuf[tm:tm + HALO, :]
        pooled = jnp.concatenate(ps, axis=1)
        yp = jnp.concatenate(pms, axis=1) * ps_ref[...]
        cat = jnp.concatenate([yc, yp], axis=1).astype(BF16)
        cat_ref[...] = cat
        mix = _dot(cat, wout_ref[...])
        mix_ref[...] = mix
        xhat, _ = _layer_norm_fwd(DEEPNORM_ALPHA * xv + (1.0 + g1) * mix)
        x1_ref[...] = xhat * g_ref[...] + b_ref[...]
        zs_ref[:, 0 * cw:1 * cw] = zb.astype(BF16)
        zs_ref[:, 1 * cw:2 * cw] = zc.astype(BF16)
        zs_ref[:, 2 * cw:3 * cw] = zv.astype(BF16)
        zs_ref[:, 3 * cw:4 * cw] = conv.astype(BF16)
        zs_ref[:, 4 * cw:5 * cw] = pooled.astype(BF16)

        for k in range(n_late):
            @pl.when(i == pass_steps[k])
            def _():
                forward(k)

        @pl.when(i == nt - 1)
        def _():
            finish()

    tile = lambda w: pl.BlockSpec((tm, w), lambda i: (i, 0))
    n_in = 9
    return pl.pallas_call(
        body, name="fwd_mix",
        out_shape=tuple(jax.ShapeDtypeStruct(s.shape, s.dtype) for s in late_stacks)
        + (jax.ShapeDtypeStruct((t_len, d), F32), jax.ShapeDtypeStruct((t_len, 5 * cw), BF16),
           jax.ShapeDtypeStruct((t_len, d), F32), jax.ShapeDtypeStruct((t_len, d), BF16),
           jax.ShapeDtypeStruct((t_len, 2 * cw), BF16)),
        grid=(nt,),
        in_specs=[tile(d)] + [VMEM_SPEC] * 8 + [HBM_SPEC] * n_late,
        out_specs=(HBM_SPEC,) * n_late + (tile(d), tile(5 * cw), tile(d), tile(d), tile(2 * cw)),
        input_output_aliases={n_in + k: k for k in range(n_late)},
        scratch_shapes=[pltpu.VMEM((tm + HALO, cw), F32), pltpu.VMEM((tm + HALO, cw), F32),
                        _dma_sems(n_cp), _dma_sems(n_cp), _dma_sems(n_cp), _dma_sems(n_cp)],
        compiler_params=_params(dimension_semantics=("arbitrary",)),
    )(x, mod, conv_w, w_in4, w_pool, pool_scale, w_out, ln_g, ln_b, *late_stacks)


def _mlp_fwd_bwd(x1, tgt, mod, w_mi4, w_mo4, ln_g, ln_b, tm):
    t_len, d = x1.shape
    fq = w_mi4.shape[2]
    nt = t_len // tm

    def body(x1_ref, tgt_ref, mod_ref, wi_ref, wo_ref, g_ref, b_ref,
             dx1_ref, da_ref, s_ref, df_ref, h2_ref, st_ref, relu_buf, acc):
        i = pl.program_id(0)

        @pl.when(i == 0)
        def _():
            acc[...] = jnp.zeros(acc.shape, F32)

        x1v = x1_ref[...]
        sh2, sc2, g2 = mod_ref[0:1, 3 * d:4 * d], mod_ref[0:1, 4 * d:5 * d], mod_ref[0:1, 5 * d:6 * d]
        h2 = (x1v * (1.0 + sc2) + sh2).astype(BF16)
        h2_ref[...] = h2
        f = jnp.zeros((tm, d), F32)
        for j in range(N_CHIPS):
            a = jnp.maximum(_dot(h2, wi_ref[j]), 0.0)
            relu_buf[j] = a
            s = (a * a).astype(BF16)
            s_ref[j] = s
            f = f + _dot(s, wo_ref[j])
        xhat, rstd = _layer_norm_fwd(DEEPNORM_ALPHA * x1v + (1.0 + g2) * f)
        gain = g_ref[...]
        err = xhat * gain + b_ref[...] - tgt_ref[...]
        dy = err * (1.0 / d)
        dr2 = _layer_norm_bwd(dy, xhat, rstd, gain)
        df = ((1.0 + g2) * dr2).astype(BF16)
        df_ref[...] = df
        dh2 = jnp.zeros((tm, d), F32)
        for j in range(N_CHIPS):
            ds = _dot_nt(df, wo_ref[j])
            da = (ds * (2.0 * relu_buf[j])).astype(BF16)
            da_ref[j] = da
            dh2 = dh2 + _dot_nt(da, wi_ref[j])
        dx1_ref[...] = DEEPNORM_ALPHA * dr2 + dh2 * (1.0 + sc2)
        acc[0] += _fold8(dy * xhat)
        acc[1] += _fold8(dy)
        acc[2] += _fold8(dr2 * f)
        acc[3] += _fold8(dh2 * x1v)
        acc[4] += _fold8(dh2)
        acc[5] += _fold8(err * err)

        @pl.when(i == nt - 1)
        def _():
            for k in range(5):
                st_ref[k:k + 1, :] = jnp.sum(acc[k], axis=0, keepdims=True)
            loss = jnp.sum(acc[5]) * (0.5 / d)
            st_ref[5:6, :] = jnp.broadcast_to(loss, (1, d))
            st_ref[6:8, :] = jnp.zeros((2, d), F32)

    tile = lambda w: pl.BlockSpec((tm, w), lambda i: (i, 0))
    tile4 = pl.BlockSpec((N_CHIPS, tm, fq), lambda i: (0, i, 0))
    return pl.pallas_call(
        body, name="mlp_fwd_bwd",
        out_shape=(jax.ShapeDtypeStruct((t_len, d), F32),
                   jax.ShapeDtypeStruct((N_CHIPS, t_len, fq), BF16), jax.ShapeDtypeStruct((N_CHIPS, t_len, fq), BF16),
                   jax.ShapeDtypeStruct((t_len, d), BF16), jax.ShapeDtypeStruct((t_len, d), BF16),
                   jax.ShapeDtypeStruct((SUBLANES, d), F32)),
        grid=(nt,),
        in_specs=[tile(d), tile(d)] + [VMEM_SPEC] * 5,
        out_specs=(tile(d), tile4, tile4, tile(d), tile(d), pl.BlockSpec((SUBLANES, d), lambda i: (0, 0))),
        scratch_shapes=[pltpu.VMEM((N_CHIPS, tm, fq), F32), pltpu.VMEM((6, SUBLANES, d), F32)],
        compiler_params=_params(dimension_semantics=("arbitrary",)),
    )(x1, tgt, mod, w_mi4, w_mo4, ln_g, ln_b)


def _mlp_wgrad(h2, da4, s4, df, pos, tt):
    t_len, d = h2.shape
    fq = da4.shape[2]
    nt = t_len // tt
    assert nt >= 3
    rh = (d // 2, fq // 2)
    last = N_CHIPS - 1

    def body(pos_ref, h2_ref, da_ref, s_ref, df_ref, qi_ref, qo_ref, gi_ref, go_ref,
             acc_i, acc_o, rcv_i, rcv_o, s_sem, r_sem):
        j, t = pl.program_id(0), pl.program_id(1)
        slot = lax.rem(j, 2)
        x, y, c = _mesh_pos()
        accs, rcvs, q_refs, g_refs = (acc_i, acc_o), (rcv_i, rcv_o), (qi_ref, qo_ref), (gi_ref, go_ref)

        def to_sibling(a, sl):
            theirs = pl.multiple_of((1 - c) * rh[a], rh[a])
            return pltpu.make_async_remote_copy(accs[a].at[sl].at[pl.ds(theirs, rh[a]), :], rcvs[a].at[sl],
                                                s_sem.at[2 * sl + a], r_sem.at[2 * sl + a],
                                                device_id=(x, y, 1 - c), device_id_type=MESH)

        def emit(chunk, sl):
            for a in range(2):
                to_sibling(a, sl).wait()
                mine = pl.multiple_of(c * rh[a], rh[a])
                q = (accs[a][sl, pl.ds(mine, rh[a]), :] + rcvs[a][sl]).astype(BF16)
                q_refs[a][...] = q

                @pl.when(chunk == pos_ref[1])
                def _():
                    g_refs[a][...] = q

        @pl.when(t == 0)
        def _():
            acc_i[slot] = jnp.zeros((d, fq), F32)
            acc_o[slot] = jnp.zeros((fq, d), F32)

        acc_i[slot] += _dot_tn(h2_ref[...], da_ref[...])
        acc_o[slot] += _dot_tn(s_ref[...], df_ref[...])

        @pl.when((t == 1) & (j >= 1))
        def _():
            emit(j - 1, 1 - slot)

        @pl.when(t == nt - 1)
        def _():
            for a in range(2):
                to_sibling(a, slot).start()

        @pl.when((t == nt - 1) & (j == last))
        def _():
            emit(j, slot)

    q_index = lambda j, t, p: (jnp.where(t == nt - 1, j, jnp.maximum(j - 1, 0)), 0, 0)
    got_index = lambda j, t, p: (p[1], 0, 0)
    blocks = ((None, rh[0], fq), (None, rh[1], d))
    return pl.pallas_call(
        body, name="mlp_wgrad",
        out_shape=tuple(jax.ShapeDtypeStruct((N_CHIPS,) + b[1:], BF16) for b in blocks) * 2,
        grid_spec=pltpu.PrefetchScalarGridSpec(
            num_scalar_prefetch=1, grid=(N_CHIPS, nt),
            in_specs=[pl.BlockSpec((tt, d), lambda j, t, p: (t, 0)),
                      pl.BlockSpec((None, tt, fq), lambda j, t, p: (j, t, 0)),
                      pl.BlockSpec((None, tt, fq), lambda j, t, p: (j, t, 0)),
                      pl.BlockSpec((tt, d), lambda j, t, p: (t, 0))],
            out_specs=tuple(pl.BlockSpec(b, q_index) for b in blocks)
            + tuple(pl.BlockSpec(b, got_index) for b in blocks),
            scratch_shapes=[pltpu.VMEM((2, d, fq), F32), pltpu.VMEM((2, fq, d), F32),
                            pltpu.VMEM((2, rh[0], fq), F32), pltpu.VMEM((2, rh[1], d), F32),
                            _dma_sems(4), _dma_sems(4)]),
        compiler_params=_params(dimension_semantics=("arbitrary", "arbitrary")),
    )(pos, h2, da4, s4, df)


def _bwd_mix(x, mix, dx1, zs, mod, conv_w, w_in4, w_pool, pool_scale, w_out, ln_g, after, tm):
    t_len, d = x.shape
    cw = w_in4.shape[2]
    dg = cw // N_GROUPS
    nt = t_len // tm

    def body(x_ref, mix_ref, dx1_ref, zs_ref, mod_ref, cw_ref, win_ref, wp_ref, ps_ref, wout_ref, g_ref, after_ref,
             gx_ref, dz_ref, dmix_ref, sd_ref, sc_ref, dwp_ref, dbuf, qbuf, acc_d, acc_c, acc_p):
        i = pl.program_id(0)
        tile_idx = nt - 1 - i

        @pl.when(i == 0)
        def _():
            dbuf[tm:tm + HALO, :] = jnp.zeros((HALO, cw), F32)
            qbuf[tm:tm + HALO, :] = jnp.zeros((HALO, cw), F32)
            acc_d[...] = jnp.zeros(acc_d.shape, F32)
            acc_c[...] = jnp.zeros(acc_c.shape, F32)
            acc_p[...] = jnp.zeros(acc_p.shape, F32)

        xv, mixv, dx1v = x_ref[...], mix_ref[...], dx1_ref[...]
        sc1, g1 = mod_ref[0:1, d:2 * d], mod_ref[0:1, 2 * d:3 * d]
        xhat, rstd = _layer_norm_fwd(DEEPNORM_ALPHA * xv + (1.0 + g1) * mixv)
        dr1 = _layer_norm_bwd(dx1v, xhat, rstd, g_ref[...])
        dmix = ((1.0 + g1) * dr1).astype(BF16)
        dmix_ref[...] = dmix
        dcat = _dot_nt(dmix, wout_ref[...])
        dyc, dyp = dcat[:, 0:cw], dcat[:, cw:2 * cw]
        zb, zc, zv = (zs_ref[:, k * cw:(k + 1) * cw].astype(F32) for k in range(3))
        conv, pooled = zs_ref[:, 3 * cw:4 * cw].astype(F32), zs_ref[:, 4 * cw:5 * cw]
        dzb = dyc * conv
        dcv = dyc * zb
        dbuf[0:tm, :] = dcv
        d1 = dbuf[pl.ds(1, tm), :]
        d2 = dbuf[pl.ds(2, tm), :]
        du = cw_ref[2:3, :] * dcv + cw_ref[1:2, :] * d1 + cw_ref[0:1, :] * d2
        dbuf[tm:tm + HALO, :] = dbuf[0:HALO, :]
        u = zc * zv
        acc_c[0] += _fold8(u * d2)
        acc_c[1] += _fold8(u * d1)
        acc_c[2] += _fold8(u * dcv)
        dzc = du * zv
        dzv = du * zc
        dpw = dyp * ps_ref[...]
        dps, pms = [], []
        for gi, win in enumerate(POOL_WINDOWS):
            sl = slice(gi * dg, (gi + 1) * dg)
            p_g = pooled[:, sl].astype(BF16)
            dpw_g = dpw[:, sl].astype(BF16)
            wp_g = wp_ref[gi].astype(BF16)
            pms.append(_dot(p_g, wp_g))
            acc_p[gi] += _dot_tn(p_g, dpw_g)
            dp_g = _dot_nt(dpw_g, wp_g)
            dps.append(dp_g)
            qbuf[0:tm, sl] = dp_g * _inv_count(tile_idx, tm, win)
        acc_c[3] += _fold8(dyp * jnp.concatenate(pms, axis=1))
        dzps = []
        for gi, win in enumerate(POOL_WINDOWS):
            sl = slice(gi * dg, (gi + 1) * dg)
            acc = qbuf[0:tm, sl]
            for s in range(1, win):
                acc = acc + qbuf[pl.ds(s, tm), sl]
            dzps.append(acc - dps[gi])
        qbuf[tm:tm + HALO, :] = qbuf[0:HALO, :]
        dz = [dzb.astype(BF16), dzc.astype(BF16), dzv.astype(BF16), jnp.concatenate(dzps, axis=1).astype(BF16)]
        dh1 = jnp.zeros((tm, d), F32)
        for j in range(N_CHIPS):
            dz_ref[j] = dz[j]
            dh1 = dh1 + _dot_nt(dz[j], win_ref[j])
        gx_ref[...] = DEEPNORM_ALPHA * dr1 + dh1 * (1.0 + sc1)
        acc_d[0] += _fold8(dx1v * xhat)
        acc_d[1] += _fold8(dx1v)
        acc_d[2] += _fold8(dr1 * mixv)
        acc_d[3] += _fold8(dh1 * xv)
        acc_d[4] += _fold8(dh1)

        @pl.when(i == nt - 1)
        def _():
            for k in range(5):
                sd_ref[k:k + 1, :] = jnp.sum(acc_d[k], axis=0, keepdims=True)
            sd_ref[5:8, :] = jnp.zeros((3, d), F32)
            for k in range(4):
                sc_ref[k:k + 1, :] = jnp.sum(acc_c[k], axis=0, keepdims=True)
            sc_ref[4:8, :] = jnp.zeros((4, cw), F32)
            dwp_ref[...] = acc_p[...]

    rtile = lambda w: pl.BlockSpec((tm, w), lambda i: (nt - 1 - i, 0))
    whole = lambda shape: pl.BlockSpec(shape, lambda i: tuple(0 for _ in shape))
    return pl.pallas_call(
        body, name="bwd_mix",
        out_shape=(jax.ShapeDtypeStruct((t_len, d), F32), jax.ShapeDtypeStruct((N_CHIPS, t_len, cw), BF16),
                   jax.ShapeDtypeStruct((t_len, d), BF16), jax.ShapeDtypeStruct((SUBLANES, d), F32),
                   jax.ShapeDtypeStruct((SUBLANES, cw), F32), jax.ShapeDtypeStruct((N_GROUPS, dg, dg), F32)),
        grid=(nt,),
        in_specs=[rtile(d), rtile(d), rtile(d), rtile(5 * cw)] + [VMEM_SPEC] * 7
        + [pl.BlockSpec(memory_space=pl.ANY)],
        out_specs=(rtile(d), pl.BlockSpec((N_CHIPS, tm, cw), lambda i: (0, nt - 1 - i, 0)), rtile(d),
                   whole((SUBLANES, d)), whole((SUBLANES, cw)), whole((N_GROUPS, dg, dg))),
        scratch_shapes=[pltpu.VMEM((tm + HALO, cw), F32), pltpu.VMEM((tm + HALO, cw), F32),
                        pltpu.VMEM((5, SUBLANES, d), F32), pltpu.VMEM((4, SUBLANES, cw), F32),
                        pltpu.VMEM((N_GROUPS, dg, dg), F32)],
        compiler_params=_params(dimension_semantics=("arbitrary",)),
    )(x, mix, dx1, zs, mod, conv_w, w_in4, w_pool, pool_scale, w_out, ln_g, after)


def _mix_wgrad(h1, dz4, cat, dmix, small, dmod_rows, tt):
    t_len, d = h1.shape
    cw = dz4.shape[2]
    nt = t_len // tt
    ro = (2 * cw) // N_CHIPS
    rh = (d // 2, ro // 2)
    n_small = len(small)

    def body(h1_ref, dz_ref, cat_ref, dmix_ref, *rest):
        p_refs = rest[:n_small]
        qi_ref, qo_ref, gi_ref, go_ref = rest[n_small:n_small + 4]
        sum_refs = rest[n_small + 4:2 * n_small + 4]
        dmod_ref, dmod_sum_ref = rest[2 * n_small + 4:2 * n_small + 6]
        scratch = rest[2 * n_small + 6:]
        acc_i, acc_o, rcv_i, rcv_o = scratch[:4]
        all_bufs = scratch[4:4 + n_small]
        s_sem, r_sem, s_small, r_small_sem = scratch[4 + n_small:]
        t = pl.program_id(0)
        x, y, c = _mesh_pos()
        chip = 2 * x + y
        groups = [(p_refs[k], sum_refs[k], None, all_bufs[k]) for k in range(n_small)]
        small_start, small_finish = _small_allreduce_steps(groups, s_small, r_small_sem)
        accs, rcvs, q_refs, g_refs = (acc_i, acc_o), (rcv_i, rcv_o), (qi_ref, qo_ref), (gi_ref, go_ref)

        @pl.when(t == 0)
        def _():
            small_start()
            acc_i[...] = jnp.zeros(acc_i.shape, F32)
            acc_o[...] = jnp.zeros(acc_o.shape, F32)

        def accumulate(j):
            acc_i[j] += _dot_tn(h1_ref[...], dz_ref[j])
            acc_o[j] += _dot_tn(cat_ref[:, j * ro:(j + 1) * ro], dmix_ref[...])

        def to_sibling(a, j):
            theirs = pl.multiple_of((1 - c) * rh[a], rh[a])
            return pltpu.make_async_remote_copy(accs[a].at[j].at[pl.ds(theirs, rh[a]), :], rcvs[a].at[j],
                                                s_sem.at[a * N_CHIPS + j], r_sem.at[a * N_CHIPS + j],
                                                device_id=(x, y, 1 - c), device_id_type=MESH)

        @pl.when(t < nt - 1)
        def _():
            for j in range(N_CHIPS):
                accumulate(j)

        @pl.when(t == nt - 1)
        def _():
            for j in range(N_CHIPS):
                accumulate(j)
                for a in range(2):
                    to_sibling(a, j).start()
            for a in range(2):
                mine = pl.multiple_of(c * rh[a], rh[a])
                for j in range(N_CHIPS):
                    to_sibling(a, j).wait()
                    q_refs[a][j] = (accs[a][j, pl.ds(mine, rh[a]), :] + rcvs[a][j]).astype(BF16)
                g_refs[a][chip] = q_refs[a][chip]
            small_finish()
            for k, (arr, row) in enumerate(dmod_rows):
                cols = slice(k * d, (k + 1) * d)
                dmod_sum_ref[:, cols] = sum_refs[arr][row:row + 1, :]
                for b in range(N_DEV):
                    dmod_ref[b:b + 1, cols] = all_bufs[arr][b, row:row + 1, :]

    stacks = ((N_CHIPS, rh[0], cw), (N_CHIPS, rh[1], d))
    n_cp = n_small * (N_DEV - 1)
    n_mod = len(dmod_rows) * d
    return pl.pallas_call(
        body, name="mix_wgrad",
        out_shape=tuple(jax.ShapeDtypeStruct(s, BF16) for s in stacks) * 2
        + tuple(jax.ShapeDtypeStruct(a.shape, F32) for a in small)
        + (jax.ShapeDtypeStruct((N_DEV, n_mod), F32), jax.ShapeDtypeStruct((1, n_mod), F32)),
        grid=(nt,),
        in_specs=[pl.BlockSpec((tt, d), lambda t: (t, 0)), pl.BlockSpec((N_CHIPS, tt, cw), lambda t: (0, t, 0)),
                  pl.BlockSpec((tt, 2 * cw), lambda t: (t, 0)), pl.BlockSpec((tt, d), lambda t: (t, 0))]
        + [VMEM_SPEC] * n_small,
        out_specs=(VMEM_SPEC,) * (6 + n_small),
        scratch_shapes=[pltpu.VMEM((N_CHIPS, d, cw), F32), pltpu.VMEM((N_CHIPS, ro, d), F32),
                        pltpu.VMEM(stacks[0], F32), pltpu.VMEM(stacks[1], F32)]
        + [pltpu.VMEM((N_DEV,) + a.shape, F32) for a in small]
        + [_dma_sems(2 * N_CHIPS), _dma_sems(2 * N_CHIPS), _dma_sems(n_cp), _dma_sems(n_cp)],
        compiler_params=_params(dimension_semantics=("arbitrary",)),
    )(h1, dz4, cat, dmix, *small)


def kernel(x, c, w_ada, b_ada, w_in, conv_w, w_pool, pool_scale, w_out, ln1_g, ln1_b, w_mlp_in, w_mlp_out, ln2_g, ln2_b, loss_target, m_w_ada, m_b_ada, m_w_in, m_conv_w, m_w_pool, m_pool_scale, m_w_out, m_ln1_g, m_ln1_b, m_w_mlp_in, m_w_mlp_out, m_ln2_g, m_ln2_b, v_w_ada, v_b_ada, v_w_in, v_conv_w, v_w_pool, v_pool_scale, v_w_out, v_ln1_g, v_ln1_b, v_w_mlp_in, v_w_mlp_out, v_ln2_g, v_ln2_b):
    t_len, d = x.shape[1], x.shape[2]
    cw = w_in.shape[2]
    cq = conv_w.shape[2]
    dg = w_pool.shape[2]
    assert cw == N_GROUPS * dg and cq * N_CHIPS == cw and LANES % cq == 0
    tm_mix = min(MIX_TOKEN_TILE, t_len)
    tm_mlp = min(MLP_TOKEN_TILE, t_len)
    tt = min(WGRAD_TILE, t_len // 4)
    chip = 2 * lax.axis_index("x") + lax.axis_index("y")
    pos = jnp.stack([lax.axis_index("c"), chip]).astype(jnp.int32)

    x2, tgt = x[0], loss_target[0]
    big_w = [w_in[0], w_out[0], w_mlp_in[0], w_mlp_out[0]]
    big_m = [m_w_in[0], m_w_out[0], m_w_mlp_in[0], m_w_mlp_out[0]]
    big_v = [v_w_in[0], v_w_out[0], v_w_mlp_in[0], v_w_mlp_out[0]]
    names = ["w_in", "w_out", "w_mlp_in", "w_mlp_out"]

    mod, cond_all, conv_full, w_in4, w_out4, w_mi_own, w_mo_own = _setup_exchange(
        c, w_ada[0], b_ada, conv_w[0], big_w, 2)
    w_out_full = w_out4.reshape(2 * cw, d)
    w_pool_g = w_pool[0]

    w_mi4, w_mo4, x1, zs, mix, h1, cat = _fwd_mix(x2, mod, conv_full, w_in4, w_pool_g, pool_scale, w_out_full,
                                                  ln1_g, ln1_b, [w_mi_own, w_mo_own], tm_mix)
    dx1, da4, s4, df, h2, st2 = _mlp_fwd_bwd(x1, tgt, mod, w_mi4, w_mo4, ln2_g, ln2_b, tm_mlp)
    mlp_qi, mlp_qo, mlp_gi, mlp_go = _mlp_wgrad(h2, da4, s4, df, pos, tt)
    mlp_started = _split_start([mlp_qi, mlp_qo, mlp_gi, mlp_go], _chunk_exchange_plan(2), 2 * (N_CHIPS - 1),
                               "exchange_mlp")
    grad_x, dz4, dmix, st1, stc, dw_pool = _bwd_mix(x2, mix, dx1, zs, mod, conv_full, w_in4, w_pool_g, pool_scale,
                                                    w_out_full, ln1_g, mlp_started[-1], tm_mix)

    dmod_rows = [(0, 4), (0, 3), (0, 2), (1, 4), (1, 3), (1, 2)]
    mix_qi, mix_qo, mix_gi, mix_go, t1, t2, tc, g_w_pool, dmod_all, g_b_ada = _mix_wgrad(
        h1, dz4, cat, dmix, [st1, st2, stc, dw_pool], dmod_rows, tt)
    loss = t2[5, 0]
    na = w_ada.shape[2]
    dmod_mine = lax.dynamic_slice_in_dim(dmod_all, chip * na, na, axis=1)

    exchange, n_cp = _chunk_exchange_plan(2), 2 * (N_CHIPS - 1)
    mlp_got = _split_wait(mlp_started, exchange, t1, "exchange_mlp")[2:]
    mix_started = _split_start([mix_qi, mix_qo, mix_gi, mix_go], exchange, n_cp, "exchange_mix")
    mlp_halves = list(_sum_chips(mlp_got, pos, mix_started[-1], "rs_total_mlp"))
    join = _sibling_join_plan([h.shape for h in mlp_halves])
    join_started = _split_start(mlp_halves, join, 2, "join_mlp")
    g_ada, d_ada, nm_ada, nv_ada = _adamw_ada(cond_all, dmod_mine, w_ada[0], m_w_ada[0], v_w_ada[0],
                                              join_started[-1])
    mlp_g = _split_wait(join_started, join, d_ada, "join_mlp")
    mlp_out = _adamw_big(mlp_g, big_w[2:], big_m[2:], big_v[2:], "adamw_mlp")
    mix_got = _split_wait(mix_started, exchange, mlp_out[1][0], "exchange_mix")[2:]
    mix_halves = list(_sum_chips(mix_got, pos, d_ada, "rs_total_mix"))
    mix_g = _sibling_join(mix_halves, "mix")
    mix_out = _adamw_big(mix_g, big_w[:2], big_m[:2], big_v[:2], "adamw_mix")
    big_g = list(mix_g) + list(mlp_g)
    big_out = mix_out + mlp_out

    sources = [g_b_ada, ("chip_cols", 2, 0, 3, cq), g_w_pool.reshape(-1, dg), ("rows", 2, 3, 4),
               ("rows", 0, 0, 1), ("rows", 0, 1, 2), ("rows", 1, 0, 1), ("rows", 1, 1, 2)]
    small_w = [b_ada, conv_w[0], w_pool.reshape(-1, dg), pool_scale, ln1_g, ln1_b, ln2_g, ln2_b]
    small_m = [m_b_ada, m_conv_w[0], m_w_pool.reshape(-1, dg), m_pool_scale, m_ln1_g, m_ln1_b, m_ln2_g, m_ln2_b]
    small_v = [v_b_ada, v_conv_w[0], v_w_pool.reshape(-1, dg), v_pool_scale, v_ln1_g, v_ln1_b, v_ln2_g, v_ln2_b]
    cut, sm = _adamw_small(pos, [t1, t2, tc], sources, small_w, small_m, small_v)
    small_g = [sources[0], cut[0], sources[2]] + list(cut[1:])
    ns = len(small_w)
    s_delta, s_m, s_v = sm[:ns], sm[ns:2 * ns], sm[2 * ns:]

    def assemble(ada, small, big):
        return [ada[None], small[0], big[0][None], small[1][None], small[2].reshape(w_pool.shape), small[3],
                big[1][None], small[4], small[5], big[2][None], big[3][None], small[6], small[7]]

    grads = assemble(g_ada, small_g, big_g)
    deltas = assemble(d_ada, s_delta, [o[0] for o in big_out])
    new_m = assemble(nm_ada, s_m, [o[1] for o in big_out])
    new_v = assemble(nv_ada, s_v, [o[2] for o in big_out])
    return (loss, grad_x[None], *grads, *deltas, *new_m, *new_v)
```

```python
import jax
import jax.numpy as jnp
from jax import lax
from jax.experimental import pallas as pl
from jax.experimental.pallas import tpu as pltpu

F32 = jnp.float32
BF16 = jnp.bfloat16
MESH = pl.DeviceIdType.MESH

LN_EPS = 1e-5
DEEPNORM_ALPHA = 2.0 ** 0.25
POOL_WINDOWS = (2, 4, 8, 16)
N_GROUPS = len(POOL_WINDOWS)
HALO = 16
N_CHIPS = 4
N_DEV = 8
LANES = 128
SUBLANES = 8
VMEM_LIMIT = 56 * 1024 * 1024
MIX_TOKEN_TILE = 512
MLP_TOKEN_TILE = 256
WGRAD_TILE = 1024

ADAM_LR = 0.001
ADAM_B1 = 0.9
ADAM_B2 = 0.999
ADAM_EPS = 1e-08
ADAM_WD = 0.01
ADAM_STEP = 10

VMEM_SPEC = pl.BlockSpec(memory_space=pltpu.VMEM)
HBM_SPEC = pl.BlockSpec(memory_space=pltpu.HBM)


def _dot(a, b):
    return jnp.dot(a, b, preferred_element_type=F32)


def _dot_nt(a, b):
    return lax.dot_general(a, b, (((1,), (1,)), ((), ())), preferred_element_type=F32)


def _dot_tn(a, b):
    return lax.dot_general(a, b, (((0,), (0,)), ((), ())), preferred_element_type=F32)


def _fold8(v):
    r, n = v.shape
    return jnp.sum(v.reshape(r // SUBLANES, SUBLANES, n), axis=0)


def _mesh_pos():
    return lax.axis_index("x"), lax.axis_index("y"), lax.axis_index("c")


def _flip(v, bit):
    return 1 - v if bit else v


def _params(**kw):
    return pltpu.CompilerParams(vmem_limit_bytes=VMEM_LIMIT, **kw)


def _dma_sems(n):
    return pltpu.SemaphoreType.DMA((n,))


def _stack_gather(stacks, rows, s_ici, r_ici, s_d2d, r_d2d):
    x, y, c = _mesh_pos()
    chip = 2 * x + y
    pairs = [(i, m) for i in range(len(stacks)) for m in range(1, N_CHIPS)]

    def blk(i, slot, which):
        rh = rows[i] // 2
        return stacks[i].at[slot].at[pl.ds(which * rh, rh), :]

    def other(m):
        return 2 * _flip(x, m & 2) + _flip(y, m & 1)

    def ici(i, m, slot, to):
        k = i * (N_CHIPS - 1) + m - 1
        return pltpu.make_async_remote_copy(blk(i, slot, c), blk(i, slot, c), s_ici.at[k], r_ici.at[k],
                                            device_id=to, device_id_type=MESH)

    def d2d(i, m, which, to):
        k = i * (N_CHIPS - 1) + m - 1
        return pltpu.make_async_remote_copy(blk(i, other(m), which), blk(i, other(m), which),
                                            s_d2d.at[k], r_d2d.at[k], device_id=to, device_id_type=MESH)

    def start():
        for i in range(len(stacks)):
            for m in (N_CHIPS - 1,) + tuple(range(1, N_CHIPS - 1)):
                ici(i, m, chip, (_flip(x, m & 2), _flip(y, m & 1), c)).start()

    def forward(which=None):
        for i, m in pairs:
            if which is None or i == which:
                ici(i, m, other(m), (x, y, c)).wait_recv()
                d2d(i, m, c, (x, y, 1 - c)).start()

    def finish():
        for i, m in pairs:
            d2d(i, m, 1 - c, (x, y, c)).wait_recv()
        for i, m in pairs:
            ici(i, m, chip, (x, y, c)).wait_send()
            d2d(i, m, c, (x, y, c)).wait_send()

    return start, forward, finish


def _chunk_exchange(qs, gots, s_sem, r_sem):
    x, y, c = _mesh_pos()
    chip = 2 * x + y
    pairs = [(i, m) for i in range(len(qs)) for m in range(1, N_CHIPS)]

    def other(m):
        return 2 * _flip(x, m & 2) + _flip(y, m & 1)

    def send(i, m, to):
        k = i * (N_CHIPS - 1) + m - 1
        return pltpu.make_async_remote_copy(qs[i].at[other(m)], gots[i].at[chip], s_sem.at[k], r_sem.at[k],
                                            device_id=to, device_id_type=MESH)

    def arrival(i, m):
        k = i * (N_CHIPS - 1) + m - 1
        return pltpu.make_async_remote_copy(qs[i].at[other(m)], gots[i].at[other(m)], s_sem.at[k], r_sem.at[k],
                                            device_id=(x, y, c), device_id_type=MESH)

    def start():
        for i, m in pairs:
            send(i, m, (_flip(x, m & 2), _flip(y, m & 1), c)).start()

    def finish():
        for i, m in pairs:
            arrival(i, m).wait_recv()
        for i, m in pairs:
            send(i, m, (x, y, c)).wait_send()

    return start, finish


def _mod_scratch(d, na, cq):
    return [pltpu.VMEM((SUBLANES, d), F32), pltpu.VMEM((N_DEV, SUBLANES, d), F32),
            pltpu.VMEM((N_CHIPS, SUBLANES, na), F32),
            pltpu.VMEM((SUBLANES, cq), F32), pltpu.VMEM((N_CHIPS, SUBLANES, cq), F32),
            _dma_sems(N_DEV - 1), _dma_sems(N_DEV - 1),
            _dma_sems(N_CHIPS - 1), _dma_sems(N_CHIPS - 1), _dma_sems(N_CHIPS - 1), _dma_sems(N_CHIPS - 1)]


def _mod_steps(d, na, cq):
    def steps(c_ref, w_ref, b_ref, cw_ref, mod_ref, cond_ref, conv_ref,
              csend, cbuf, mbuf, cvsend, cvbuf, s1, r1, s2, r2, s3, r3):
        x, y, c = _mesh_pos()
        me = 4 * x + 2 * y + c
        chip = 2 * x + y
        csend[...] = jnp.broadcast_to(c_ref[...], (SUBLANES, d))
        cbuf[me] = csend[...]
        first = []
        for m in range(1, N_DEV):
            peer = (_flip(x, m & 4), _flip(y, m & 2), _flip(c, m & 1))
            cp = pltpu.make_async_remote_copy(csend, cbuf.at[me], s1.at[m - 1], r1.at[m - 1],
                                              device_id=peer, device_id_type=MESH)
            cp.start()
            first.append(cp)
        cvsend[...] = jnp.zeros((SUBLANES, cq), F32)
        cvsend[0:3, :] = cw_ref[...]
        cvbuf[chip] = cvsend[...]
        for m in range(1, N_DEV):
            src = 4 * _flip(x, m & 4) + 2 * _flip(y, m & 2) + _flip(c, m & 1)
            pltpu.make_async_remote_copy(csend, cbuf.at[src], s1.at[m - 1], r1.at[m - 1],
                                         device_id=(x, y, c), device_id_type=MESH).wait_recv()
        rows = lax.broadcasted_iota(jnp.int32, (SUBLANES, d), 0)
        call = jnp.zeros((SUBLANES, d), F32)
        for b in range(N_DEV):
            call = jnp.where(rows == b, cbuf[b], call)
        cond = call * jax.nn.sigmoid(call)
        cond_ref[...] = cond
        part = jnp.dot(cond, w_ref[...], preferred_element_type=F32, precision=lax.Precision.HIGHEST)
        mbuf[chip] = part
        second = []
        for m in range(1, N_CHIPS):
            peer = (_flip(x, m & 2), _flip(y, m & 1), c)
            cp = pltpu.make_async_remote_copy(mbuf.at[chip], mbuf.at[chip], s2.at[m - 1], r2.at[m - 1],
                                              device_id=peer, device_id_type=MESH)
            cp.start()
            second.append(cp)
            cp = pltpu.make_async_remote_copy(cvsend, cvbuf.at[chip], s3.at[m - 1], r3.at[m - 1],
                                              device_id=peer, device_id_type=MESH)
            cp.start()
            second.append(cp)
        for m in range(1, N_CHIPS):
            src = 2 * _flip(x, m & 2) + _flip(y, m & 1)
            pltpu.make_async_remote_copy(mbuf.at[src], mbuf.at[src], s2.at[m - 1], r2.at[m - 1],
                                         device_id=(x, y, c), device_id_type=MESH).wait_recv()
            pltpu.make_async_remote_copy(cvsend, cvbuf.at[src], s3.at[m - 1], r3.at[m - 1],
                                         device_id=(x, y, c), device_id_type=MESH).wait_recv()
        for cp in first + second:
            cp.wait_send()
        rows_n = lax.broadcasted_iota(jnp.int32, (SUBLANES, na), 0)
        for k in range(N_CHIPS):
            mine = jnp.sum(jnp.where(rows_n == me, mbuf[k], 0.0), axis=0, keepdims=True)
            mod_ref[:, k * na:(k + 1) * na] = jnp.broadcast_to(mine + b_ref[:, k * na:(k + 1) * na], (SUBLANES, na))
            conv_ref[:, k * cq:(k + 1) * cq] = cvbuf[k]

    return steps


def _setup_exchange(c_row, w_ada_s, b_ada, conv_w_s, shards, n_now):
    n = len(shards)
    n_cp = n_now * (N_CHIPS - 1)
    d, na, cq = c_row.shape[1], w_ada_s.shape[1], conv_w_s.shape[1]
    mod_scratch = _mod_scratch(d, na, cq)
    mod_steps = _mod_steps(d, na, cq)

    def body(*refs):
        mod_in, ins = refs[:4], refs[4:4 + n]
        mod_out, outs = refs[4 + n:7 + n], refs[7 + n:7 + 2 * n]
        bufs = refs[7 + 2 * n:7 + 3 * n]
        mod_scr = refs[7 + 3 * n:7 + 3 * n + len(mod_scratch)]
        s_ici, r_ici, s_d2d, r_d2d, s_loc = refs[7 + 3 * n + len(mod_scratch):7 + 3 * n + len(mod_scratch) + 5]
        w_ada_buf, s_in = refs[7 + 3 * n + len(mod_scratch) + 5:7 + 3 * n + len(mod_scratch) + 7]
        late_bufs = refs[7 + 3 * n + len(mod_scratch) + 7:]
        x, y, _ = _mesh_pos()
        chip = 2 * x + y
        loads = [pltpu.make_async_copy(mod_in[1], w_ada_buf, s_in.at[0])]
        loads += [pltpu.make_async_copy(ins[i], late_bufs[i - n_now], s_in.at[1 + i - n_now]) for i in range(n_now, n)]
        for cp in loads:
            cp.start()

        def place(i, src):
            bufs[i][...] = src[...].astype(BF16)
            cp = pltpu.make_async_copy(bufs[i], outs[i].at[chip], s_loc.at[i])
            cp.start()
            return cp

        start, forward, finish = _stack_gather(outs[:n_now], [s.shape[0] for s in shards[:n_now]],
                                               s_ici, r_ici, s_d2d, r_d2d)
        for cp in [place(i, ins[i]) for i in range(n_now)]:
            cp.wait()
        start()
        later = []
        for i in range(n_now, n):
            loads[1 + i - n_now].wait()
            later.append(place(i, late_bufs[i - n_now]))
        loads[0].wait()
        mod_steps(mod_in[0], w_ada_buf, mod_in[2], mod_in[3], *mod_out, *mod_scr)
        forward()
        finish()
        for cp in later:
            cp.wait()

    any_spec = pl.BlockSpec(memory_space=pl.ANY)
    return pl.pallas_call(
        body, name="setup_exchange",
        out_shape=(jax.ShapeDtypeStruct((SUBLANES, N_CHIPS * na), F32), jax.ShapeDtypeStruct((SUBLANES, d), F32),
                   jax.ShapeDtypeStruct((SUBLANES, N_CHIPS * cq), F32))
        + tuple(jax.ShapeDtypeStruct((N_CHIPS,) + s.shape, BF16) for s in shards),
        in_specs=[VMEM_SPEC, any_spec, VMEM_SPEC, VMEM_SPEC] + [VMEM_SPEC] * n_now + [any_spec] * (n - n_now),
        out_specs=(VMEM_SPEC,) * 3 + (HBM_SPEC,) * n,
        scratch_shapes=[pltpu.VMEM(s.shape, BF16) for s in shards] + mod_scratch
        + [_dma_sems(n_cp), _dma_sems(n_cp), _dma_sems(n_cp), _dma_sems(n_cp), _dma_sems(n)]
        + [pltpu.VMEM(w_ada_s.shape, F32), _dma_sems(1 + n - n_now)]
        + [pltpu.VMEM(s.shape, F32) for s in shards[n_now:]],
        compiler_params=_params(),
    )(c_row, w_ada_s, b_ada, conv_w_s, *shards)


def _sibling_join_plan(shapes):
    def plan(refs, s_sem, r_sem):
        x, y, c = _mesh_pos()

        def half(i, which):
            rh = shapes[i][0] // 2
            return refs[i].at[pl.ds(which * rh, rh), :]

        def start():
            for i in range(len(refs)):
                pltpu.make_async_remote_copy(half(i, c), half(i, c), s_sem.at[i], r_sem.at[i],
                                             device_id=(x, y, 1 - c), device_id_type=MESH).start()

        def finish():
            for i in range(len(refs)):
                pltpu.make_async_remote_copy(half(i, 1 - c), half(i, 1 - c), s_sem.at[i], r_sem.at[i],
                                             device_id=(x, y, c), device_id_type=MESH).wait_recv()
            for i in range(len(refs)):
                pltpu.make_async_remote_copy(half(i, c), half(i, c), s_sem.at[i], r_sem.at[i],
                                             device_id=(x, y, c), device_id_type=MESH).wait_send()

        return start, finish

    return plan


def _chunk_exchange_plan(n):
    return lambda refs, s_sem, r_sem: _chunk_exchange(refs[:n], refs[n:], s_sem, r_sem)


def _sibling_join(gs, tag):
    n = len(gs)
    plan = _sibling_join_plan([g.shape for g in gs])

    def body(*refs):
        start, finish = plan(refs[n:2 * n], refs[2 * n], refs[2 * n + 1])
        start()
        finish()

    return pl.pallas_call(
        body, name="rs_sibling_join_" + tag,
        out_shape=tuple(jax.ShapeDtypeStruct(g.shape, g.dtype) for g in gs),
        in_specs=[HBM_SPEC] * n, out_specs=(HBM_SPEC,) * n,
        input_output_aliases={i: i for i in range(n)},
        scratch_shapes=[_dma_sems(n), _dma_sems(n)],
        compiler_params=_params(),
    )(*gs)


SEM_SPEC = pl.BlockSpec(memory_space=pltpu.SEMAPHORE)
SIDE_EFFECT = pltpu.SideEffectType.DATAFLOW_SIDE_EFFECTING


def _split_start(arrays, plan, n_cp, tag):
    n = len(arrays)

    def body(*refs):
        start, _ = plan(refs[n + 2:2 * n + 2], refs[n], refs[n + 1])
        start()
        token = refs[2 * n + 2]
        token[...] = jnp.zeros_like(token)

    through = [pltpu.with_memory_space_constraint(a, pltpu.HBM) for a in arrays]
    return pl.pallas_call(
        body, name=tag + "_start",
        out_shape=(pltpu.SemaphoreType.DMA((n_cp,)), pltpu.SemaphoreType.DMA((n_cp,)))
        + tuple(pltpu.HBM(a.shape, a.dtype) for a in through) + (jax.ShapeDtypeStruct((SUBLANES, LANES), F32),),
        in_specs=[HBM_SPEC] * n, out_specs=(SEM_SPEC, SEM_SPEC) + (HBM_SPEC,) * n + (VMEM_SPEC,),
        input_output_aliases={k: 2 + k for k in range(n)},
        compiler_params=pltpu.CompilerParams(has_side_effects=SIDE_EFFECT),
    )(*through)


def _split_wait(started, plan, after, tag):
    send_sems, recv_sems, arrays = started[0], started[1], started[2:-1]
    n = len(arrays)

    def body(*refs):
        _, finish = plan(refs[:n], refs[n], refs[n + 1])
        finish()

    return pl.pallas_call(
        body, name=tag + "_wait",
        out_shape=tuple(pltpu.HBM(a.shape, a.dtype) for a in arrays),
        in_specs=[HBM_SPEC] * n + [SEM_SPEC, SEM_SPEC, pl.BlockSpec(memory_space=pl.ANY)],
        out_specs=(HBM_SPEC,) * n,
        input_output_aliases={k: k for k in range(n)},
        compiler_params=pltpu.CompilerParams(has_side_effects=SIDE_EFFECT),
    )(*arrays, send_sems, recv_sems, after)


def _small_allreduce_steps(groups, s_sem, r_sem):
    x, y, c = _mesh_pos()
    me = 4 * x + 2 * y + c
    pairs = [(k, m) for k in range(len(groups)) for m in range(1, N_DEV)]

    def send(k, m, to):
        p_ref, all_ref = groups[k][0], groups[k][3]
        i = k * (N_DEV - 1) + m - 1
        return pltpu.make_async_remote_copy(p_ref, all_ref.at[me], s_sem.at[i], r_sem.at[i],
                                            device_id=to, device_id_type=MESH)

    def start():
        for p_ref, _, _, all_ref in groups:
            all_ref[me] = p_ref[...]
        for k, m in pairs:
            send(k, m, (_flip(x, m & 4), _flip(y, m & 2), _flip(c, m & 1))).start()

    def finish():
        for k, m in pairs:
            p_ref, all_ref = groups[k][0], groups[k][3]
            i = k * (N_DEV - 1) + m - 1
            src = 4 * _flip(x, m & 4) + 2 * _flip(y, m & 2) + _flip(c, m & 1)
            pltpu.make_async_remote_copy(p_ref, all_ref.at[src], s_sem.at[i], r_sem.at[i],
                                         device_id=(x, y, c), device_id_type=MESH).wait_recv()
        for k, m in pairs:
            send(k, m, (x, y, c)).wait_send()
        for _, sum_ref, all_out_ref, all_ref in groups:
            sum_ref[...] = (((all_ref[0] + all_ref[1]) + (all_ref[2] + all_ref[3]))
                            + ((all_ref[4] + all_ref[5]) + (all_ref[6] + all_ref[7])))
            if all_out_ref is not None:
                all_out_ref[...] = all_ref[...]

    return start, finish


def _row_block(rows, cap=256):
    rb = min(rows, cap)
    assert rows % rb == 0
    return rb


def _sum_chips(gots, pos, after, name, nb=2):
    n = len(gots)
    blocks = [(g.shape[1] // nb, g.shape[2]) for g in gots]

    def body(pos_ref, *refs):
        for a_ref, o_ref in zip(refs[:n], refs[n + 1:]):
            a = a_ref[...].astype(F32)
            o_ref[...] = (a[0] + a[1]) + (a[2] + a[3])

    return pl.pallas_call(
        body, name=name,
        out_shape=tuple(jax.ShapeDtypeStruct((2 * g.shape[1], g.shape[2]), F32) for g in gots),
        grid_spec=pltpu.PrefetchScalarGridSpec(
            num_scalar_prefetch=1, grid=(nb,),
            in_specs=[pl.BlockSpec((N_CHIPS,) + b, lambda i, p: (0, i, 0)) for b in blocks]
            + [pl.BlockSpec(memory_space=pl.ANY)],
            out_specs=tuple(pl.BlockSpec(b, lambda i, p: (p[0] * nb + i, 0)) for b in blocks)),
        compiler_params=_params(dimension_semantics=("arbitrary",)),
    )(pos, *gots, after)


def _adamw_math(w, g, m, v):
    m = ADAM_B1 * m + (1.0 - ADAM_B1) * g
    v = ADAM_B2 * v + (1.0 - ADAM_B2) * jnp.square(g)
    m_hat = m / (1.0 - ADAM_B1 ** ADAM_STEP)
    v_hat = v / (1.0 - ADAM_B2 ** ADAM_STEP)
    delta = -ADAM_LR * (m_hat / (jnp.sqrt(v_hat) + ADAM_EPS) + ADAM_WD * w)
    return delta, m, v


def _adamw_big(gs, ws, ms, vs, name, nb=4):
    n = len(ws)
    specs = [pl.BlockSpec((w.shape[0] // nb, w.shape[1]), lambda i: (i, 0)) for w in ws]

    def body(*refs):
        g_refs, w_refs, m_refs, v_refs = (refs[k * n:(k + 1) * n] for k in range(4))
        outs = refs[4 * n:]
        for i in range(n):
            d, mn, vn = _adamw_math(w_refs[i][...], g_refs[i][...], m_refs[i][...], v_refs[i][...])
            outs[3 * i][...] = d
            outs[3 * i + 1][...] = mn
            outs[3 * i + 2][...] = vn

    out = pl.pallas_call(
        body, name=name,
        out_shape=tuple(jax.ShapeDtypeStruct(w.shape, F32) for w in ws for _ in range(3)),
        grid=(nb,), in_specs=specs * 4, out_specs=tuple(s for s in specs for _ in range(3)),
        compiler_params=_params(dimension_semantics=("arbitrary",)),
    )(*gs, *ws, *ms, *vs)
    return [out[3 * i:3 * i + 3] for i in range(n)]


def _adamw_ada(cond, dmod, w, m, v, after):
    d, na = w.shape
    rb = _row_block(d, 128)

    def body(c_ref, dm_ref, w_ref, m_ref, v_ref, after_ref, g_ref, d_ref, mo_ref, vo_ref):
        g = lax.dot_general(c_ref[...], dm_ref[...], (((0,), (0,)), ((), ())), preferred_element_type=F32,
                            precision=lax.Precision.HIGHEST)
        dl, mn, vn = _adamw_math(w_ref[...], g, m_ref[...], v_ref[...])
        g_ref[...] = g
        d_ref[...] = dl
        mo_ref[...] = mn
        vo_ref[...] = vn

    spec = pl.BlockSpec((rb, na), lambda i: (i, 0))
    return pl.pallas_call(
        body, name="adamw_w_ada",
        out_shape=(jax.ShapeDtypeStruct((d, na), F32),) * 4,
        grid=(d // rb,),
        in_specs=[pl.BlockSpec((N_DEV, rb), lambda i: (0, i)), pl.BlockSpec((N_DEV, na), lambda i: (0, 0)),
                  spec, spec, spec, pl.BlockSpec(memory_space=pl.ANY)],
        out_specs=(spec,) * 4,
        compiler_params=_params(dimension_semantics=("arbitrary",)),
    )(cond, dmod, w, m, v, after)


def _adamw_small(pos, stats, sources, ws, ms, vs):
    n = len(ws)
    direct = [s for s in sources if not isinstance(s, tuple)]
    n_cut = n - len(direct)

    def body(pos_ref, *refs):
        stat_refs, direct_refs = refs[:len(stats)], refs[len(stats):len(stats) + len(direct)]
        w_refs, m_refs, v_refs = (refs[len(stats) + len(direct) + k * n:len(stats) + len(direct) + (k + 1) * n]
                                  for k in range(3))
        outs = refs[len(stats) + len(direct) + 3 * n:]
        cut_refs, res = outs[:n_cut], outs[n_cut:]
        chip = pos_ref[1]
        i_direct = i_cut = 0
        for i, src in enumerate(sources):
            if not isinstance(src, tuple):
                g = direct_refs[i_direct][...]
                i_direct += 1
            else:
                if src[0] == "rows":
                    g = stat_refs[src[1]][src[2]:src[3], :]
                else:
                    _, k, lo, hi, width = src
                    g = jnp.zeros((hi - lo, width), F32)
                    for j in range(N_CHIPS):
                        g = jnp.where(chip == j, stat_refs[k][lo:hi, j * width:(j + 1) * width], g)
                cut_refs[i_cut][...] = g
                i_cut += 1
            d, mn, vn = _adamw_math(w_refs[i][...], g, m_refs[i][...], v_refs[i][...])
            res[i][...] = d
            res[n + i][...] = mn
            res[2 * n + i][...] = vn

    shapes = tuple(jax.ShapeDtypeStruct(w.shape, F32) for w in ws)
    cut_shapes = tuple(jax.ShapeDtypeStruct(w.shape, F32) for w, s in zip(ws, sources) if isinstance(s, tuple))
    out = pl.pallas_call(
        body, name="adamw_small",
        out_shape=cut_shapes + shapes * 3,
        in_specs=[pl.BlockSpec(memory_space=pltpu.SMEM)] + [VMEM_SPEC] * (len(stats) + len(direct) + 3 * n),
        out_specs=(VMEM_SPEC,) * (n_cut + 3 * n),
        compiler_params=_params(),
    )(pos, *stats, *direct, *ws, *ms, *vs)
    return out[:n_cut], out[n_cut:]


def _layer_norm_fwd(r):
    mu = jnp.mean(r, axis=-1, keepdims=True)
    xc = r - mu
    var = jnp.mean(jnp.square(xc), axis=-1, keepdims=True)
    rstd = lax.rsqrt(var + LN_EPS)
    return xc * rstd, rstd


def _layer_norm_bwd(dy, xhat, rstd, gain):
    dxh = dy * gain
    m1 = jnp.mean(dxh, axis=-1, keepdims=True)
    m2 = jnp.mean(dxh * xhat, axis=-1, keepdims=True)
    return rstd * (dxh - m1 - xhat * m2)


def _inv_count(tile, tm, win):
    t = (tile * tm + lax.broadcasted_iota(jnp.int32, (tm, 1), 0) + 1).astype(F32)
    return 1.0 / jnp.minimum(t, float(win))


def _fwd_mix(x, mod, conv_w, w_in4, w_pool, pool_scale, w_out, ln_g, ln_b, late_stacks, tm):
    t_len, d = x.shape
    cw = w_in4.shape[2]
    dg = cw // N_GROUPS
    nt = t_len // tm
    n_late = len(late_stacks)
    n_cp = n_late * (N_CHIPS - 1)
    pass_steps = [min(((k + 1) * nt) // n_late + (1 if k + 1 < n_late else -2), nt - 1) if nt > 2 else nt - 1
                  for k in range(n_late)]
    pass_steps = [max(p, 0) for p in pass_steps]

    def body(x_ref, mod_ref, cw_ref, win_ref, wp_ref, ps_ref, wout_ref, g_ref, b_ref, *rest):
        late = rest[n_late:2 * n_late]
        x1_ref, zs_ref, mix_ref, h1_ref, cat_ref = rest[2 * n_late:2 * n_late + 5]
        ubuf, vbuf, s_ici, r_ici, s_d2d, r_d2d = rest[2 * n_late + 5:]
        i = pl.program_id(0)
        start, forward, finish = _stack_gather(late, [s.shape[1] for s in late_stacks], s_ici, r_ici, s_d2d, r_d2d)

        @pl.when(i == 0)
        def _():
            start()
            ubuf[0:HALO, :] = jnp.zeros((HALO, cw), F32)
            vbuf[0:HALO, :] = jnp.zeros((HALO, cw), F32)

        xv = x_ref[...]
        sh1, sc1, g1 = mod_ref[0:1, 0:d], mod_ref[0:1, d:2 * d], mod_ref[0:1, 2 * d:3 * d]
        h1 = (xv * (1.0 + sc1) + sh1).astype(BF16)
        h1_ref[...] = h1
        zb = _dot(h1, win_ref[0])
        zc = _dot(h1, win_ref[1])
        zv = _dot(h1, win_ref[2])
        zp = _dot(h1, win_ref[3])
        u = zc * zv
        ubuf[HALO:HALO + tm, :] = u
        u1 = ubuf[pl.ds(HALO - 1, tm), :]
        u2 = ubuf[pl.ds(HALO - 2, tm), :]
        conv = cw_ref[0:1, :] * u2 + cw_ref[1:2, :] * u1 + cw_ref[2:3, :] * u
        ubuf[0:HALO, :] = ubuf[tm:tm + HALO, :]
        yc = zb * conv
        vbuf[HALO:HALO + tm, :] = zp
        ps, pms = [], []
        for gi, win in enumerate(POOL_WINDOWS):
            sl = slice(gi * dg, (gi + 1) * dg)
            acc = zp[:, sl]
            for s in range(1, win):
                acc = acc + vbuf[pl.ds(HALO - s, tm), sl]
            p_g = acc * _inv_count(i, tm, win) - zp[:, sl]
            ps.append(p_g)
            pms.append(_dot(p_g.astype(BF16), wp_ref[gi].astype(BF16)))
        vbuf[0:HALO, :] = vbuf[tm:tm + HALO, :]
        pooled = jnp.concatenate(ps, axis=1)
        yp = jnp.concatenate(pms, axis=1) * ps_ref[...]
        cat = jnp.concatenate([yc, yp], axis=1).astype(BF16)
        cat_ref[...] = cat
        mix = _dot(cat, wout_ref[...])
        mix_ref[...] = mix
        xhat, _ = _layer_norm_fwd(DEEPNORM_ALPHA * xv + (1.0 + g1) * mix)
        x1_ref[...] = xhat * g_ref[...] + b_ref[...]
        zs_ref[:, 0 * cw:1 * cw] = zb.astype(BF16)
        zs_ref[:, 1 * cw:2 * cw] = zc.astype(BF16)
        zs_ref[:, 2 * cw:3 * cw] = zv.astype(BF16)
        zs_ref[:, 3 * cw:4 * cw] = conv.astype(BF16)
        zs_ref[:, 4 * cw:5 * cw] = pooled.astype(BF16)

        for k in range(n_late):
            @pl.when(i == pass_steps[k])
            def _():
                forward(k)

        @pl.when(i == nt - 1)
        def _():
            finish()

    tile = lambda w: pl.BlockSpec((tm, w), lambda i: (i, 0))
    n_in = 9
    return pl.pallas_call(
        body, name="fwd_mix",
        out_shape=tuple(jax.ShapeDtypeStruct(s.shape, s.dtype) for s in late_stacks)
        + (jax.ShapeDtypeStruct((t_len, d), F32), jax.ShapeDtypeStruct((t_len, 5 * cw), BF16),
           jax.ShapeDtypeStruct((t_len, d), F32), jax.ShapeDtypeStruct((t_len, d), BF16),
           jax.ShapeDtypeStruct((t_len, 2 * cw), BF16)),
        grid=(nt,),
        in_specs=[tile(d)] + [VMEM_SPEC] * 8 + [HBM_SPEC] * n_late,
        out_specs=(HBM_SPEC,) * n_late + (tile(d), tile(5 * cw), tile(d), tile(d), tile(2 * cw)),
        input_output_aliases={n_in + k: k for k in range(n_late)},
        scratch_shapes=[pltpu.VMEM((tm + HALO, cw), F32), pltpu.VMEM((tm + HALO, cw), F32),
                        _dma_sems(n_cp), _dma_sems(n_cp), _dma_sems(n_cp), _dma_sems(n_cp)],
        compiler_params=_params(dimension_semantics=("arbitrary",)),
    )(x, mod, conv_w, w_in4, w_pool, pool_scale, w_out, ln_g, ln_b, *late_stacks)


def _mlp_fwd_bwd(x1, tgt, mod, w_mi4, w_mo4, ln_g, ln_b, tm):
    t_len, d = x1.shape
    fq = w_mi4.shape[2]
    nt = t_len // tm

    def body(x1_ref, tgt_ref, mod_ref, wi_ref, wo_ref, g_ref, b_ref,
             dx1_ref, da_ref, s_ref, df_ref, h2_ref, st_ref, relu_buf, acc):
        i = pl.program_id(0)

        @pl.when(i == 0)
        def _():
            acc[...] = jnp.zeros(acc.shape, F32)

        x1v = x1_ref[...]
        sh2, sc2, g2 = mod_ref[0:1, 3 * d:4 * d], mod_ref[0:1, 4 * d:5 * d], mod_ref[0:1, 5 * d:6 * d]
        h2 = (x1v * (1.0 + sc2) + sh2).astype(BF16)
        h2_ref[...] = h2
        f = jnp.zeros((tm, d), F32)
        for j in range(N_CHIPS):
            a = jnp.maximum(_dot(h2, wi_ref[j]), 0.0)
            relu_buf[j] = a
            s = (a * a).astype(BF16)
            s_ref[j] = s
            f = f + _dot(s, wo_ref[j])
        xhat, rstd = _layer_norm_fwd(DEEPNORM_ALPHA * x1v + (1.0 + g2) * f)
        gain = g_ref[...]
        err = xhat * gain + b_ref[...] - tgt_ref[...]
        dy = err * (1.0 / d)
        dr2 = _layer_norm_bwd(dy, xhat, rstd, gain)
        df = ((1.0 + g2) * dr2).astype(BF16)
        df_ref[...] = df
        dh2 = jnp.zeros((tm, d), F32)
        for j in range(N_CHIPS):
            ds = _dot_nt(df, wo_ref[j])
            da = (ds * (2.0 * relu_buf[j])).astype(BF16)
            da_ref[j] = da
            dh2 = dh2 + _dot_nt(da, wi_ref[j])
        dx1_ref[...] = DEEPNORM_ALPHA * dr2 + dh2 * (1.0 + sc2)
        acc[0] += _fold8(dy * xhat)
        acc[1] += _fold8(dy)
        acc[2] += _fold8(dr2 * f)
        acc[3] += _fold8(dh2 * x1v)
        acc[4] += _fold8(dh2)
        acc[5] += _fold8(err * err)

        @pl.when(i == nt - 1)
        def _():
            for k in range(5):
                st_ref[k:k + 1, :] = jnp.sum(acc[k], axis=0, keepdims=True)
            loss = jnp.sum(acc[5]) * (0.5 / d)
            st_ref[5:6, :] = jnp.broadcast_to(loss, (1, d))
            st_ref[6:8, :] = jnp.zeros((2, d), F32)

    tile = lambda w: pl.BlockSpec((tm, w), lambda i: (i, 0))
    tile4 = pl.BlockSpec((N_CHIPS, tm, fq), lambda i: (0, i, 0))
    return pl.pallas_call(
        body, name="mlp_fwd_bwd",
        out_shape=(jax.ShapeDtypeStruct((t_len, d), F32),
                   jax.ShapeDtypeStruct((N_CHIPS, t_len, fq), BF16), jax.ShapeDtypeStruct((N_CHIPS, t_len, fq), BF16),
                   jax.ShapeDtypeStruct((t_len, d), BF16), jax.ShapeDtypeStruct((t_len, d), BF16),
                   jax.ShapeDtypeStruct((SUBLANES, d), F32)),
        grid=(nt,),
        in_specs=[tile(d), tile(d)] + [VMEM_SPEC] * 5,
        out_specs=(tile(d), tile4, tile4, tile(d), tile(d), pl.BlockSpec((SUBLANES, d), lambda i: (0, 0))),
        scratch_shapes=[pltpu.VMEM((N_CHIPS, tm, fq), F32), pltpu.VMEM((6, SUBLANES, d), F32)],
        compiler_params=_params(dimension_semantics=("arbitrary",)),
    )(x1, tgt, mod, w_mi4, w_mo4, ln_g, ln_b)


def _mlp_wgrad(h2, da4, s4, df, pos, tt):
    t_len, d = h2.shape
    fq = da4.shape[2]
    nt = t_len // tt
    assert nt >= 3
    rh = (d // 2, fq // 2)
    last = N_CHIPS - 1

    def body(pos_ref, h2_ref, da_ref, s_ref, df_ref, qi_ref, qo_ref, gi_ref, go_ref,
             acc_i, acc_o, rcv_i, rcv_o, s_sem, r_sem):
        j, t = pl.program_id(0), pl.program_id(1)
        slot = lax.rem(j, 2)
        x, y, c = _mesh_pos()
        accs, rcvs, q_refs, g_refs = (acc_i, acc_o), (rcv_i, rcv_o), (qi_ref, qo_ref), (gi_ref, go_ref)

        def to_sibling(a, sl):
            theirs = pl.multiple_of((1 - c) * rh[a], rh[a])
            return pltpu.make_async_remote_copy(accs[a].at[sl].at[pl.ds(theirs, rh[a]), :], rcvs[a].at[sl],
                                                s_sem.at[2 * sl + a], r_sem.at[2 * sl + a],
                                                device_id=(x, y, 1 - c), device_id_type=MESH)

        def emit(chunk, sl):
            for a in range(2):
                to_sibling(a, sl).wait()
                mine = pl.multiple_of(c * rh[a], rh[a])
                q = (accs[a][sl, pl.ds(mine, rh[a]), :] + rcvs[a][sl]).astype(BF16)
                q_refs[a][...] = q

                @pl.when(chunk == pos_ref[1])
                def _():
                    g_refs[a][...] = q

        @pl.when(t == 0)
        def _():
            acc_i[slot] = jnp.zeros((d, fq), F32)
            acc_o[slot] = jnp.zeros((fq, d), F32)

        acc_i[slot] += _dot_tn(h2_ref[...], da_ref[...])
        acc_o[slot] += _dot_tn(s_ref[...], df_ref[...])

        @pl.when((t == 1) & (j >= 1))
        def _():
            emit(j - 1, 1 - slot)

        @pl.when(t == nt - 1)
        def _():
            for a in range(2):
                to_sibling(a, slot).start()

        @pl.when((t == nt - 1) & (j == last))
        def _():
            emit(j, slot)

    q_index = lambda j, t, p: (jnp.where(t == nt - 1, j, jnp.maximum(j - 1, 0)), 0, 0)
    got_index = lambda j, t, p: (p[1], 0, 0)
    blocks = ((None, rh[0], fq), (None, rh[1], d))
    return pl.pallas_call(
        body, name="mlp_wgrad",
        out_shape=tuple(jax.ShapeDtypeStruct((N_CHIPS,) + b[1:], BF16) for b in blocks) * 2,
        grid_spec=pltpu.PrefetchScalarGridSpec(
            num_scalar_prefetch=1, grid=(N_CHIPS, nt),
            in_specs=[pl.BlockSpec((tt, d), lambda j, t, p: (t, 0)),
                      pl.BlockSpec((None, tt, fq), lambda j, t, p: (j, t, 0)),
                      pl.BlockSpec((None, tt, fq), lambda j, t, p: (j, t, 0)),
                      pl.BlockSpec((tt, d), lambda j, t, p: (t, 0))],
            out_specs=tuple(pl.BlockSpec(b, q_index) for b in blocks)
            + tuple(pl.BlockSpec(b, got_index) for b in blocks),
            scratch_shapes=[pltpu.VMEM((2, d, fq), F32), pltpu.VMEM((2, fq, d), F32),
                            pltpu.VMEM((2, rh[0], fq), F32), pltpu.VMEM((2, rh[1], d), F32),
                            _dma_sems(4), _dma_sems(4)]),
        compiler_params=_params(dimension_semantics=("arbitrary", "arbitrary")),
    )(pos, h2, da4, s4, df)


def _bwd_mix(x, mix, dx1, zs, mod, conv_w, w_in4, w_pool, pool_scale, w_out, ln_g, after, tm):
    t_len, d = x.shape
    cw = w_in4.shape[2]
    dg = cw // N_GROUPS
    nt = t_len // tm

    def body(x_ref, mix_ref, dx1_ref, zs_ref, mod_ref, cw_ref, win_ref, wp_ref, ps_ref, wout_ref, g_ref, after_ref,
             gx_ref, dz_ref, dmix_ref, sd_ref, sc_ref, dwp_ref, dbuf, qbuf, acc_d, acc_c, acc_p):
        i = pl.program_id(0)
        tile_idx = nt - 1 - i

        @pl.when(i == 0)
        def _():
            dbuf[tm:tm + HALO, :] = jnp.zeros((HALO, cw), F32)
            qbuf[tm:tm + HALO, :] = jnp.zeros((HALO, cw), F32)
            acc_d[...] = jnp.zeros(acc_d.shape, F32)
            acc_c[...] = jnp.zeros(acc_c.shape, F32)
            acc_p[...] = jnp.zeros(acc_p.shape, F32)

        xv, mixv, dx1v = x_ref[...], mix_ref[...], dx1_ref[...]
        sc1, g1 = mod_ref[0:1, d:2 * d], mod_ref[0:1, 2 * d:3 * d]
        xhat, rstd = _layer_norm_fwd(DEEPNORM_ALPHA * xv + (1.0 + g1) * mixv)
        dr1 = _layer_norm_bwd(dx1v, xhat, rstd, g_ref[...])
        dmix = ((1.0 + g1) * dr1).astype(BF16)
        dmix_ref[...] = dmix
        dcat = _dot_nt(dmix, wout_ref[...])
        dyc, dyp = dcat[:, 0:cw], dcat[:, cw:2 * cw]
        zb, zc, zv = (zs_ref[:, k * cw:(k + 1) * cw].astype(F32) for k in range(3))
        conv, pooled = zs_ref[:, 3 * cw:4 * cw].astype(F32), zs_ref[:, 4 * cw:5 * cw]
        dzb = dyc * conv
        dcv = dyc * zb
        dbuf[0:tm, :] = dcv
        d1 = dbuf[pl.ds(1, tm), :]
        d2 = dbuf[pl.ds(2, tm), :]
        du = cw_ref[2:3, :] * dcv + cw_ref[1:2, :] * d1 + cw_ref[0:1, :] * d2
        dbuf[tm:tm + HALO, :] = dbuf[0:HALO, :]
        u = zc * zv
        acc_c[0] += _fold8(u * d2)
        acc_c[1] += _fold8(u * d1)
        acc_c[2] += _fold8(u * dcv)
        dzc = du * zv
        dzv = du * zc
        dpw = dyp * ps_ref[...]
        dps, pms = [], []
        for gi, win in enumerate(POOL_WINDOWS):
            sl = slice(gi * dg, (gi + 1) * dg)
            p_g = pooled[:, sl].astype(BF16)
            dpw_g = dpw[:, sl].astype(BF16)
            wp_g = wp_ref[gi].astype(BF16)
            pms.append(_dot(p_g, wp_g))
            acc_p[gi] += _dot_tn(p_g, dpw_g)
            dp_g = _dot_nt(dpw_g, wp_g)
            dps.append(dp_g)
            qbuf[0:tm, sl] = dp_g * _inv_count(tile_idx, tm, win)
        acc_c[3] += _fold8(dyp * jnp.concatenate(pms, axis=1))
        dzps = []
        for gi, win in enumerate(POOL_WINDOWS):
            sl = slice(gi * dg, (gi + 1) * dg)
            acc = qbuf[0:tm, sl]
            for s in range(1, win):
                acc = acc + qbuf[pl.ds(s, tm), sl]
            dzps.append(acc - dps[gi])
        qbuf[tm:tm + HALO, :] = qbuf[0:HALO, :]
        dz = [dzb.astype(BF16), dzc.astype(BF16), dzv.astype(BF16), jnp.concatenate(dzps, axis=1).astype(BF16)]
        dh1 = jnp.zeros((tm, d), F32)
        for j in range(N_CHIPS):
            dz_ref[j] = dz[j]
            dh1 = dh1 + _dot_nt(dz[j], win_ref[j])
        gx_ref[...] = DEEPNORM_ALPHA * dr1 + dh1 * (1.0 + sc1)
        acc_d[0] += _fold8(dx1v * xhat)
        acc_d[1] += _fold8(dx1v)
        acc_d[2] += _fold8(dr1 * mixv)
        acc_d[3] += _fold8(dh1 * xv)
        acc_d[4] += _fold8(dh1)

        @pl.when(i == nt - 1)
        def _():
            for k in range(5):
                sd_ref[k:k + 1, :] = jnp.sum(acc_d[k], axis=0, keepdims=True)
            sd_ref[5:8, :] = jnp.zeros((3, d), F32)
            for k in range(4):
                sc_ref[k:k + 1, :] = jnp.sum(acc_c[k], axis=0, keepdims=True)
            sc_ref[4:8, :] = jnp.zeros((4, cw), F32)
            dwp_ref[...] = acc_p[...]

    rtile = lambda w: pl.BlockSpec((tm, w), lambda i: (nt - 1 - i, 0))
    whole = lambda shape: pl.BlockSpec(shape, lambda i: tuple(0 for _ in shape))
    return pl.pallas_call(
        body, name="bwd_mix",
        out_shape=(jax.ShapeDtypeStruct((t_len, d), F32), jax.ShapeDtypeStruct((N_CHIPS, t_len, cw), BF16),
                   jax.ShapeDtypeStruct((t_len, d), BF16), jax.ShapeDtypeStruct((SUBLANES, d), F32),
                   jax.ShapeDtypeStruct((SUBLANES, cw), F32), jax.ShapeDtypeStruct((N_GROUPS, dg, dg), F32)),
        grid=(nt,),
        in_specs=[rtile(d), rtile(d), rtile(d), rtile(5 * cw)] + [VMEM_SPEC] * 7
        + [pl.BlockSpec(memory_space=pl.ANY)],
        out_specs=(rtile(d), pl.BlockSpec((N_CHIPS, tm, cw), lambda i: (0, nt - 1 - i, 0)), rtile(d),
                   whole((SUBLANES, d)), whole((SUBLANES, cw)), whole((N_GROUPS, dg, dg))),
        scratch_shapes=[pltpu.VMEM((tm + HALO, cw), F32), pltpu.VMEM((tm + HALO, cw), F32),
                        pltpu.VMEM((5, SUBLANES, d), F32), pltpu.VMEM((4, SUBLANES, cw), F32),
                        pltpu.VMEM((N_GROUPS, dg, dg), F32)],
        compiler_params=_params(dimension_semantics=("arbitrary",)),
    )(x, mix, dx1, zs, mod, conv_w, w_in4, w_pool, pool_scale, w_out, ln_g, after)


def _mix_wgrad(h1, dz4, cat, dmix, small, dmod_rows, tt):
    t_len, d = h1.shape
    cw = dz4.shape[2]
    nt = t_len // tt
    ro = (2 * cw) // N_CHIPS
    rh = (d // 2, ro // 2)
    n_small = len(small)

    def body(h1_ref, dz_ref, cat_ref, dmix_ref, *rest):
        p_refs = rest[:n_small]
        qi_ref, qo_ref, gi_ref, go_ref = rest[n_small:n_small + 4]
        sum_refs = rest[n_small + 4:2 * n_small + 4]
        dmod_ref, dmod_sum_ref = rest[2 * n_small + 4:2 * n_small + 6]
        scratch = rest[2 * n_small + 6:]
        acc_i, acc_o, rcv_i, rcv_o = scratch[:4]
        all_bufs = scratch[4:4 + n_small]
        s_sem, r_sem, s_small, r_small_sem = scratch[4 + n_small:]
        t = pl.program_id(0)
        x, y, c = _mesh_pos()
        chip = 2 * x + y
        groups = [(p_refs[k], sum_refs[k], None, all_bufs[k]) for k in range(n_small)]
        small_start, small_finish = _small_allreduce_steps(groups, s_small, r_small_sem)
        accs, rcvs, q_refs, g_refs = (acc_i, acc_o), (rcv_i, rcv_o), (qi_ref, qo_ref), (gi_ref, go_ref)

        @pl.when(t == 0)
        def _():
            small_start()
            acc_i[...] = jnp.zeros(acc_i.shape, F32)
            acc_o[...] = jnp.zeros(acc_o.shape, F32)

        def accumulate(j):
            acc_i[j] += _dot_tn(h1_ref[...], dz_ref[j])
            acc_o[j] += _dot_tn(cat_ref[:, j * ro:(j + 1) * ro], dmix_ref[...])

        def to_sibling(a, j):
            theirs = pl.multiple_of((1 - c) * rh[a], rh[a])
            return pltpu.make_async_remote_copy(accs[a].at[j].at[pl.ds(theirs, rh[a]), :], rcvs[a].at[j],
                                                s_sem.at[a * N_CHIPS + j], r_sem.at[a * N_CHIPS + j],
                                                device_id=(x, y, 1 - c), device_id_type=MESH)

        @pl.when(t < nt - 1)
        def _():
            for j in range(N_CHIPS):
                accumulate(j)

        @pl.when(t == nt - 1)
        def _():
            for j in range(N_CHIPS):
                accumulate(j)
                for a in range(2):
                    to_sibling(a, j).start()
            for a in range(2):
                mine = pl.multiple_of(c * rh[a], rh[a])
                for j in range(N_CHIPS):
                    to_sibling(a, j).wait()
                    q_refs[a][j] = (accs[a][j, pl.ds(mine, rh[a]), :] + rcvs[a][j]).astype(BF16)
                g_refs[a][chip] = q_refs[a][chip]
            small_finish()
            for k, (arr, row) in enumerate(dmod_rows):
                cols = slice(k * d, (k + 1) * d)
                dmod_sum_ref[:, cols] = sum_refs[arr][row:row + 1, :]
                for b in range(N_DEV):
                    dmod_ref[b:b + 1, cols] = all_bufs[arr][b, row:row + 1, :]

    stacks = ((N_CHIPS, rh[0], cw), (N_CHIPS, rh[1], d))
    n_cp = n_small * (N_DEV - 1)
    n_mod = len(dmod_rows) * d
    return pl.pallas_call(
        body, name="mix_wgrad",
        out_shape=tuple(jax.ShapeDtypeStruct(s, BF16) for s in stacks) * 2
        + tuple(jax.ShapeDtypeStruct(a.shape, F32) for a in small)
        + (jax.ShapeDtypeStruct((N_DEV, n_mod), F32), jax.ShapeDtypeStruct((1, n_mod), F32)),
        grid=(nt,),
        in_specs=[pl.BlockSpec((tt, d), lambda t: (t, 0)), pl.BlockSpec((N_CHIPS, tt, cw), lambda t: (0, t, 0)),
                  pl.BlockSpec((tt, 2 * cw), lambda t: (t, 0)), pl.BlockSpec((tt, d), lambda t: (t, 0))]
        + [VMEM_SPEC] * n_small,
        out_specs=(VMEM_SPEC,) * (6 + n_small),
        scratch_shapes=[pltpu.VMEM((N_CHIPS, d, cw), F32), pltpu.VMEM((N_CHIPS, ro, d), F32),
                        pltpu.VMEM(stacks[0], F32), pltpu.VMEM(stacks[1], F32)]
        + [pltpu.VMEM((N_DEV,) + a.shape, F32) for a in small]
        + [_dma_sems(2 * N_CHIPS), _dma_sems(2 * N_CHIPS), _dma_sems(n_cp), _dma_sems(n_cp)],
        compiler_params=_params(dimension_semantics=("arbitrary",)),
    )(h1, dz4, cat, dmix, *small)


def kernel(x, c, w_ada, b_ada, w_in, conv_w, w_pool, pool_scale, w_out, ln1_g, ln1_b, w_mlp_in, w_mlp_out, ln2_g, ln2_b, loss_target, m_w_ada, m_b_ada, m_w_in, m_conv_w, m_w_pool, m_pool_scale, m_w_out, m_ln1_g, m_ln1_b, m_w_mlp_in, m_w_mlp_out, m_ln2_g, m_ln2_b, v_w_ada, v_b_ada, v_w_in, v_conv_w, v_w_pool, v_pool_scale, v_w_out, v_ln1_g, v_ln1_b, v_w_mlp_in, v_w_mlp_out, v_ln2_g, v_ln2_b):
    t_len, d = x.shape[1], x.shape[2]
    cw = w_in.shape[2]
    cq = conv_w.shape[2]
    dg = w_pool.shape[2]
    assert cw == N_GROUPS * dg and cq * N_CHIPS == cw and LANES % cq == 0
    tm_mix = min(MIX_TOKEN_TILE, t_len)
    tm_mlp = min(MLP_TOKEN_TILE, t_len)
    tt = min(WGRAD_TILE, t_len // 4)
    chip = 2 * lax.axis_index("x") + lax.axis_index("y")
    pos = jnp.stack([lax.axis_index("c"), chip]).astype(jnp.int32)

    x2, tgt = x[0], loss_target[0]
    big_w = [w_in[0], w_out[0], w_mlp_in[0], w_mlp_out[0]]
    big_m = [m_w_in[0], m_w_out[0], m_w_mlp_in[0], m_w_mlp_out[0]]
    big_v = [v_w_in[0], v_w_out[0], v_w_mlp_in[0], v_w_mlp_out[0]]
    names = ["w_in", "w_out", "w_mlp_in", "w_mlp_out"]

    mod, cond_all, conv_full, w_in4, w_out4, w_mi_own, w_mo_own = _setup_exchange(
        c, w_ada[0], b_ada, conv_w[0], big_w, 2)
    w_out_full = w_out4.reshape(2 * cw, d)
    w_pool_g = w_pool[0]

    w_mi4, w_mo4, x1, zs, mix, h1, cat = _fwd_mix(x2, mod, conv_full, w_in4, w_pool_g, pool_scale, w_out_full,
                                                  ln1_g, ln1_b, [w_mi_own, w_mo_own], tm_mix)
    dx1, da4, s4, df, h2, st2 = _mlp_fwd_bwd(x1, tgt, mod, w_mi4, w_mo4, ln2_g, ln2_b, tm_mlp)
    mlp_qi, mlp_qo, mlp_gi, mlp_go = _mlp_wgrad(h2, da4, s4, df, pos, tt)
    mlp_started = _split_start([mlp_qi, mlp_qo, mlp_gi, mlp_go], _chunk_exchange_plan(2), 2 * (N_CHIPS - 1),
                               "exchange_mlp")
    grad_x, dz4, dmix, st1, stc, dw_pool = _bwd_mix(x2, mix, dx1, zs, mod, conv_full, w_in4, w_pool_g, pool_scale,
                                                    w_out_full, ln1_g, mlp_started[-1], tm_mix)

    dmod_rows = [(0, 4), (0, 3), (0, 2), (1, 4), (1, 3), (1, 2)]
    mix_qi, mix_qo, mix_gi, mix_go, t1, t2, tc, g_w_pool, dmod_all, g_b_ada = _mix_wgrad(
        h1, dz4, cat, dmix, [st1, st2, stc, dw_pool], dmod_rows, tt)
    loss = t2[5, 0]
    na = w_ada.shape[2]
    dmod_mine = lax.dynamic_slice_in_dim(dmod_all, chip * na, na, axis=1)

    exchange, n_cp = _chunk_exchange_plan(2), 2 * (N_CHIPS - 1)
    mlp_got = _split_wait(mlp_started, exchange, t1, "exchange_mlp")[2:]
    mix_started = _split_start([mix_qi, mix_qo, mix_gi, mix_go], exchange, n_cp, "exchange_mix")
    mlp_halves = list(_sum_chips(mlp_got, pos, mix_started[-1], "rs_total_mlp"))
    join = _sibling_join_plan([h.shape for h in mlp_halves])
    join_started = _split_start(mlp_halves, join, 2, "join_mlp")
    g_ada, d_ada, nm_ada, nv_ada = _adamw_ada(cond_all, dmod_mine, w_ada[0], m_w_ada[0], v_w_ada[0],
                                              join_started[-1])
    mlp_g = _split_wait(join_started, join, d_ada, "join_mlp")
    mlp_out = _adamw_big(mlp_g, big_w[2:], big_m[2:], big_v[2:], "adamw_mlp")
    mix_got = _split_wait(mix_started, exchange, mlp_out[1][0], "exchange_mix")[2:]
    mix_halves = list(_sum_chips(mix_got, pos, d_ada, "rs_total_mix"))
    mix_g = _sibling_join(mix_halves, "mix")
    mix_out = _adamw_big(mix_g, big_w[:2], big_m[:2], big_v[:2], "adamw_mix")
    big_g = list(mix_g) + list(mlp_g)
    big_out = mix_out + mlp_out

    sources = [g_b_ada, ("chip_cols", 2, 0, 3, cq), g_w_pool.reshape(-1, dg), ("rows", 2, 3, 4),
               ("rows", 0, 0, 1), ("rows", 0, 1, 2), ("rows", 1, 0, 1), ("rows", 1, 1, 2)]
    small_w = [b_ada, conv_w[0], w_pool.reshape(-1, dg), pool_scale, ln1_g, ln1_b, ln2_g, ln2_b]
    small_m = [m_b_ada, m_conv_w[0], m_w_pool.reshape(-1, dg), m_pool_scale, m_ln1_g, m_ln1_b, m_ln2_g, m_ln2_b]
    small_v = [v_b_ada, v_conv_w[0], v_w_pool.reshape(-1, dg), v_pool_scale, v_ln1_g, v_ln1_b, v_ln2_g, v_ln2_b]
    cut, sm = _adamw_small(pos, [t1, t2, tc], sources, small_w, small_m, small_v)
    small_g = [sources[0], cut[0], sources[2]] + list(cut[1:])
    ns = len(small_w)
    s_delta, s_m, s_v = sm[:ns], sm[ns:2 * ns], sm[2 * ns:]

    def assemble(ada, small, big):
        return [ada[None], small[0], big[0][None], small[1][None], small[2].reshape(w_pool.shape), small[3],
                big[1][None], small[4], small[5], big[2][None], big[3][None], small[6], small[7]]

    grads = assemble(g_ada, small_g, big_g)
    deltas = assemble(d_ada, s_delta, [o[0] for o in big_out])
    new_m = assemble(nm_ada, s_m, [o[1] for o in big_out])
    new_v = assemble(nv_ada, s_v, [o[2] for o in big_out])
    return (loss, grad_x[None], *grads, *deltas, *new_m, *new_v)
```
